```python
import math
import jax
import jax.numpy as jnp
from jax import lax
import numpy as np

D_MODEL = 1024
BATCH = 1
SEQ = 16384
DEPTH = 2
DEC_BATCH = 8
DEC_SEQ = 16
PAST_LEN = 4096

CHUNK = 64
D_CONV_MIX = 256
CONV_A_W = 3
N_HEADS = 6
N_KV_HEADS = 2
HEAD_DIM = 64
D_ATT = N_HEADS * HEAD_DIM
N_IDX_HEADS = 4
D_IDX = 64
TOPK_MAX = 256
Q_BLOCK = 128
N_BUCKETS = 32
MAX_DISTANCE = 128
SSM_HEADS = 6
SSM_HEAD_DIM = 64
D_SSM = SSM_HEADS * SSM_HEAD_DIM
SSM_GROUPS = 2
SSM_HPG = SSM_HEADS // SSM_GROUPS
SSM_STATE = 128
SSM_CONV_W = 4
SSD_BLOCK = 128
D_XBC = D_SSM + 2 * SSM_GROUPS * SSM_STATE
D_MIX = D_CONV_MIX + D_ATT + D_SSM
D_FF = 2816
CONV_F_W = 3
NORM_EPS = 1e-6
SPLIT_SIZES = (D_CONV_MIX, D_CONV_MIX, D_CONV_MIX,
               D_ATT, N_KV_HEADS * HEAD_DIM, N_KV_HEADS * HEAD_DIM,
               N_IDX_HEADS * D_IDX, D_IDX, N_IDX_HEADS,
               D_SSM, D_XBC, SSM_HEADS)
D_IN = 3 * D_CONV_MIX + D_ATT + 2 * N_KV_HEADS * HEAD_DIM + N_IDX_HEADS * D_IDX + D_IDX + N_IDX_HEADS + D_SSM + D_XBC + SSM_HEADS

kernel_name = 'hybrid_stream_encoder_step'


def rmsnorm(x, w):
    xf = x.astype(jnp.float32)
    y = xf * lax.rsqrt(jnp.mean(xf * xf, axis=-1, keepdims=True) + NORM_EPS)
    return (y * w.astype(jnp.float32)).astype(x.dtype)


def causal_dwconv(x, buf, w):
    t = x.shape[1]
    xp = jnp.concatenate([buf.astype(x.dtype), x], axis=1)
    y = xp[:, 0:t] * w[0]
    for j in range(1, w.shape[0]):
        y = y + xp[:, j:j + t] * w[j]
    return y, xp[:, t:]


def t5_bucket(rel):
    half = N_BUCKETS // 2
    max_exact = half // 2
    n = jnp.abs(rel)
    nf = jnp.maximum(n, 1).astype(jnp.float32)
    large = max_exact + (jnp.log(nf / max_exact) / math.log(MAX_DISTANCE / max_exact)
                         * (half - max_exact)).astype(jnp.int32)
    large = jnp.minimum(large, half - 1)
    return jnp.where(rel > 0, half, 0) + jnp.where(n < max_exact, n, large)


def indexer_select(qi, wi, ki, q_pos, n_sel):
    s = jax.nn.relu(jnp.einsum('bqhd,bkd->bqhk', qi, ki).astype(jnp.float32) * D_IDX ** -0.5)
    score = jnp.einsum('bqh,bqhk->bqk', wi.astype(jnp.float32) * N_IDX_HEADS ** -0.5, s)
    k_pos = jnp.arange(ki.shape[1], dtype=jnp.int32)
    q_chunk = (q_pos // CHUNK)[:, None]
    adm = (k_pos // CHUNK)[None, :] <= q_chunk
    score = jnp.where(adm[None], score, -jnp.inf)
    _, sel = lax.top_k(score, n_sel)
    valid = (sel // CHUNK) <= q_chunk[None]
    return sel, valid


def sparse_attend(q, k_all, v_all, sel, valid, q_pos, rel_bias):
    b, tq = q.shape[:2]
    n_sel = sel.shape[-1]
    grp = N_HEADS // N_KV_HEADS
    bi = jnp.arange(b)[:, None, None]
    kg = k_all[bi, sel]
    vg = v_all[bi, sel]
    qg = q.reshape(b, tq, N_KV_HEADS, grp, HEAD_DIM)
    logits = jnp.einsum('bqhgd,bqkhd->bqhgk', qg, kg).astype(jnp.float32) * HEAD_DIM ** -0.5
    bias = rel_bias[t5_bucket(sel - q_pos[None, :, None])].astype(jnp.float32)
    bias = bias.reshape(b, tq, n_sel, N_KV_HEADS, grp).transpose(0, 1, 3, 4, 2)
    logits = jnp.where(valid[:, :, None, None, :], logits + bias, -jnp.inf)
    p = jax.nn.softmax(logits, axis=-1)
    out = jnp.einsum('bqhgk,bqkhd->bqhgd', p.astype(vg.dtype), vg)
    return out.reshape(b, tq, D_ATT)


def dsa_prompt(q, k, v, qi, ki, wi, rel_bias):
    b, s = q.shape[:2]
    nb = s // Q_BLOCK
    n_sel = min(TOPK_MAX, s // 4)

    def blocks(a):
        return a.reshape((b, nb, Q_BLOCK) + a.shape[2:]).swapaxes(0, 1)

    pos = jnp.arange(s, dtype=jnp.int32).reshape(nb, Q_BLOCK)

    def one(args):
        q_b, qi_b, wi_b, pos_b = args
        sel, valid = indexer_select(qi_b, wi_b, ki, pos_b, n_sel)
        return sparse_attend(q_b, k, v, sel, valid, pos_b, rel_bias)

    out = lax.map(one, (blocks(q), blocks(qi), blocks(wi), pos))
    return out.swapaxes(0, 1).reshape(b, s, D_ATT)


def ssd(x, dt, a, bm, cm, h0, blk):
    f32 = jnp.float32
    b, t = x.shape[:2]
    nc = t // blk
    x = x.astype(f32).reshape(b, nc, blk, SSM_GROUPS, SSM_HPG, SSM_HEAD_DIM)
    dt = dt.astype(f32).reshape(b, nc, blk, SSM_GROUPS, SSM_HPG)
    bm = bm.astype(f32).reshape(b, nc, blk, SSM_GROUPS, SSM_STATE)
    cm = cm.astype(f32).reshape(b, nc, blk, SSM_GROUPS, SSM_STATE)
    a_cum = jnp.cumsum(dt * a, axis=2)
    xdt = x * dt[..., None]
    seg = a_cum[:, :, :, None] - a_cum[:, :, None, :]
    causal = jnp.tril(jnp.ones((blk, blk), dtype=bool))[:, :, None, None]
    decay = jnp.exp(jnp.where(causal, seg, -jnp.inf))
    cb = jnp.einsum('bclgn,bcsgn->bclsg', cm, bm)
    y_diag = jnp.einsum('bclsg,bclsgr,bcsgrp->bclgrp', cb, decay, xdt)
    last = a_cum[:, :, -1]
    w_state = jnp.exp(last[:, :, None] - a_cum)
    states = jnp.einsum('bclgn,bclgr,bclgrp->bcgrpn', bm, w_state, xdt)

    def step(h, inp):
        s_c, d_c = inp
        return h * d_c[..., None, None] + s_c, h

    h_fin, h_prev = lax.scan(step, h0.astype(f32),
                             (states.swapaxes(0, 1), jnp.exp(last).swapaxes(0, 1)))
    h_prev = h_prev.swapaxes(0, 1)
    y_off = jnp.einsum('bclgn,bcgrpn,bclgr->bclgrp', cm, h_prev, jnp.exp(a_cum))
    y = (y_diag + y_off).reshape(b, t, SSM_GROUPS, SSM_HPG, SSM_HEAD_DIM)
    return y, h_fin


def gated_group_rmsnorm(y, z, w):
    b, t = y.shape[:2]
    g = (y.astype(jnp.float32) * jax.nn.silu(z.astype(jnp.float32))).reshape(b, t, SSM_GROUPS, -1)
    g = g * lax.rsqrt(jnp.mean(g * g, axis=-1, keepdims=True) + NORM_EPS)
    return g.reshape(b, t, D_SSM) * w.astype(jnp.float32)


def trunk_layer(x, p, rel_bias, buf_a, buf_s, h0, buf_f, kv_past, ssd_blk):
    b, t, _ = x.shape
    u = rmsnorm(x, p['norm_mix_pre'])
    proj = u @ p['w_in']
    points = [int(v) for v in np.cumsum(SPLIT_SIZES)[:-1]]
    a_b, a_c, a_h, q, k, v, qi, ki, wi, z, xbc, dt_raw = jnp.split(proj, points, axis=-1)

    a_conv, buf_a_new = causal_dwconv(a_c * a_h, buf_a, p['conv_a_w'])
    y_a = a_b * a_conv

    q = q.reshape(b, t, N_HEADS, HEAD_DIM)
    k = k.reshape(b, t, N_KV_HEADS, HEAD_DIM)
    v = v.reshape(b, t, N_KV_HEADS, HEAD_DIM)
    qi = qi.reshape(b, t, N_IDX_HEADS, D_IDX)
    if kv_past is None:
        y_b = dsa_prompt(q, k, v, qi, ki, wi, rel_bias)
    else:
        ck, cv, cki = kv_past
        past = ck.shape[1]
        k_all = jnp.concatenate([ck.astype(k.dtype), k], axis=1)
        v_all = jnp.concatenate([cv.astype(v.dtype), v], axis=1)
        ki_all = jnp.concatenate([cki.astype(ki.dtype), ki], axis=1)
        n_sel = min(TOPK_MAX, (past + t) // 4)
        q_pos = past + jnp.arange(t, dtype=jnp.int32)
        sel, valid = indexer_select(qi, wi, ki_all, q_pos, n_sel)
        y_b = sparse_attend(q, k_all, v_all, sel, valid, q_pos, rel_bias)

    xbc_c, buf_s_new = causal_dwconv(xbc, buf_s, p['conv_ssm_w'])
    xbc_c = jax.nn.silu(xbc_c + p['conv_ssm_b'])
    xs, bm, cm = jnp.split(xbc_c, [D_SSM, D_SSM + SSM_GROUPS * SSM_STATE], axis=-1)
    dt = jax.nn.softplus(dt_raw.astype(jnp.float32) + p['dt_bias'].astype(jnp.float32))
    dt = dt.reshape(b, t, SSM_GROUPS, SSM_HPG)
    a = -jnp.exp(p['a_log'].astype(jnp.float32)).reshape(SSM_GROUPS, SSM_HPG)
    xs5 = xs.reshape(b, t, SSM_GROUPS, SSM_HPG, SSM_HEAD_DIM)
    y_s, h_new = ssd(xs5, dt, a, bm.reshape(b, t, SSM_GROUPS, SSM_STATE),
                     cm.reshape(b, t, SSM_GROUPS, SSM_STATE), h0, ssd_blk)
    y_s = y_s + xs5.astype(jnp.float32) * p['d_skip'].astype(jnp.float32).reshape(SSM_GROUPS, SSM_HPG)[..., None]
    y_c = gated_group_rmsnorm(y_s.reshape(b, t, D_SSM), z, p['ssm_norm_w'])

    mix = jnp.concatenate([y_a, y_b.astype(x.dtype), y_c.astype(x.dtype)], axis=-1) @ p['w_out']
    x = x + rmsnorm(mix, p['norm_mix_post'])

    u = rmsnorm(x, p['norm_ffn_pre'])
    g, buf_f_new = causal_dwconv(u @ p['w_gate'], buf_f, p['conv_ffn_w'])
    f = (jax.nn.silu(g + p['conv_ffn_b']) * (u @ p['w_up'])) @ p['w_down']
    x = x + rmsnorm(f, p['norm_ffn_post'])
    return x, (k, v, ki, buf_a_new, buf_s_new, h_new.astype(h0.dtype), buf_f_new)


def setup_inputs(seed: int = 0) -> dict:
    key = jax.random.key(seed)
    ks = jax.random.split(key, 30)
    f32 = jnp.float32

    def nrm(k, shape, scale):
        return jax.random.normal(k, shape, f32) * scale

    def gain(k, shape):
        return 1.0 + 0.05 * jax.random.normal(k, shape, f32)

    dt0 = jnp.exp(jax.random.uniform(ks[20], (DEPTH, SSM_HEADS), f32, math.log(1e-3), math.log(1e-1)))
    return {
        'x_prompt': nrm(ks[0], (BATCH, SEQ, D_MODEL), 1.0),
        'x_sample': nrm(ks[1], (DEC_BATCH, DEC_SEQ, D_MODEL), 1.0),
        'cache_k': nrm(ks[2], (DEPTH, DEC_BATCH, PAST_LEN, N_KV_HEADS, HEAD_DIM), 1.0),
        'cache_v': nrm(ks[3], (DEPTH, DEC_BATCH, PAST_LEN, N_KV_HEADS, HEAD_DIM), 1.0),
        'cache_kidx': nrm(ks[4], (DEPTH, DEC_BATCH, PAST_LEN, D_IDX), 1.0),
        'state_conv_a': nrm(ks[5], (DEPTH, DEC_BATCH, CONV_A_W - 1, D_CONV_MIX), 1.0),
        'state_conv_ssm': nrm(ks[6], (DEPTH, DEC_BATCH, SSM_CONV_W - 1, D_XBC), 1.0),
        'state_ssm': nrm(ks[7], (DEPTH, DEC_BATCH, SSM_GROUPS, SSM_HPG, SSM_HEAD_DIM, SSM_STATE), 0.1),
        'state_conv_ffn': nrm(ks[8], (DEPTH, DEC_BATCH, CONV_F_W - 1, D_FF), 1.0),
        'rel_bias': nrm(ks[9], (N_BUCKETS, N_HEADS), 0.5),
        'norm_mix_pre': gain(ks[10], (DEPTH, D_MODEL)),
        'norm_mix_post': gain(ks[11], (DEPTH, D_MODEL)),
        'norm_ffn_pre': gain(ks[12], (DEPTH, D_MODEL)),
        'norm_ffn_post': gain(ks[13], (DEPTH, D_MODEL)),
        'w_in': nrm(ks[14], (DEPTH, D_MODEL, D_IN), D_MODEL ** -0.5),
        'conv_a_w': nrm(ks[15], (DEPTH, CONV_A_W, D_CONV_MIX), CONV_A_W ** -0.5),
        'conv_ssm_w': nrm(ks[16], (DEPTH, SSM_CONV_W, D_XBC), SSM_CONV_W ** -0.5),
        'conv_ssm_b': nrm(ks[17], (DEPTH, D_XBC), 0.02),
        'dt_bias': dt0 + jnp.log(-jnp.expm1(-dt0)),
        'a_log': jnp.log(jax.random.uniform(ks[18], (DEPTH, SSM_HEADS), f32, 1.0, 16.0)),
        'd_skip': 1.0 + 0.1 * jax.random.normal(ks[19], (DEPTH, SSM_HEADS), f32),
        'ssm_norm_w': gain(ks[21], (DEPTH, D_SSM)),
        'w_out': nrm(ks[22], (DEPTH, D_MIX, D_MODEL), D_MIX ** -0.5),
        'w_gate': nrm(ks[23], (DEPTH, D_MODEL, D_FF), D_MODEL ** -0.5),
        'w_up': nrm(ks[24], (DEPTH, D_MODEL, D_FF), D_MODEL ** -0.5),
        'conv_ffn_w': nrm(ks[25], (DEPTH, CONV_F_W, D_FF), CONV_F_W ** -0.5),
        'conv_ffn_b': nrm(ks[26], (DEPTH, D_FF), 0.02),
        'w_down': nrm(ks[27], (DEPTH, D_FF, D_MODEL), D_FF ** -0.5),
    }


def reference(x_prompt, x_sample, cache_k, cache_v, cache_kidx, state_conv_a, state_conv_ssm,
              state_ssm, state_conv_ffn, rel_bias, norm_mix_pre, norm_mix_post, norm_ffn_pre,
              norm_ffn_post, w_in, conv_a_w, conv_ssm_w, conv_ssm_b, dt_bias, a_log, d_skip,
              ssm_norm_w, w_out, w_gate, w_up, conv_ffn_w, conv_ffn_b, w_down):
    bp = x_prompt.shape[0]
    dtp = x_prompt.dtype
    yp, ys = x_prompt, x_sample
    outs_p, outs_s = [], []
    for l in range(DEPTH):
        p = {'norm_mix_pre': norm_mix_pre[l], 'norm_mix_post': norm_mix_post[l],
             'norm_ffn_pre': norm_ffn_pre[l], 'norm_ffn_post': norm_ffn_post[l],
             'w_in': w_in[l], 'conv_a_w': conv_a_w[l], 'conv_ssm_w': conv_ssm_w[l],
             'conv_ssm_b': conv_ssm_b[l], 'dt_bias': dt_bias[l], 'a_log': a_log[l],
             'd_skip': d_skip[l], 'ssm_norm_w': ssm_norm_w[l], 'w_out': w_out[l],
             'w_gate': w_gate[l], 'w_up': w_up[l], 'conv_ffn_w': conv_ffn_w[l],
             'conv_ffn_b': conv_ffn_b[l], 'w_down': w_down[l]}
        yp, st_p = trunk_layer(
            yp, p, rel_bias,
            jnp.zeros((bp, CONV_A_W - 1, D_CONV_MIX), dtp),
            jnp.zeros((bp, SSM_CONV_W - 1, D_XBC), dtp),
            jnp.zeros((bp, SSM_GROUPS, SSM_HPG, SSM_HEAD_DIM, SSM_STATE), dtp),
            jnp.zeros((bp, CONV_F_W - 1, D_FF), dtp),
            None, SSD_BLOCK)
        ys, st_s = trunk_layer(
            ys, p, rel_bias, state_conv_a[l], state_conv_ssm[l], state_ssm[l], state_conv_ffn[l],
            (cache_k[l], cache_v[l], cache_kidx[l]), ys.shape[1])
        outs_p.append(st_p)
        outs_s.append(st_s)

    def stack(outs, i):
        return jnp.stack([o[i] for o in outs], axis=0)

    return (yp, ys,
            stack(outs_p, 0), stack(outs_s, 0),
            stack(outs_p, 1), stack(outs_s, 1),
            stack(outs_p, 2), stack(outs_s, 2),
            stack(outs_p, 3), stack(outs_s, 3),
            stack(outs_p, 4), stack(outs_s, 4),
            stack(outs_p, 5), stack(outs_s, 5),
            stack(outs_p, 6), stack(outs_s, 6))
```

```python
import functools

import jax
import jax.numpy as jnp
from jax import lax
from jax.experimental import pallas as pl
from jax.experimental.pallas import tpu as pltpu

F32 = jnp.float32
BF16 = jnp.bfloat16

D_MODEL = 1024
CHUNK = 64
D_CONV_MIX = 256
CONV_A_W = 3
N_HEADS = 6
N_KV_HEADS = 2
HEAD_DIM = 64
D_ATT = N_HEADS * HEAD_DIM
GRP = N_HEADS // N_KV_HEADS
N_IDX_HEADS = 4
D_IDX = 64
TOPK = 256
N_BUCKETS = 32
SSM_HEADS = 6
SSM_HEAD_DIM = 64
D_SSM = SSM_HEADS * SSM_HEAD_DIM
SSM_GROUPS = 2
SSM_HPG = SSM_HEADS // SSM_GROUPS
SSM_STATE = 128
SSM_CONV_W = 4
D_XBC = D_SSM + 2 * SSM_GROUPS * SSM_STATE
D_FF = 2816
CONV_F_W = 3
NORM_EPS = 1e-6

LANE = 128
CARRY = 8
SSD_Q = 128
KEY_TILE = 128
BIG = 4
VMEM_LIMIT = 56 * 1024 * 1024
NEG = -1e30

C_AB, C_AC, C_AH = 0, 256, 512
C_Q = 768
C_K = 1152
C_V = 1280
C_QI = 1408
C_KI = 1664
C_Z = 1792
C_XBC = 2176
C_DT = 3072
D_IN_PACKED = 3200
D_IN_HEAD = 1732
D_IN_MID = 3012


def _rms(x, w):
    return x * lax.rsqrt(jnp.mean(x * x, axis=-1, keepdims=True) + NORM_EPS) * w


def _dot(a, b):
    return jnp.dot(a, b, preferred_element_type=F32)


def _silu(x):
    return x * jax.nn.sigmoid(x)


def _resident(shape, index_map):
    return pl.BlockSpec(shape, index_map, pipeline_mode=pl.Buffered(1))


def _in_proj_kernel(x_ref, nw_ref, w_ref, caw_ref, csw_ref, csb_ref, dtb_ref, bufa_ref, bufs_ref,
                    ya_ref, q_ref, k_ref, v_ref, qi_ref, ki_ref, sgn_ref, z_ref, xs_ref, bm_ref,
                    cm_ref, dt_ref, bufa_out, bufs_out, ea, es, *, tm):
    i = pl.program_id(1)
    u = _rms(x_ref[0], nw_ref[...]).astype(BF16)

    def proj(c0, width):
        return _dot(u, w_ref[:, c0:c0 + width])

    @pl.when(i == 0)
    def _():
        ea[CARRY - 2:CARRY, :] = bufa_ref[0]
        es[CARRY - 3:CARRY, :] = bufs_ref[0]

    ea[CARRY:CARRY + tm, :] = proj(C_AC, D_CONV_MIX) * proj(C_AH, D_CONV_MIX)
    conv = ea[CARRY - 2:CARRY - 2 + tm, :] * caw_ref[0:1, :]
    for j in range(1, CONV_A_W):
        conv = conv + ea[CARRY - 2 + j:CARRY - 2 + j + tm, :] * caw_ref[j:j + 1, :]
    ya_ref[0] = (proj(C_AB, D_CONV_MIX) * conv).astype(BF16)
    tail_a = ea[CARRY + tm - 2:CARRY + tm, :]
    bufa_out[0] = tail_a
    ea[CARRY - 2:CARRY, :] = tail_a

    q_ref[0] = (proj(C_Q, D_ATT) * HEAD_DIM ** -0.5).astype(BF16)
    k_ref[0] = proj(C_K, N_KV_HEADS * HEAD_DIM)
    v_ref[0] = proj(C_V, N_KV_HEADS * HEAD_DIM)
    kiw = proj(C_KI, LANE)
    ki_ref[0] = kiw[:, :D_IDX]
    wi = kiw[:, D_IDX:D_IDX + N_IDX_HEADS]
    sgn_ref[0] = jnp.where(wi >= 0.0, 1.0, -1.0).astype(F32)
    wabs = jnp.abs(wi) * (D_IDX ** -0.5 * N_IDX_HEADS ** -0.5)
    qi = proj(C_QI, N_IDX_HEADS * D_IDX)
    for h in range(N_IDX_HEADS):
        qi_ref[0, :, h * D_IDX:(h + 1) * D_IDX] = (
            qi[:, h * D_IDX:(h + 1) * D_IDX] * wabs[:, h:h + 1]).astype(BF16)

    z_ref[0] = proj(C_Z, D_SSM)
    es[CARRY:CARRY + tm, :] = proj(C_XBC, D_XBC)
    xc = es[CARRY - 3:CARRY - 3 + tm, :] * csw_ref[0:1, :]
    for j in range(1, SSM_CONV_W):
        xc = xc + es[CARRY - 3 + j:CARRY - 3 + j + tm, :] * csw_ref[j:j + 1, :]
    xc = _silu(xc + csb_ref[...])
    xs_ref[0] = xc[:, :D_SSM]
    bm_ref[0] = xc[:, D_SSM:D_SSM + SSM_GROUPS * SSM_STATE]
    cm_ref[0] = xc[:, D_SSM + SSM_GROUPS * SSM_STATE:]
    tail_s = es[CARRY + tm - 3:CARRY + tm, :]
    bufs_out[0] = tail_s
    es[CARRY - 3:CARRY, :] = tail_s
    dtr = proj(C_DT, LANE) + dtb_ref[...]
    dt_ref[0] = jnp.maximum(dtr, 0.0) + jnp.log1p(jnp.exp(-jnp.abs(dtr)))


def _in_proj(x, nw, w_packed, caw, csw, csb, dtb, bufa, bufs, *, tm):
    b, t, _ = x.shape
    grid = (b, t // tm)
    row = lambda width: pl.BlockSpec((1, tm, width), lambda bi, i: (bi, i, 0))
    full = lambda a: _resident(a.shape, lambda bi, i: (0,) * a.ndim)
    state = lambda r, width: pl.BlockSpec((1, r, width), lambda bi, i: (bi, 0, 0))
    out_shapes = (
        jax.ShapeDtypeStruct((b, t, D_CONV_MIX), BF16),
        jax.ShapeDtypeStruct((b, t, D_ATT), BF16),
        jax.ShapeDtypeStruct((b, t, 128), F32),
        jax.ShapeDtypeStruct((b, t, 128), F32),
        jax.ShapeDtypeStruct((b, t, 256), BF16),
        jax.ShapeDtypeStruct((b, t, D_IDX), F32),
        jax.ShapeDtypeStruct((b, t, N_IDX_HEADS), F32),
        jax.ShapeDtypeStruct((b, t, D_SSM), F32),
        jax.ShapeDtypeStruct((b, t, D_SSM), F32),
        jax.ShapeDtypeStruct((b, t, 256), F32),
        jax.ShapeDtypeStruct((b, t, 256), F32),
        jax.ShapeDtypeStruct((b, t, LANE), F32),
        jax.ShapeDtypeStruct((b, CONV_A_W - 1, D_CONV_MIX), F32),
        jax.ShapeDtypeStruct((b, SSM_CONV_W - 1, D_XBC), F32),
    )
    out_specs = (row(D_CONV_MIX), row(D_ATT), row(128), row(128), row(256), row(D_IDX),
                 row(N_IDX_HEADS), row(D_SSM), row(D_SSM), row(256), row(256), row(LANE),
                 state(CONV_A_W - 1, D_CONV_MIX), state(SSM_CONV_W - 1, D_XBC))
    in_specs = [row(D_MODEL), full(nw), full(w_packed), full(caw), full(csw), full(csb), full(dtb),
                state(CONV_A_W - 1, D_CONV_MIX), state(SSM_CONV_W - 1, D_XBC)]
    return pl.pallas_call(
        functools.partial(_in_proj_kernel, tm=tm),
        grid=grid, in_specs=in_specs, out_specs=out_specs, out_shape=out_shapes,
        scratch_shapes=[pltpu.VMEM((CARRY + tm, D_CONV_MIX), F32),
                        pltpu.VMEM((CARRY + tm, D_XBC), F32)],
        compiler_params=pltpu.CompilerParams(
            dimension_semantics=("arbitrary", "arbitrary"), vmem_limit_bytes=VMEM_LIMIT),
        name="in_proj",
    )(x, nw, w_packed, caw, csw, csb, dtb, bufa, bufs)


def _for_tiles(lo, hi, fn, carry):
    nbig = jnp.maximum(hi - lo, 0) // BIG
    carry = lax.fori_loop(0, nbig, lambda j, c: fn(lo + j * BIG, BIG, c), carry)
    return lax.fori_loop(lo + nbig * BIG, hi, lambda t, c: fn(t, 1, c), carry)


def _t5_bucket(rel):
    n = jnp.abs(rel)
    large = jnp.full(rel.shape, 8, jnp.int32)
    for brk in (12, 16, 23, 32, 46, 64, 91):
        large = large + jnp.where(n >= brk, 1, 0)
    return jnp.where(rel > 0, N_BUCKETS // 2, 0) + jnp.where(n < 8, n, large)


def _dsa_kernel(relb_ref, q_ref, qi_ref, sgn_ref, kt_ref, v_ref, kit_ref, o_ref, sc, bias,
                *, tq, q_pos0, l_valid):
    i = pl.program_id(1)
    q0 = q_pos0 + i * tq
    tqt = q0 // KEY_TILE
    near0 = pl.multiple_of(tqt * KEY_TILE, KEY_TILE)

    @pl.when(i == 0)
    def _():
        r = lax.broadcasted_iota(jnp.int32, (tq, 2 * KEY_TILE), 0)
        c = lax.broadcasted_iota(jnp.int32, (tq, 2 * KEY_TILE), 1)
        bucket = _t5_bucket(c - KEY_TILE - r)
        for h in range(N_HEADS):
            tab = jnp.zeros((tq, 2 * KEY_TILE), F32)
            for bk in range(N_BUCKETS):
                tab = jnp.where(bucket == bk, relb_ref[bk, h], tab)
            g, j = divmod(h, GRP)
            bias[g, j * tq:(j + 1) * tq, :] = tab - relb_ref[N_BUCKETS // 2 - 1, h]

    qis = jnp.concatenate([qi_ref[0, :, h * D_IDX:(h + 1) * D_IDX] for h in range(N_IDX_HEADS)], axis=0)
    sgn = sgn_ref[0]
    sgns = jnp.concatenate([sgn[:, h:h + 1] for h in range(N_IDX_HEADS)], axis=0)

    def scores(t0, n):
        cols = pl.ds(pl.multiple_of(t0 * KEY_TILE, KEY_TILE), n * KEY_TILE)
        s = jnp.maximum(_dot(qis, kit_ref[0, :, cols]), 0.0) * sgns
        tot = jnp.zeros((tq, n * KEY_TILE), F32)
        for h in range(N_IDX_HEADS):
            tot = tot + s[h * tq:(h + 1) * tq, :]
        return tot

    def score_far(t0, n, c):
        s = scores(t0, n)
        for k in range(n):
            sc[t0 + k] = s[:, k * KEY_TILE:(k + 1) * KEY_TILE]
        return c

    _for_tiles(1, tqt, score_far, 0)
    s_near = scores(tqt, 2)
    kpos = near0 - KEY_TILE + lax.broadcasted_iota(jnp.int32, (tq, 2 * KEY_TILE), 1)
    qpos = q0 + lax.broadcasted_iota(jnp.int32, (tq, 2 * KEY_TILE), 0)
    adm = (kpos >= 0) & (kpos < l_valid) & ((kpos >> 6) <= (qpos >> 6))
    s_near = jnp.where(adm, s_near, -jnp.inf)
    sc[tqt] = s_near[:, :KEY_TILE]
    sc[tqt + 1] = s_near[:, KEY_TILE:]

    def count_ge(thr_b):
        def body(t0, n, acc):
            for k in range(n):
                acc = acc + jnp.where(sc[t0 + k] >= thr_b, 1.0, 0.0)
            return acc
        acc = _for_tiles(1, tqt + 2, body, jnp.zeros((tq, KEY_TILE), F32))
        return jnp.sum(acc, axis=1, keepdims=True)

    def key_to_f32(key):
        return lax.bitcast_convert_type(key ^ ((key >> 31) & jnp.int32(0x7FFFFFFF)), F32)

    def descend(it, t):
        trial = t ^ lax.shift_left(jnp.int32(1), 31 - it)
        cnt = count_ge(jnp.broadcast_to(key_to_f32(trial), (tq, KEY_TILE)))
        return jnp.where(cnt >= float(TOPK), trial, t)

    int_min = jnp.int32(-2 ** 31)
    tkey = lax.fori_loop(0, 32, descend, jnp.full((tq, 1), int_min, jnp.int32))
    tkey = jnp.maximum(tkey, int_min + jnp.int32(0x00800000))
    thr = key_to_f32(tkey)
    thr_b = jnp.broadcast_to(thr, (tq, KEY_TILE))

    def count2(t0, n, c):
        gt, ge = c
        for k in range(n):
            s = sc[t0 + k]
            gt = gt + jnp.where(s > thr_b, 1.0, 0.0)
            ge = ge + jnp.where(s >= thr_b, 1.0, 0.0)
        return gt, ge

    zero = jnp.zeros((tq, KEY_TILE), F32)
    gt, ge = _for_tiles(1, tqt + 2, count2, (zero, zero))
    c_gt = jnp.sum(gt, axis=1, keepdims=True)
    c_ge = jnp.sum(ge, axis=1, keepdims=True)
    quota = float(TOPK) - c_gt

    @pl.when(jnp.max(c_ge) > float(TOPK))
    def _():
        ri = lax.broadcasted_iota(jnp.int32, (KEY_TILE, KEY_TILE), 0)
        ci = lax.broadcasted_iota(jnp.int32, (KEY_TILE, KEY_TILE), 1)
        upper = jnp.where(ri <= ci, 1.0, 0.0).astype(BF16)

        def drop(t, seen):
            s = sc[t]
            tie = jnp.where(s == thr_b, 1.0, 0.0)
            rank = seen + _dot(tie.astype(BF16), upper)
            sc[t] = jnp.where((tie > 0.0) & (rank > quota), -jnp.inf, s)
            return seen + jnp.sum(tie, axis=1, keepdims=True)

        lax.fori_loop(1, tqt + 2, drop, jnp.zeros((tq, 1), F32))

    qg = []
    for g in range(N_KV_HEADS):
        qg.append(jnp.concatenate(
            [q_ref[0, :, (g * GRP + j) * HEAD_DIM:(g * GRP + j + 1) * HEAD_DIM] for j in range(GRP)],
            axis=0))

    def attend(t0, n, carry, extra=None):
        cols = pl.ds(pl.multiple_of(t0 * KEY_TILE, KEY_TILE), n * KEY_TILE)
        pen = jnp.concatenate(
            [jnp.where(sc[t0 + k] >= thr_b, 0.0, NEG) for k in range(n)], axis=1)
        pen = jnp.concatenate([pen] * GRP, axis=0)
        vt = v_ref[0, cols, :]
        out = []
        for g in range(N_KV_HEADS):
            m, l, acc = carry[g]
            s = _dot(qg[g], kt_ref[0, g * HEAD_DIM:(g + 1) * HEAD_DIM, cols]) + pen
            if extra is not None:
                s = s + extra[g]
            m_new = jnp.maximum(m, jnp.max(s, axis=1, keepdims=True))
            alpha = jnp.exp(m - m_new)
            p = jnp.exp(s - m_new)
            l = alpha * l + jnp.sum(p, axis=1, keepdims=True)
            acc = alpha * acc + _dot(p.astype(BF16), vt)
            out.append((m_new, l, acc))
        return tuple(out)

    init = tuple((jnp.full((GRP * tq, 1), NEG, F32), jnp.zeros((GRP * tq, 1), F32),
                  jnp.zeros((GRP * tq, N_KV_HEADS * HEAD_DIM), F32)) for _ in range(N_KV_HEADS))
    carry = _for_tiles(1, tqt, attend, init)
    carry = attend(tqt, 2, carry, extra=(bias[0], bias[1]))
    for g in range(N_KV_HEADS):
        _, l, acc = carry[g]
        og = acc[:, g * HEAD_DIM:(g + 1) * HEAD_DIM] / l
        for j in range(GRP):
            h = g * GRP + j
            o_ref[0, :, h * HEAD_DIM:(h + 1) * HEAD_DIM] = og[j * tq:(j + 1) * tq, :].astype(BF16)


def _dsa(rel_bias, q, qi, sgn, kt, v, kit, *, tq, q_pos0, l_valid):
    b, t, _ = q.shape
    lk = kt.shape[2]
    assert lk % KEY_TILE == 0 and q_pos0 % KEY_TILE == 0
    assert tq == KEY_TILE or t == tq
    nt = lk // KEY_TILE
    row = lambda width: pl.BlockSpec((1, tq, width), lambda bi, i: (bi, i, 0))
    return pl.pallas_call(
        functools.partial(_dsa_kernel, tq=tq, q_pos0=q_pos0, l_valid=l_valid),
        grid=(b, t // tq),
        in_specs=[pl.BlockSpec(memory_space=pltpu.SMEM),
                  row(D_ATT), row(N_IDX_HEADS * D_IDX), row(N_IDX_HEADS),
                  _resident((1, N_KV_HEADS * HEAD_DIM, lk), lambda bi, i: (bi, 0, 0)),
                  _resident((1, lk, N_KV_HEADS * HEAD_DIM), lambda bi, i: (bi, 0, 0)),
                  _resident((1, D_IDX, lk), lambda bi, i: (bi, 0, 0))],
        out_specs=row(D_ATT),
        out_shape=jax.ShapeDtypeStruct((b, t, D_ATT), BF16),
        scratch_shapes=[pltpu.VMEM((nt, tq, KEY_TILE), F32),
                        pltpu.VMEM((N_KV_HEADS, GRP * tq, 2 * KEY_TILE), F32)],
        compiler_params=pltpu.CompilerParams(
            dimension_semantics=("arbitrary", "arbitrary"), vmem_limit_bytes=VMEM_LIMIT),
        name="dsa",
    )(rel_bias, q, qi, sgn, kt, v, kit)


def _split3(x):
    hi = x.astype(BF16)
    r = x - hi.astype(F32)
    mid = r.astype(BF16)
    lo = (r - mid.astype(F32)).astype(BF16)
    return hi, mid, lo


def _ssd_kernel(xs_ref, bm_ref, cm_ref, dt_ref, z_ref, alog_ref, dsk_ref, nw_ref, h0_ref,
                y_ref, h_ref, ht, *, q_in):
    c = pl.program_id(1)
    nc = pl.num_programs(1)

    @pl.when(c == 0)
    def _():
        for hh in range(SSM_HEADS):
            ht[hh] = h0_ref[0, hh].T

    def rows(ref):
        x = ref[0]
        if q_in == SSD_Q:
            return x
        return jnp.concatenate([x, jnp.zeros((SSD_Q - q_in, x.shape[1]), x.dtype)], axis=0)

    xs, bm, cm, dt, z = rows(xs_ref), rows(bm_ref), rows(cm_ref), rows(dt_ref), rows(z_ref)
    a = -jnp.exp(alog_ref[...])
    da = dt * a
    ri = lax.broadcasted_iota(jnp.int32, (SSD_Q, SSD_Q), 0)
    ci = lax.broadcasted_iota(jnp.int32, (SSD_Q, SSD_Q), 1)
    causal = ri >= ci
    tril = jnp.where(causal, 1.0, 0.0).astype(BF16)
    hi, mid, lo = _split3(da)
    acum = _dot(tril, hi) + _dot(tril, mid) + _dot(tril, lo)
    acum_t = acum.T

    for g in range(SSM_GROUPS):
        bm_g = bm[:, g * SSM_STATE:(g + 1) * SSM_STATE]
        cm_g = cm[:, g * SSM_STATE:(g + 1) * SSM_STATE].astype(BF16)
        bm_t = bm_g.T.astype(BF16)
        cb = _dot(cm_g, bm_t)
        gs = []
        ssq = jnp.zeros((SSD_Q, 1), F32)
        for r in range(SSM_HPG):
            hh = g * SSM_HPG + r
            lanes = slice(hh * SSM_HEAD_DIM, (hh + 1) * SSM_HEAD_DIM)
            xh = xs[:, lanes]
            col = acum[:, hh:hh + 1]
            rowv = acum_t[hh:hh + 1, :]
            decay = jnp.exp(jnp.where(causal, col - rowv, NEG))
            xdt = xh * dt[:, hh:hh + 1]
            y = _dot((cb * decay).astype(BF16), xdt.astype(BF16))
            h_prev = ht[hh]
            y = y + jnp.exp(col) * _dot(cm_g, h_prev.astype(BF16))
            last = acum[SSD_Q - 1:SSD_Q, hh:hh + 1]
            xw = (xdt * jnp.exp(last - col)).astype(BF16)
            ht[hh] = h_prev * jnp.exp(last) + _dot(bm_t, xw)
            y = y + xh * dsk_ref[:, lanes]
            gate = y * _silu(z[:, lanes])
            ssq = ssq + jnp.sum(gate * gate, axis=1, keepdims=True)
            gs.append(gate)
        scale = lax.rsqrt(ssq / float(SSM_HPG * SSM_HEAD_DIM) + NORM_EPS)
        for r in range(SSM_HPG):
            hh = g * SSM_HPG + r
            lanes = slice(hh * SSM_HEAD_DIM, (hh + 1) * SSM_HEAD_DIM)
            y_ref[0, :, lanes] = (gs[r] * scale * nw_ref[:, lanes])[:q_in].astype(BF16)

    @pl.when(c == nc - 1)
    def _():
        for hh in range(SSM_HEADS):
            h_ref[0, hh] = ht[hh].T


def _ssd(xs, bm, cm, dt, z, alog, dsk, nw, h0, *, q_in):
    b, t, _ = xs.shape
    row = lambda width: pl.BlockSpec((1, q_in, width), lambda bi, c: (bi, c, 0))
    full = lambda a: _resident(a.shape, lambda bi, c: (0,) * a.ndim)
    hspec = pl.BlockSpec((1, SSM_HEADS, SSM_HEAD_DIM, SSM_STATE), lambda bi, c: (bi, 0, 0, 0))
    return pl.pallas_call(
        functools.partial(_ssd_kernel, q_in=q_in),
        grid=(b, t // q_in),
        in_specs=[row(D_SSM), row(256), row(256), row(LANE), row(D_SSM),
                  full(alog), full(dsk), full(nw), hspec],
        out_specs=(row(D_SSM), hspec),
        out_shape=(jax.ShapeDtypeStruct((b, t, D_SSM), BF16),
                   jax.ShapeDtypeStruct((b, SSM_HEADS, SSM_HEAD_DIM, SSM_STATE), F32)),
        scratch_shapes=[pltpu.VMEM((SSM_HEADS, SSM_STATE, SSM_HEAD_DIM), F32)],
        compiler_params=pltpu.CompilerParams(
            dimension_semantics=("arbitrary", "arbitrary"), vmem_limit_bytes=VMEM_LIMIT),
        name="ssd",
    )(xs, bm, cm, dt, z, alog, dsk, nw, h0)


def _mix_ffn_kernel(x_ref, ya_ref, yb_ref, yc_ref, wo_ref, npost_ref, nfpre_ref, wg_ref, wu_ref,
                    wd_ref, cfw_ref, cfb_ref, nfpost_ref, buff_ref, o_ref, buff_out, eg, *, tm):
    i = pl.program_id(1)

    @pl.when(i == 0)
    def _():
        eg[CARRY - 2:CARRY, :] = buff_ref[0]

    mix = (_dot(ya_ref[0], wo_ref[0:D_CONV_MIX, :])
           + _dot(yb_ref[0], wo_ref[D_CONV_MIX:D_CONV_MIX + D_ATT, :])
           + _dot(yc_ref[0], wo_ref[D_CONV_MIX + D_ATT:, :]))
    x1 = x_ref[0] + _rms(mix, npost_ref[...])
    u = _rms(x1, nfpre_ref[...]).astype(BF16)
    eg[CARRY:CARRY + tm, :] = _dot(u, wg_ref[...])
    gc = eg[CARRY - 2:CARRY - 2 + tm, :] * cfw_ref[0:1, :]
    for j in range(1, CONV_F_W):
        gc = gc + eg[CARRY - 2 + j:CARRY - 2 + j + tm, :] * cfw_ref[j:j + 1, :]
    hid = (_silu(gc + cfb_ref[...]) * _dot(u, wu_ref[...])).astype(BF16)
    o_ref[0] = x1 + _rms(_dot(hid, wd_ref[...]), nfpost_ref[...])
    tail = eg[CARRY + tm - 2:CARRY + tm, :]
    buff_out[0] = tail
    eg[CARRY - 2:CARRY, :] = tail


def _mix_ffn(x, ya, yb, yc, wo, npost, nfpre, wg, wu, wd, cfw, cfb, nfpost, buff, *, tm):
    b, t, _ = x.shape
    row = lambda width: pl.BlockSpec((1, tm, width), lambda bi, i: (bi, i, 0))
    full = lambda a: _resident(a.shape, lambda bi, i: (0,) * a.ndim)
    state = pl.BlockSpec((1, CONV_F_W - 1, D_FF), lambda bi, i: (bi, 0, 0))
    return pl.pallas_call(
        functools.partial(_mix_ffn_kernel, tm=tm),
        grid=(b, t // tm),
        in_specs=[row(D_MODEL), row(D_CONV_MIX), row(D_ATT), row(D_SSM), full(wo), full(npost),
                  full(nfpre), full(wg), full(wu), full(wd), full(cfw), full(cfb), full(nfpost),
                  state],
        out_specs=(row(D_MODEL), state),
        out_shape=(jax.ShapeDtypeStruct((b, t, D_MODEL), F32),
                   jax.ShapeDtypeStruct((b, CONV_F_W - 1, D_FF), F32)),
        scratch_shapes=[pltpu.VMEM((CARRY + tm, D_FF), F32)],
        compiler_params=pltpu.CompilerParams(
            dimension_semantics=("arbitrary", "arbitrary"), vmem_limit_bytes=VMEM_LIMIT),
        name="mix_ffn",
    )(x, ya, yb, yc, wo, npost, nfpre, wg, wu, wd, cfw, cfb, nfpost, buff)


def _pack_keys(k_all, ki_all, v_all):
    l = k_all.shape[1]
    lk = -(-(l + KEY_TILE) // KEY_TILE) * KEY_TILE
    pad = lambda a: jnp.pad(a.astype(BF16), ((0, 0), (KEY_TILE, lk - KEY_TILE - l), (0, 0)))
    return pad(k_all).swapaxes(1, 2), pad(v_all), pad(ki_all).swapaxes(1, 2)


def _layer(x, p, rel_bias, buf_a, buf_s, h0, buf_f, kv_past, *, tm, tq):
    b, t, _ = x.shape
    (ya, q, k, v, qi, ki, sgn, z, xs, bm, cm, dt, buf_a_new, buf_s_new) = _in_proj(
        x, p['norm_mix_pre'], p['w_in'], p['conv_a_w'], p['conv_ssm_w'], p['conv_ssm_b'],
        p['dt_bias'], buf_a, buf_s, tm=tm)
    if kv_past is None:
        k_all, v_all, ki_all, past = k, v, ki, 0
    else:
        ck, cv, cki = kv_past
        past = ck.shape[1]
        k_all = jnp.concatenate([ck.reshape(b, past, -1), k], axis=1)
        v_all = jnp.concatenate([cv.reshape(b, past, -1), v], axis=1)
        ki_all = jnp.concatenate([cki, ki], axis=1)
    kt, vp, kit = _pack_keys(k_all, ki_all, v_all)
    yb = _dsa(rel_bias, q, qi, sgn, kt, vp, kit, tq=tq, q_pos0=past, l_valid=past + t)
    yc, h_new = _ssd(xs, bm, cm, dt, z, p['a_log'], p['d_skip'], p['ssm_norm_w'],
                     h0.reshape(b, SSM_HEADS, SSM_HEAD_DIM, SSM_STATE), q_in=min(t, SSD_Q))
    x_new, buf_f_new = _mix_ffn(x, ya, yb, yc, p['w_out'], p['norm_mix_post'], p['norm_ffn_pre'],
                                p['w_gate'], p['w_up'], p['w_down'], p['conv_ffn_w'],
                                p['conv_ffn_b'], p['norm_ffn_post'], buf_f, tm=tm)
    st = (k.reshape(b, t, N_KV_HEADS, HEAD_DIM), v.reshape(b, t, N_KV_HEADS, HEAD_DIM), ki,
          buf_a_new, buf_s_new, h_new.reshape(b, SSM_GROUPS, SSM_HPG, SSM_HEAD_DIM, SSM_STATE),
          buf_f_new)
    return x_new, st


def _pack_w_in(w):
    d = w.shape[0]
    return jnp.concatenate(
        [w[:, :D_IN_HEAD], jnp.zeros((d, C_Z - D_IN_HEAD), w.dtype), w[:, D_IN_HEAD:D_IN_MID],
         w[:, D_IN_MID:], jnp.zeros((d, D_IN_PACKED - C_DT - SSM_HEADS), w.dtype)],
        axis=1).astype(BF16)


def _lane_pad(v):
    return jnp.pad(v, (0, LANE - v.shape[0]))[None, :]


def kernel(x_prompt, x_sample, cache_k, cache_v, cache_kidx, state_conv_a, state_conv_ssm, state_ssm, state_conv_ffn, rel_bias, norm_mix_pre, norm_mix_post, norm_ffn_pre, norm_ffn_post, w_in, conv_a_w, conv_ssm_w, conv_ssm_b, dt_bias, a_log, d_skip, ssm_norm_w, w_out, w_gate, w_up, conv_ffn_w, conv_ffn_b, w_down):
    depth = w_in.shape[0]
    bp = x_prompt.shape[0]
    yp, ys = x_prompt, x_sample
    outs_p, outs_s = [], []
    for l in range(depth):
        p = {'norm_mix_pre': norm_mix_pre[l][None], 'norm_mix_post': norm_mix_post[l][None],
             'norm_ffn_pre': norm_ffn_pre[l][None], 'norm_ffn_post': norm_ffn_post[l][None],
             'w_in': _pack_w_in(w_in[l]), 'conv_a_w': conv_a_w[l], 'conv_ssm_w': conv_ssm_w[l],
             'conv_ssm_b': conv_ssm_b[l][None], 'dt_bias': _lane_pad(dt_bias[l]),
             'a_log': _lane_pad(a_log[l]), 'd_skip': jnp.repeat(d_skip[l], SSM_HEAD_DIM)[None],
             'ssm_norm_w': ssm_norm_w[l][None], 'w_out': w_out[l].astype(BF16),
             'w_gate': w_gate[l].astype(BF16), 'w_up': w_up[l].astype(BF16),
             'conv_ffn_w': conv_ffn_w[l], 'conv_ffn_b': conv_ffn_b[l][None],
             'w_down': w_down[l].astype(BF16)}
        yp, st_p = _layer(
            yp, p, rel_bias,
            jnp.zeros((bp, CONV_A_W - 1, D_CONV_MIX), F32),
            jnp.zeros((bp, SSM_CONV_W - 1, D_XBC), F32),
            jnp.zeros((bp, SSM_GROUPS, SSM_HPG, SSM_HEAD_DIM, SSM_STATE), F32),
            jnp.zeros((bp, CONV_F_W - 1, D_FF), F32),
            None, tm=256, tq=KEY_TILE)
        ys, st_s = _layer(
            ys, p, rel_bias, state_conv_a[l], state_conv_ssm[l], state_ssm[l], state_conv_ffn[l],
            (cache_k[l], cache_v[l], cache_kidx[l]), tm=ys.shape[1], tq=ys.shape[1])
        outs_p.append(st_p)
        outs_s.append(st_s)

    def stack(outs, i):
        return jnp.stack([o[i] for o in outs], axis=0)

    res = [yp, ys]
    for i in range(7):
        res.append(stack(outs_p, i))
        res.append(stack(outs_s, i))
    return tuple(res)
```

```python
import functools

import jax
import jax.numpy as jnp
from jax import lax
from jax.experimental import pallas as pl
from jax.experimental.pallas import tpu as pltpu

F32 = jnp.float32
BF16 = jnp.bfloat16

D_MODEL = 1024
CHUNK = 64
D_CONV_MIX = 256
CONV_A_W = 3
N_HEADS = 6
N_KV_HEADS = 2
HEAD_DIM = 64
D_ATT = N_HEADS * HEAD_DIM
GRP = N_HEADS // N_KV_HEADS
N_IDX_HEADS = 4
D_IDX = 64
TOPK = 256
N_BUCKETS = 32
SSM_HEADS = 6
SSM_HEAD_DIM = 64
D_SSM = SSM_HEADS * SSM_HEAD_DIM
SSM_GROUPS = 2
SSM_HPG = SSM_HEADS // SSM_GROUPS
SSM_STATE = 128
SSM_CONV_W = 4
D_XBC = D_SSM + 2 * SSM_GROUPS * SSM_STATE
D_FF = 2816
CONV_F_W = 3
NORM_EPS = 1e-6

LANE = 128
CARRY = 8
SSD_Q = 128
KEY_TILE = 128
BIG = 4
VMEM_LIMIT = 56 * 1024 * 1024
NEG = -1e30
LOG2E = 1.4426950408889634
F32_MAX = 3.4028234663852886e38
F32_TINY = 1.1754943508222875e-38
NO_QUOTA = 1e9
MAX_SEARCH_STEPS = 256

C_AB, C_AC, C_AH = 0, 256, 512
C_Q = 768
C_K = 1152
C_V = 1280
C_QI = 1408
C_KI = 1664
C_Z = 1792
C_XBC = 2176
C_DT = 3072
D_IN_PACKED = 3200
D_IN_HEAD = 1732
D_IN_MID = 3012


def _rms(x, w):
    return x * lax.rsqrt(jnp.mean(x * x, axis=-1, keepdims=True) + NORM_EPS) * w


def _dot(a, b):
    return jnp.dot(a, b, preferred_element_type=F32)


def _silu(x):
    return x * jax.nn.sigmoid(x)


def _resident(shape, index_map):
    return pl.BlockSpec(shape, index_map, pipeline_mode=pl.Buffered(1))


def _in_proj_kernel(x_ref, nw_ref, w_ref, caw_ref, csw_ref, csb_ref, dtb_ref, bufa_ref, bufs_ref,
                    ya_ref, q_ref, k_ref, v_ref, qi_ref, ki_ref, sgn_ref, z_ref, xs_ref, bm_ref,
                    cm_ref, dt_ref, bufa_out, bufs_out, ea, es, *, tm):
    i = pl.program_id(1)
    u = _rms(x_ref[0], nw_ref[...]).astype(BF16)

    def proj(c0, width):
        return _dot(u, w_ref[:, c0:c0 + width])

    @pl.when(i == 0)
    def _():
        ea[CARRY - 2:CARRY, :] = bufa_ref[0]
        es[CARRY - 3:CARRY, :] = bufs_ref[0]

    ea[CARRY:CARRY + tm, :] = proj(C_AC, D_CONV_MIX) * proj(C_AH, D_CONV_MIX)
    conv = ea[CARRY - 2:CARRY - 2 + tm, :] * caw_ref[0:1, :]
    for j in range(1, CONV_A_W):
        conv = conv + ea[CARRY - 2 + j:CARRY - 2 + j + tm, :] * caw_ref[j:j + 1, :]
    ya_ref[0] = (proj(C_AB, D_CONV_MIX) * conv).astype(BF16)
    tail_a = ea[CARRY + tm - 2:CARRY + tm, :]
    bufa_out[0] = tail_a
    ea[CARRY - 2:CARRY, :] = tail_a

    q_ref[0] = (proj(C_Q, D_ATT) * (HEAD_DIM ** -0.5 * LOG2E)).astype(BF16)
    k_ref[0] = proj(C_K, N_KV_HEADS * HEAD_DIM)
    v_ref[0] = proj(C_V, N_KV_HEADS * HEAD_DIM)
    kiw = proj(C_KI, LANE)
    ki_ref[0] = kiw[:, :D_IDX]
    wi = kiw[:, D_IDX:D_IDX + N_IDX_HEADS]
    sgn_ref[0] = jnp.where(wi >= 0.0, 1.0, -1.0).astype(F32)
    wabs = jnp.abs(wi) * (D_IDX ** -0.5 * N_IDX_HEADS ** -0.5)
    qi = proj(C_QI, N_IDX_HEADS * D_IDX)
    for h in range(N_IDX_HEADS):
        qi_ref[0, :, h * D_IDX:(h + 1) * D_IDX] = (
            qi[:, h * D_IDX:(h + 1) * D_IDX] * wabs[:, h:h + 1]).astype(BF16)

    z_ref[0] = proj(C_Z, D_SSM)
    es[CARRY:CARRY + tm, :] = proj(C_XBC, D_XBC)
    xc = es[CARRY - 3:CARRY - 3 + tm, :] * csw_ref[0:1, :]
    for j in range(1, SSM_CONV_W):
        xc = xc + es[CARRY - 3 + j:CARRY - 3 + j + tm, :] * csw_ref[j:j + 1, :]
    xc = _silu(xc + csb_ref[...])
    xs_ref[0] = xc[:, :D_SSM]
    bm_ref[0] = xc[:, D_SSM:D_SSM + SSM_GROUPS * SSM_STATE]
    cm_ref[0] = xc[:, D_SSM + SSM_GROUPS * SSM_STATE:]
    tail_s = es[CARRY + tm - 3:CARRY + tm, :]
    bufs_out[0] = tail_s
    es[CARRY - 3:CARRY, :] = tail_s
    dtr = proj(C_DT, LANE) + dtb_ref[...]
    dt_ref[0] = jnp.maximum(dtr, 0.0) + jnp.log1p(jnp.exp(-jnp.abs(dtr)))


def _in_proj(x, nw, w_packed, caw, csw, csb, dtb, bufa, bufs, *, tm):
    b, t, _ = x.shape
    grid = (b, t // tm)
    row = lambda width: pl.BlockSpec((1, tm, width), lambda bi, i: (bi, i, 0))
    full = lambda a: _resident(a.shape, lambda bi, i: (0,) * a.ndim)
    state = lambda r, width: pl.BlockSpec((1, r, width), lambda bi, i: (bi, 0, 0))
    out_shapes = (
        jax.ShapeDtypeStruct((b, t, D_CONV_MIX), BF16),
        jax.ShapeDtypeStruct((b, t, D_ATT), BF16),
        jax.ShapeDtypeStruct((b, t, 128), F32),
        jax.ShapeDtypeStruct((b, t, 128), F32),
        jax.ShapeDtypeStruct((b, t, 256), BF16),
        jax.ShapeDtypeStruct((b, t, D_IDX), F32),
        jax.ShapeDtypeStruct((b, t, N_IDX_HEADS), F32),
        jax.ShapeDtypeStruct((b, t, D_SSM), F32),
        jax.ShapeDtypeStruct((b, t, D_SSM), F32),
        jax.ShapeDtypeStruct((b, t, 256), F32),
        jax.ShapeDtypeStruct((b, t, 256), F32),
        jax.ShapeDtypeStruct((b, t, LANE), F32),
        jax.ShapeDtypeStruct((b, CONV_A_W - 1, D_CONV_MIX), F32),
        jax.ShapeDtypeStruct((b, SSM_CONV_W - 1, D_XBC), F32),
    )
    out_specs = (row(D_CONV_MIX), row(D_ATT), row(128), row(128), row(256), row(D_IDX),
                 row(N_IDX_HEADS), row(D_SSM), row(D_SSM), row(256), row(256), row(LANE),
                 state(CONV_A_W - 1, D_CONV_MIX), state(SSM_CONV_W - 1, D_XBC))
    in_specs = [row(D_MODEL), full(nw), full(w_packed), full(caw), full(csw), full(csb), full(dtb),
                state(CONV_A_W - 1, D_CONV_MIX), state(SSM_CONV_W - 1, D_XBC)]
    return pl.pallas_call(
        functools.partial(_in_proj_kernel, tm=tm),
        grid=grid, in_specs=in_specs, out_specs=out_specs, out_shape=out_shapes,
        scratch_shapes=[pltpu.VMEM((CARRY + tm, D_CONV_MIX), F32),
                        pltpu.VMEM((CARRY + tm, D_XBC), F32)],
        compiler_params=pltpu.CompilerParams(
            dimension_semantics=("arbitrary", "arbitrary"), vmem_limit_bytes=VMEM_LIMIT),
        name="in_proj",
    )(x, nw, w_packed, caw, csw, csb, dtb, bufa, bufs)


def _t5_bucket(rel):
    n = jnp.abs(rel)
    large = jnp.full(rel.shape, 8, jnp.int32)
    for brk in (12, 16, 23, 32, 46, 64, 91):
        large = large + jnp.where(n >= brk, 1, 0)
    return jnp.where(rel > 0, N_BUCKETS // 2, 0) + jnp.where(n < 8, n, large)


def _f32_key(x):
    bits = lax.bitcast_convert_type(x, jnp.int32)
    return bits ^ ((bits >> 31) & jnp.int32(0x7FFFFFFF))


def _key_f32(key):
    return lax.bitcast_convert_type(key ^ ((key >> 31) & jnp.int32(0x7FFFFFFF)), F32)


def _dsa_kernel(relb_ref, q_ref, qi_ref, sgn_ref, kt_ref, v_ref, kit_ref, o_ref, sc, bias,
                *, tq, q_pos0, l_valid):
    i = pl.program_id(1)
    q0 = q_pos0 + i * tq
    qt = q0 // KEY_TILE
    n0 = jnp.maximum(qt - 1, 0)
    variant = jnp.minimum(qt, 1)
    steps_far = (n0 + BIG - 1) // BIG
    steps_all = (n0 + 2 + BIG - 1) // BIG
    kw = BIG * KEY_TILE
    kf = float(TOPK)

    def step_cols(j):
        return pl.ds(pl.multiple_of(j * kw, kw), kw)

    near_cols = pl.ds(pl.multiple_of(n0 * KEY_TILE, KEY_TILE), 2 * KEY_TILE)

    @pl.when(i == 0)
    def _():
        sc[...] = jnp.full(sc.shape, -jnp.inf, F32)
        r = lax.broadcasted_iota(jnp.int32, (tq, 2 * KEY_TILE), 0)
        c = lax.broadcasted_iota(jnp.int32, (tq, 2 * KEY_TILE), 1)
        for var in range(2):
            bucket = _t5_bucket(c - var * KEY_TILE - r)
            for h in range(N_HEADS):
                tab = jnp.zeros((tq, 2 * KEY_TILE), F32)
                for bk in range(N_BUCKETS):
                    tab = jnp.where(bucket == bk, relb_ref[bk, h], tab)
                g, j = divmod(h, GRP)
                bias[var, g, j * tq:(j + 1) * tq, :] = (tab - relb_ref[N_BUCKETS // 2 - 1, h]) * LOG2E

    qis = jnp.concatenate([qi_ref[0, :, h * D_IDX:(h + 1) * D_IDX] for h in range(N_IDX_HEADS)], axis=0)
    sgn = sgn_ref[0]
    sgns = jnp.concatenate([sgn[:, h:h + 1] for h in range(N_IDX_HEADS)], axis=0)

    def scores(cols):
        s = jnp.maximum(_dot(qis, kit_ref[0, :, cols]), 0.0) * sgns
        tot = s[0:tq, :]
        for h in range(1, N_IDX_HEADS):
            tot = tot + s[h * tq:(h + 1) * tq, :]
        return tot

    def score_far(j, c):
        s = scores(step_cols(j))
        for k in range(BIG):
            t = j * BIG + k
            sc[t] = jnp.where(t < n0, s[:, k * KEY_TILE:(k + 1) * KEY_TILE], -jnp.inf)
        return c

    lax.fori_loop(0, steps_far, score_far, 0)
    s_near = scores(near_cols)
    kpos = n0 * KEY_TILE + lax.broadcasted_iota(jnp.int32, (tq, 2 * KEY_TILE), 1)
    qpos = q0 + lax.broadcasted_iota(jnp.int32, (tq, 2 * KEY_TILE), 0)
    adm = (kpos < l_valid) & ((kpos >> 6) <= (qpos >> 6))
    s_near = jnp.where(adm, s_near, -jnp.inf)
    sc[n0] = s_near[:, :KEY_TILE]
    sc[n0 + 1] = s_near[:, KEY_TILE:]

    zero = jnp.zeros((tq, KEY_TILE), F32)

    def count_ge(t):
        tb = jnp.broadcast_to(t, (tq, KEY_TILE))

        def body(j, acc):
            for k in range(BIG):
                acc = acc + jnp.where(sc[j * BIG + k] >= tb, 1.0, 0.0)
            return acc
        return jnp.sum(lax.fori_loop(0, steps_all, body, zero), axis=1, keepdims=True)

    def first_pass(j, c):
        ge, gt, mx = c
        for k in range(BIG):
            s = sc[j * BIG + k]
            ge = ge + jnp.where(s >= 0.0, 1.0, 0.0)
            gt = gt + jnp.where(s >= F32_TINY, 1.0, 0.0)
            mx = jnp.maximum(mx, s)
        return ge, gt, mx

    ge, gt, mx = lax.fori_loop(0, steps_all, first_pass, (zero, zero, jnp.full((tq, KEY_TILE), -F32_MAX, F32)))
    ge0 = jnp.sum(ge, axis=1, keepdims=True)
    gt0 = jnp.sum(gt, axis=1, keepdims=True)
    rmax = jnp.max(mx, axis=1, keepdims=True)
    qcol = q0 + lax.broadcasted_iota(jnp.int32, (tq, 1), 0)
    nadm = jnp.minimum(((qcol >> 6) + 1) * CHUNK, l_valid).astype(F32)
    small = nadm <= kf
    tie0 = (~small) & (gt0 < kf) & (ge0 >= kf)
    pos = (~small) & (gt0 >= kf)
    neg = (~small) & (ge0 < kf)
    lo0 = jnp.where(pos, F32_TINY, -F32_MAX)
    flo0 = jnp.where(pos, gt0, nadm)
    hi0 = jnp.where(neg, 0.0, jnp.inf)
    fhi0 = jnp.where(neg, ge0, 0.0)
    done0 = jnp.where(small | tie0 | (pos & (gt0 == kf)), 1.0, 0.0)
    thr0 = jnp.where(small, -F32_MAX, jnp.where(tie0, 0.0, F32_TINY))
    quota0 = jnp.where(tie0, kf - gt0, NO_QUOTA)

    def search_cond(c):
        return (c[0] < MAX_SEARCH_STEPS) & (c[1] > 0)

    def search_body(c):
        it, _, lo, flo, hi, fhi, done, thr, quota = c
        klo, khi = _f32_key(lo), _f32_key(hi)
        la = jnp.log(flo)
        frac = (la - jnp.log(kf)) / (la - jnp.log(jnp.maximum(fhi, 0.5)))
        interp = lo + (hi - lo) * jnp.where(it % 4 == 1, 0.5, frac)
        t_up = lo + (rmax - lo) * 0.5
        t_dn = hi - jnp.maximum(jnp.maximum(jnp.abs(rmax - hi), jnp.abs(hi)), F32_TINY)
        open_hi = hi == jnp.inf
        open_lo = lo <= -F32_MAX
        t = jnp.where(open_hi, t_up, jnp.where(open_lo, t_dn, interp))
        tk = jnp.minimum(jnp.maximum(_f32_key(t), klo + 1), khi - 1)
        tk = jnp.where(it % 4 == 3, klo + lax.shift_right_logical(khi - klo, 1), tk)
        t = _key_f32(tk)
        f = count_ge(t)
        act = done == 0.0
        hit = act & (f == kf)
        up = act & (f > kf)
        dn = act & (f < kf)
        lo = jnp.where(up, t, lo)
        flo = jnp.where(up, f, flo)
        hi = jnp.where(dn, t, hi)
        fhi = jnp.where(dn, f, fhi)
        adj = act & (~hit) & (_f32_key(lo) + 1 == _f32_key(hi))
        thr = jnp.where(hit, t, jnp.where(adj, lo, thr))
        quota = jnp.where(adj, kf - fhi, quota)
        done = jnp.where(hit | adj, 1.0, done)
        n_open = jnp.sum(1.0 - done)
        return it + 1, n_open, lo, flo, hi, fhi, done, thr, quota

    init = (jnp.int32(0), jnp.sum(1.0 - done0), lo0, flo0, hi0, fhi0, done0, thr0, quota0)
    _, _, _, _, _, _, _, thr, quota = lax.while_loop(search_cond, search_body, init)
    thr_b = jnp.broadcast_to(thr, (tq, KEY_TILE))
    quota_b = jnp.broadcast_to(quota, (tq, KEY_TILE))
    has_ties = jnp.min(quota) < NO_QUOTA

    @pl.when(jnp.logical_not(has_ties))
    def _():
        def body(j, c):
            for k in range(BIG):
                t = j * BIG + k
                sc[t] = jnp.where(sc[t] >= thr_b, 0.0, NEG)
            return c
        lax.fori_loop(0, steps_all, body, 0)

    @pl.when(has_ties)
    def _():
        ri = lax.broadcasted_iota(jnp.int32, (KEY_TILE, KEY_TILE), 0)
        ci = lax.broadcasted_iota(jnp.int32, (KEY_TILE, KEY_TILE), 1)
        upper = jnp.where(ri <= ci, 1.0, 0.0).astype(BF16)
        ones = jnp.ones((KEY_TILE, KEY_TILE), BF16)

        def body(j, seen):
            for k in range(BIG):
                t = j * BIG + k
                s = sc[t]
                tie = jnp.where(s == thr_b, 1.0, 0.0)
                tie16 = tie.astype(BF16)
                rank = seen + _dot(tie16, upper)
                keep = (s > thr_b) | ((tie > 0.0) & (rank <= quota_b))
                sc[t] = jnp.where(keep, 0.0, NEG)
                seen = seen + _dot(tie16, ones)
            return seen
        lax.fori_loop(0, steps_all, body, zero)

    lane = lax.broadcasted_iota(jnp.int32, (GRP * tq, 2 * HEAD_DIM), 1)
    qz = []
    for g in range(N_KV_HEADS):
        qg = jnp.concatenate(
            [q_ref[0, :, (g * GRP + j) * HEAD_DIM:(g * GRP + j + 1) * HEAD_DIM] for j in range(GRP)],
            axis=0).astype(F32)
        qz.append(jnp.concatenate([qg, jnp.zeros_like(qg)], axis=1))

    def penalties(j):
        return [jnp.concatenate([jnp.where(j * BIG + k < n0, sc[j * BIG + k], NEG)] * GRP, axis=0)
                for k in range(BIG)]

    def near_penalty():
        return jnp.concatenate(
            [jnp.concatenate([sc[n0], sc[n0 + 1]], axis=1)] * GRP, axis=0)

    def max_far(j, mxs):
        pens = penalties(j)
        out = []
        for g in range(N_KV_HEADS):
            s = _dot(qz[g].astype(BF16), kt_ref[0, g, :, step_cols(j)])
            m = mxs[g]
            for k in range(BIG):
                m = jnp.maximum(m, s[:, k * KEY_TILE:(k + 1) * KEY_TILE] + pens[k])
            out.append(m)
        return tuple(out)

    neg_tile = jnp.full((GRP * tq, KEY_TILE), NEG, F32)
    mxs = lax.fori_loop(0, steps_far, max_far, (neg_tile, neg_tile))
    pen_near = near_penalty()
    qa = []
    for g in range(N_KV_HEADS):
        s = _dot(qz[g].astype(BF16), kt_ref[0, g, :, near_cols]) + bias[variant, g] + pen_near
        m = jnp.maximum(mxs[g], jnp.maximum(s[:, :KEY_TILE], s[:, KEY_TILE:]))
        m = jnp.max(m, axis=1, keepdims=True)
        qa.append(jnp.where(lane == HEAD_DIM, -m, qz[g]).astype(BF16))

    def acc_far(j, accs):
        pens = penalties(j)
        pen = jnp.concatenate(pens, axis=1)
        out = []
        for g in range(N_KV_HEADS):
            p = jnp.exp2(_dot(qa[g], kt_ref[0, g, :, step_cols(j)]) + pen)
            out.append(accs[g] + _dot(p.astype(BF16), v_ref[0, g, step_cols(j), :]))
        return tuple(out)

    zacc = jnp.zeros((GRP * tq, 2 * HEAD_DIM), F32)
    accs = lax.fori_loop(0, steps_far, acc_far, (zacc, zacc))
    for g in range(N_KV_HEADS):
        p = jnp.exp2(_dot(qa[g], kt_ref[0, g, :, near_cols]) + bias[variant, g] + pen_near)
        acc = accs[g] + _dot(p.astype(BF16), v_ref[0, g, near_cols, :])
        og = acc[:, :HEAD_DIM] / acc[:, HEAD_DIM:HEAD_DIM + 1]
        for j in range(GRP):
            h = g * GRP + j
            o_ref[0, :, h * HEAD_DIM:(h + 1) * HEAD_DIM] = og[j * tq:(j + 1) * tq, :].astype(BF16)


def _dsa(rel_bias, q, qi, sgn, kt, v, kit, *, tq, q_pos0, l_valid):
    b, t, _ = q.shape
    lk = kt.shape[3]
    assert lk % (BIG * KEY_TILE) == 0 and q_pos0 % KEY_TILE == 0
    assert tq == KEY_TILE or t == tq
    assert q_pos0 + t <= lk
    nt = lk // KEY_TILE
    row = lambda width: pl.BlockSpec((1, tq, width), lambda bi, i: (bi, i, 0))
    return pl.pallas_call(
        functools.partial(_dsa_kernel, tq=tq, q_pos0=q_pos0, l_valid=l_valid),
        grid=(b, t // tq),
        in_specs=[pl.BlockSpec(memory_space=pltpu.SMEM),
                  row(D_ATT), row(N_IDX_HEADS * D_IDX), row(N_IDX_HEADS),
                  _resident((1, N_KV_HEADS, 2 * HEAD_DIM, lk), lambda bi, i: (bi, 0, 0, 0)),
                  _resident((1, N_KV_HEADS, lk, 2 * HEAD_DIM), lambda bi, i: (bi, 0, 0, 0)),
                  _resident((1, D_IDX, lk), lambda bi, i: (bi, 0, 0))],
        out_specs=row(D_ATT),
        out_shape=jax.ShapeDtypeStruct((b, t, D_ATT), BF16),
        scratch_shapes=[pltpu.VMEM((nt, tq, KEY_TILE), F32),
                        pltpu.VMEM((2, N_KV_HEADS, GRP * tq, 2 * KEY_TILE), F32)],
        compiler_params=pltpu.CompilerParams(
            dimension_semantics=("arbitrary", "arbitrary"), vmem_limit_bytes=VMEM_LIMIT),
        name="dsa",
    )(rel_bias, q, qi, sgn, kt, v, kit)


def _split3(x):
    hi = x.astype(BF16)
    r = x - hi.astype(F32)
    mid = r.astype(BF16)
    lo = (r - mid.astype(F32)).astype(BF16)
    return hi, mid, lo


def _ssd_kernel(xs_ref, bm_ref, cm_ref, dt_ref, z_ref, alog_ref, dsk_ref, nw_ref, h0_ref,
                y_ref, h_ref, ht, *, q_in):
    c = pl.program_id(1)
    nc = pl.num_programs(1)

    @pl.when(c == 0)
    def _():
        for hh in range(SSM_HEADS):
            ht[hh] = h0_ref[0, hh].T

    def rows(ref):
        x = ref[0]
        if q_in == SSD_Q:
            return x
        return jnp.concatenate([x, jnp.zeros((SSD_Q - q_in, x.shape[1]), x.dtype)], axis=0)

    xs, bm, cm, dt, z = rows(xs_ref), rows(bm_ref), rows(cm_ref), rows(dt_ref), rows(z_ref)
    a = -jnp.exp(alog_ref[...])
    da = dt * a
    ri = lax.broadcasted_iota(jnp.int32, (SSD_Q, SSD_Q), 0)
    ci = lax.broadcasted_iota(jnp.int32, (SSD_Q, SSD_Q), 1)
    causal = ri >= ci
    tril = jnp.where(causal, 1.0, 0.0).astype(BF16)
    hi, mid, lo = _split3(da)
    acum = _dot(tril, hi) + _dot(tril, mid) + _dot(tril, lo)
    acum_t = acum.T

    for g in range(SSM_GROUPS):
        bm_g = bm[:, g * SSM_STATE:(g + 1) * SSM_STATE]
        cm_g = cm[:, g * SSM_STATE:(g + 1) * SSM_STATE].astype(BF16)
        bm_t = bm_g.T.astype(BF16)
        cb = _dot(cm_g, bm_t)
        gs = []
        ssq = jnp.zeros((SSD_Q, 1), F32)
        for r in range(SSM_HPG):
            hh = g * SSM_HPG + r
            lanes = slice(hh * SSM_HEAD_DIM, (hh + 1) * SSM_HEAD_DIM)
            xh = xs[:, lanes]
            col = acum[:, hh:hh + 1]
            rowv = acum_t[hh:hh + 1, :]
            decay = jnp.exp(jnp.where(causal, col - rowv, NEG))
            xdt = xh * dt[:, hh:hh + 1]
            y = _dot((cb * decay).astype(BF16), xdt.astype(BF16))
            h_prev = ht[hh]
            y = y + jnp.exp(col) * _dot(cm_g, h_prev.astype(BF16))
            last = acum[SSD_Q - 1:SSD_Q, hh:hh + 1]
            xw = (xdt * jnp.exp(last - col)).astype(BF16)
            ht[hh] = h_prev * jnp.exp(last) + _dot(bm_t, xw)
            y = y + xh * dsk_ref[:, lanes]
            gate = y * _silu(z[:, lanes])
            ssq = ssq + jnp.sum(gate * gate, axis=1, keepdims=True)
            gs.append(gate)
        scale = lax.rsqrt(ssq / float(SSM_HPG * SSM_HEAD_DIM) + NORM_EPS)
        for r in range(SSM_HPG):
            hh = g * SSM_HPG + r
            lanes = slice(hh * SSM_HEAD_DIM, (hh + 1) * SSM_HEAD_DIM)
            y_ref[0, :, lanes] = (gs[r] * scale * nw_ref[:, lanes])[:q_in].astype(BF16)

    @pl.when(c == nc - 1)
    def _():
        for hh in range(SSM_HEADS):
            h_ref[0, hh] = ht[hh].T


def _ssd(xs, bm, cm, dt, z, alog, dsk, nw, h0, *, q_in):
    b, t, _ = xs.shape
    row = lambda width: pl.BlockSpec((1, q_in, width), lambda bi, c: (bi, c, 0))
    full = lambda a: _resident(a.shape, lambda bi, c: (0,) * a.ndim)
    hspec = pl.BlockSpec((1, SSM_HEADS, SSM_HEAD_DIM, SSM_STATE), lambda bi, c: (bi, 0, 0, 0))
    return pl.pallas_call(
        functools.partial(_ssd_kernel, q_in=q_in),
        grid=(b, t // q_in),
        in_specs=[row(D_SSM), row(256), row(256), row(LANE), row(D_SSM),
                  full(alog), full(dsk), full(nw), hspec],
        out_specs=(row(D_SSM), hspec),
        out_shape=(jax.ShapeDtypeStruct((b, t, D_SSM), BF16),
                   jax.ShapeDtypeStruct((b, SSM_HEADS, SSM_HEAD_DIM, SSM_STATE), F32)),
        scratch_shapes=[pltpu.VMEM((SSM_HEADS, SSM_STATE, SSM_HEAD_DIM), F32)],
        compiler_params=pltpu.CompilerParams(
            dimension_semantics=("arbitrary", "arbitrary"), vmem_limit_bytes=VMEM_LIMIT),
        name="ssd",
    )(xs, bm, cm, dt, z, alog, dsk, nw, h0)


def _mix_ffn_kernel(x_ref, ya_ref, yb_ref, yc_ref, wo_ref, npost_ref, nfpre_ref, wg_ref, wu_ref,
                    wd_ref, cfw_ref, cfb_ref, nfpost_ref, buff_ref, o_ref, buff_out, eg, *, tm):
    i = pl.program_id(1)

    @pl.when(i == 0)
    def _():
        eg[CARRY - 2:CARRY, :] = buff_ref[0]

    mix = (_dot(ya_ref[0], wo_ref[0:D_CONV_MIX, :])
           + _dot(yb_ref[0], wo_ref[D_CONV_MIX:D_CONV_MIX + D_ATT, :])
           + _dot(yc_ref[0], wo_ref[D_CONV_MIX + D_ATT:, :]))
    x1 = x_ref[0] + _rms(mix, npost_ref[...])
    u = _rms(x1, nfpre_ref[...]).astype(BF16)
    eg[CARRY:CARRY + tm, :] = _dot(u, wg_ref[...])
    gc = eg[CARRY - 2:CARRY - 2 + tm, :] * cfw_ref[0:1, :]
    for j in range(1, CONV_F_W):
        gc = gc + eg[CARRY - 2 + j:CARRY - 2 + j + tm, :] * cfw_ref[j:j + 1, :]
    hid = (_silu(gc + cfb_ref[...]) * _dot(u, wu_ref[...])).astype(BF16)
    o_ref[0] = x1 + _rms(_dot(hid, wd_ref[...]), nfpost_ref[...])
    tail = eg[CARRY + tm - 2:CARRY + tm, :]
    buff_out[0] = tail
    eg[CARRY - 2:CARRY, :] = tail


def _mix_ffn(x, ya, yb, yc, wo, npost, nfpre, wg, wu, wd, cfw, cfb, nfpost, buff, *, tm):
    b, t, _ = x.shape
    row = lambda width: pl.BlockSpec((1, tm, width), lambda bi, i: (bi, i, 0))
    full = lambda a: _resident(a.shape, lambda bi, i: (0,) * a.ndim)
    state = pl.BlockSpec((1, CONV_F_W - 1, D_FF), lambda bi, i: (bi, 0, 0))
    return pl.pallas_call(
        functools.partial(_mix_ffn_kernel, tm=tm),
        grid=(b, t // tm),
        in_specs=[row(D_MODEL), row(D_CONV_MIX), row(D_ATT), row(D_SSM), full(wo), full(npost),
                  full(nfpre), full(wg), full(wu), full(wd), full(cfw), full(cfb), full(nfpost),
                  state],
        out_specs=(row(D_MODEL), state),
        out_shape=(jax.ShapeDtypeStruct((b, t, D_MODEL), F32),
                   jax.ShapeDtypeStruct((b, CONV_F_W - 1, D_FF), F32)),
        scratch_shapes=[pltpu.VMEM((CARRY + tm, D_FF), F32)],
        compiler_params=pltpu.CompilerParams(
            dimension_semantics=("arbitrary", "arbitrary"), vmem_limit_bytes=VMEM_LIMIT),
        name="mix_ffn",
    )(x, ya, yb, yc, wo, npost, nfpre, wg, wu, wd, cfw, cfb, nfpost, buff)


def _pack_keys(k_all, ki_all, v_all):
    b, l, _ = k_all.shape
    step = BIG * KEY_TILE
    lk = -(-l // step) * step
    pad = lambda a: jnp.pad(a.astype(BF16), ((0, 0), (0, lk - l), (0, 0)))
    ones = jnp.ones((b, lk, N_KV_HEADS, 1), BF16)
    zeros = jnp.zeros((b, lk, N_KV_HEADS, HEAD_DIM - 1), BF16)
    aug = lambda a: jnp.concatenate([pad(a).reshape(b, lk, N_KV_HEADS, HEAD_DIM), ones, zeros], axis=-1)
    return aug(k_all).transpose(0, 2, 3, 1), aug(v_all).transpose(0, 2, 1, 3), pad(ki_all).swapaxes(1, 2)


def _layer(x, p, rel_bias, buf_a, buf_s, h0, buf_f, kv_past, *, tm, tq):
    b, t, _ = x.shape
    (ya, q, k, v, qi, ki, sgn, z, xs, bm, cm, dt, buf_a_new, buf_s_new) = _in_proj(
        x, p['norm_mix_pre'], p['w_in'], p['conv_a_w'], p['conv_ssm_w'], p['conv_ssm_b'],
        p['dt_bias'], buf_a, buf_s, tm=tm)
    if kv_past is None:
        k_all, v_all, ki_all, past = k, v, ki, 0
    else:
        ck, cv, cki = kv_past
        past = ck.shape[1]
        k_all = jnp.concatenate([ck.reshape(b, past, -1), k], axis=1)
        v_all = jnp.concatenate([cv.reshape(b, past, -1), v], axis=1)
        ki_all = jnp.concatenate([cki, ki], axis=1)
    kt, vp, kit = _pack_keys(k_all, ki_all, v_all)
    yb = _dsa(rel_bias, q, qi, sgn, kt, vp, kit, tq=tq, q_pos0=past, l_valid=past + t)
    yc, h_new = _ssd(xs, bm, cm, dt, z, p['a_log'], p['d_skip'], p['ssm_norm_w'],
                     h0.reshape(b, SSM_HEADS, SSM_HEAD_DIM, SSM_STATE), q_in=min(t, SSD_Q))
    x_new, buf_f_new = _mix_ffn(x, ya, yb, yc, p['w_out'], p['norm_mix_post'], p['norm_ffn_pre'],
                                p['w_gate'], p['w_up'], p['w_down'], p['conv_ffn_w'],
                                p['conv_ffn_b'], p['norm_ffn_post'], buf_f, tm=tm)
    st = (k.reshape(b, t, N_KV_HEADS, HEAD_DIM), v.reshape(b, t, N_KV_HEADS, HEAD_DIM), ki,
          buf_a_new, buf_s_new, h_new.reshape(b, SSM_GROUPS, SSM_HPG, SSM_HEAD_DIM, SSM_STATE),
          buf_f_new)
    return x_new, st


def _pack_w_in(w):
    d = w.shape[0]
    return jnp.concatenate(
        [w[:, :D_IN_HEAD], jnp.zeros((d, C_Z - D_IN_HEAD), w.dtype), w[:, D_IN_HEAD:D_IN_MID],
         w[:, D_IN_MID:], jnp.zeros((d, D_IN_PACKED - C_DT - SSM_HEADS), w.dtype)],
        axis=1).astype(BF16)


def _lane_pad(v):
    return jnp.pad(v, (0, LANE - v.shape[0]))[None, :]


def kernel(x_prompt, x_sample, cache_k, cache_v, cache_kidx, state_conv_a, state_conv_ssm, state_ssm, state_conv_ffn, rel_bias, norm_mix_pre, norm_mix_post, norm_ffn_pre, norm_ffn_post, w_in, conv_a_w, conv_ssm_w, conv_ssm_b, dt_bias, a_log, d_skip, ssm_norm_w, w_out, w_gate, w_up, conv_ffn_w, conv_ffn_b, w_down):
    depth = w_in.shape[0]
    bp = x_prompt.shape[0]
    yp, ys = x_prompt, x_sample
    outs_p, outs_s = [], []
    for l in range(depth):
        p = {'norm_mix_pre': norm_mix_pre[l][None], 'norm_mix_post': norm_mix_post[l][None],
             'norm_ffn_pre': norm_ffn_pre[l][None], 'norm_ffn_post': norm_ffn_post[l][None],
             'w_in': _pack_w_in(w_in[l]), 'conv_a_w': conv_a_w[l], 'conv_ssm_w': conv_ssm_w[l],
             'conv_ssm_b': conv_ssm_b[l][None], 'dt_bias': _lane_pad(dt_bias[l]),
             'a_log': _lane_pad(a_log[l]), 'd_skip': jnp.repeat(d_skip[l], SSM_HEAD_DIM)[None],
             'ssm_norm_w': ssm_norm_w[l][None], 'w_out': w_out[l].astype(BF16),
             'w_gate': w_gate[l].astype(BF16), 'w_up': w_up[l].astype(BF16),
             'conv_ffn_w': conv_ffn_w[l], 'conv_ffn_b': conv_ffn_b[l][None],
             'w_down': w_down[l].astype(BF16)}
        yp, st_p = _layer(
            yp, p, rel_bias,
            jnp.zeros((bp, CONV_A_W - 1, D_CONV_MIX), F32),
            jnp.zeros((bp, SSM_CONV_W - 1, D_XBC), F32),
            jnp.zeros((bp, SSM_GROUPS, SSM_HPG, SSM_HEAD_DIM, SSM_STATE), F32),
            jnp.zeros((bp, CONV_F_W - 1, D_FF), F32),
            None, tm=256, tq=KEY_TILE)
        ys, st_s = _layer(
            ys, p, rel_bias, state_conv_a[l], state_conv_ssm[l], state_ssm[l], state_conv_ffn[l],
            (cache_k[l], cache_v[l], cache_kidx[l]), tm=ys.shape[1], tq=ys.shape[1])
        outs_p.append(st_p)
        outs_s.append(st_s)

    def stack(outs, i):
        return jnp.stack([o[i] for o in outs], axis=0)

    res = [yp, ys]
    for i in range(7):
        res.append(stack(outs_p, i))
        res.append(stack(outs_s, i))
    return tuple(res)
```

```python
import functools

import jax
import jax.numpy as jnp
from jax import lax
from jax.experimental import pallas as pl
from jax.experimental.pallas import tpu as pltpu

F32 = jnp.float32
BF16 = jnp.bfloat16

D_MODEL = 1024
CHUNK = 64
D_CONV_MIX = 256
CONV_A_W = 3
N_HEADS = 6
N_KV_HEADS = 2
HEAD_DIM = 64
D_ATT = N_HEADS * HEAD_DIM
GRP = N_HEADS // N_KV_HEADS
N_IDX_HEADS = 4
D_IDX = 64
TOPK = 256
N_BUCKETS = 32
SSM_HEADS = 6
SSM_HEAD_DIM = 64
D_SSM = SSM_HEADS * SSM_HEAD_DIM
SSM_GROUPS = 2
SSM_HPG = SSM_HEADS // SSM_GROUPS
SSM_STATE = 128
SSM_CONV_W = 4
D_XBC = D_SSM + 2 * SSM_GROUPS * SSM_STATE
D_FF = 2816
CONV_F_W = 3
NORM_EPS = 1e-6

LANE = 128
CARRY = 8
SSD_Q = 128
KEY_TILE = 128
BIG = 4
VMEM_LIMIT = 56 * 1024 * 1024
NEG = -1e30
LOG2E = 1.4426950408889634
F32_MAX = 3.4028234663852886e38
F32_TINY = 1.1754943508222875e-38
NO_QUOTA = 1e9
MAX_SEARCH_STEPS = 256

C_AB, C_AC, C_AH = 0, 256, 512
C_Q = 768
C_K = 1152
C_V = 1280
C_QI = 1408
C_KI = 1664
C_Z = 1792
C_XBC = 2176
C_DT = 3072
D_IN_PACKED = 3200
D_IN_HEAD = 1732
D_IN_MID = 3012


def _rms(x, w):
    return x * lax.rsqrt(jnp.mean(x * x, axis=-1, keepdims=True) + NORM_EPS) * w


def _dot(a, b):
    return jnp.dot(a, b, preferred_element_type=F32)


def _silu(x):
    return x * jax.nn.sigmoid(x)


VT_ROWS = HEAD_DIM + 16


def _augment_vt(vt):
    extra_shape = vt.shape[:-2] + (VT_ROWS - HEAD_DIM, vt.shape[-1])
    row = lax.broadcasted_iota(jnp.int32, extra_shape, len(extra_shape) - 2)
    extra = jnp.where(row == 0, 1.0, 0.0).astype(vt.dtype)
    parts = []
    for g in range(N_KV_HEADS):
        parts += [vt[..., g * HEAD_DIM:(g + 1) * HEAD_DIM, :], extra]
    return jnp.concatenate(parts, axis=-2)


def _resident(shape, index_map):
    return pl.BlockSpec(shape, index_map, pipeline_mode=pl.Buffered(1))


def _in_proj_kernel(x_ref, nw_ref, w_ref, caw_ref, csw_ref, csb_ref, dtb_ref, bufa_ref, bufs_ref,
                    ya_ref, q_ref, k_ref, v_ref, qi_ref, ki_ref, sgn_ref, z_ref, xs_ref, bm_ref,
                    cm_ref, dt_ref, bufa_out, bufs_out, *rest, tm, key_side):
    if key_side:
        kb_ref, vt_ref, kib_ref, ea, es = rest
    else:
        ea, es = rest
    i = pl.program_id(1)
    u = _rms(x_ref[0], nw_ref[...]).astype(BF16)

    def proj(c0, width):
        return _dot(u, w_ref[:, c0:c0 + width])

    @pl.when(i == 0)
    def _():
        ea[CARRY - 2:CARRY, :] = bufa_ref[0]
        es[CARRY - 3:CARRY, :] = bufs_ref[0]

    ea[CARRY:CARRY + tm, :] = proj(C_AC, D_CONV_MIX) * proj(C_AH, D_CONV_MIX)
    conv = ea[CARRY - 2:CARRY - 2 + tm, :] * caw_ref[0:1, :]
    for j in range(1, CONV_A_W):
        conv = conv + ea[CARRY - 2 + j:CARRY - 2 + j + tm, :] * caw_ref[j:j + 1, :]
    ya_ref[0] = (proj(C_AB, D_CONV_MIX) * conv).astype(BF16)
    tail_a = ea[CARRY + tm - 2:CARRY + tm, :]
    bufa_out[0] = tail_a
    ea[CARRY - 2:CARRY, :] = tail_a

    q_ref[0] = (proj(C_Q, D_ATT) * (HEAD_DIM ** -0.5 * LOG2E)).astype(BF16)
    k = proj(C_K, N_KV_HEADS * HEAD_DIM)
    v = proj(C_V, N_KV_HEADS * HEAD_DIM)
    kiw = proj(C_KI, LANE)
    k_ref[0] = k
    v_ref[0] = v
    ki_ref[0] = kiw[:, :D_IDX]
    wi = kiw[:, D_IDX:D_IDX + N_IDX_HEADS]
    kiw_rows = kiw if tm >= LANE else jnp.concatenate([kiw, jnp.zeros((LANE - tm, LANE), F32)], axis=0)
    sgn_ref[0] = jnp.where(kiw_rows.T[D_IDX:D_IDX + 8, :] >= 0.0, 1.0, -1.0).astype(F32)
    if key_side:
        kb_ref[0] = k.astype(BF16)
        vt_ref[0] = _augment_vt(v.T).astype(BF16)
        kib_ref[0] = kiw[:, :D_IDX].astype(BF16)
    wabs = jnp.abs(wi) * (D_IDX ** -0.5 * N_IDX_HEADS ** -0.5)
    qi = proj(C_QI, N_IDX_HEADS * D_IDX)
    for h in range(N_IDX_HEADS):
        qi_ref[0, :, h * D_IDX:(h + 1) * D_IDX] = (
            qi[:, h * D_IDX:(h + 1) * D_IDX] * wabs[:, h:h + 1]).astype(BF16)

    z_ref[0] = proj(C_Z, D_SSM)
    es[CARRY:CARRY + tm, :] = proj(C_XBC, D_XBC)
    xc = es[CARRY - 3:CARRY - 3 + tm, :] * csw_ref[0:1, :]
    for j in range(1, SSM_CONV_W):
        xc = xc + es[CARRY - 3 + j:CARRY - 3 + j + tm, :] * csw_ref[j:j + 1, :]
    xc = _silu(xc + csb_ref[...])
    xs_ref[0] = xc[:, :D_SSM]
    bm_ref[0] = xc[:, D_SSM:D_SSM + SSM_GROUPS * SSM_STATE]
    cm_ref[0] = xc[:, D_SSM + SSM_GROUPS * SSM_STATE:]
    tail_s = es[CARRY + tm - 3:CARRY + tm, :]
    bufs_out[0] = tail_s
    es[CARRY - 3:CARRY, :] = tail_s
    dtr = proj(C_DT, LANE) + dtb_ref[...]
    dt_ref[0] = jnp.maximum(dtr, 0.0) + jnp.log1p(jnp.exp(-jnp.abs(dtr)))


def _in_proj(x, nw, w_packed, caw, csw, csb, dtb, bufa, bufs, *, tm, key_side):
    b, t, _ = x.shape
    tq = max(tm, LANE)
    grid = (b, t // tm)
    row = lambda width: pl.BlockSpec((1, tm, width), lambda bi, i: (bi, i, 0))
    full = lambda a: _resident(a.shape, lambda bi, i: (0,) * a.ndim)
    state = lambda r, width: pl.BlockSpec((1, r, width), lambda bi, i: (bi, 0, 0))
    out_shapes = (
        jax.ShapeDtypeStruct((b, t, D_CONV_MIX), BF16),
        jax.ShapeDtypeStruct((b, t, D_ATT), BF16),
        jax.ShapeDtypeStruct((b, t, 128), F32),
        jax.ShapeDtypeStruct((b, t, 128), F32),
        jax.ShapeDtypeStruct((b, t, 256), BF16),
        jax.ShapeDtypeStruct((b, t, D_IDX), F32),
        jax.ShapeDtypeStruct((b, 8, t // tm * tq), F32),
        jax.ShapeDtypeStruct((b, t, D_SSM), F32),
        jax.ShapeDtypeStruct((b, t, D_SSM), F32),
        jax.ShapeDtypeStruct((b, t, 256), F32),
        jax.ShapeDtypeStruct((b, t, 256), F32),
        jax.ShapeDtypeStruct((b, t, LANE), F32),
        jax.ShapeDtypeStruct((b, CONV_A_W - 1, D_CONV_MIX), F32),
        jax.ShapeDtypeStruct((b, SSM_CONV_W - 1, D_XBC), F32),
    )
    out_specs = (row(D_CONV_MIX), row(D_ATT), row(128), row(128), row(256), row(D_IDX),
                 pl.BlockSpec((1, 8, tq), lambda bi, i: (bi, 0, i)),
                 row(D_SSM), row(D_SSM), row(256), row(256), row(LANE),
                 state(CONV_A_W - 1, D_CONV_MIX), state(SSM_CONV_W - 1, D_XBC))
    if key_side:
        out_shapes += (jax.ShapeDtypeStruct((b, t, 128), BF16),
                       jax.ShapeDtypeStruct((b, N_KV_HEADS * VT_ROWS, t), BF16),
                       jax.ShapeDtypeStruct((b, t, D_IDX), BF16))
        out_specs += (row(128), pl.BlockSpec((1, N_KV_HEADS * VT_ROWS, tm), lambda bi, i: (bi, 0, i)), row(D_IDX))
    in_specs = [row(D_MODEL), full(nw), full(w_packed), full(caw), full(csw), full(csb), full(dtb),
                state(CONV_A_W - 1, D_CONV_MIX), state(SSM_CONV_W - 1, D_XBC)]
    return pl.pallas_call(
        functools.partial(_in_proj_kernel, tm=tm, key_side=key_side),
        grid=grid, in_specs=in_specs, out_specs=out_specs, out_shape=out_shapes,
        scratch_shapes=[pltpu.VMEM((CARRY + tm, D_CONV_MIX), F32),
                        pltpu.VMEM((CARRY + tm, D_XBC), F32)],
        compiler_params=pltpu.CompilerParams(
            dimension_semantics=("arbitrary", "arbitrary"), vmem_limit_bytes=VMEM_LIMIT),
        name="in_proj",
    )(x, nw, w_packed, caw, csw, csb, dtb, bufa, bufs)


def _t5_bucket(rel):
    n = jnp.abs(rel)
    large = jnp.full(rel.shape, 8, jnp.int32)
    for brk in (12, 16, 23, 32, 46, 64, 91):
        large = large + jnp.where(n >= brk, 1, 0)
    return jnp.where(rel > 0, N_BUCKETS // 2, 0) + jnp.where(n < 8, n, large)


def _f32_key(x):
    bits = lax.bitcast_convert_type(x, jnp.int32)
    return bits ^ ((bits >> 31) & jnp.int32(0x7FFFFFFF))


def _key_f32(key):
    return lax.bitcast_convert_type(key ^ ((key >> 31) & jnp.int32(0x7FFFFFFF)), F32)


def _dsa_kernel(relb_ref, q_ref, qi_ref, sgn_ref, k_ref, vt_ref, ki_ref, o_ref, sc, bias, sbuf,
                *, q_pos0, l_valid):
    tq = KEY_TILE
    i = pl.program_id(1)
    q0 = q_pos0 + i * tq
    qt = q0 // KEY_TILE
    n0 = jnp.maximum(qt - 1, 0)
    variant = jnp.minimum(qt, 1)
    steps_far = (n0 + BIG - 1) // BIG
    steps_all = (n0 + 2 + BIG - 1) // BIG
    kw = BIG * KEY_TILE
    kf = float(TOPK)

    def step_keys(j):
        return pl.ds(pl.multiple_of(j * kw, kw), kw)

    near_keys = pl.ds(pl.multiple_of(n0 * KEY_TILE, KEY_TILE), 2 * KEY_TILE)

    @pl.when(i == 0)
    def _():
        sc[...] = jnp.full(sc.shape, -jnp.inf, F32)
        kk = lax.broadcasted_iota(jnp.int32, (2 * KEY_TILE, tq), 0)
        r = lax.broadcasted_iota(jnp.int32, (2 * KEY_TILE, tq), 1)
        for var in range(2):
            bucket = _t5_bucket(kk - var * KEY_TILE - r)
            for h in range(N_HEADS):
                tab = jnp.zeros((2 * KEY_TILE, tq), F32)
                for bk in range(N_BUCKETS):
                    tab = jnp.where(bucket == bk, relb_ref[bk, h], tab)
                g, j = divmod(h, GRP)
                bias[var, g, :, j * tq:(j + 1) * tq] = (tab - relb_ref[N_BUCKETS // 2 - 1, h]) * LOG2E

    qit = qi_ref[0].astype(F32).T
    qis = jnp.concatenate([qit[h * D_IDX:(h + 1) * D_IDX, :] for h in range(N_IDX_HEADS)],
                          axis=1).astype(BF16)
    sgn = sgn_ref[0]

    def scores(keys):
        s = jnp.maximum(_dot(ki_ref[0, keys, :], qis), 0.0)
        tot = s[:, 0:tq] * sgn[0:1, :]
        for h in range(1, N_IDX_HEADS):
            tot = tot + s[:, h * tq:(h + 1) * tq] * sgn[h:h + 1, :]
        return tot

    def score_far(j, c):
        s = scores(step_keys(j))
        for k in range(BIG):
            t = j * BIG + k
            sc[t] = jnp.where(t < n0, s[k * KEY_TILE:(k + 1) * KEY_TILE, :], -jnp.inf)
        return c

    lax.fori_loop(0, steps_far, score_far, 0)
    s_near = scores(near_keys)
    kpos = n0 * KEY_TILE + lax.broadcasted_iota(jnp.int32, (2 * KEY_TILE, tq), 0)
    qpos = q0 + lax.broadcasted_iota(jnp.int32, (2 * KEY_TILE, tq), 1)
    adm = (kpos < l_valid) & ((kpos >> 6) <= (qpos >> 6))
    s_near = jnp.where(adm, s_near, -jnp.inf)
    sc[n0] = s_near[:KEY_TILE, :]
    sc[n0 + 1] = s_near[KEY_TILE:, :]

    zero = jnp.zeros((KEY_TILE, tq), F32)

    def count_ge(t):
        tb = jnp.broadcast_to(t, (KEY_TILE, tq))

        def body(j, acc):
            for k in range(BIG):
                acc = acc + jnp.where(sc[j * BIG + k] >= tb, 1.0, 0.0)
            return acc
        return jnp.sum(lax.fori_loop(0, steps_all, body, zero), axis=0, keepdims=True)

    def first_pass(j, c):
        ge, gt, mx = c
        for k in range(BIG):
            s = sc[j * BIG + k]
            ge = ge + jnp.where(s >= 0.0, 1.0, 0.0)
            gt = gt + jnp.where(s >= F32_TINY, 1.0, 0.0)
            mx = jnp.maximum(mx, s)
        return ge, gt, mx

    ge, gt, mx = lax.fori_loop(0, steps_all, first_pass, (zero, zero, jnp.full((KEY_TILE, tq), -F32_MAX, F32)))
    ge0 = jnp.sum(ge, axis=0, keepdims=True)
    gt0 = jnp.sum(gt, axis=0, keepdims=True)
    rmax = jnp.max(mx, axis=0, keepdims=True)
    qrow = q0 + lax.broadcasted_iota(jnp.int32, (1, tq), 1)
    nadm = jnp.minimum(((qrow >> 6) + 1) * CHUNK, l_valid).astype(F32)
    small = nadm <= kf
    tie0 = (~small) & (gt0 < kf) & (ge0 >= kf)
    pos = (~small) & (gt0 >= kf)
    neg = (~small) & (ge0 < kf)
    lo0 = jnp.where(pos, F32_TINY, -F32_MAX)
    flo0 = jnp.where(pos, gt0, nadm)
    hi0 = jnp.where(neg, 0.0, jnp.inf)
    fhi0 = jnp.where(neg, ge0, 0.0)
    done0 = jnp.where(small | tie0 | (pos & (gt0 == kf)), 1.0, 0.0)
    thr0 = jnp.where(small, -F32_MAX, jnp.where(tie0, 0.0, F32_TINY))
    quota0 = jnp.where(tie0, kf - gt0, NO_QUOTA)

    def search_cond(c):
        return (c[0] < MAX_SEARCH_STEPS) & (c[1] > 0)

    def search_body(c):
        it, _, lo, flo, hi, fhi, done, thr, quota = c
        klo, khi = _f32_key(lo), _f32_key(hi)
        la = jnp.log(flo)
        frac = (la - jnp.log(kf)) / (la - jnp.log(jnp.maximum(fhi, 0.5)))
        interp = lo + (hi - lo) * jnp.where(it % 4 == 1, 0.5, frac)
        t_up = lo + (rmax - lo) * 0.5
        t_dn = hi - jnp.maximum(jnp.maximum(jnp.abs(rmax - hi), jnp.abs(hi)), F32_TINY)
        open_hi = hi == jnp.inf
        open_lo = lo <= -F32_MAX
        t = jnp.where(open_hi, t_up, jnp.where(open_lo, t_dn, interp))
        tk = jnp.minimum(jnp.maximum(_f32_key(t), klo + 1), khi - 1)
        tk = jnp.where(it % 4 == 3, klo + lax.shift_right_logical(khi - klo, 1), tk)
        t = _key_f32(tk)
        f = count_ge(t)
        act = done == 0.0
        hit = act & (f == kf)
        up = act & (f > kf)
        dn = act & (f < kf)
        lo = jnp.where(up, t, lo)
        flo = jnp.where(up, f, flo)
        hi = jnp.where(dn, t, hi)
        fhi = jnp.where(dn, f, fhi)
        adj = act & (~hit) & (_f32_key(lo) + 1 == _f32_key(hi))
        thr = jnp.where(hit, t, jnp.where(adj, lo, thr))
        quota = jnp.where(adj, kf - fhi, quota)
        done = jnp.where(hit | adj, 1.0, done)
        n_open = jnp.sum(1.0 - done)
        return it + 1, n_open, lo, flo, hi, fhi, done, thr, quota

    init = (jnp.int32(0), jnp.sum(1.0 - done0), lo0, flo0, hi0, fhi0, done0, thr0, quota0)
    _, _, _, _, _, _, _, thr, quota = lax.while_loop(search_cond, search_body, init)
    thr_b = jnp.broadcast_to(thr, (tq, KEY_TILE))
    quota_b = jnp.broadcast_to(quota, (tq, KEY_TILE))
    has_ties = jnp.min(quota) < NO_QUOTA

    @pl.when(jnp.logical_not(has_ties))
    def _():
        def body(j, c):
            for k in range(BIG):
                t = j * BIG + k
                sc[t] = jnp.where(sc[t] >= thr_b, 0.0, NEG)
            return c
        lax.fori_loop(0, steps_all, body, 0)

    @pl.when(has_ties)
    def _():
        ri = lax.broadcasted_iota(jnp.int32, (KEY_TILE, KEY_TILE), 0)
        ci = lax.broadcasted_iota(jnp.int32, (KEY_TILE, KEY_TILE), 1)
        lower = jnp.where(ri >= ci, 1.0, 0.0).astype(BF16)

        def body(j, seen):
            for k in range(BIG):
                t = j * BIG + k
                s = sc[t]
                tie = jnp.where(s == thr_b, 1.0, 0.0)
                rank = seen + _dot(lower, tie.astype(BF16))
                keep = (s > thr_b) | ((tie > 0.0) & (rank <= quota_b))
                sc[t] = jnp.where(keep, 0.0, NEG)
                seen = seen + jnp.sum(tie, axis=0, keepdims=True)
            return seen
        lax.fori_loop(0, steps_all, body, jnp.zeros((1, tq), F32))

    qt_all = q_ref[0].astype(F32).T
    zeros_half = jnp.zeros((HEAD_DIM, GRP * tq), F32)
    qa = []
    for g in range(N_KV_HEADS):
        piece = jnp.concatenate(
            [qt_all[(g * GRP + j) * HEAD_DIM:(g * GRP + j + 1) * HEAD_DIM, :] for j in range(GRP)],
            axis=1)
        halves = [piece, zeros_half] if g == 0 else [zeros_half, piece]
        qa.append(jnp.concatenate(halves, axis=0).astype(BF16))

    def logits(slot, keys, pens, m_old, extra=None):
        pen = jnp.concatenate(pens, axis=0)
        pen = jnp.concatenate([pen] * GRP, axis=1)
        kblk = k_ref[0, keys, :]
        m_new = []
        for g in range(N_KV_HEADS):
            s = _dot(kblk, qa[g]) + pen
            if extra is not None:
                s = s + extra[g]
            sbuf[slot, g, 0:pen.shape[0], :] = s
            m_new.append(jnp.maximum(m_old[g], jnp.max(s, axis=0, keepdims=True)))
        return tuple(m_new)

    def accumulate(slot, keys, nk, m_old, m_new, accs):
        out = []
        for g in range(N_KV_HEADS):
            p = jnp.exp2(sbuf[slot, g, 0:nk, :] - m_new[g])
            out.append(jnp.exp2(m_old[g] - m_new[g]) * accs[g]
                       + _dot(vt_ref[0, g * VT_ROWS:(g + 1) * VT_ROWS, keys], p.astype(BF16)))
        return tuple(out)

    last_far = jnp.maximum(steps_far - 1, 0)

    def far_keys(j):
        return step_keys(jnp.minimum(j, last_far))

    def far_pens(j):
        return [jnp.where(j * BIG + k < n0, sc[jnp.minimum(j * BIG + k, n0)], NEG) for k in range(BIG)]

    m_init = tuple(jnp.full((1, GRP * tq), NEG, F32) for _ in range(N_KV_HEADS))
    acc_init = tuple(jnp.zeros((VT_ROWS, GRP * tq), F32) for _ in range(N_KV_HEADS))
    m_first = logits(0, far_keys(0), far_pens(0), m_init)

    def stage_pair(i, c):
        m_old, m_cur, accs = c
        a = 2 * i + 1
        m_a = logits(1, far_keys(a), far_pens(a), m_cur)
        accs = accumulate(0, far_keys(a - 1), kw, m_old, m_cur, accs)
        m_b = logits(0, far_keys(a + 1), far_pens(a + 1), m_a)
        accs = accumulate(1, far_keys(a), kw, m_cur, m_a, accs)
        return m_a, m_b, accs

    pairs = steps_far // 2
    m_old, m_cur, accs = lax.fori_loop(0, pairs, stage_pair, (m_init, m_first, acc_init))
    m_last = logits(1, near_keys, [sc[n0], sc[n0 + 1]], m_cur, extra=(bias[variant, 0], bias[variant, 1]))
    accs = accumulate(0, far_keys(2 * pairs), kw, m_old, m_cur, accs)
    accs = accumulate(1, near_keys, 2 * KEY_TILE, m_cur, m_last, accs)
    heads = []
    for g in range(N_KV_HEADS):
        acc = accs[g]
        og = acc[:HEAD_DIM, :] / acc[HEAD_DIM:HEAD_DIM + 1, :]
        heads += [og[:, j * tq:(j + 1) * tq] for j in range(GRP)]
    o_ref[0] = jnp.concatenate(heads, axis=0).T.astype(BF16)


def _dsa(rel_bias, q, qi, sgn, k, vt, ki, *, q_pos0, l_valid):
    b, t, _ = q.shape
    lk = k.shape[1]
    tq = KEY_TILE
    assert lk % (BIG * KEY_TILE) == 0 and q_pos0 % KEY_TILE == 0 and t % tq == 0
    assert q_pos0 + t <= lk
    nt = lk // KEY_TILE
    row = lambda width: pl.BlockSpec((1, tq, width), lambda bi, i: (bi, i, 0))
    return pl.pallas_call(
        functools.partial(_dsa_kernel, q_pos0=q_pos0, l_valid=l_valid),
        grid=(b, t // tq),
        in_specs=[pl.BlockSpec(memory_space=pltpu.SMEM),
                  row(D_ATT), row(N_IDX_HEADS * D_IDX),
                  pl.BlockSpec((1, 8, tq), lambda bi, i: (bi, 0, i)),
                  _resident((1, lk, N_KV_HEADS * HEAD_DIM), lambda bi, i: (bi, 0, 0)),
                  _resident((1, N_KV_HEADS * VT_ROWS, lk), lambda bi, i: (bi, 0, 0)),
                  _resident((1, lk, D_IDX), lambda bi, i: (bi, 0, 0))],
        out_specs=row(D_ATT),
        out_shape=jax.ShapeDtypeStruct((b, t, D_ATT), BF16),
        scratch_shapes=[pltpu.VMEM((nt, KEY_TILE, tq), F32),
                        pltpu.VMEM((2, N_KV_HEADS, 2 * KEY_TILE, GRP * tq), F32),
                        pltpu.VMEM((2, N_KV_HEADS, BIG * KEY_TILE, GRP * tq), F32)],
        compiler_params=pltpu.CompilerParams(
            dimension_semantics=("arbitrary", "arbitrary"), vmem_limit_bytes=VMEM_LIMIT),
        name="dsa",
    )(rel_bias, q, qi, sgn, k, vt, ki)


def _split3(x):
    hi = x.astype(BF16)
    r = x - hi.astype(F32)
    mid = r.astype(BF16)
    lo = (r - mid.astype(F32)).astype(BF16)
    return hi, mid, lo


def _ssd_kernel(xs_ref, bm_ref, cm_ref, dt_ref, z_ref, alog_ref, dsk_ref, nw_ref, h0_ref,
                y_ref, h_ref, ht, *, q_in):
    c = pl.program_id(1)
    nc = pl.num_programs(1)

    @pl.when(c == 0)
    def _():
        for hh in range(SSM_HEADS):
            ht[hh] = h0_ref[0, hh].T

    def rows(ref):
        x = ref[0]
        if q_in == SSD_Q:
            return x
        return jnp.concatenate([x, jnp.zeros((SSD_Q - q_in, x.shape[1]), x.dtype)], axis=0)

    xs, bm, cm, dt, z = rows(xs_ref), rows(bm_ref), rows(cm_ref), rows(dt_ref), rows(z_ref)
    a = -jnp.exp(alog_ref[...])
    da = dt * a
    ri = lax.broadcasted_iota(jnp.int32, (SSD_Q, SSD_Q), 0)
    ci = lax.broadcasted_iota(jnp.int32, (SSD_Q, SSD_Q), 1)
    causal = ri >= ci
    tril = jnp.where(causal, 1.0, 0.0).astype(BF16)
    hi, mid, lo = _split3(da)
    acum = _dot(tril, hi) + _dot(tril, mid) + _dot(tril, lo)
    acum_t = acum.T

    for g in range(SSM_GROUPS):
        bm_g = bm[:, g * SSM_STATE:(g + 1) * SSM_STATE]
        cm_g = cm[:, g * SSM_STATE:(g + 1) * SSM_STATE].astype(BF16)
        bm_t = bm_g.T.astype(BF16)
        cb = _dot(cm_g, bm_t)
        gs = []
        ssq = jnp.zeros((SSD_Q, 1), F32)
        for r in range(SSM_HPG):
            hh = g * SSM_HPG + r
            lanes = slice(hh * SSM_HEAD_DIM, (hh + 1) * SSM_HEAD_DIM)
            xh = xs[:, lanes]
            col = acum[:, hh:hh + 1]
            rowv = acum_t[hh:hh + 1, :]
            decay = jnp.exp(jnp.where(causal, col - rowv, NEG))
            xdt = xh * dt[:, hh:hh + 1]
            y = _dot((cb * decay).astype(BF16), xdt.astype(BF16))
            h_prev = ht[hh]
            y = y + jnp.exp(col) * _dot(cm_g, h_prev.astype(BF16))
            last = acum[SSD_Q - 1:SSD_Q, hh:hh + 1]
            xw = (xdt * jnp.exp(last - col)).astype(BF16)
            ht[hh] = h_prev * jnp.exp(last) + _dot(bm_t, xw)
            y = y + xh * dsk_ref[:, lanes]
            gate = y * _silu(z[:, lanes])
            ssq = ssq + jnp.sum(gate * gate, axis=1, keepdims=True)
            gs.append(gate)
        scale = lax.rsqrt(ssq / float(SSM_HPG * SSM_HEAD_DIM) + NORM_EPS)
        for r in range(SSM_HPG):
            hh = g * SSM_HPG + r
            lanes = slice(hh * SSM_HEAD_DIM, (hh + 1) * SSM_HEAD_DIM)
            y_ref[0, :, lanes] = (gs[r] * scale * nw_ref[:, lanes])[:q_in].astype(BF16)

    @pl.when(c == nc - 1)
    def _():
        for hh in range(SSM_HEADS):
            h_ref[0, hh] = ht[hh].T


def _ssd(xs, bm, cm, dt, z, alog, dsk, nw, h0, *, q_in):
    b, t, _ = xs.shape
    row = lambda width: pl.BlockSpec((1, q_in, width), lambda bi, c: (bi, c, 0))
    full = lambda a: _resident(a.shape, lambda bi, c: (0,) * a.ndim)
    hspec = pl.BlockSpec((1, SSM_HEADS, SSM_HEAD_DIM, SSM_STATE), lambda bi, c: (bi, 0, 0, 0))
    return pl.pallas_call(
        functools.partial(_ssd_kernel, q_in=q_in),
        grid=(b, t // q_in),
        in_specs=[row(D_SSM), row(256), row(256), row(LANE), row(D_SSM),
                  full(alog), full(dsk), full(nw), hspec],
        out_specs=(row(D_SSM), hspec),
        out_shape=(jax.ShapeDtypeStruct((b, t, D_SSM), BF16),
                   jax.ShapeDtypeStruct((b, SSM_HEADS, SSM_HEAD_DIM, SSM_STATE), F32)),
        scratch_shapes=[pltpu.VMEM((SSM_HEADS, SSM_STATE, SSM_HEAD_DIM), F32)],
        compiler_params=pltpu.CompilerParams(
            dimension_semantics=("arbitrary", "arbitrary"), vmem_limit_bytes=VMEM_LIMIT),
        name="ssd",
    )(xs, bm, cm, dt, z, alog, dsk, nw, h0)


def _mix_ffn_kernel(x_ref, ya_ref, yb_ref, yc_ref, wo_ref, npost_ref, nfpre_ref, wg_ref, wu_ref,
                    wd_ref, cfw_ref, cfb_ref, nfpost_ref, buff_ref, o_ref, buff_out, eg, *, tm):
    i = pl.program_id(1)

    @pl.when(i == 0)
    def _():
        eg[CARRY - 2:CARRY, :] = buff_ref[0]

    mix = (_dot(ya_ref[0], wo_ref[0:D_CONV_MIX, :])
           + _dot(yb_ref[0], wo_ref[D_CONV_MIX:D_CONV_MIX + D_ATT, :])
           + _dot(yc_ref[0], wo_ref[D_CONV_MIX + D_ATT:, :]))
    x1 = x_ref[0] + _rms(mix, npost_ref[...])
    u = _rms(x1, nfpre_ref[...]).astype(BF16)
    eg[CARRY:CARRY + tm, :] = _dot(u, wg_ref[...])
    gc = eg[CARRY - 2:CARRY - 2 + tm, :] * cfw_ref[0:1, :]
    for j in range(1, CONV_F_W):
        gc = gc + eg[CARRY - 2 + j:CARRY - 2 + j + tm, :] * cfw_ref[j:j + 1, :]
    hid = (_silu(gc + cfb_ref[...]) * _dot(u, wu_ref[...])).astype(BF16)
    o_ref[0] = x1 + _rms(_dot(hid, wd_ref[...]), nfpost_ref[...])
    tail = eg[CARRY + tm - 2:CARRY + tm, :]
    buff_out[0] = tail
    eg[CARRY - 2:CARRY, :] = tail


def _mix_ffn(x, ya, yb, yc, wo, npost, nfpre, wg, wu, wd, cfw, cfb, nfpost, buff, *, tm):
    b, t, _ = x.shape
    row = lambda width: pl.BlockSpec((1, tm, width), lambda bi, i: (bi, i, 0))
    full = lambda a: _resident(a.shape, lambda bi, i: (0,) * a.ndim)
    state = pl.BlockSpec((1, CONV_F_W - 1, D_FF), lambda bi, i: (bi, 0, 0))
    return pl.pallas_call(
        functools.partial(_mix_ffn_kernel, tm=tm),
        grid=(b, t // tm),
        in_specs=[row(D_MODEL), row(D_CONV_MIX), row(D_ATT), row(D_SSM), full(wo), full(npost),
                  full(nfpre), full(wg), full(wu), full(wd), full(cfw), full(cfb), full(nfpost),
                  state],
        out_specs=(row(D_MODEL), state),
        out_shape=(jax.ShapeDtypeStruct((b, t, D_MODEL), F32),
                   jax.ShapeDtypeStruct((b, CONV_F_W - 1, D_FF), F32)),
        scratch_shapes=[pltpu.VMEM((CARRY + tm, D_FF), F32)],
        compiler_params=pltpu.CompilerParams(
            dimension_semantics=("arbitrary", "arbitrary"), vmem_limit_bytes=VMEM_LIMIT),
        name="mix_ffn",
    )(x, ya, yb, yc, wo, npost, nfpre, wg, wu, wd, cfw, cfb, nfpost, buff)


def _append_keys(cache, new):
    b, past = cache.shape[:2]
    rows = jnp.concatenate([cache.reshape(b, past, -1), new], axis=1).astype(BF16)
    step = BIG * KEY_TILE
    return jnp.pad(rows, ((0, 0), (0, -rows.shape[1] % step), (0, 0)))


def _layer(x, p, rel_bias, buf_a, buf_s, h0, buf_f, kv_past, *, tm):
    b, t, _ = x.shape
    outs = _in_proj(x, p['norm_mix_pre'], p['w_in'], p['conv_a_w'], p['conv_ssm_w'], p['conv_ssm_b'],
                    p['dt_bias'], buf_a, buf_s, tm=tm, key_side=kv_past is None)
    (ya, q, k, v, qi, ki, sgn, z, xs, bm, cm, dt, buf_a_new, buf_s_new) = outs[:14]
    if kv_past is None:
        kb, vt, kib = outs[14:]
        yb = _dsa(rel_bias, q, qi, sgn, kb, vt, kib, q_pos0=0, l_valid=t)
    else:
        ck, cv, cki = kv_past
        past = ck.shape[1]
        qpad = lambda a: jnp.pad(a, ((0, 0), (0, -t % KEY_TILE), (0, 0)))
        yb = _dsa(rel_bias, qpad(q), qpad(qi), sgn, _append_keys(ck, k),
                  _augment_vt(_append_keys(cv, v).swapaxes(1, 2)), _append_keys(cki, ki),
                  q_pos0=past, l_valid=past + t)[:, :t]
    yc, h_new = _ssd(xs, bm, cm, dt, z, p['a_log'], p['d_skip'], p['ssm_norm_w'],
                     h0.reshape(b, SSM_HEADS, SSM_HEAD_DIM, SSM_STATE), q_in=min(t, SSD_Q))
    x_new, buf_f_new = _mix_ffn(x, ya, yb, yc, p['w_out'], p['norm_mix_post'], p['norm_ffn_pre'],
                                p['w_gate'], p['w_up'], p['w_down'], p['conv_ffn_w'],
                                p['conv_ffn_b'], p['norm_ffn_post'], buf_f, tm=tm)
    st = (k.reshape(b, t, N_KV_HEADS, HEAD_DIM), v.reshape(b, t, N_KV_HEADS, HEAD_DIM), ki,
          buf_a_new, buf_s_new, h_new.reshape(b, SSM_GROUPS, SSM_HPG, SSM_HEAD_DIM, SSM_STATE),
          buf_f_new)
    return x_new, st


def _pack_w_in(w):
    d = w.shape[0]
    return jnp.concatenate(
        [w[:, :D_IN_HEAD], jnp.zeros((d, C_Z - D_IN_HEAD), w.dtype), w[:, D_IN_HEAD:D_IN_MID],
         w[:, D_IN_MID:], jnp.zeros((d, D_IN_PACKED - C_DT - SSM_HEADS), w.dtype)],
        axis=1).astype(BF16)


def _lane_pad(v):
    return jnp.pad(v, (0, LANE - v.shape[0]))[None, :]


def kernel(x_prompt, x_sample, cache_k, cache_v, cache_kidx, state_conv_a, state_conv_ssm, state_ssm, state_conv_ffn, rel_bias, norm_mix_pre, norm_mix_post, norm_ffn_pre, norm_ffn_post, w_in, conv_a_w, conv_ssm_w, conv_ssm_b, dt_bias, a_log, d_skip, ssm_norm_w, w_out, w_gate, w_up, conv_ffn_w, conv_ffn_b, w_down):
    depth = w_in.shape[0]
    bp = x_prompt.shape[0]
    yp, ys = x_prompt, x_sample
    outs_p, outs_s = [], []
    for l in range(depth):
        p = {'norm_mix_pre': norm_mix_pre[l][None], 'norm_mix_post': norm_mix_post[l][None],
             'norm_ffn_pre': norm_ffn_pre[l][None], 'norm_ffn_post': norm_ffn_post[l][None],
             'w_in': _pack_w_in(w_in[l]), 'conv_a_w': conv_a_w[l], 'conv_ssm_w': conv_ssm_w[l],
             'conv_ssm_b': conv_ssm_b[l][None], 'dt_bias': _lane_pad(dt_bias[l]),
             'a_log': _lane_pad(a_log[l]), 'd_skip': jnp.repeat(d_skip[l], SSM_HEAD_DIM)[None],
             'ssm_norm_w': ssm_norm_w[l][None], 'w_out': w_out[l].astype(BF16),
             'w_gate': w_gate[l].astype(BF16), 'w_up': w_up[l].astype(BF16),
             'conv_ffn_w': conv_ffn_w[l], 'conv_ffn_b': conv_ffn_b[l][None],
             'w_down': w_down[l].astype(BF16)}
        yp, st_p = _layer(
            yp, p, rel_bias,
            jnp.zeros((bp, CONV_A_W - 1, D_CONV_MIX), F32),
            jnp.zeros((bp, SSM_CONV_W - 1, D_XBC), F32),
            jnp.zeros((bp, SSM_GROUPS, SSM_HPG, SSM_HEAD_DIM, SSM_STATE), F32),
            jnp.zeros((bp, CONV_F_W - 1, D_FF), F32),
            None, tm=256)
        ys, st_s = _layer(
            ys, p, rel_bias, state_conv_a[l], state_conv_ssm[l], state_ssm[l], state_conv_ffn[l],
            (cache_k[l], cache_v[l], cache_kidx[l]), tm=ys.shape[1])
        outs_p.append(st_p)
        outs_s.append(st_s)

    def stack(outs, i):
        return jnp.stack([o[i] for o in outs], axis=0)

    res = [yp, ys]
    for i in range(7):
        res.append(stack(outs_p, i))
        res.append(stack(outs_s, i))
    return tuple(res)
```

```python
import functools

import jax
import jax.numpy as jnp
from jax import lax
from jax.experimental import pallas as pl
from jax.experimental.pallas import tpu as pltpu

F32 = jnp.float32
BF16 = jnp.bfloat16

D_MODEL = 1024
CHUNK = 64
D_CONV_MIX = 256
CONV_A_W = 3
N_HEADS = 6
N_KV_HEADS = 2
HEAD_DIM = 64
D_ATT = N_HEADS * HEAD_DIM
GRP = N_HEADS // N_KV_HEADS
N_IDX_HEADS = 4
D_IDX = 64
TOPK = 256
N_BUCKETS = 32
SSM_HEADS = 6
SSM_HEAD_DIM = 64
D_SSM = SSM_HEADS * SSM_HEAD_DIM
SSM_GROUPS = 2
SSM_HPG = SSM_HEADS // SSM_GROUPS
SSM_STATE = 128
SSM_CONV_W = 4
D_XBC = D_SSM + 2 * SSM_GROUPS * SSM_STATE
D_FF = 2816
CONV_F_W = 3
NORM_EPS = 1e-6

LANE = 128
CARRY = 8
SSD_Q = 128
KEY_TILE = 128
BIG = 4
VMEM_LIMIT = 56 * 1024 * 1024
NEG = -1e30
LOG2E = 1.4426950408889634
F32_MAX = 3.4028234663852886e38
F32_TINY = 1.1754943508222875e-38
NO_QUOTA = 1e9
MAX_SEARCH_STEPS = 256

C_AB, C_AC, C_AH = 0, 256, 512
C_Q = 768
C_K = 1152
C_V = 1280
C_QI = 1408
C_KI = 1664
C_Z = 1792
C_XBC = 2176
C_DT = 3072
D_IN_PACKED = 3200
D_IN_HEAD = 1732
D_IN_MID = 3012


def _rms(x, w):
    return x * lax.rsqrt(jnp.mean(x * x, axis=-1, keepdims=True) + NORM_EPS) * w


def _dot(a, b):
    return jnp.dot(a, b, preferred_element_type=F32)


def _silu(x):
    return x * jax.nn.sigmoid(x)


VT_ROWS = HEAD_DIM + 16


def _augment_vt(vt):
    extra_shape = vt.shape[:-2] + (VT_ROWS - HEAD_DIM, vt.shape[-1])
    row = lax.broadcasted_iota(jnp.int32, extra_shape, len(extra_shape) - 2)
    extra = jnp.where(row == 0, 1.0, 0.0).astype(vt.dtype)
    parts = []
    for g in range(N_KV_HEADS):
        parts += [vt[..., g * HEAD_DIM:(g + 1) * HEAD_DIM, :], extra]
    return jnp.concatenate(parts, axis=-2)


def _resident(shape, index_map):
    return pl.BlockSpec(shape, index_map, pipeline_mode=pl.Buffered(1))


def _in_proj_kernel(x_ref, nw_ref, w_ref, caw_ref, csw_ref, csb_ref, dtb_ref, bufa_ref, bufs_ref,
                    ya_ref, q_ref, k_ref, v_ref, qi_ref, ki_ref, sgn_ref, z_ref, xs_ref, bm_ref,
                    cm_ref, dt_ref, bufa_out, bufs_out, *rest, tm, key_side):
    if key_side:
        kb_ref, vt_ref, kib_ref, ea, es = rest
    else:
        ea, es = rest
    i = pl.program_id(1)
    u = _rms(x_ref[0], nw_ref[...]).astype(BF16)

    def proj(c0, width):
        return _dot(u, w_ref[:, c0:c0 + width])

    @pl.when(i == 0)
    def _():
        ea[CARRY - 2:CARRY, :] = bufa_ref[0]
        es[CARRY - 3:CARRY, :] = bufs_ref[0]

    ea[CARRY:CARRY + tm, :] = proj(C_AC, D_CONV_MIX) * proj(C_AH, D_CONV_MIX)
    conv = ea[CARRY - 2:CARRY - 2 + tm, :] * caw_ref[0:1, :]
    for j in range(1, CONV_A_W):
        conv = conv + ea[CARRY - 2 + j:CARRY - 2 + j + tm, :] * caw_ref[j:j + 1, :]
    ya_ref[0] = (proj(C_AB, D_CONV_MIX) * conv).astype(BF16)
    tail_a = ea[CARRY + tm - 2:CARRY + tm, :]
    bufa_out[0] = tail_a
    ea[CARRY - 2:CARRY, :] = tail_a

    q_ref[0] = (proj(C_Q, D_ATT) * (HEAD_DIM ** -0.5 * LOG2E)).astype(BF16)
    k = proj(C_K, N_KV_HEADS * HEAD_DIM)
    v = proj(C_V, N_KV_HEADS * HEAD_DIM)
    kiw = proj(C_KI, LANE)
    k_ref[0] = k
    v_ref[0] = v
    ki_ref[0] = kiw[:, :D_IDX]
    wi = kiw[:, D_IDX:D_IDX + N_IDX_HEADS]
    kiw_rows = kiw if tm >= LANE else jnp.concatenate([kiw, jnp.zeros((LANE - tm, LANE), F32)], axis=0)
    sgn_ref[0] = jnp.where(kiw_rows.T[D_IDX:D_IDX + 8, :] >= 0.0, 1.0, -1.0).astype(F32)
    if key_side:
        kb_ref[0] = k.astype(BF16)
        vt_ref[0] = _augment_vt(v.T).astype(BF16)
        kib_ref[0] = kiw[:, :D_IDX].astype(BF16)
    wabs = jnp.abs(wi) * (D_IDX ** -0.5 * N_IDX_HEADS ** -0.5)
    qi = proj(C_QI, N_IDX_HEADS * D_IDX)
    for h in range(N_IDX_HEADS):
        qi_ref[0, :, h * D_IDX:(h + 1) * D_IDX] = (
            qi[:, h * D_IDX:(h + 1) * D_IDX] * wabs[:, h:h + 1]).astype(BF16)

    z_ref[0] = proj(C_Z, D_SSM)
    es[CARRY:CARRY + tm, :] = proj(C_XBC, D_XBC)
    xc = es[CARRY - 3:CARRY - 3 + tm, :] * csw_ref[0:1, :]
    for j in range(1, SSM_CONV_W):
        xc = xc + es[CARRY - 3 + j:CARRY - 3 + j + tm, :] * csw_ref[j:j + 1, :]
    xc = _silu(xc + csb_ref[...])
    xs_ref[0] = xc[:, :D_SSM]
    bm_ref[0] = xc[:, D_SSM:D_SSM + SSM_GROUPS * SSM_STATE]
    cm_ref[0] = xc[:, D_SSM + SSM_GROUPS * SSM_STATE:]
    tail_s = es[CARRY + tm - 3:CARRY + tm, :]
    bufs_out[0] = tail_s
    es[CARRY - 3:CARRY, :] = tail_s
    dtr = proj(C_DT, LANE) + dtb_ref[...]
    dt_ref[0] = jnp.maximum(dtr, 0.0) + jnp.log1p(jnp.exp(-jnp.abs(dtr)))


def _in_proj(x, nw, w_packed, caw, csw, csb, dtb, bufa, bufs, *, tm, key_side):
    b, t, _ = x.shape
    tq = max(tm, LANE)
    grid = (b, t // tm)
    row = lambda width: pl.BlockSpec((1, tm, width), lambda bi, i: (bi, i, 0))
    full = lambda a: _resident(a.shape, lambda bi, i: (0,) * a.ndim)
    state = lambda r, width: pl.BlockSpec((1, r, width), lambda bi, i: (bi, 0, 0))
    out_shapes = (
        jax.ShapeDtypeStruct((b, t, D_CONV_MIX), BF16),
        jax.ShapeDtypeStruct((b, t, D_ATT), BF16),
        jax.ShapeDtypeStruct((b, t, 128), F32),
        jax.ShapeDtypeStruct((b, t, 128), F32),
        jax.ShapeDtypeStruct((b, t, 256), BF16),
        jax.ShapeDtypeStruct((b, t, D_IDX), F32),
        jax.ShapeDtypeStruct((b, 8, t // tm * tq), F32),
        jax.ShapeDtypeStruct((b, t, D_SSM), F32),
        jax.ShapeDtypeStruct((b, t, D_SSM), F32),
        jax.ShapeDtypeStruct((b, t, 256), F32),
        jax.ShapeDtypeStruct((b, t, 256), F32),
        jax.ShapeDtypeStruct((b, t, LANE), F32),
        jax.ShapeDtypeStruct((b, CONV_A_W - 1, D_CONV_MIX), F32),
        jax.ShapeDtypeStruct((b, SSM_CONV_W - 1, D_XBC), F32),
    )
    out_specs = (row(D_CONV_MIX), row(D_ATT), row(128), row(128), row(256), row(D_IDX),
                 pl.BlockSpec((1, 8, tq), lambda bi, i: (bi, 0, i)),
                 row(D_SSM), row(D_SSM), row(256), row(256), row(LANE),
                 state(CONV_A_W - 1, D_CONV_MIX), state(SSM_CONV_W - 1, D_XBC))
    if key_side:
        out_shapes += (jax.ShapeDtypeStruct((b, t, 128), BF16),
                       jax.ShapeDtypeStruct((b, N_KV_HEADS * VT_ROWS, t), BF16),
                       jax.ShapeDtypeStruct((b, t, D_IDX), BF16))
        out_specs += (row(128), pl.BlockSpec((1, N_KV_HEADS * VT_ROWS, tm), lambda bi, i: (bi, 0, i)), row(D_IDX))
    in_specs = [row(D_MODEL), full(nw), full(w_packed), full(caw), full(csw), full(csb), full(dtb),
                state(CONV_A_W - 1, D_CONV_MIX), state(SSM_CONV_W - 1, D_XBC)]
    return pl.pallas_call(
        functools.partial(_in_proj_kernel, tm=tm, key_side=key_side),
        grid=grid, in_specs=in_specs, out_specs=out_specs, out_shape=out_shapes,
        scratch_shapes=[pltpu.VMEM((CARRY + tm, D_CONV_MIX), F32),
                        pltpu.VMEM((CARRY + tm, D_XBC), F32)],
        compiler_params=pltpu.CompilerParams(
            dimension_semantics=("arbitrary", "arbitrary"), vmem_limit_bytes=VMEM_LIMIT),
        name="in_proj",
    )(x, nw, w_packed, caw, csw, csb, dtb, bufa, bufs)


def _t5_bucket(rel):
    n = jnp.abs(rel)
    large = jnp.full(rel.shape, 8, jnp.int32)
    for brk in (12, 16, 23, 32, 46, 64, 91):
        large = large + jnp.where(n >= brk, 1, 0)
    return jnp.where(rel > 0, N_BUCKETS // 2, 0) + jnp.where(n < 8, n, large)


def _f32_key(x):
    bits = lax.bitcast_convert_type(x, jnp.int32)
    return bits ^ ((bits >> 31) & jnp.int32(0x7FFFFFFF))


def _key_f32(key):
    return lax.bitcast_convert_type(key ^ ((key >> 31) & jnp.int32(0x7FFFFFFF)), F32)


def _dsa_kernel(relb_ref, q_ref, qi_ref, sgn_ref, k_ref, vt_ref, ki_ref, o_ref, sc, bias, sbuf,
                *, q_pos0, l_valid):
    tq = KEY_TILE
    i = pl.program_id(1)
    q0 = q_pos0 + i * tq
    qt = q0 // KEY_TILE
    n0 = jnp.maximum(qt - 1, 0)
    variant = jnp.minimum(qt, 1)
    steps_far = (n0 + BIG - 1) // BIG
    steps_all = (n0 + 2 + BIG - 1) // BIG
    kw = BIG * KEY_TILE
    kf = float(TOPK)

    def step_keys(j):
        return pl.ds(pl.multiple_of(j * kw, kw), kw)

    near_keys = pl.ds(pl.multiple_of(n0 * KEY_TILE, KEY_TILE), 2 * KEY_TILE)

    @pl.when(i == 0)
    def _():
        sc[...] = jnp.full(sc.shape, -jnp.inf, F32)
        kk = lax.broadcasted_iota(jnp.int32, (2 * KEY_TILE, tq), 0)
        r = lax.broadcasted_iota(jnp.int32, (2 * KEY_TILE, tq), 1)
        for var in range(2):
            bucket = _t5_bucket(kk - var * KEY_TILE - r)
            for h in range(N_HEADS):
                tab = jnp.zeros((2 * KEY_TILE, tq), F32)
                for bk in range(N_BUCKETS):
                    tab = jnp.where(bucket == bk, relb_ref[bk, h], tab)
                g, j = divmod(h, GRP)
                bias[var, g, :, j * tq:(j + 1) * tq] = (tab - relb_ref[N_BUCKETS // 2 - 1, h]) * LOG2E

    qit = qi_ref[0].astype(F32).T
    qis = jnp.concatenate([qit[h * D_IDX:(h + 1) * D_IDX, :] for h in range(N_IDX_HEADS)],
                          axis=1).astype(BF16)
    sgn = sgn_ref[0]

    def scores(keys):
        s = jnp.maximum(_dot(ki_ref[0, keys, :], qis), 0.0)
        tot = s[:, 0:tq] * sgn[0:1, :]
        for h in range(1, N_IDX_HEADS):
            tot = tot + s[:, h * tq:(h + 1) * tq] * sgn[h:h + 1, :]
        return tot

    zero = jnp.zeros((KEY_TILE, tq), F32)

    def tally(c, s):
        ge, gt, mx = c
        return (ge + jnp.where(s >= 0.0, 1.0, 0.0), gt + jnp.where(s >= F32_TINY, 1.0, 0.0),
                jnp.maximum(mx, s))

    def score_far(j, c):
        s = scores(step_keys(j))
        for k in range(BIG):
            t = j * BIG + k
            tile = jnp.where(t < n0, s[k * KEY_TILE:(k + 1) * KEY_TILE, :], -jnp.inf)
            sc[t] = tile
            c = tally(c, tile)
        return c

    c = lax.fori_loop(0, steps_far, score_far, (zero, zero, jnp.full((KEY_TILE, tq), -F32_MAX, F32)))
    s_near = scores(near_keys)
    kpos = n0 * KEY_TILE + lax.broadcasted_iota(jnp.int32, (2 * KEY_TILE, tq), 0)
    qpos = q0 + lax.broadcasted_iota(jnp.int32, (2 * KEY_TILE, tq), 1)
    adm = (kpos < l_valid) & ((kpos >> 6) <= (qpos >> 6))
    s_near = jnp.where(adm, s_near, -jnp.inf)
    sc[n0] = s_near[:KEY_TILE, :]
    sc[n0 + 1] = s_near[KEY_TILE:, :]
    ge, gt, mx = tally(tally(c, s_near[:KEY_TILE, :]), s_near[KEY_TILE:, :])


    def count_ge(t):
        tb = jnp.broadcast_to(t, (KEY_TILE, tq))

        def body(j, acc):
            for k in range(BIG):
                acc = acc + jnp.where(sc[j * BIG + k] >= tb, 1.0, 0.0)
            return acc
        return jnp.sum(lax.fori_loop(0, steps_all, body, zero), axis=0, keepdims=True)

    ge0 = jnp.sum(ge, axis=0, keepdims=True)
    gt0 = jnp.sum(gt, axis=0, keepdims=True)
    rmax = jnp.max(mx, axis=0, keepdims=True)
    qrow = q0 + lax.broadcasted_iota(jnp.int32, (1, tq), 1)
    nadm = jnp.minimum(((qrow >> 6) + 1) * CHUNK, l_valid).astype(F32)
    small = nadm <= kf
    tie0 = (~small) & (gt0 < kf) & (ge0 >= kf)
    pos = (~small) & (gt0 >= kf)
    neg = (~small) & (ge0 < kf)
    lo0 = jnp.where(pos, F32_TINY, -F32_MAX)
    flo0 = jnp.where(pos, gt0, nadm)
    hi0 = jnp.where(neg, 0.0, jnp.inf)
    fhi0 = jnp.where(neg, ge0, 0.0)
    done0 = jnp.where(small | tie0 | (pos & (gt0 == kf)), 1.0, 0.0)
    thr0 = jnp.where(small, -F32_MAX, jnp.where(tie0, 0.0, F32_TINY))
    quota0 = jnp.where(tie0, kf - gt0, NO_QUOTA)

    def search_cond(c):
        return (c[0] < MAX_SEARCH_STEPS) & (c[1] > 0)

    def search_body(c):
        it, _, lo, flo, hi, fhi, done, thr, quota = c
        klo, khi = _f32_key(lo), _f32_key(hi)
        la = jnp.log(flo)
        frac = (la - jnp.log(kf)) / (la - jnp.log(jnp.maximum(fhi, 0.5)))
        interp = lo + (hi - lo) * jnp.where(it % 4 == 1, 0.5, frac)
        t_up = lo + (rmax - lo) * 0.5
        t_dn = hi - jnp.maximum(jnp.maximum(jnp.abs(rmax - hi), jnp.abs(hi)), F32_TINY)
        open_hi = hi == jnp.inf
        open_lo = lo <= -F32_MAX
        t = jnp.where(open_hi, t_up, jnp.where(open_lo, t_dn, interp))
        tk = jnp.minimum(jnp.maximum(_f32_key(t), klo + 1), khi - 1)
        tk = jnp.where(it % 4 == 3, klo + lax.shift_right_logical(khi - klo, 1), tk)
        t = _key_f32(tk)
        f = count_ge(t)
        act = done == 0.0
        hit = act & (f == kf)
        up = act & (f > kf)
        dn = act & (f < kf)
        lo = jnp.where(up, t, lo)
        flo = jnp.where(up, f, flo)
        hi = jnp.where(dn, t, hi)
        fhi = jnp.where(dn, f, fhi)
        adj = act & (~hit) & (_f32_key(lo) + 1 == _f32_key(hi))
        thr = jnp.where(hit, t, jnp.where(adj, lo, thr))
        quota = jnp.where(adj, kf - fhi, quota)
        done = jnp.where(hit | adj, 1.0, done)
        n_open = jnp.sum(1.0 - done)
        return it + 1, n_open, lo, flo, hi, fhi, done, thr, quota

    init = (jnp.int32(0), jnp.sum(1.0 - done0), lo0, flo0, hi0, fhi0, done0, thr0, quota0)
    _, _, _, _, _, _, _, thr, quota = lax.while_loop(search_cond, search_body, init)
    thr_b = jnp.broadcast_to(thr, (tq, KEY_TILE))
    quota_b = jnp.broadcast_to(quota, (tq, KEY_TILE))
    has_ties = jnp.min(quota) < NO_QUOTA

    @pl.when(jnp.logical_not(has_ties))
    def _():
        def body(j, c):
            for k in range(BIG):
                t = j * BIG + k
                sc[t] = jnp.where(sc[t] >= thr_b, 0.0, NEG)
            return c
        lax.fori_loop(0, steps_all, body, 0)

    @pl.when(has_ties)
    def _():
        ri = lax.broadcasted_iota(jnp.int32, (KEY_TILE, KEY_TILE), 0)
        ci = lax.broadcasted_iota(jnp.int32, (KEY_TILE, KEY_TILE), 1)
        lower = jnp.where(ri >= ci, 1.0, 0.0).astype(BF16)

        def body(j, seen):
            for k in range(BIG):
                t = j * BIG + k
                s = sc[t]
                tie = jnp.where(s == thr_b, 1.0, 0.0)
                rank = seen + _dot(lower, tie.astype(BF16))
                keep = (s > thr_b) | ((tie > 0.0) & (rank <= quota_b))
                sc[t] = jnp.where(keep, 0.0, NEG)
                seen = seen + jnp.sum(tie, axis=0, keepdims=True)
            return seen
        lax.fori_loop(0, steps_all, body, jnp.zeros((1, tq), F32))

    qt_all = q_ref[0].astype(F32).T
    zeros_half = jnp.zeros((HEAD_DIM, GRP * tq), F32)
    qa = []
    for g in range(N_KV_HEADS):
        piece = jnp.concatenate(
            [qt_all[(g * GRP + j) * HEAD_DIM:(g * GRP + j + 1) * HEAD_DIM, :] for j in range(GRP)],
            axis=1)
        halves = [piece, zeros_half] if g == 0 else [zeros_half, piece]
        qa.append(jnp.concatenate(halves, axis=0).astype(BF16))

    def logits(slot, keys, pens, m_old, extra=None):
        pen = jnp.concatenate(pens, axis=0)
        pen = jnp.concatenate([pen] * GRP, axis=1)
        kblk = k_ref[0, keys, :]
        m_new = []
        for g in range(N_KV_HEADS):
            s = _dot(kblk, qa[g]) + pen
            if extra is not None:
                s = s + extra[g]
            sbuf[slot, g, 0:pen.shape[0], :] = s
            m_new.append(jnp.maximum(m_old[g], jnp.max(s, axis=0, keepdims=True)))
        return tuple(m_new)

    def accumulate(slot, keys, nk, m_old, m_new, accs):
        out = []
        for g in range(N_KV_HEADS):
            p = jnp.exp2(sbuf[slot, g, 0:nk, :] - m_new[g])
            out.append(jnp.exp2(m_old[g] - m_new[g]) * accs[g]
                       + _dot(vt_ref[0, g * VT_ROWS:(g + 1) * VT_ROWS, keys], p.astype(BF16)))
        return tuple(out)

    last_far = jnp.maximum(steps_far - 1, 0)

    def far_keys(j):
        return step_keys(jnp.minimum(j, last_far))

    def far_pens(j):
        return [jnp.where(j * BIG + k < n0, sc[jnp.minimum(j * BIG + k, n0)], NEG) for k in range(BIG)]

    m_init = tuple(jnp.full((1, GRP * tq), NEG, F32) for _ in range(N_KV_HEADS))
    acc_init = tuple(jnp.zeros((VT_ROWS, GRP * tq), F32) for _ in range(N_KV_HEADS))
    m_first = logits(0, far_keys(0), far_pens(0), m_init)

    def stage_pair(i, c):
        m_old, m_cur, accs = c
        a = 2 * i + 1
        m_a = logits(1, far_keys(a), far_pens(a), m_cur)
        accs = accumulate(0, far_keys(a - 1), kw, m_old, m_cur, accs)
        m_b = logits(0, far_keys(a + 1), far_pens(a + 1), m_a)
        accs = accumulate(1, far_keys(a), kw, m_cur, m_a, accs)
        return m_a, m_b, accs

    pairs = steps_far // 2
    m_old, m_cur, accs = lax.fori_loop(0, pairs, stage_pair, (m_init, m_first, acc_init))
    m_last = logits(1, near_keys, [sc[n0], sc[n0 + 1]], m_cur, extra=(bias[variant, 0], bias[variant, 1]))
    accs = accumulate(0, far_keys(2 * pairs), kw, m_old, m_cur, accs)
    accs = accumulate(1, near_keys, 2 * KEY_TILE, m_cur, m_last, accs)
    heads = []
    for g in range(N_KV_HEADS):
        acc = accs[g]
        og = acc[:HEAD_DIM, :] / acc[HEAD_DIM:HEAD_DIM + 1, :]
        heads += [og[:, j * tq:(j + 1) * tq] for j in range(GRP)]
    o_ref[0] = jnp.concatenate(heads, axis=0).T.astype(BF16)


def _dsa(rel_bias, q, qi, sgn, k, vt, ki, *, q_pos0, l_valid):
    b, t, _ = q.shape
    lk = k.shape[1]
    tq = KEY_TILE
    assert lk % (BIG * KEY_TILE) == 0 and q_pos0 % KEY_TILE == 0 and t % tq == 0
    assert q_pos0 + t <= lk
    nt = lk // KEY_TILE
    row = lambda width: pl.BlockSpec((1, tq, width), lambda bi, i: (bi, i, 0))
    return pl.pallas_call(
        functools.partial(_dsa_kernel, q_pos0=q_pos0, l_valid=l_valid),
        grid=(b, t // tq),
        in_specs=[pl.BlockSpec(memory_space=pltpu.SMEM),
                  row(D_ATT), row(N_IDX_HEADS * D_IDX),
                  pl.BlockSpec((1, 8, tq), lambda bi, i: (bi, 0, i)),
                  _resident((1, lk, N_KV_HEADS * HEAD_DIM), lambda bi, i: (bi, 0, 0)),
                  _resident((1, N_KV_HEADS * VT_ROWS, lk), lambda bi, i: (bi, 0, 0)),
                  _resident((1, lk, D_IDX), lambda bi, i: (bi, 0, 0))],
        out_specs=row(D_ATT),
        out_shape=jax.ShapeDtypeStruct((b, t, D_ATT), BF16),
        scratch_shapes=[pltpu.VMEM((nt, KEY_TILE, tq), F32),
                        pltpu.VMEM((2, N_KV_HEADS, 2 * KEY_TILE, GRP * tq), F32),
                        pltpu.VMEM((2, N_KV_HEADS, BIG * KEY_TILE, GRP * tq), F32)],
        compiler_params=pltpu.CompilerParams(
            dimension_semantics=("arbitrary", "arbitrary"), vmem_limit_bytes=VMEM_LIMIT),
        name="dsa",
    )(rel_bias, q, qi, sgn, k, vt, ki)


def _split3(x):
    hi = x.astype(BF16)
    r = x - hi.astype(F32)
    mid = r.astype(BF16)
    lo = (r - mid.astype(F32)).astype(BF16)
    return hi, mid, lo


def _ssd_kernel(xs_ref, bm_ref, cm_ref, dt_ref, z_ref, alog_ref, dsk_ref, nw_ref, h0_ref,
                y_ref, h_ref, ht, *, q_in):
    c = pl.program_id(1)
    nc = pl.num_programs(1)

    @pl.when(c == 0)
    def _():
        for hh in range(SSM_HEADS):
            ht[hh] = h0_ref[0, hh].T

    def rows(ref):
        x = ref[0]
        if q_in == SSD_Q:
            return x
        return jnp.concatenate([x, jnp.zeros((SSD_Q - q_in, x.shape[1]), x.dtype)], axis=0)

    xs, bm, cm, dt, z = rows(xs_ref), rows(bm_ref), rows(cm_ref), rows(dt_ref), rows(z_ref)
    a = -jnp.exp(alog_ref[...])
    da = dt * a
    ri = lax.broadcasted_iota(jnp.int32, (SSD_Q, SSD_Q), 0)
    ci = lax.broadcasted_iota(jnp.int32, (SSD_Q, SSD_Q), 1)
    causal = ri >= ci
    tril = jnp.where(causal, 1.0, 0.0).astype(BF16)
    hi, mid, lo = _split3(da)
    acum = _dot(tril, hi) + _dot(tril, mid) + _dot(tril, lo)
    acum_t = acum.T

    for g in range(SSM_GROUPS):
        bm_g = bm[:, g * SSM_STATE:(g + 1) * SSM_STATE]
        cm_g = cm[:, g * SSM_STATE:(g + 1) * SSM_STATE].astype(BF16)
        bm_t = bm_g.T.astype(BF16)
        cb = _dot(cm_g, bm_t)
        gs = []
        ssq = jnp.zeros((SSD_Q, 1), F32)
        for r in range(SSM_HPG):
            hh = g * SSM_HPG + r
            lanes = slice(hh * SSM_HEAD_DIM, (hh + 1) * SSM_HEAD_DIM)
            xh = xs[:, lanes]
            col = acum[:, hh:hh + 1]
            rowv = acum_t[hh:hh + 1, :]
            decay = jnp.exp(jnp.where(causal, col - rowv, NEG))
            xdt = xh * dt[:, hh:hh + 1]
            y = _dot((cb * decay).astype(BF16), xdt.astype(BF16))
            h_prev = ht[hh]
            y = y + jnp.exp(col) * _dot(cm_g, h_prev.astype(BF16))
            last = acum[SSD_Q - 1:SSD_Q, hh:hh + 1]
            xw = (xdt * jnp.exp(last - col)).astype(BF16)
            ht[hh] = h_prev * jnp.exp(last) + _dot(bm_t, xw)
            y = y + xh * dsk_ref[:, lanes]
            gate = y * _silu(z[:, lanes])
            ssq = ssq + jnp.sum(gate * gate, axis=1, keepdims=True)
            gs.append(gate)
        scale = lax.rsqrt(ssq / float(SSM_HPG * SSM_HEAD_DIM) + NORM_EPS)
        for r in range(SSM_HPG):
            hh = g * SSM_HPG + r
            lanes = slice(hh * SSM_HEAD_DIM, (hh + 1) * SSM_HEAD_DIM)
            y_ref[0, :, lanes] = (gs[r] * scale * nw_ref[:, lanes])[:q_in].astype(BF16)

    @pl.when(c == nc - 1)
    def _():
        for hh in range(SSM_HEADS):
            h_ref[0, hh] = ht[hh].T


def _ssd(xs, bm, cm, dt, z, alog, dsk, nw, h0, *, q_in):
    b, t, _ = xs.shape
    row = lambda width: pl.BlockSpec((1, q_in, width), lambda bi, c: (bi, c, 0))
    full = lambda a: _resident(a.shape, lambda bi, c: (0,) * a.ndim)
    hspec = pl.BlockSpec((1, SSM_HEADS, SSM_HEAD_DIM, SSM_STATE), lambda bi, c: (bi, 0, 0, 0))
    return pl.pallas_call(
        functools.partial(_ssd_kernel, q_in=q_in),
        grid=(b, t // q_in),
        in_specs=[row(D_SSM), row(256), row(256), row(LANE), row(D_SSM),
                  full(alog), full(dsk), full(nw), hspec],
        out_specs=(row(D_SSM), hspec),
        out_shape=(jax.ShapeDtypeStruct((b, t, D_SSM), BF16),
                   jax.ShapeDtypeStruct((b, SSM_HEADS, SSM_HEAD_DIM, SSM_STATE), F32)),
        scratch_shapes=[pltpu.VMEM((SSM_HEADS, SSM_STATE, SSM_HEAD_DIM), F32)],
        compiler_params=pltpu.CompilerParams(
            dimension_semantics=("arbitrary", "arbitrary"), vmem_limit_bytes=VMEM_LIMIT),
        name="ssd",
    )(xs, bm, cm, dt, z, alog, dsk, nw, h0)


def _mix_ffn_kernel(x_ref, ya_ref, yb_ref, yc_ref, wo_ref, npost_ref, nfpre_ref, wg_ref, wu_ref,
                    wd_ref, cfw_ref, cfb_ref, nfpost_ref, buff_ref, o_ref, buff_out, eg, *, tm):
    i = pl.program_id(1)

    @pl.when(i == 0)
    def _():
        eg[CARRY - 2:CARRY, :] = buff_ref[0]

    mix = (_dot(ya_ref[0], wo_ref[0:D_CONV_MIX, :])
           + _dot(yb_ref[0], wo_ref[D_CONV_MIX:D_CONV_MIX + D_ATT, :])
           + _dot(yc_ref[0], wo_ref[D_CONV_MIX + D_ATT:, :]))
    x1 = x_ref[0] + _rms(mix, npost_ref[...])
    u = _rms(x1, nfpre_ref[...]).astype(BF16)
    eg[CARRY:CARRY + tm, :] = _dot(u, wg_ref[...])
    gc = eg[CARRY - 2:CARRY - 2 + tm, :] * cfw_ref[0:1, :]
    for j in range(1, CONV_F_W):
        gc = gc + eg[CARRY - 2 + j:CARRY - 2 + j + tm, :] * cfw_ref[j:j + 1, :]
    hid = (_silu(gc + cfb_ref[...]) * _dot(u, wu_ref[...])).astype(BF16)
    o_ref[0] = x1 + _rms(_dot(hid, wd_ref[...]), nfpost_ref[...])
    tail = eg[CARRY + tm - 2:CARRY + tm, :]
    buff_out[0] = tail
    eg[CARRY - 2:CARRY, :] = tail


def _mix_ffn(x, ya, yb, yc, wo, npost, nfpre, wg, wu, wd, cfw, cfb, nfpost, buff, *, tm):
    b, t, _ = x.shape
    row = lambda width: pl.BlockSpec((1, tm, width), lambda bi, i: (bi, i, 0))
    full = lambda a: _resident(a.shape, lambda bi, i: (0,) * a.ndim)
    state = pl.BlockSpec((1, CONV_F_W - 1, D_FF), lambda bi, i: (bi, 0, 0))
    return pl.pallas_call(
        functools.partial(_mix_ffn_kernel, tm=tm),
        grid=(b, t // tm),
        in_specs=[row(D_MODEL), row(D_CONV_MIX), row(D_ATT), row(D_SSM), full(wo), full(npost),
                  full(nfpre), full(wg), full(wu), full(wd), full(cfw), full(cfb), full(nfpost),
                  state],
        out_specs=(row(D_MODEL), state),
        out_shape=(jax.ShapeDtypeStruct((b, t, D_MODEL), F32),
                   jax.ShapeDtypeStruct((b, CONV_F_W - 1, D_FF), F32)),
        scratch_shapes=[pltpu.VMEM((CARRY + tm, D_FF), F32)],
        compiler_params=pltpu.CompilerParams(
            dimension_semantics=("arbitrary", "arbitrary"), vmem_limit_bytes=VMEM_LIMIT),
        name="mix_ffn",
    )(x, ya, yb, yc, wo, npost, nfpre, wg, wu, wd, cfw, cfb, nfpost, buff)


def _pack_keys_kernel(ck_ref, cv_ref, cki_ref, k_ref, v_ref, ki_ref, kb_ref, vt_ref, kib_ref, *, past_steps):
    j = pl.program_id(1)
    rows = kb_ref.shape[1]

    def pick(cache_ref, new_ref):
        new = new_ref[0]
        new = jnp.concatenate([new, jnp.zeros((rows - new.shape[0], new.shape[1]), F32)], axis=0)
        return jnp.where(j < past_steps, cache_ref[0], new)

    kb_ref[0] = pick(ck_ref, k_ref).astype(BF16)
    vt_ref[0] = _augment_vt(pick(cv_ref, v_ref).T).astype(BF16)
    kib_ref[0] = pick(cki_ref, ki_ref).astype(BF16)


def _pack_keys(ck, cv, cki, k, v, ki):
    b, past, _ = ck.shape
    t = k.shape[1]
    rows = BIG * KEY_TILE
    assert past % rows == 0 and t <= rows
    past_steps = past // rows
    lk = past + rows
    cache = lambda width: pl.BlockSpec((1, rows, width),
                                       lambda bi, j: (bi, jnp.minimum(j, past_steps - 1), 0))
    new = lambda width: pl.BlockSpec((1, t, width), lambda bi, j: (bi, 0, 0))
    out = lambda width: pl.BlockSpec((1, rows, width), lambda bi, j: (bi, j, 0))
    return pl.pallas_call(
        functools.partial(_pack_keys_kernel, past_steps=past_steps),
        grid=(b, past_steps + 1),
        in_specs=[cache(128), cache(128), cache(D_IDX), new(128), new(128), new(D_IDX)],
        out_specs=(out(128), pl.BlockSpec((1, N_KV_HEADS * VT_ROWS, rows), lambda bi, j: (bi, 0, j)),
                   out(D_IDX)),
        out_shape=(jax.ShapeDtypeStruct((b, lk, 128), BF16),
                   jax.ShapeDtypeStruct((b, N_KV_HEADS * VT_ROWS, lk), BF16),
                   jax.ShapeDtypeStruct((b, lk, D_IDX), BF16)),
        compiler_params=pltpu.CompilerParams(
            dimension_semantics=("arbitrary", "arbitrary"), vmem_limit_bytes=VMEM_LIMIT),
        name="pack_keys",
    )(ck, cv, cki, k, v, ki)


def _layer(x, p, rel_bias, buf_a, buf_s, h0, buf_f, kv_past, *, tm):
    b, t, _ = x.shape
    outs = _in_proj(x, p['norm_mix_pre'], p['w_in'], p['conv_a_w'], p['conv_ssm_w'], p['conv_ssm_b'],
                    p['dt_bias'], buf_a, buf_s, tm=tm, key_side=kv_past is None)
    (ya, q, k, v, qi, ki, sgn, z, xs, bm, cm, dt, buf_a_new, buf_s_new) = outs[:14]
    if kv_past is None:
        kb, vt, kib = outs[14:]
        yb = _dsa(rel_bias, q, qi, sgn, kb, vt, kib, q_pos0=0, l_valid=t)
    else:
        ck, cv, cki = kv_past
        past = ck.shape[1]
        qpad = lambda a: jnp.pad(a, ((0, 0), (0, -t % KEY_TILE), (0, 0)))
        kb, vt, kib = _pack_keys(ck.reshape(b, past, -1), cv.reshape(b, past, -1), cki, k, v, ki)
        yb = _dsa(rel_bias, qpad(q), qpad(qi), sgn, kb, vt, kib, q_pos0=past, l_valid=past + t)[:, :t]
    yc, h_new = _ssd(xs, bm, cm, dt, z, p['a_log'], p['d_skip'], p['ssm_norm_w'],
                     h0.reshape(b, SSM_HEADS, SSM_HEAD_DIM, SSM_STATE), q_in=min(t, SSD_Q))
    x_new, buf_f_new = _mix_ffn(x, ya, yb, yc, p['w_out'], p['norm_mix_post'], p['norm_ffn_pre'],
                                p['w_gate'], p['w_up'], p['w_down'], p['conv_ffn_w'],
                                p['conv_ffn_b'], p['norm_ffn_post'], buf_f, tm=tm)
    st = (k.reshape(b, t, N_KV_HEADS, HEAD_DIM), v.reshape(b, t, N_KV_HEADS, HEAD_DIM), ki,
          buf_a_new, buf_s_new, h_new.reshape(b, SSM_GROUPS, SSM_HPG, SSM_HEAD_DIM, SSM_STATE),
          buf_f_new)
    return x_new, st


def _pack_w_in(w):
    d = w.shape[0]
    return jnp.concatenate(
        [w[:, :D_IN_HEAD], jnp.zeros((d, C_Z - D_IN_HEAD), w.dtype), w[:, D_IN_HEAD:D_IN_MID],
         w[:, D_IN_MID:], jnp.zeros((d, D_IN_PACKED - C_DT - SSM_HEADS), w.dtype)],
        axis=1).astype(BF16)


def _lane_pad(v):
    return jnp.pad(v, (0, LANE - v.shape[0]))[None, :]


def kernel(x_prompt, x_sample, cache_k, cache_v, cache_kidx, state_conv_a, state_conv_ssm, state_ssm, state_conv_ffn, rel_bias, norm_mix_pre, norm_mix_post, norm_ffn_pre, norm_ffn_post, w_in, conv_a_w, conv_ssm_w, conv_ssm_b, dt_bias, a_log, d_skip, ssm_norm_w, w_out, w_gate, w_up, conv_ffn_w, conv_ffn_b, w_down):
    depth = w_in.shape[0]
    bp = x_prompt.shape[0]
    yp, ys = x_prompt, x_sample
    outs_p, outs_s = [], []
    for l in range(depth):
        p = {'norm_mix_pre': norm_mix_pre[l][None], 'norm_mix_post': norm_mix_post[l][None],
             'norm_ffn_pre': norm_ffn_pre[l][None], 'norm_ffn_post': norm_ffn_post[l][None],
             'w_in': _pack_w_in(w_in[l]), 'conv_a_w': conv_a_w[l], 'conv_ssm_w': conv_ssm_w[l],
             'conv_ssm_b': conv_ssm_b[l][None], 'dt_bias': _lane_pad(dt_bias[l]),
             'a_log': _lane_pad(a_log[l]), 'd_skip': jnp.repeat(d_skip[l], SSM_HEAD_DIM)[None],
             'ssm_norm_w': ssm_norm_w[l][None], 'w_out': w_out[l].astype(BF16),
             'w_gate': w_gate[l].astype(BF16), 'w_up': w_up[l].astype(BF16),
             'conv_ffn_w': conv_ffn_w[l], 'conv_ffn_b': conv_ffn_b[l][None],
             'w_down': w_down[l].astype(BF16)}
        yp, st_p = _layer(
            yp, p, rel_bias,
            jnp.zeros((bp, CONV_A_W - 1, D_CONV_MIX), F32),
            jnp.zeros((bp, SSM_CONV_W - 1, D_XBC), F32),
            jnp.zeros((bp, SSM_GROUPS, SSM_HPG, SSM_HEAD_DIM, SSM_STATE), F32),
            jnp.zeros((bp, CONV_F_W - 1, D_FF), F32),
            None, tm=256)
        ys, st_s = _layer(
            ys, p, rel_bias, state_conv_a[l], state_conv_ssm[l], state_ssm[l], state_conv_ffn[l],
            (cache_k[l], cache_v[l], cache_kidx[l]), tm=ys.shape[1])
        outs_p.append(st_p)
        outs_s.append(st_s)

    def stack(outs, i):
        return jnp.stack([o[i] for o in outs], axis=0)

    res = [yp, ys]
    for i in range(7):
        res.append(stack(outs_p, i))
        res.append(stack(outs_s, i))
    return tuple(res)
```

```python
import functools

import jax
import jax.numpy as jnp
from jax import lax
from jax.experimental import pallas as pl
from jax.experimental.pallas import tpu as pltpu

F32 = jnp.float32
BF16 = jnp.bfloat16

D_MODEL = 1024
CHUNK = 64
D_CONV_MIX = 256
CONV_A_W = 3
N_HEADS = 6
N_KV_HEADS = 2
HEAD_DIM = 64
D_ATT = N_HEADS * HEAD_DIM
GRP = N_HEADS // N_KV_HEADS
N_IDX_HEADS = 4
D_IDX = 64
TOPK = 256
N_BUCKETS = 32
SSM_HEADS = 6
SSM_HEAD_DIM = 64
D_SSM = SSM_HEADS * SSM_HEAD_DIM
SSM_GROUPS = 2
SSM_HPG = SSM_HEADS // SSM_GROUPS
SSM_STATE = 128
SSM_CONV_W = 4
D_XBC = D_SSM + 2 * SSM_GROUPS * SSM_STATE
D_FF = 2816
CONV_F_W = 3
NORM_EPS = 1e-6

LANE = 128
CARRY = 8
SSD_Q = 128
KEY_TILE = 128
BIG = 4
VMEM_LIMIT = 56 * 1024 * 1024
NEG = -1e30
LOG2E = 1.4426950408889634
F32_MAX = 3.4028234663852886e38
F32_TINY = 1.1754943508222875e-38
NO_QUOTA = 1e9
MAX_SEARCH_STEPS = 256

C_AB, C_AC, C_AH = 0, 256, 512
C_Q = 768
C_K = 1152
C_V = 1280
C_QI = 1408
C_KI = 1664
C_Z = 1792
C_XBC = 2176
C_DT = 3072
D_IN_PACKED = 3200
D_IN = 3018
D_IN_HEAD = 1732
DT_LANE0 = LANE - SSM_HEADS
W_IN_TILE_COLS = ([c * LANE for c in range(C_Z // LANE)]
                  + [D_IN_HEAD + c * LANE for c in range((C_DT - C_Z) // LANE)] + [D_IN - LANE])


def _rms(x, w):
    return x * lax.rsqrt(jnp.mean(x * x, axis=-1, keepdims=True) + NORM_EPS) * w


def _dot(a, b):
    return jnp.dot(a, b, preferred_element_type=F32)


def _silu(x):
    return x * jax.nn.sigmoid(x)


VT_ROWS = HEAD_DIM + 16


def _augment_vt(vt):
    extra_shape = vt.shape[:-2] + (VT_ROWS - HEAD_DIM, vt.shape[-1])
    row = lax.broadcasted_iota(jnp.int32, extra_shape, len(extra_shape) - 2)
    extra = jnp.where(row == 0, 1.0, 0.0).astype(vt.dtype)
    parts = []
    for g in range(N_KV_HEADS):
        parts += [vt[..., g * HEAD_DIM:(g + 1) * HEAD_DIM, :], extra]
    return jnp.concatenate(parts, axis=-2)


def _resident(shape, index_map):
    return pl.BlockSpec(shape, index_map, pipeline_mode=pl.Buffered(1))


def _in_proj_kernel(x_ref, nw_ref, w_ref, caw_ref, csw_ref, csb_ref, dtb_ref, bufa_ref, bufs_ref,
                    ya_ref, q_ref, k_ref, v_ref, qi_ref, ki_ref, sgn_ref, z_ref, xs_ref, bm_ref,
                    cm_ref, dt_ref, bufa_out, bufs_out, *rest, tm, key_side):
    if key_side:
        kb_ref, vt_ref, kib_ref, ea, es = rest
    else:
        ea, es = rest
    i = pl.program_id(1)
    u = _rms(x_ref[0], nw_ref[...]).astype(BF16)

    def proj(c0, width):
        return _dot(u, w_ref[:, c0:c0 + width])

    @pl.when(i == 0)
    def _():
        ea[CARRY - 2:CARRY, :] = bufa_ref[0]
        es[CARRY - 3:CARRY, :] = bufs_ref[0]

    ea[CARRY:CARRY + tm, :] = proj(C_AC, D_CONV_MIX) * proj(C_AH, D_CONV_MIX)
    conv = ea[CARRY - 2:CARRY - 2 + tm, :] * caw_ref[0:1, :]
    for j in range(1, CONV_A_W):
        conv = conv + ea[CARRY - 2 + j:CARRY - 2 + j + tm, :] * caw_ref[j:j + 1, :]
    ya_ref[0] = (proj(C_AB, D_CONV_MIX) * conv).astype(BF16)
    tail_a = ea[CARRY + tm - 2:CARRY + tm, :]
    bufa_out[0] = tail_a
    ea[CARRY - 2:CARRY, :] = tail_a

    q_ref[0] = (proj(C_Q, D_ATT) * (HEAD_DIM ** -0.5 * LOG2E)).astype(BF16)
    k = proj(C_K, N_KV_HEADS * HEAD_DIM)
    v = proj(C_V, N_KV_HEADS * HEAD_DIM)
    kiw = proj(C_KI, LANE)
    k_ref[0] = k
    v_ref[0] = v
    ki_ref[0] = kiw[:, :D_IDX]
    wi = kiw[:, D_IDX:D_IDX + N_IDX_HEADS]
    kiw_rows = kiw if tm >= LANE else jnp.concatenate([kiw, jnp.zeros((LANE - tm, LANE), F32)], axis=0)
    sgn_ref[0] = jnp.where(kiw_rows.T[D_IDX:D_IDX + 8, :] >= 0.0, 1.0, -1.0).astype(F32)
    if key_side:
        kb_ref[0] = k.astype(BF16)
        vt_ref[0] = _augment_vt(v.T).astype(BF16)
        kib_ref[0] = kiw[:, :D_IDX].astype(BF16)
    wabs = jnp.abs(wi) * (D_IDX ** -0.5 * N_IDX_HEADS ** -0.5)
    qi = proj(C_QI, N_IDX_HEADS * D_IDX)
    for h in range(N_IDX_HEADS):
        qi_ref[0, :, h * D_IDX:(h + 1) * D_IDX] = (
            qi[:, h * D_IDX:(h + 1) * D_IDX] * wabs[:, h:h + 1]).astype(BF16)

    z_ref[0] = proj(C_Z, D_SSM)
    es[CARRY:CARRY + tm, :] = proj(C_XBC, D_XBC)
    xc = es[CARRY - 3:CARRY - 3 + tm, :] * csw_ref[0:1, :]
    for j in range(1, SSM_CONV_W):
        xc = xc + es[CARRY - 3 + j:CARRY - 3 + j + tm, :] * csw_ref[j:j + 1, :]
    xc = _silu(xc + csb_ref[...])
    xs_ref[0] = xc[:, :D_SSM]
    bm_ref[0] = xc[:, D_SSM:D_SSM + SSM_GROUPS * SSM_STATE]
    cm_ref[0] = xc[:, D_SSM + SSM_GROUPS * SSM_STATE:]
    tail_s = es[CARRY + tm - 3:CARRY + tm, :]
    bufs_out[0] = tail_s
    es[CARRY - 3:CARRY, :] = tail_s
    dtr = proj(C_DT, LANE) + dtb_ref[...]
    dt_ref[0] = jnp.maximum(dtr, 0.0) + jnp.log1p(jnp.exp(-jnp.abs(dtr)))


def _in_proj(x, nw, w_packed, caw, csw, csb, dtb, bufa, bufs, *, tm, key_side):
    b, t, _ = x.shape
    tq = max(tm, LANE)
    grid = (b, t // tm)
    row = lambda width: pl.BlockSpec((1, tm, width), lambda bi, i: (bi, i, 0))
    full = lambda a: _resident(a.shape, lambda bi, i: (0,) * a.ndim)
    state = lambda r, width: pl.BlockSpec((1, r, width), lambda bi, i: (bi, 0, 0))
    out_shapes = (
        jax.ShapeDtypeStruct((b, t, D_CONV_MIX), BF16),
        jax.ShapeDtypeStruct((b, t, D_ATT), BF16),
        jax.ShapeDtypeStruct((b, t, 128), F32),
        jax.ShapeDtypeStruct((b, t, 128), F32),
        jax.ShapeDtypeStruct((b, t, 256), BF16),
        jax.ShapeDtypeStruct((b, t, D_IDX), F32),
        jax.ShapeDtypeStruct((b, 8, t // tm * tq), F32),
        jax.ShapeDtypeStruct((b, t, D_SSM), F32),
        jax.ShapeDtypeStruct((b, t, D_SSM), F32),
        jax.ShapeDtypeStruct((b, t, 256), F32),
        jax.ShapeDtypeStruct((b, t, 256), F32),
        jax.ShapeDtypeStruct((b, t, LANE), F32),
        jax.ShapeDtypeStruct((b, CONV_A_W - 1, D_CONV_MIX), F32),
        jax.ShapeDtypeStruct((b, SSM_CONV_W - 1, D_XBC), F32),
    )
    out_specs = (row(D_CONV_MIX), row(D_ATT), row(128), row(128), row(256), row(D_IDX),
                 pl.BlockSpec((1, 8, tq), lambda bi, i: (bi, 0, i)),
                 row(D_SSM), row(D_SSM), row(256), row(256), row(LANE),
                 state(CONV_A_W - 1, D_CONV_MIX), state(SSM_CONV_W - 1, D_XBC))
    if key_side:
        out_shapes += (jax.ShapeDtypeStruct((b, t, 128), BF16),
                       jax.ShapeDtypeStruct((b, N_KV_HEADS * VT_ROWS, t), BF16),
                       jax.ShapeDtypeStruct((b, t, D_IDX), BF16))
        out_specs += (row(128), pl.BlockSpec((1, N_KV_HEADS * VT_ROWS, tm), lambda bi, i: (bi, 0, i)), row(D_IDX))
    in_specs = [row(D_MODEL), full(nw), full(w_packed), full(caw), full(csw), full(csb), full(dtb),
                state(CONV_A_W - 1, D_CONV_MIX), state(SSM_CONV_W - 1, D_XBC)]
    return pl.pallas_call(
        functools.partial(_in_proj_kernel, tm=tm, key_side=key_side),
        grid=grid, in_specs=in_specs, out_specs=out_specs, out_shape=out_shapes,
        scratch_shapes=[pltpu.VMEM((CARRY + tm, D_CONV_MIX), F32),
                        pltpu.VMEM((CARRY + tm, D_XBC), F32)],
        compiler_params=pltpu.CompilerParams(
            dimension_semantics=("arbitrary", "arbitrary"), vmem_limit_bytes=VMEM_LIMIT),
        name="in_proj",
    )(x, nw, w_packed, caw, csw, csb, dtb, bufa, bufs)


def _t5_bucket(rel):
    n = jnp.abs(rel)
    large = jnp.full(rel.shape, 8, jnp.int32)
    for brk in (12, 16, 23, 32, 46, 64, 91):
        large = large + jnp.where(n >= brk, 1, 0)
    return jnp.where(rel > 0, N_BUCKETS // 2, 0) + jnp.where(n < 8, n, large)


def _f32_key(x):
    bits = lax.bitcast_convert_type(x, jnp.int32)
    return bits ^ ((bits >> 31) & jnp.int32(0x7FFFFFFF))


def _key_f32(key):
    return lax.bitcast_convert_type(key ^ ((key >> 31) & jnp.int32(0x7FFFFFFF)), F32)


def _dsa_kernel(relb_ref, q_ref, qi_ref, sgn_ref, k_ref, vt_ref, ki_ref, o_ref, sc, bias, sbuf,
                *, q_pos0, l_valid):
    tq = KEY_TILE
    i = pl.program_id(1)
    q0 = q_pos0 + i * tq
    qt = q0 // KEY_TILE
    n0 = jnp.maximum(qt - 1, 0)
    variant = jnp.minimum(qt, 1)
    steps_far = (n0 + BIG - 1) // BIG
    steps_all = (n0 + 2 + BIG - 1) // BIG
    kw = BIG * KEY_TILE
    kf = float(TOPK)

    def step_keys(j):
        return pl.ds(pl.multiple_of(j * kw, kw), kw)

    near_keys = pl.ds(pl.multiple_of(n0 * KEY_TILE, KEY_TILE), 2 * KEY_TILE)

    @pl.when(i == 0)
    def _():
        sc[...] = jnp.full(sc.shape, -jnp.inf, F32)
        kk = lax.broadcasted_iota(jnp.int32, (2 * KEY_TILE, tq), 0)
        r = lax.broadcasted_iota(jnp.int32, (2 * KEY_TILE, tq), 1)
        for var in range(2):
            bucket = _t5_bucket(kk - var * KEY_TILE - r)
            for h in range(N_HEADS):
                tab = jnp.zeros((2 * KEY_TILE, tq), F32)
                for bk in range(N_BUCKETS):
                    tab = jnp.where(bucket == bk, relb_ref[bk, h], tab)
                g, j = divmod(h, GRP)
                bias[var, g, :, j * tq:(j + 1) * tq] = (tab - relb_ref[N_BUCKETS // 2 - 1, h]) * LOG2E

    qit = qi_ref[0].astype(F32).T
    qis = jnp.concatenate([qit[h * D_IDX:(h + 1) * D_IDX, :] for h in range(N_IDX_HEADS)],
                          axis=1).astype(BF16)
    sgn = sgn_ref[0]

    def scores(keys):
        s = jnp.maximum(_dot(ki_ref[0, keys, :], qis), 0.0)
        tot = s[:, 0:tq] * sgn[0:1, :]
        for h in range(1, N_IDX_HEADS):
            tot = tot + s[:, h * tq:(h + 1) * tq] * sgn[h:h + 1, :]
        return tot

    zero = jnp.zeros((KEY_TILE, tq), F32)

    def tally(c, s):
        ge, gt, mx = c
        return (ge + jnp.where(s >= 0.0, 1.0, 0.0), gt + jnp.where(s >= F32_TINY, 1.0, 0.0),
                jnp.maximum(mx, s))

    def score_far(j, c):
        s = scores(step_keys(j))
        for k in range(BIG):
            t = j * BIG + k
            tile = jnp.where(t < n0, s[k * KEY_TILE:(k + 1) * KEY_TILE, :], -jnp.inf)
            sc[t] = tile
            c = tally(c, tile)
        return c

    c = lax.fori_loop(0, steps_far, score_far, (zero, zero, jnp.full((KEY_TILE, tq), -F32_MAX, F32)))
    s_near = scores(near_keys)
    kpos = n0 * KEY_TILE + lax.broadcasted_iota(jnp.int32, (2 * KEY_TILE, tq), 0)
    qpos = q0 + lax.broadcasted_iota(jnp.int32, (2 * KEY_TILE, tq), 1)
    adm = (kpos < l_valid) & ((kpos >> 6) <= (qpos >> 6))
    s_near = jnp.where(adm, s_near, -jnp.inf)
    sc[n0] = s_near[:KEY_TILE, :]
    sc[n0 + 1] = s_near[KEY_TILE:, :]
    ge, gt, mx = tally(tally(c, s_near[:KEY_TILE, :]), s_near[KEY_TILE:, :])


    def count_ge(t):
        tb = jnp.broadcast_to(t, (KEY_TILE, tq))

        def body(j, acc):
            for k in range(BIG):
                acc = acc + jnp.where(sc[j * BIG + k] >= tb, 1.0, 0.0)
            return acc
        return jnp.sum(lax.fori_loop(0, steps_all, body, zero), axis=0, keepdims=True)

    ge0 = jnp.sum(ge, axis=0, keepdims=True)
    gt0 = jnp.sum(gt, axis=0, keepdims=True)
    rmax = jnp.max(mx, axis=0, keepdims=True)
    qrow = q0 + lax.broadcasted_iota(jnp.int32, (1, tq), 1)
    nadm = jnp.minimum(((qrow >> 6) + 1) * CHUNK, l_valid).astype(F32)
    small = nadm <= kf
    tie0 = (~small) & (gt0 < kf) & (ge0 >= kf)
    pos = (~small) & (gt0 >= kf)
    neg = (~small) & (ge0 < kf)
    lo0 = jnp.where(pos, F32_TINY, -F32_MAX)
    flo0 = jnp.where(pos, gt0, nadm)
    hi0 = jnp.where(neg, 0.0, jnp.inf)
    fhi0 = jnp.where(neg, ge0, 0.0)
    done0 = jnp.where(small | tie0 | (pos & (gt0 == kf)), 1.0, 0.0)
    thr0 = jnp.where(small, -F32_MAX, jnp.where(tie0, 0.0, F32_TINY))
    quota0 = jnp.where(tie0, kf - gt0, NO_QUOTA)

    def search_cond(c):
        return (c[0] < MAX_SEARCH_STEPS) & (c[1] > 0)

    def search_body(c):
        it, _, lo, flo, hi, fhi, done, thr, quota = c
        klo, khi = _f32_key(lo), _f32_key(hi)
        la = jnp.log(flo)
        frac = (la - jnp.log(kf)) / (la - jnp.log(jnp.maximum(fhi, 0.5)))
        interp = lo + (hi - lo) * jnp.where(it % 4 == 1, 0.5, frac)
        t_up = lo + (rmax - lo) * 0.5
        t_dn = hi - jnp.maximum(jnp.maximum(jnp.abs(rmax - hi), jnp.abs(hi)), F32_TINY)
        open_hi = hi == jnp.inf
        open_lo = lo <= -F32_MAX
        t = jnp.where(open_hi, t_up, jnp.where(open_lo, t_dn, interp))
        tk = jnp.minimum(jnp.maximum(_f32_key(t), klo + 1), khi - 1)
        tk = jnp.where(it % 4 == 3, klo + lax.shift_right_logical(khi - klo, 1), tk)
        t = _key_f32(tk)
        f = count_ge(t)
        act = done == 0.0
        hit = act & (f == kf)
        up = act & (f > kf)
        dn = act & (f < kf)
        lo = jnp.where(up, t, lo)
        flo = jnp.where(up, f, flo)
        hi = jnp.where(dn, t, hi)
        fhi = jnp.where(dn, f, fhi)
        adj = act & (~hit) & (_f32_key(lo) + 1 == _f32_key(hi))
        thr = jnp.where(hit, t, jnp.where(adj, lo, thr))
        quota = jnp.where(adj, kf - fhi, quota)
        done = jnp.where(hit | adj, 1.0, done)
        n_open = jnp.sum(1.0 - done)
        return it + 1, n_open, lo, flo, hi, fhi, done, thr, quota

    init = (jnp.int32(0), jnp.sum(1.0 - done0), lo0, flo0, hi0, fhi0, done0, thr0, quota0)
    _, _, _, _, _, _, _, thr, quota = lax.while_loop(search_cond, search_body, init)
    thr_b = jnp.broadcast_to(thr, (tq, KEY_TILE))
    quota_b = jnp.broadcast_to(quota, (tq, KEY_TILE))
    has_ties = jnp.min(quota) < NO_QUOTA

    @pl.when(jnp.logical_not(has_ties))
    def _():
        def body(j, c):
            for k in range(BIG):
                t = j * BIG + k
                sc[t] = jnp.where(sc[t] >= thr_b, 0.0, NEG)
            return c
        lax.fori_loop(0, steps_all, body, 0)

    @pl.when(has_ties)
    def _():
        ri = lax.broadcasted_iota(jnp.int32, (KEY_TILE, KEY_TILE), 0)
        ci = lax.broadcasted_iota(jnp.int32, (KEY_TILE, KEY_TILE), 1)
        lower = jnp.where(ri >= ci, 1.0, 0.0).astype(BF16)

        def body(j, seen):
            for k in range(BIG):
                t = j * BIG + k
                s = sc[t]
                tie = jnp.where(s == thr_b, 1.0, 0.0)
                rank = seen + _dot(lower, tie.astype(BF16))
                keep = (s > thr_b) | ((tie > 0.0) & (rank <= quota_b))
                sc[t] = jnp.where(keep, 0.0, NEG)
                seen = seen + jnp.sum(tie, axis=0, keepdims=True)
            return seen
        lax.fori_loop(0, steps_all, body, jnp.zeros((1, tq), F32))

    qt_all = q_ref[0].astype(F32).T
    zeros_half = jnp.zeros((HEAD_DIM, GRP * tq), F32)
    qa = []
    for g in range(N_KV_HEADS):
        piece = jnp.concatenate(
            [qt_all[(g * GRP + j) * HEAD_DIM:(g * GRP + j + 1) * HEAD_DIM, :] for j in range(GRP)],
            axis=1)
        halves = [piece, zeros_half] if g == 0 else [zeros_half, piece]
        qa.append(jnp.concatenate(halves, axis=0).astype(BF16))

    def logits(slot, keys, pens, m_old, extra=None):
        pen = jnp.concatenate(pens, axis=0)
        pen = jnp.concatenate([pen] * GRP, axis=1)
        kblk = k_ref[0, keys, :]
        m_new = []
        for g in range(N_KV_HEADS):
            s = _dot(kblk, qa[g]) + pen
            if extra is not None:
                s = s + extra[g]
            sbuf[slot, g, 0:pen.shape[0], :] = s
            m_new.append(jnp.maximum(m_old[g], jnp.max(s, axis=0, keepdims=True)))
        return tuple(m_new)

    def accumulate(slot, keys, nk, m_old, m_new, accs):
        out = []
        for g in range(N_KV_HEADS):
            p = jnp.exp2(sbuf[slot, g, 0:nk, :] - m_new[g])
            out.append(jnp.exp2(m_old[g] - m_new[g]) * accs[g]
                       + _dot(vt_ref[0, g * VT_ROWS:(g + 1) * VT_ROWS, keys], p.astype(BF16)))
        return tuple(out)

    last_far = jnp.maximum(steps_far - 1, 0)

    def far_keys(j):
        return step_keys(jnp.minimum(j, last_far))

    def far_pens(j):
        return [jnp.where(j * BIG + k < n0, sc[jnp.minimum(j * BIG + k, n0)], NEG) for k in range(BIG)]

    m_init = tuple(jnp.full((1, GRP * tq), NEG, F32) for _ in range(N_KV_HEADS))
    acc_init = tuple(jnp.zeros((VT_ROWS, GRP * tq), F32) for _ in range(N_KV_HEADS))
    m_first = logits(0, far_keys(0), far_pens(0), m_init)

    def stage_pair(i, c):
        m_old, m_cur, accs = c
        a = 2 * i + 1
        m_a = logits(1, far_keys(a), far_pens(a), m_cur)
        accs = accumulate(0, far_keys(a - 1), kw, m_old, m_cur, accs)
        m_b = logits(0, far_keys(a + 1), far_pens(a + 1), m_a)
        accs = accumulate(1, far_keys(a), kw, m_cur, m_a, accs)
        return m_a, m_b, accs

    pairs = steps_far // 2
    m_old, m_cur, accs = lax.fori_loop(0, pairs, stage_pair, (m_init, m_first, acc_init))
    m_last = logits(1, near_keys, [sc[n0], sc[n0 + 1]], m_cur, extra=(bias[variant, 0], bias[variant, 1]))
    accs = accumulate(0, far_keys(2 * pairs), kw, m_old, m_cur, accs)
    accs = accumulate(1, near_keys, 2 * KEY_TILE, m_cur, m_last, accs)
    heads = []
    for g in range(N_KV_HEADS):
        acc = accs[g]
        og = acc[:HEAD_DIM, :] / acc[HEAD_DIM:HEAD_DIM + 1, :]
        heads += [og[:, j * tq:(j + 1) * tq] for j in range(GRP)]
    o_ref[0] = jnp.concatenate(heads, axis=0).T.astype(BF16)


def _dsa(rel_bias, q, qi, sgn, k, vt, ki, *, q_pos0, l_valid):
    b, t, _ = q.shape
    lk = k.shape[1]
    tq = KEY_TILE
    assert lk % (BIG * KEY_TILE) == 0 and q_pos0 % KEY_TILE == 0 and t % tq == 0
    assert q_pos0 + t <= lk
    nt = lk // KEY_TILE
    row = lambda width: pl.BlockSpec((1, tq, width), lambda bi, i: (bi, i, 0))
    return pl.pallas_call(
        functools.partial(_dsa_kernel, q_pos0=q_pos0, l_valid=l_valid),
        grid=(b, t // tq),
        in_specs=[pl.BlockSpec(memory_space=pltpu.SMEM),
                  row(D_ATT), row(N_IDX_HEADS * D_IDX),
                  pl.BlockSpec((1, 8, tq), lambda bi, i: (bi, 0, i)),
                  _resident((1, lk, N_KV_HEADS * HEAD_DIM), lambda bi, i: (bi, 0, 0)),
                  _resident((1, N_KV_HEADS * VT_ROWS, lk), lambda bi, i: (bi, 0, 0)),
                  _resident((1, lk, D_IDX), lambda bi, i: (bi, 0, 0))],
        out_specs=row(D_ATT),
        out_shape=jax.ShapeDtypeStruct((b, t, D_ATT), BF16),
        scratch_shapes=[pltpu.VMEM((nt, KEY_TILE, tq), F32),
                        pltpu.VMEM((2, N_KV_HEADS, 2 * KEY_TILE, GRP * tq), F32),
                        pltpu.VMEM((2, N_KV_HEADS, BIG * KEY_TILE, GRP * tq), F32)],
        compiler_params=pltpu.CompilerParams(
            dimension_semantics=("arbitrary", "arbitrary"), vmem_limit_bytes=VMEM_LIMIT),
        name="dsa",
    )(rel_bias, q, qi, sgn, k, vt, ki)


def _split3(x):
    hi = x.astype(BF16)
    r = x - hi.astype(F32)
    mid = r.astype(BF16)
    lo = (r - mid.astype(F32)).astype(BF16)
    return hi, mid, lo


def _ssd_kernel(xs_ref, bm_ref, cm_ref, dt_ref, z_ref, alog_ref, dsk_ref, nw_ref, h0_ref,
                y_ref, h_ref, ht, *, q_in):
    c = pl.program_id(1)
    nc = pl.num_programs(1)

    @pl.when(c == 0)
    def _():
        for hh in range(SSM_HEADS):
            ht[hh] = h0_ref[0, hh].T

    def rows(ref):
        x = ref[0]
        if q_in == SSD_Q:
            return x
        return jnp.concatenate([x, jnp.zeros((SSD_Q - q_in, x.shape[1]), x.dtype)], axis=0)

    xs, bm, cm, dt, z = rows(xs_ref), rows(bm_ref), rows(cm_ref), rows(dt_ref), rows(z_ref)
    a = -jnp.exp(alog_ref[...])
    da = dt * a
    ri = lax.broadcasted_iota(jnp.int32, (SSD_Q, SSD_Q), 0)
    ci = lax.broadcasted_iota(jnp.int32, (SSD_Q, SSD_Q), 1)
    causal = ri >= ci
    tril = jnp.where(causal, 1.0, 0.0).astype(BF16)
    hi, mid, lo = _split3(da)
    acum = _dot(tril, hi) + _dot(tril, mid) + _dot(tril, lo)
    acum_t = acum.T

    for g in range(SSM_GROUPS):
        bm_g = bm[:, g * SSM_STATE:(g + 1) * SSM_STATE]
        cm_g = cm[:, g * SSM_STATE:(g + 1) * SSM_STATE].astype(BF16)
        bm_t = bm_g.T.astype(BF16)
        cb = _dot(cm_g, bm_t)
        gs = []
        ssq = jnp.zeros((SSD_Q, 1), F32)
        for r in range(SSM_HPG):
            hh = g * SSM_HPG + r
            lanes = slice(hh * SSM_HEAD_DIM, (hh + 1) * SSM_HEAD_DIM)
            xh = xs[:, lanes]
            dl = DT_LANE0 + hh
            col = acum[:, dl:dl + 1]
            rowv = acum_t[dl:dl + 1, :]
            decay = jnp.exp(jnp.where(causal, col - rowv, NEG))
            xdt = xh * dt[:, dl:dl + 1]
            y = _dot((cb * decay).astype(BF16), xdt.astype(BF16))
            h_prev = ht[hh]
            y = y + jnp.exp(col) * _dot(cm_g, h_prev.astype(BF16))
            last = acum[SSD_Q - 1:SSD_Q, dl:dl + 1]
            xw = (xdt * jnp.exp(last - col)).astype(BF16)
            ht[hh] = h_prev * jnp.exp(last) + _dot(bm_t, xw)
            y = y + xh * dsk_ref[:, lanes]
            gate = y * _silu(z[:, lanes])
            ssq = ssq + jnp.sum(gate * gate, axis=1, keepdims=True)
            gs.append(gate)
        scale = lax.rsqrt(ssq / float(SSM_HPG * SSM_HEAD_DIM) + NORM_EPS)
        for r in range(SSM_HPG):
            hh = g * SSM_HPG + r
            lanes = slice(hh * SSM_HEAD_DIM, (hh + 1) * SSM_HEAD_DIM)
            y_ref[0, :, lanes] = (gs[r] * scale * nw_ref[:, lanes])[:q_in].astype(BF16)

    @pl.when(c == nc - 1)
    def _():
        for hh in range(SSM_HEADS):
            h_ref[0, hh] = ht[hh].T


def _ssd(xs, bm, cm, dt, z, alog, dsk, nw, h0, *, q_in):
    b, t, _ = xs.shape
    row = lambda width: pl.BlockSpec((1, q_in, width), lambda bi, c: (bi, c, 0))
    full = lambda a: _resident(a.shape, lambda bi, c: (0,) * a.ndim)
    hspec = pl.BlockSpec((1, SSM_HEADS, SSM_HEAD_DIM, SSM_STATE), lambda bi, c: (bi, 0, 0, 0))
    return pl.pallas_call(
        functools.partial(_ssd_kernel, q_in=q_in),
        grid=(b, t // q_in),
        in_specs=[row(D_SSM), row(256), row(256), row(LANE), row(D_SSM),
                  full(alog), full(dsk), full(nw), hspec],
        out_specs=(row(D_SSM), hspec),
        out_shape=(jax.ShapeDtypeStruct((b, t, D_SSM), BF16),
                   jax.ShapeDtypeStruct((b, SSM_HEADS, SSM_HEAD_DIM, SSM_STATE), F32)),
        scratch_shapes=[pltpu.VMEM((SSM_HEADS, SSM_STATE, SSM_HEAD_DIM), F32)],
        compiler_params=pltpu.CompilerParams(
            dimension_semantics=("arbitrary", "arbitrary"), vmem_limit_bytes=VMEM_LIMIT),
        name="ssd",
    )(xs, bm, cm, dt, z, alog, dsk, nw, h0)


def _mix_ffn_kernel(x_ref, ya_ref, yb_ref, yc_ref, wo_ref, npost_ref, nfpre_ref, wg_ref, wu_ref,
                    wd_ref, cfw_ref, cfb_ref, nfpost_ref, buff_ref, o_ref, buff_out, eg, *, tm):
    i = pl.program_id(1)

    @pl.when(i == 0)
    def _():
        eg[CARRY - 2:CARRY, :] = buff_ref[0]

    mix = (_dot(ya_ref[0], wo_ref[0:D_CONV_MIX, :])
           + _dot(yb_ref[0], wo_ref[D_CONV_MIX:D_CONV_MIX + D_ATT, :])
           + _dot(yc_ref[0], wo_ref[D_CONV_MIX + D_ATT:, :]))
    x1 = x_ref[0] + _rms(mix, npost_ref[...])
    u = _rms(x1, nfpre_ref[...]).astype(BF16)
    eg[CARRY:CARRY + tm, :] = _dot(u, wg_ref[...])
    gc = eg[CARRY - 2:CARRY - 2 + tm, :] * cfw_ref[0:1, :]
    for j in range(1, CONV_F_W):
        gc = gc + eg[CARRY - 2 + j:CARRY - 2 + j + tm, :] * cfw_ref[j:j + 1, :]
    hid = (_silu(gc + cfb_ref[...]) * _dot(u, wu_ref[...])).astype(BF16)
    o_ref[0] = x1 + _rms(_dot(hid, wd_ref[...]), nfpost_ref[...])
    tail = eg[CARRY + tm - 2:CARRY + tm, :]
    buff_out[0] = tail
    eg[CARRY - 2:CARRY, :] = tail


def _mix_ffn(x, ya, yb, yc, wo, npost, nfpre, wg, wu, wd, cfw, cfb, nfpost, buff, *, tm):
    b, t, _ = x.shape
    row = lambda width: pl.BlockSpec((1, tm, width), lambda bi, i: (bi, i, 0))
    full = lambda a: _resident(a.shape, lambda bi, i: (0,) * a.ndim)
    state = pl.BlockSpec((1, CONV_F_W - 1, D_FF), lambda bi, i: (bi, 0, 0))
    return pl.pallas_call(
        functools.partial(_mix_ffn_kernel, tm=tm),
        grid=(b, t // tm),
        in_specs=[row(D_MODEL), row(D_CONV_MIX), row(D_ATT), row(D_SSM), full(wo), full(npost),
                  full(nfpre), full(wg), full(wu), full(wd), full(cfw), full(cfb), full(nfpost),
                  state],
        out_specs=(row(D_MODEL), state),
        out_shape=(jax.ShapeDtypeStruct((b, t, D_MODEL), F32),
                   jax.ShapeDtypeStruct((b, CONV_F_W - 1, D_FF), F32)),
        scratch_shapes=[pltpu.VMEM((CARRY + tm, D_FF), F32)],
        compiler_params=pltpu.CompilerParams(
            dimension_semantics=("arbitrary", "arbitrary"), vmem_limit_bytes=VMEM_LIMIT),
        name="mix_ffn",
    )(x, ya, yb, yc, wo, npost, nfpre, wg, wu, wd, cfw, cfb, nfpost, buff)


def _pack_keys_kernel(ck_ref, cv_ref, cki_ref, k_ref, v_ref, ki_ref, kb_ref, vt_ref, kib_ref, *, past_steps):
    j = pl.program_id(1)
    rows = kb_ref.shape[1]

    def pick(cache_ref, new_ref):
        new = new_ref[0]
        new = jnp.concatenate([new, jnp.zeros((rows - new.shape[0], new.shape[1]), F32)], axis=0)
        return jnp.where(j < past_steps, cache_ref[0], new)

    kb_ref[0] = pick(ck_ref, k_ref).astype(BF16)
    vt_ref[0] = _augment_vt(pick(cv_ref, v_ref).T).astype(BF16)
    kib_ref[0] = pick(cki_ref, ki_ref).astype(BF16)


def _pack_keys(ck, cv, cki, k, v, ki):
    b, past, _ = ck.shape
    t = k.shape[1]
    rows = BIG * KEY_TILE
    assert past % rows == 0 and t <= rows
    past_steps = past // rows
    lk = past + rows
    cache = lambda width: pl.BlockSpec((1, rows, width),
                                       lambda bi, j: (bi, jnp.minimum(j, past_steps - 1), 0))
    new = lambda width: pl.BlockSpec((1, t, width), lambda bi, j: (bi, 0, 0))
    out = lambda width: pl.BlockSpec((1, rows, width), lambda bi, j: (bi, j, 0))
    return pl.pallas_call(
        functools.partial(_pack_keys_kernel, past_steps=past_steps),
        grid=(b, past_steps + 1),
        in_specs=[cache(128), cache(128), cache(D_IDX), new(128), new(128), new(D_IDX)],
        out_specs=(out(128), pl.BlockSpec((1, N_KV_HEADS * VT_ROWS, rows), lambda bi, j: (bi, 0, j)),
                   out(D_IDX)),
        out_shape=(jax.ShapeDtypeStruct((b, lk, 128), BF16),
                   jax.ShapeDtypeStruct((b, N_KV_HEADS * VT_ROWS, lk), BF16),
                   jax.ShapeDtypeStruct((b, lk, D_IDX), BF16)),
        compiler_params=pltpu.CompilerParams(
            dimension_semantics=("arbitrary", "arbitrary"), vmem_limit_bytes=VMEM_LIMIT),
        name="pack_keys",
    )(ck, cv, cki, k, v, ki)


def _layer(x, p, rel_bias, buf_a, buf_s, h0, buf_f, kv_past, *, tm):
    b, t, _ = x.shape
    outs = _in_proj(x, p['norm_mix_pre'], p['w_in'], p['conv_a_w'], p['conv_ssm_w'], p['conv_ssm_b'],
                    p['dt_bias'], buf_a, buf_s, tm=tm, key_side=kv_past is None)
    (ya, q, k, v, qi, ki, sgn, z, xs, bm, cm, dt, buf_a_new, buf_s_new) = outs[:14]
    if kv_past is None:
        kb, vt, kib = outs[14:]
        yb = _dsa(rel_bias, q, qi, sgn, kb, vt, kib, q_pos0=0, l_valid=t)
    else:
        ck, cv, cki = kv_past
        past = ck.shape[1]
        qpad = lambda a: jnp.pad(a, ((0, 0), (0, -t % KEY_TILE), (0, 0)))
        kb, vt, kib = _pack_keys(ck.reshape(b, past, -1), cv.reshape(b, past, -1), cki, k, v, ki)
        yb = _dsa(rel_bias, qpad(q), qpad(qi), sgn, kb, vt, kib, q_pos0=past, l_valid=past + t)[:, :t]
    yc, h_new = _ssd(xs, bm, cm, dt, z, p['a_log'], p['d_skip'], p['ssm_norm_w'],
                     h0.reshape(b, SSM_HEADS, SSM_HEAD_DIM, SSM_STATE), q_in=min(t, SSD_Q))
    x_new, buf_f_new = _mix_ffn(x, ya, yb, yc, p['w_out'], p['norm_mix_post'], p['norm_ffn_pre'],
                                p['w_gate'], p['w_up'], p['w_down'], p['conv_ffn_w'],
                                p['conv_ffn_b'], p['norm_ffn_post'], buf_f, tm=tm)
    st = (k.reshape(b, t, N_KV_HEADS, HEAD_DIM), v.reshape(b, t, N_KV_HEADS, HEAD_DIM), ki,
          buf_a_new, buf_s_new, h_new.reshape(b, SSM_GROUPS, SSM_HPG, SSM_HEAD_DIM, SSM_STATE),
          buf_f_new)
    return x_new, st


def _pack_w_in_kernel(wt_ref, o_ref):
    for c, c0 in enumerate(W_IN_TILE_COLS):
        o_ref[:, c * LANE:(c + 1) * LANE] = wt_ref[c0:c0 + LANE, :].T.astype(BF16)


def _pack_w_in(wt):
    return pl.pallas_call(
        _pack_w_in_kernel,
        out_shape=jax.ShapeDtypeStruct((wt.shape[1], D_IN_PACKED), BF16),
        compiler_params=pltpu.CompilerParams(vmem_limit_bytes=VMEM_LIMIT),
        name="pack_w_in",
    )(wt)


def _dt_lanes(v):
    return jnp.pad(v, (DT_LANE0, 0))[None, :]


def kernel(x_prompt, x_sample, cache_k, cache_v, cache_kidx, state_conv_a, state_conv_ssm, state_ssm, state_conv_ffn, rel_bias, norm_mix_pre, norm_mix_post, norm_ffn_pre, norm_ffn_post, w_in, conv_a_w, conv_ssm_w, conv_ssm_b, dt_bias, a_log, d_skip, ssm_norm_w, w_out, w_gate, w_up, conv_ffn_w, conv_ffn_b, w_down):
    depth = w_in.shape[0]
    bp = x_prompt.shape[0]
    yp, ys = x_prompt, x_sample
    outs_p, outs_s = [], []
    for l in range(depth):
        p = {'norm_mix_pre': norm_mix_pre[l][None], 'norm_mix_post': norm_mix_post[l][None],
             'norm_ffn_pre': norm_ffn_pre[l][None], 'norm_ffn_post': norm_ffn_post[l][None],
             'w_in': _pack_w_in(jnp.swapaxes(w_in[l], 0, 1)), 'conv_a_w': conv_a_w[l], 'conv_ssm_w': conv_ssm_w[l],
             'conv_ssm_b': conv_ssm_b[l][None], 'dt_bias': _dt_lanes(dt_bias[l]),
             'a_log': _dt_lanes(a_log[l]), 'd_skip': jnp.repeat(d_skip[l], SSM_HEAD_DIM)[None],
             'ssm_norm_w': ssm_norm_w[l][None], 'w_out': w_out[l].astype(BF16),
             'w_gate': w_gate[l].astype(BF16), 'w_up': w_up[l].astype(BF16),
             'conv_ffn_w': conv_ffn_w[l], 'conv_ffn_b': conv_ffn_b[l][None],
             'w_down': w_down[l].astype(BF16)}
        yp, st_p = _layer(
            yp, p, rel_bias,
            jnp.zeros((bp, CONV_A_W - 1, D_CONV_MIX), F32),
            jnp.zeros((bp, SSM_CONV_W - 1, D_XBC), F32),
            jnp.zeros((bp, SSM_GROUPS, SSM_HPG, SSM_HEAD_DIM, SSM_STATE), F32),
            jnp.zeros((bp, CONV_F_W - 1, D_FF), F32),
            None, tm=256)
        ys, st_s = _layer(
            ys, p, rel_bias, state_conv_a[l], state_conv_ssm[l], state_ssm[l], state_conv_ffn[l],
            (cache_k[l], cache_v[l], cache_kidx[l]), tm=ys.shape[1])
        outs_p.append(st_p)
        outs_s.append(st_s)

    def stack(outs, i):
        return jnp.stack([o[i] for o in outs], axis=0)

    res = [yp, ys]
    for i in range(7):
        res.append(stack(outs_p, i))
        res.append(stack(outs_s, i))
    return tuple(res)
```

```python
import functools

import jax
import jax.numpy as jnp
from jax import lax
from jax.experimental import pallas as pl
from jax.experimental.pallas import tpu as pltpu

F32 = jnp.float32
BF16 = jnp.bfloat16

D_MODEL = 1024
CHUNK = 64
D_CONV_MIX = 256
CONV_A_W = 3
N_HEADS = 6
N_KV_HEADS = 2
HEAD_DIM = 64
D_ATT = N_HEADS * HEAD_DIM
GRP = N_HEADS // N_KV_HEADS
N_IDX_HEADS = 4
D_IDX = 64
TOPK = 256
N_BUCKETS = 32
SSM_HEADS = 6
SSM_HEAD_DIM = 64
D_SSM = SSM_HEADS * SSM_HEAD_DIM
SSM_GROUPS = 2
SSM_HPG = SSM_HEADS // SSM_GROUPS
SSM_STATE = 128
SSM_CONV_W = 4
D_XBC = D_SSM + 2 * SSM_GROUPS * SSM_STATE
D_FF = 2816
CONV_F_W = 3
NORM_EPS = 1e-6

LANE = 128
CARRY = 8
SSD_Q = 128
KEY_TILE = 128
BIG = 4
VMEM_LIMIT = 56 * 1024 * 1024
NEG = -1e30
LOG2E = 1.4426950408889634
F32_MAX = 3.4028234663852886e38
F32_TINY = 1.1754943508222875e-38
NO_QUOTA = 1e9
MAX_SEARCH_STEPS = 256

C_AB, C_AC, C_AH = 0, 256, 512
C_Q = 768
C_K = 1152
C_V = 1280
C_QI = 1408
C_KI = 1664
C_Z = 1792
C_XBC = 2176
C_DT = 3072
D_IN_PACKED = 3200
D_IN = 3018
D_IN_HEAD = 1732
DT_LANE0 = LANE - SSM_HEADS
W_IN_TILE_COLS = ([c * LANE for c in range(C_Z // LANE)]
                  + [D_IN_HEAD + c * LANE for c in range((C_DT - C_Z) // LANE)] + [D_IN - LANE])


def _rms(x, w):
    return x * lax.rsqrt(jnp.mean(x * x, axis=-1, keepdims=True) + NORM_EPS) * w


def _dot(a, b):
    return jnp.dot(a, b, preferred_element_type=F32)


def _silu(x):
    return x * jax.nn.sigmoid(x)


VT_ROWS = HEAD_DIM + 16


def _augment_vt(vt):
    extra_shape = vt.shape[:-2] + (VT_ROWS - HEAD_DIM, vt.shape[-1])
    row = lax.broadcasted_iota(jnp.int32, extra_shape, len(extra_shape) - 2)
    extra = jnp.where(row == 0, 1.0, 0.0).astype(vt.dtype)
    parts = []
    for g in range(N_KV_HEADS):
        parts += [vt[..., g * HEAD_DIM:(g + 1) * HEAD_DIM, :], extra]
    return jnp.concatenate(parts, axis=-2)


def _resident(shape, index_map):
    return pl.BlockSpec(shape, index_map, pipeline_mode=pl.Buffered(1))


def _in_proj_kernel(x_ref, nw_ref, w_ref, caw_ref, csw_ref, csb_ref, dtb_ref, bufa_ref, bufs_ref,
                    ya_ref, q_ref, k_ref, v_ref, qi_ref, ki_ref, sgn_ref, z_ref, xs_ref, bm_ref,
                    cm_ref, dt_ref, bufa_out, bufs_out, *rest, tm, key_side):
    if key_side:
        kb_ref, vt_ref, kib_ref, ea, es = rest
    else:
        ea, es = rest
    i = pl.program_id(1)
    u = _rms(x_ref[0], nw_ref[...]).astype(BF16)

    def proj(c0, width):
        return _dot(u, w_ref[:, c0:c0 + width])

    @pl.when(i == 0)
    def _():
        ea[CARRY - 2:CARRY, :] = bufa_ref[0]
        es[CARRY - 3:CARRY, :] = bufs_ref[0]

    ea[CARRY:CARRY + tm, :] = proj(C_AC, D_CONV_MIX) * proj(C_AH, D_CONV_MIX)
    conv = ea[CARRY - 2:CARRY - 2 + tm, :] * caw_ref[0:1, :]
    for j in range(1, CONV_A_W):
        conv = conv + ea[CARRY - 2 + j:CARRY - 2 + j + tm, :] * caw_ref[j:j + 1, :]
    ya_ref[0] = (proj(C_AB, D_CONV_MIX) * conv).astype(BF16)
    tail_a = ea[CARRY + tm - 2:CARRY + tm, :]
    bufa_out[0] = tail_a
    ea[CARRY - 2:CARRY, :] = tail_a

    q_ref[0] = (proj(C_Q, D_ATT) * (HEAD_DIM ** -0.5 * LOG2E)).astype(BF16)
    k = proj(C_K, N_KV_HEADS * HEAD_DIM)
    v = proj(C_V, N_KV_HEADS * HEAD_DIM)
    kiw = proj(C_KI, LANE)
    k_ref[0] = k
    v_ref[0] = v
    ki_ref[0] = kiw[:, :D_IDX]
    wi = kiw[:, D_IDX:D_IDX + N_IDX_HEADS]
    kiw_rows = kiw if tm >= LANE else jnp.concatenate([kiw, jnp.zeros((LANE - tm, LANE), F32)], axis=0)
    sgn_ref[0] = jnp.where(kiw_rows.T[D_IDX:D_IDX + 8, :] >= 0.0, 1.0, -1.0).astype(F32)
    if key_side:
        kb_ref[0] = k.astype(BF16)
        vt_ref[0] = _augment_vt(v.T).astype(BF16)
        kib_ref[0] = kiw[:, :D_IDX].astype(BF16)
    wabs = jnp.abs(wi) * (D_IDX ** -0.5 * N_IDX_HEADS ** -0.5)
    qi = proj(C_QI, N_IDX_HEADS * D_IDX)
    for h in range(N_IDX_HEADS):
        qi_ref[0, :, h * D_IDX:(h + 1) * D_IDX] = (
            qi[:, h * D_IDX:(h + 1) * D_IDX] * wabs[:, h:h + 1]).astype(BF16)

    z_ref[0] = proj(C_Z, D_SSM)
    es[CARRY:CARRY + tm, :] = proj(C_XBC, D_XBC)
    xc = es[CARRY - 3:CARRY - 3 + tm, :] * csw_ref[0:1, :]
    for j in range(1, SSM_CONV_W):
        xc = xc + es[CARRY - 3 + j:CARRY - 3 + j + tm, :] * csw_ref[j:j + 1, :]
    xc = _silu(xc + csb_ref[...])
    xs_ref[0] = xc[:, :D_SSM]
    bm_ref[0] = xc[:, D_SSM:D_SSM + SSM_GROUPS * SSM_STATE]
    cm_ref[0] = xc[:, D_SSM + SSM_GROUPS * SSM_STATE:]
    tail_s = es[CARRY + tm - 3:CARRY + tm, :]
    bufs_out[0] = tail_s
    es[CARRY - 3:CARRY, :] = tail_s
    dtr = proj(C_DT, LANE) + dtb_ref[...]
    dt_ref[0] = jnp.maximum(dtr, 0.0) + jnp.log1p(jnp.exp(-jnp.abs(dtr)))


def _in_proj(x, nw, w_packed, caw, csw, csb, dtb, bufa, bufs, *, tm, key_side):
    b, t, _ = x.shape
    tq = max(tm, LANE)
    grid = (b, t // tm)
    row = lambda width: pl.BlockSpec((1, tm, width), lambda bi, i: (bi, i, 0))
    full = lambda a: _resident(a.shape, lambda bi, i: (0,) * a.ndim)
    state = lambda r, width: pl.BlockSpec((1, r, width), lambda bi, i: (bi, 0, 0))
    out_shapes = (
        jax.ShapeDtypeStruct((b, t, D_CONV_MIX), BF16),
        jax.ShapeDtypeStruct((b, t, D_ATT), BF16),
        jax.ShapeDtypeStruct((b, t, 128), F32),
        jax.ShapeDtypeStruct((b, t, 128), F32),
        jax.ShapeDtypeStruct((b, t, 256), BF16),
        jax.ShapeDtypeStruct((b, t, D_IDX), F32),
        jax.ShapeDtypeStruct((b, 8, t // tm * tq), F32),
        jax.ShapeDtypeStruct((b, t, D_SSM), F32),
        jax.ShapeDtypeStruct((b, t, D_SSM), F32),
        jax.ShapeDtypeStruct((b, t, 256), F32),
        jax.ShapeDtypeStruct((b, t, 256), F32),
        jax.ShapeDtypeStruct((b, t, LANE), F32),
        jax.ShapeDtypeStruct((b, CONV_A_W - 1, D_CONV_MIX), F32),
        jax.ShapeDtypeStruct((b, SSM_CONV_W - 1, D_XBC), F32),
    )
    out_specs = (row(D_CONV_MIX), row(D_ATT), row(128), row(128), row(256), row(D_IDX),
                 pl.BlockSpec((1, 8, tq), lambda bi, i: (bi, 0, i)),
                 row(D_SSM), row(D_SSM), row(256), row(256), row(LANE),
                 state(CONV_A_W - 1, D_CONV_MIX), state(SSM_CONV_W - 1, D_XBC))
    if key_side:
        out_shapes += (jax.ShapeDtypeStruct((b, t, 128), BF16),
                       jax.ShapeDtypeStruct((b, N_KV_HEADS * VT_ROWS, t), BF16),
                       jax.ShapeDtypeStruct((b, t, D_IDX), BF16))
        out_specs += (row(128), pl.BlockSpec((1, N_KV_HEADS * VT_ROWS, tm), lambda bi, i: (bi, 0, i)), row(D_IDX))
    in_specs = [row(D_MODEL), full(nw), full(w_packed), full(caw), full(csw), full(csb), full(dtb),
                state(CONV_A_W - 1, D_CONV_MIX), state(SSM_CONV_W - 1, D_XBC)]
    return pl.pallas_call(
        functools.partial(_in_proj_kernel, tm=tm, key_side=key_side),
        grid=grid, in_specs=in_specs, out_specs=out_specs, out_shape=out_shapes,
        scratch_shapes=[pltpu.VMEM((CARRY + tm, D_CONV_MIX), F32),
                        pltpu.VMEM((CARRY + tm, D_XBC), F32)],
        compiler_params=pltpu.CompilerParams(
            dimension_semantics=("arbitrary", "arbitrary"), vmem_limit_bytes=VMEM_LIMIT),
        name="in_proj",
    )(x, nw, w_packed, caw, csw, csb, dtb, bufa, bufs)


def _t5_bucket(rel):
    n = jnp.abs(rel)
    large = jnp.full(rel.shape, 8, jnp.int32)
    for brk in (12, 16, 23, 32, 46, 64, 91):
        large = large + jnp.where(n >= brk, 1, 0)
    return jnp.where(rel > 0, N_BUCKETS // 2, 0) + jnp.where(n < 8, n, large)


def _f32_key(x):
    bits = lax.bitcast_convert_type(x, jnp.int32)
    return bits ^ ((bits >> 31) & jnp.int32(0x7FFFFFFF))


def _key_f32(key):
    return lax.bitcast_convert_type(key ^ ((key >> 31) & jnp.int32(0x7FFFFFFF)), F32)


def _dsa_kernel(relb_ref, q_ref, qi_ref, sgn_ref, k_ref, vt_ref, ki_ref, o_ref, sc, bias, sbuf,
                *, q_pos0, l_valid):
    tq = KEY_TILE
    i = pl.program_id(1)
    q0 = q_pos0 + i * tq
    qt = q0 // KEY_TILE
    n0 = jnp.maximum(qt - 1, 0)
    variant = jnp.minimum(qt, 1)
    steps_far = (n0 + BIG - 1) // BIG
    steps_all = (n0 + 2 + BIG - 1) // BIG
    kw = BIG * KEY_TILE
    kf = float(TOPK)

    def step_keys(j):
        return pl.ds(pl.multiple_of(j * kw, kw), kw)

    near_keys = pl.ds(pl.multiple_of(n0 * KEY_TILE, KEY_TILE), 2 * KEY_TILE)

    @pl.when(i == 0)
    def _():
        sc[...] = jnp.full(sc.shape, -jnp.inf, F32)

    @pl.when((i == 0) & (pl.program_id(0) == 0))
    def _():
        kk = lax.broadcasted_iota(jnp.int32, (2 * KEY_TILE, tq), 0)
        r = lax.broadcasted_iota(jnp.int32, (2 * KEY_TILE, tq), 1)
        for var in range(2):
            bucket = _t5_bucket(kk - var * KEY_TILE - r)
            for h in range(N_HEADS):
                tab = jnp.zeros((2 * KEY_TILE, tq), F32)
                for bk in range(N_BUCKETS):
                    tab = jnp.where(bucket == bk, relb_ref[bk, h], tab)
                g, j = divmod(h, GRP)
                bias[var, g, :, j * tq:(j + 1) * tq] = (tab - relb_ref[N_BUCKETS // 2 - 1, h]) * LOG2E

    qit = qi_ref[0].astype(F32).T
    qis = jnp.concatenate([qit[h * D_IDX:(h + 1) * D_IDX, :] for h in range(N_IDX_HEADS)],
                          axis=1).astype(BF16)
    sgn = sgn_ref[0]

    def scores(keys):
        s = jnp.maximum(_dot(ki_ref[0, keys, :], qis), 0.0)
        tot = s[:, 0:tq] * sgn[0:1, :]
        for h in range(1, N_IDX_HEADS):
            tot = tot + s[:, h * tq:(h + 1) * tq] * sgn[h:h + 1, :]
        return tot

    zero = jnp.zeros((KEY_TILE, tq), F32)

    def tally(c, s):
        ge, gt, mx = c
        return (ge + jnp.where(s >= 0.0, 1.0, 0.0), gt + jnp.where(s >= F32_TINY, 1.0, 0.0),
                jnp.maximum(mx, s))

    def score_far(j, c):
        s = scores(step_keys(j))
        for k in range(BIG):
            t = j * BIG + k
            tile = jnp.where(t < n0, s[k * KEY_TILE:(k + 1) * KEY_TILE, :], -jnp.inf)
            sc[t] = tile
            c = tally(c, tile)
        return c

    c = lax.fori_loop(0, steps_far, score_far, (zero, zero, jnp.full((KEY_TILE, tq), -F32_MAX, F32)))
    s_near = scores(near_keys)
    kpos = n0 * KEY_TILE + lax.broadcasted_iota(jnp.int32, (2 * KEY_TILE, tq), 0)
    qpos = q0 + lax.broadcasted_iota(jnp.int32, (2 * KEY_TILE, tq), 1)
    adm = (kpos < l_valid) & ((kpos >> 6) <= (qpos >> 6))
    s_near = jnp.where(adm, s_near, -jnp.inf)
    sc[n0] = s_near[:KEY_TILE, :]
    sc[n0 + 1] = s_near[KEY_TILE:, :]
    for k in range(BIG - 1):
        @pl.when(n0 + 2 + k < steps_all * BIG)
        def _():
            sc[n0 + 2 + k] = jnp.full((KEY_TILE, tq), -jnp.inf, F32)
    ge, gt, mx = tally(tally(c, s_near[:KEY_TILE, :]), s_near[KEY_TILE:, :])


    def count_ge(t):
        tb = jnp.broadcast_to(t, (KEY_TILE, tq))

        def body(j, acc):
            for k in range(BIG):
                acc = acc + jnp.where(sc[j * BIG + k] >= tb, 1.0, 0.0)
            return acc
        return jnp.sum(lax.fori_loop(0, steps_all, body, zero), axis=0, keepdims=True)

    ge0 = jnp.sum(ge, axis=0, keepdims=True)
    gt0 = jnp.sum(gt, axis=0, keepdims=True)
    rmax = jnp.max(mx, axis=0, keepdims=True)
    qrow = q0 + lax.broadcasted_iota(jnp.int32, (1, tq), 1)
    nadm = jnp.minimum(((qrow >> 6) + 1) * CHUNK, l_valid).astype(F32)
    small = nadm <= kf
    tie0 = (~small) & (gt0 < kf) & (ge0 >= kf)
    pos = (~small) & (gt0 >= kf)
    neg = (~small) & (ge0 < kf)
    lo0 = jnp.where(pos, F32_TINY, -F32_MAX)
    flo0 = jnp.where(pos, gt0, nadm)
    hi0 = jnp.where(neg, 0.0, jnp.inf)
    fhi0 = jnp.where(neg, ge0, 0.0)
    done0 = jnp.where(small | tie0 | (pos & (gt0 == kf)), 1.0, 0.0)
    thr0 = jnp.where(small, -F32_MAX, jnp.where(tie0, 0.0, F32_TINY))
    quota0 = jnp.where(tie0, kf - gt0, NO_QUOTA)

    def search_cond(c):
        return (c[0] < MAX_SEARCH_STEPS) & (c[1] > 0)

    def search_body(c):
        it, _, lo, flo, hi, fhi, done, thr, quota = c
        klo, khi = _f32_key(lo), _f32_key(hi)
        la = jnp.log(flo)
        frac = (la - jnp.log(kf)) / (la - jnp.log(jnp.maximum(fhi, 0.5)))
        interp = lo + (hi - lo) * jnp.where(it % 4 == 1, 0.5, frac)
        t_up = lo + (rmax - lo) * 0.5
        t_dn = hi - jnp.maximum(jnp.maximum(jnp.abs(rmax - hi), jnp.abs(hi)), F32_TINY)
        open_hi = hi == jnp.inf
        open_lo = lo <= -F32_MAX
        t = jnp.where(open_hi, t_up, jnp.where(open_lo, t_dn, interp))
        tk = jnp.minimum(jnp.maximum(_f32_key(t), klo + 1), khi - 1)
        tk = jnp.where(it % 4 == 3, klo + lax.shift_right_logical(khi - klo, 1), tk)
        t = _key_f32(tk)
        f = count_ge(t)
        act = done == 0.0
        hit = act & (f == kf)
        up = act & (f > kf)
        dn = act & (f < kf)
        lo = jnp.where(up, t, lo)
        flo = jnp.where(up, f, flo)
        hi = jnp.where(dn, t, hi)
        fhi = jnp.where(dn, f, fhi)
        adj = act & (~hit) & (_f32_key(lo) + 1 == _f32_key(hi))
        thr = jnp.where(hit, t, jnp.where(adj, lo, thr))
        quota = jnp.where(adj, kf - fhi, quota)
        done = jnp.where(hit | adj, 1.0, done)
        n_open = jnp.sum(1.0 - done)
        return it + 1, n_open, lo, flo, hi, fhi, done, thr, quota

    init = (jnp.int32(0), jnp.sum(1.0 - done0), lo0, flo0, hi0, fhi0, done0, thr0, quota0)
    _, _, _, _, _, _, _, thr, quota = lax.while_loop(search_cond, search_body, init)
    thr_b = jnp.broadcast_to(thr, (tq, KEY_TILE))
    quota_b = jnp.broadcast_to(quota, (tq, KEY_TILE))
    has_ties = jnp.min(quota) < NO_QUOTA

    @pl.when(jnp.logical_not(has_ties))
    def _():
        def body(j, c):
            for k in range(BIG):
                t = j * BIG + k
                sc[t] = jnp.where(sc[t] >= thr_b, 0.0, NEG)
            return c
        lax.fori_loop(0, steps_all, body, 0)

    @pl.when(has_ties)
    def _():
        ri = lax.broadcasted_iota(jnp.int32, (KEY_TILE, KEY_TILE), 0)
        ci = lax.broadcasted_iota(jnp.int32, (KEY_TILE, KEY_TILE), 1)
        lower = jnp.where(ri >= ci, 1.0, 0.0).astype(BF16)

        def rank_tiles(first, n, seen):
            for k in range(n):
                t = first + k
                s = sc[t]
                tie = jnp.where(s == thr_b, 1.0, 0.0)
                rank = seen + _dot(lower, tie.astype(BF16))
                keep = (s > thr_b) | ((tie > 0.0) & (rank <= quota_b))
                sc[t] = jnp.where(keep, 0.0, NEG)
                seen = seen + jnp.sum(tie, axis=0, keepdims=True)
            return seen

        pairs = steps_all // 2
        seen = lax.fori_loop(0, pairs, lambda j, c: rank_tiles(j * 2 * BIG, 2 * BIG, c),
                             jnp.zeros((1, tq), F32))
        lax.fori_loop(2 * pairs, steps_all, lambda j, c: rank_tiles(j * BIG, BIG, c), seen)

    qt_all = q_ref[0].astype(F32).T
    zeros_half = jnp.zeros((HEAD_DIM, GRP * tq), F32)
    qa = []
    for g in range(N_KV_HEADS):
        piece = jnp.concatenate(
            [qt_all[(g * GRP + j) * HEAD_DIM:(g * GRP + j + 1) * HEAD_DIM, :] for j in range(GRP)],
            axis=1)
        halves = [piece, zeros_half] if g == 0 else [zeros_half, piece]
        qa.append(jnp.concatenate(halves, axis=0).astype(BF16))

    def logits(slot, keys, pens, m_old, extra=None):
        pen = jnp.concatenate(pens, axis=0)
        pen = jnp.concatenate([pen] * GRP, axis=1)
        kblk = k_ref[0, keys, :]
        m_new = []
        for g in range(N_KV_HEADS):
            s = _dot(kblk, qa[g]) + pen
            if extra is not None:
                s = s + extra[g]
            sbuf[slot, g, 0:pen.shape[0], :] = s
            m_new.append(jnp.maximum(m_old[g], jnp.max(s, axis=0, keepdims=True)))
        return tuple(m_new)

    def accumulate(slot, keys, nk, m_old, m_new, accs):
        out = []
        for g in range(N_KV_HEADS):
            p = jnp.exp2(sbuf[slot, g, 0:nk, :] - m_new[g])
            out.append(jnp.exp2(m_old[g] - m_new[g]) * accs[g]
                       + _dot(vt_ref[0, g * VT_ROWS:(g + 1) * VT_ROWS, keys], p.astype(BF16)))
        return tuple(out)

    last_far = jnp.maximum(steps_far - 1, 0)

    def far_keys(j):
        return step_keys(jnp.minimum(j, last_far))

    def far_pens(j):
        return [jnp.where(j * BIG + k < n0, sc[jnp.minimum(j * BIG + k, n0)], NEG) for k in range(BIG)]

    m_init = tuple(jnp.full((1, GRP * tq), NEG, F32) for _ in range(N_KV_HEADS))
    acc_init = tuple(jnp.zeros((VT_ROWS, GRP * tq), F32) for _ in range(N_KV_HEADS))
    m_first = logits(0, far_keys(0), far_pens(0), m_init)

    def stage_pair(i, c):
        m_old, m_cur, accs = c
        a = 2 * i + 1
        m_a = logits(1, far_keys(a), far_pens(a), m_cur)
        accs = accumulate(0, far_keys(a - 1), kw, m_old, m_cur, accs)
        m_b = logits(0, far_keys(a + 1), far_pens(a + 1), m_a)
        accs = accumulate(1, far_keys(a), kw, m_cur, m_a, accs)
        return m_a, m_b, accs

    pairs = steps_far // 2
    m_old, m_cur, accs = lax.fori_loop(0, pairs, stage_pair, (m_init, m_first, acc_init))
    m_last = logits(1, near_keys, [sc[n0], sc[n0 + 1]], m_cur, extra=(bias[variant, 0], bias[variant, 1]))
    accs = accumulate(0, far_keys(2 * pairs), kw, m_old, m_cur, accs)
    accs = accumulate(1, near_keys, 2 * KEY_TILE, m_cur, m_last, accs)
    heads = []
    for g in range(N_KV_HEADS):
        acc = accs[g]
        og = acc[:HEAD_DIM, :] / acc[HEAD_DIM:HEAD_DIM + 1, :]
        heads += [og[:, j * tq:(j + 1) * tq] for j in range(GRP)]
    o_ref[0] = jnp.concatenate(heads, axis=0).T.astype(BF16)


def _dsa(rel_bias, q, qi, sgn, k, vt, ki, *, q_pos0, l_valid):
    b, t, _ = q.shape
    lk = k.shape[1]
    tq = KEY_TILE
    assert lk % (BIG * KEY_TILE) == 0 and q_pos0 % KEY_TILE == 0 and t % tq == 0
    assert q_pos0 + t <= lk
    nt = lk // KEY_TILE
    row = lambda width: pl.BlockSpec((1, tq, width), lambda bi, i: (bi, i, 0))
    return pl.pallas_call(
        functools.partial(_dsa_kernel, q_pos0=q_pos0, l_valid=l_valid),
        grid=(b, t // tq),
        in_specs=[pl.BlockSpec(memory_space=pltpu.SMEM),
                  row(D_ATT), row(N_IDX_HEADS * D_IDX),
                  pl.BlockSpec((1, 8, tq), lambda bi, i: (bi, 0, i)),
                  _resident((1, lk, N_KV_HEADS * HEAD_DIM), lambda bi, i: (bi, 0, 0)),
                  _resident((1, N_KV_HEADS * VT_ROWS, lk), lambda bi, i: (bi, 0, 0)),
                  _resident((1, lk, D_IDX), lambda bi, i: (bi, 0, 0))],
        out_specs=row(D_ATT),
        out_shape=jax.ShapeDtypeStruct((b, t, D_ATT), BF16),
        scratch_shapes=[pltpu.VMEM((nt, KEY_TILE, tq), F32),
                        pltpu.VMEM((2, N_KV_HEADS, 2 * KEY_TILE, GRP * tq), F32),
                        pltpu.VMEM((2, N_KV_HEADS, BIG * KEY_TILE, GRP * tq), F32)],
        compiler_params=pltpu.CompilerParams(
            dimension_semantics=("arbitrary", "arbitrary"), vmem_limit_bytes=VMEM_LIMIT),
        name="dsa",
    )(rel_bias, q, qi, sgn, k, vt, ki)


def _split3(x):
    hi = x.astype(BF16)
    r = x - hi.astype(F32)
    mid = r.astype(BF16)
    lo = (r - mid.astype(F32)).astype(BF16)
    return hi, mid, lo


def _ssd_kernel(xs_ref, bm_ref, cm_ref, dt_ref, z_ref, alog_ref, dsk_ref, nw_ref, h0_ref,
                y_ref, h_ref, ht, *, q_in):
    c = pl.program_id(1)
    nc = pl.num_programs(1)

    @pl.when(c == 0)
    def _():
        for hh in range(SSM_HEADS):
            ht[hh] = h0_ref[0, hh].T

    def rows(ref):
        x = ref[0]
        if q_in == SSD_Q:
            return x
        return jnp.concatenate([x, jnp.zeros((SSD_Q - q_in, x.shape[1]), x.dtype)], axis=0)

    xs, bm, cm, dt, z = rows(xs_ref), rows(bm_ref), rows(cm_ref), rows(dt_ref), rows(z_ref)
    a = -jnp.exp(alog_ref[...])
    da = dt * a
    ri = lax.broadcasted_iota(jnp.int32, (SSD_Q, SSD_Q), 0)
    ci = lax.broadcasted_iota(jnp.int32, (SSD_Q, SSD_Q), 1)
    causal = ri >= ci
    tril = jnp.where(causal, 1.0, 0.0).astype(BF16)
    hi, mid, lo = _split3(da)
    acum = _dot(tril, hi) + _dot(tril, mid) + _dot(tril, lo)
    acum_t = acum.T

    for g in range(SSM_GROUPS):
        bm_g = bm[:, g * SSM_STATE:(g + 1) * SSM_STATE]
        cm_g = cm[:, g * SSM_STATE:(g + 1) * SSM_STATE].astype(BF16)
        bm_t = bm_g.T.astype(BF16)
        cb = _dot(cm_g, bm_t)
        gs = []
        ssq = jnp.zeros((SSD_Q, 1), F32)
        for r in range(SSM_HPG):
            hh = g * SSM_HPG + r
            lanes = slice(hh * SSM_HEAD_DIM, (hh + 1) * SSM_HEAD_DIM)
            xh = xs[:, lanes]
            dl = DT_LANE0 + hh
            col = acum[:, dl:dl + 1]
            rowv = acum_t[dl:dl + 1, :]
            decay = jnp.exp(jnp.where(causal, col - rowv, NEG))
            xdt = xh * dt[:, dl:dl + 1]
            y = _dot((cb * decay).astype(BF16), xdt.astype(BF16))
            h_prev = ht[hh]
            y = y + jnp.exp(col) * _dot(cm_g, h_prev.astype(BF16))
            last = acum[SSD_Q - 1:SSD_Q, dl:dl + 1]
            xw = (xdt * jnp.exp(last - col)).astype(BF16)
            ht[hh] = h_prev * jnp.exp(last) + _dot(bm_t, xw)
            y = y + xh * dsk_ref[:, lanes]
            gate = y * _silu(z[:, lanes])
            ssq = ssq + jnp.sum(gate * gate, axis=1, keepdims=True)
            gs.append(gate)
        scale = lax.rsqrt(ssq / float(SSM_HPG * SSM_HEAD_DIM) + NORM_EPS)
        for r in range(SSM_HPG):
            hh = g * SSM_HPG + r
            lanes = slice(hh * SSM_HEAD_DIM, (hh + 1) * SSM_HEAD_DIM)
            y_ref[0, :, lanes] = (gs[r] * scale * nw_ref[:, lanes])[:q_in].astype(BF16)

    @pl.when(c == nc - 1)
    def _():
        for hh in range(SSM_HEADS):
            h_ref[0, hh] = ht[hh].T


def _ssd(xs, bm, cm, dt, z, alog, dsk, nw, h0, *, q_in):
    b, t, _ = xs.shape
    row = lambda width: pl.BlockSpec((1, q_in, width), lambda bi, c: (bi, c, 0))
    full = lambda a: _resident(a.shape, lambda bi, c: (0,) * a.ndim)
    hspec = pl.BlockSpec((1, SSM_HEADS, SSM_HEAD_DIM, SSM_STATE), lambda bi, c: (bi, 0, 0, 0))
    return pl.pallas_call(
        functools.partial(_ssd_kernel, q_in=q_in),
        grid=(b, t // q_in),
        in_specs=[row(D_SSM), row(256), row(256), row(LANE), row(D_SSM),
                  full(alog), full(dsk), full(nw), hspec],
        out_specs=(row(D_SSM), hspec),
        out_shape=(jax.ShapeDtypeStruct((b, t, D_SSM), BF16),
                   jax.ShapeDtypeStruct((b, SSM_HEADS, SSM_HEAD_DIM, SSM_STATE), F32)),
        scratch_shapes=[pltpu.VMEM((SSM_HEADS, SSM_STATE, SSM_HEAD_DIM), F32)],
        compiler_params=pltpu.CompilerParams(
            dimension_semantics=("arbitrary", "arbitrary"), vmem_limit_bytes=VMEM_LIMIT),
        name="ssd",
    )(xs, bm, cm, dt, z, alog, dsk, nw, h0)


def _mix_ffn_kernel(x_ref, ya_ref, yb_ref, yc_ref, wo_ref, npost_ref, nfpre_ref, wg_ref, wu_ref,
                    wd_ref, cfw_ref, cfb_ref, nfpost_ref, buff_ref, o_ref, buff_out, eg, *, tm):
    i = pl.program_id(1)

    @pl.when(i == 0)
    def _():
        eg[CARRY - 2:CARRY, :] = buff_ref[0]

    mix = (_dot(ya_ref[0], wo_ref[0:D_CONV_MIX, :])
           + _dot(yb_ref[0], wo_ref[D_CONV_MIX:D_CONV_MIX + D_ATT, :])
           + _dot(yc_ref[0], wo_ref[D_CONV_MIX + D_ATT:, :]))
    x1 = x_ref[0] + _rms(mix, npost_ref[...])
    u = _rms(x1, nfpre_ref[...]).astype(BF16)
    eg[CARRY:CARRY + tm, :] = _dot(u, wg_ref[...])
    gc = eg[CARRY - 2:CARRY - 2 + tm, :] * cfw_ref[0:1, :]
    for j in range(1, CONV_F_W):
        gc = gc + eg[CARRY - 2 + j:CARRY - 2 + j + tm, :] * cfw_ref[j:j + 1, :]
    hid = (_silu(gc + cfb_ref[...]) * _dot(u, wu_ref[...])).astype(BF16)
    o_ref[0] = x1 + _rms(_dot(hid, wd_ref[...]), nfpost_ref[...])
    tail = eg[CARRY + tm - 2:CARRY + tm, :]
    buff_out[0] = tail
    eg[CARRY - 2:CARRY, :] = tail


def _mix_ffn(x, ya, yb, yc, wo, npost, nfpre, wg, wu, wd, cfw, cfb, nfpost, buff, *, tm):
    b, t, _ = x.shape
    row = lambda width: pl.BlockSpec((1, tm, width), lambda bi, i: (bi, i, 0))
    full = lambda a: _resident(a.shape, lambda bi, i: (0,) * a.ndim)
    state = pl.BlockSpec((1, CONV_F_W - 1, D_FF), lambda bi, i: (bi, 0, 0))
    return pl.pallas_call(
        functools.partial(_mix_ffn_kernel, tm=tm),
        grid=(b, t // tm),
        in_specs=[row(D_MODEL), row(D_CONV_MIX), row(D_ATT), row(D_SSM), full(wo), full(npost),
                  full(nfpre), full(wg), full(wu), full(wd), full(cfw), full(cfb), full(nfpost),
                  state],
        out_specs=(row(D_MODEL), state),
        out_shape=(jax.ShapeDtypeStruct((b, t, D_MODEL), F32),
                   jax.ShapeDtypeStruct((b, CONV_F_W - 1, D_FF), F32)),
        scratch_shapes=[pltpu.VMEM((CARRY + tm, D_FF), F32)],
        compiler_params=pltpu.CompilerParams(
            dimension_semantics=("arbitrary", "arbitrary"), vmem_limit_bytes=VMEM_LIMIT),
        name="mix_ffn",
    )(x, ya, yb, yc, wo, npost, nfpre, wg, wu, wd, cfw, cfb, nfpost, buff)


def _pack_keys_kernel(ck_ref, cv_ref, cki_ref, k_ref, v_ref, ki_ref, kb_ref, vt_ref, kib_ref, *, past_steps):
    j = pl.program_id(1)
    rows = kb_ref.shape[1]

    def pick(cache_ref, new_ref):
        new = new_ref[0]
        new = jnp.concatenate([new, jnp.zeros((rows - new.shape[0], new.shape[1]), F32)], axis=0)
        return jnp.where(j < past_steps, cache_ref[0], new)

    kb_ref[0] = pick(ck_ref, k_ref).astype(BF16)
    vt_ref[0] = _augment_vt(pick(cv_ref, v_ref).T).astype(BF16)
    kib_ref[0] = pick(cki_ref, ki_ref).astype(BF16)


def _pack_keys(ck, cv, cki, k, v, ki):
    b, past, _ = ck.shape
    t = k.shape[1]
    rows = BIG * KEY_TILE
    assert past % rows == 0 and t <= rows
    past_steps = past // rows
    lk = past + rows
    cache = lambda width: pl.BlockSpec((1, rows, width),
                                       lambda bi, j: (bi, jnp.minimum(j, past_steps - 1), 0))
    new = lambda width: pl.BlockSpec((1, t, width), lambda bi, j: (bi, 0, 0))
    out = lambda width: pl.BlockSpec((1, rows, width), lambda bi, j: (bi, j, 0))
    return pl.pallas_call(
        functools.partial(_pack_keys_kernel, past_steps=past_steps),
        grid=(b, past_steps + 1),
        in_specs=[cache(128), cache(128), cache(D_IDX), new(128), new(128), new(D_IDX)],
        out_specs=(out(128), pl.BlockSpec((1, N_KV_HEADS * VT_ROWS, rows), lambda bi, j: (bi, 0, j)),
                   out(D_IDX)),
        out_shape=(jax.ShapeDtypeStruct((b, lk, 128), BF16),
                   jax.ShapeDtypeStruct((b, N_KV_HEADS * VT_ROWS, lk), BF16),
                   jax.ShapeDtypeStruct((b, lk, D_IDX), BF16)),
        compiler_params=pltpu.CompilerParams(
            dimension_semantics=("arbitrary", "arbitrary"), vmem_limit_bytes=VMEM_LIMIT),
        name="pack_keys",
    )(ck, cv, cki, k, v, ki)


def _layer(x, p, rel_bias, buf_a, buf_s, h0, buf_f, kv_past, *, tm):
    b, t, _ = x.shape
    outs = _in_proj(x, p['norm_mix_pre'], p['w_in'], p['conv_a_w'], p['conv_ssm_w'], p['conv_ssm_b'],
                    p['dt_bias'], buf_a, buf_s, tm=tm, key_side=kv_past is None)
    (ya, q, k, v, qi, ki, sgn, z, xs, bm, cm, dt, buf_a_new, buf_s_new) = outs[:14]
    if kv_past is None:
        kb, vt, kib = outs[14:]
        yb = _dsa(rel_bias, q, qi, sgn, kb, vt, kib, q_pos0=0, l_valid=t)
    else:
        ck, cv, cki = kv_past
        past = ck.shape[1]
        qpad = lambda a: jnp.pad(a, ((0, 0), (0, -t % KEY_TILE), (0, 0)))
        kb, vt, kib = _pack_keys(ck.reshape(b, past, -1), cv.reshape(b, past, -1), cki, k, v, ki)
        yb = _dsa(rel_bias, qpad(q), qpad(qi), sgn, kb, vt, kib, q_pos0=past, l_valid=past + t)[:, :t]
    yc, h_new = _ssd(xs, bm, cm, dt, z, p['a_log'], p['d_skip'], p['ssm_norm_w'],
                     h0.reshape(b, SSM_HEADS, SSM_HEAD_DIM, SSM_STATE), q_in=min(t, SSD_Q))
    x_new, buf_f_new = _mix_ffn(x, ya, yb, yc, p['w_out'], p['norm_mix_post'], p['norm_ffn_pre'],
                                p['w_gate'], p['w_up'], p['w_down'], p['conv_ffn_w'],
                                p['conv_ffn_b'], p['norm_ffn_post'], buf_f, tm=tm)
    st = (k.reshape(b, t, N_KV_HEADS, HEAD_DIM), v.reshape(b, t, N_KV_HEADS, HEAD_DIM), ki,
          buf_a_new, buf_s_new, h_new.reshape(b, SSM_GROUPS, SSM_HPG, SSM_HEAD_DIM, SSM_STATE),
          buf_f_new)
    return x_new, st


def _pack_w_in_kernel(wt_ref, o_ref):
    for c, c0 in enumerate(W_IN_TILE_COLS):
        o_ref[:, c * LANE:(c + 1) * LANE] = wt_ref[c0:c0 + LANE, :].T.astype(BF16)


def _pack_w_in(wt):
    return pl.pallas_call(
        _pack_w_in_kernel,
        out_shape=jax.ShapeDtypeStruct((wt.shape[1], D_IN_PACKED), BF16),
        compiler_params=pltpu.CompilerParams(vmem_limit_bytes=VMEM_LIMIT),
        name="pack_w_in",
    )(wt)


def _dt_lanes(v):
    return jnp.pad(v, (DT_LANE0, 0))[None, :]


def kernel(x_prompt, x_sample, cache_k, cache_v, cache_kidx, state_conv_a, state_conv_ssm, state_ssm, state_conv_ffn, rel_bias, norm_mix_pre, norm_mix_post, norm_ffn_pre, norm_ffn_post, w_in, conv_a_w, conv_ssm_w, conv_ssm_b, dt_bias, a_log, d_skip, ssm_norm_w, w_out, w_gate, w_up, conv_ffn_w, conv_ffn_b, w_down):
    depth = w_in.shape[0]
    bp = x_prompt.shape[0]
    yp, ys = x_prompt, x_sample
    outs_p, outs_s = [], []
    for l in range(depth):
        p = {'norm_mix_pre': norm_mix_pre[l][None], 'norm_mix_post': norm_mix_post[l][None],
             'norm_ffn_pre': norm_ffn_pre[l][None], 'norm_ffn_post': norm_ffn_post[l][None],
             'w_in': _pack_w_in(jnp.swapaxes(w_in[l], 0, 1)), 'conv_a_w': conv_a_w[l], 'conv_ssm_w': conv_ssm_w[l],
             'conv_ssm_b': conv_ssm_b[l][None], 'dt_bias': _dt_lanes(dt_bias[l]),
             'a_log': _dt_lanes(a_log[l]), 'd_skip': jnp.repeat(d_skip[l], SSM_HEAD_DIM)[None],
             'ssm_norm_w': ssm_norm_w[l][None], 'w_out': w_out[l].astype(BF16),
             'w_gate': w_gate[l].astype(BF16), 'w_up': w_up[l].astype(BF16),
             'conv_ffn_w': conv_ffn_w[l], 'conv_ffn_b': conv_ffn_b[l][None],
             'w_down': w_down[l].astype(BF16)}
        yp, st_p = _layer(
            yp, p, rel_bias,
            jnp.zeros((bp, CONV_A_W - 1, D_CONV_MIX), F32),
            jnp.zeros((bp, SSM_CONV_W - 1, D_XBC), F32),
            jnp.zeros((bp, SSM_GROUPS, SSM_HPG, SSM_HEAD_DIM, SSM_STATE), F32),
            jnp.zeros((bp, CONV_F_W - 1, D_FF), F32),
            None, tm=256)
        ys, st_s = _layer(
            ys, p, rel_bias, state_conv_a[l], state_conv_ssm[l], state_ssm[l], state_conv_ffn[l],
            (cache_k[l], cache_v[l], cache_kidx[l]), tm=ys.shape[1])
        outs_p.append(st_p)
        outs_s.append(st_s)

    def stack(outs, i):
        return jnp.stack([o[i] for o in outs], axis=0)

    res = [yp, ys]
    for i in range(7):
        res.append(stack(outs_p, i))
        res.append(stack(outs_s, i))
    return tuple(res)
```

```python
import functools

import jax
import jax.numpy as jnp
from jax import lax
from jax.experimental import pallas as pl
from jax.experimental.pallas import tpu as pltpu

F32 = jnp.float32
BF16 = jnp.bfloat16

D_MODEL = 1024
CHUNK = 64
D_CONV_MIX = 256
CONV_A_W = 3
N_HEADS = 6
N_KV_HEADS = 2
HEAD_DIM = 64
D_ATT = N_HEADS * HEAD_DIM
GRP = N_HEADS // N_KV_HEADS
N_IDX_HEADS = 4
D_IDX = 64
TOPK = 256
N_BUCKETS = 32
SSM_HEADS = 6
SSM_HEAD_DIM = 64
D_SSM = SSM_HEADS * SSM_HEAD_DIM
SSM_GROUPS = 2
SSM_HPG = SSM_HEADS // SSM_GROUPS
SSM_STATE = 128
SSM_CONV_W = 4
D_XBC = D_SSM + 2 * SSM_GROUPS * SSM_STATE
D_FF = 2816
CONV_F_W = 3
NORM_EPS = 1e-6

LANE = 128
CARRY = 8
SSD_Q = 128
KEY_TILE = 128
BIG = 4
VMEM_LIMIT = 56 * 1024 * 1024
NEG = -1e30
LOG2E = 1.4426950408889634
F32_MAX = 3.4028234663852886e38
F32_TINY = 1.1754943508222875e-38
NO_QUOTA = 1e9
MAX_SEARCH_STEPS = 256
BLIND_SEARCH_STEPS = 14

C_AB, C_AC, C_AH = 0, 256, 512
C_Q = 768
C_K = 1152
C_V = 1280
C_QI = 1408
C_KI = 1664
C_Z = 1792
C_XBC = 2176
C_DT = 3072
D_IN_PACKED = 3200
D_IN = 3018
D_IN_HEAD = 1732
DT_LANE0 = LANE - SSM_HEADS
W_IN_TILE_COLS = ([c * LANE for c in range(C_Z // LANE)]
                  + [D_IN_HEAD + c * LANE for c in range((C_DT - C_Z) // LANE)] + [D_IN - LANE])


def _rms(x, w):
    return x * lax.rsqrt(jnp.mean(x * x, axis=-1, keepdims=True) + NORM_EPS) * w


def _dot(a, b):
    return jnp.dot(a, b, preferred_element_type=F32)


def _silu(x):
    return x * jax.nn.sigmoid(x)


VT_ROWS = HEAD_DIM + 16


def _augment_vt(vt):
    extra_shape = vt.shape[:-2] + (VT_ROWS - HEAD_DIM, vt.shape[-1])
    row = lax.broadcasted_iota(jnp.int32, extra_shape, len(extra_shape) - 2)
    extra = jnp.where(row == 0, 1.0, 0.0).astype(vt.dtype)
    parts = []
    for g in range(N_KV_HEADS):
        parts += [vt[..., g * HEAD_DIM:(g + 1) * HEAD_DIM, :], extra]
    return jnp.concatenate(parts, axis=-2)


def _resident(shape, index_map):
    return pl.BlockSpec(shape, index_map, pipeline_mode=pl.Buffered(1))


def _in_proj_kernel(x_ref, nw_ref, w_ref, caw_ref, csw_ref, csb_ref, dtb_ref, bufa_ref, bufs_ref,
                    ya_ref, q_ref, k_ref, v_ref, qi_ref, ki_ref, sgn_ref, z_ref, xs_ref, bm_ref,
                    cm_ref, dt_ref, bufa_out, bufs_out, *rest, tm, key_side):
    if key_side:
        kb_ref, vt_ref, kib_ref, ea, es = rest
    else:
        ea, es = rest
    i = pl.program_id(1)
    u = _rms(x_ref[0], nw_ref[...]).astype(BF16)

    def proj(c0, width):
        return _dot(u, w_ref[:, c0:c0 + width])

    @pl.when(i == 0)
    def _():
        ea[CARRY - 2:CARRY, :] = bufa_ref[0]
        es[CARRY - 3:CARRY, :] = bufs_ref[0]

    ea[CARRY:CARRY + tm, :] = proj(C_AC, D_CONV_MIX) * proj(C_AH, D_CONV_MIX)
    conv = ea[CARRY - 2:CARRY - 2 + tm, :] * caw_ref[0:1, :]
    for j in range(1, CONV_A_W):
        conv = conv + ea[CARRY - 2 + j:CARRY - 2 + j + tm, :] * caw_ref[j:j + 1, :]
    ya_ref[0] = (proj(C_AB, D_CONV_MIX) * conv).astype(BF16)
    tail_a = ea[CARRY + tm - 2:CARRY + tm, :]
    bufa_out[0] = tail_a
    ea[CARRY - 2:CARRY, :] = tail_a

    q_ref[0] = (proj(C_Q, D_ATT) * (HEAD_DIM ** -0.5 * LOG2E)).astype(BF16)
    k = proj(C_K, N_KV_HEADS * HEAD_DIM)
    v = proj(C_V, N_KV_HEADS * HEAD_DIM)
    kiw = proj(C_KI, LANE)
    k_ref[0] = k
    v_ref[0] = v
    ki_ref[0] = kiw[:, :D_IDX]
    wi = kiw[:, D_IDX:D_IDX + N_IDX_HEADS]
    kiw_rows = kiw if tm >= LANE else jnp.concatenate([kiw, jnp.zeros((LANE - tm, LANE), F32)], axis=0)
    sgn_ref[0] = jnp.where(kiw_rows.T[D_IDX:D_IDX + 8, :] >= 0.0, 1.0, -1.0).astype(F32)
    if key_side:
        kb_ref[0] = k.astype(BF16)
        vt_ref[0] = _augment_vt(v.T).astype(BF16)
        kib_ref[0] = kiw[:, :D_IDX].astype(BF16)
    wabs = jnp.abs(wi) * (D_IDX ** -0.5 * N_IDX_HEADS ** -0.5)
    qi = proj(C_QI, N_IDX_HEADS * D_IDX)
    for h in range(N_IDX_HEADS):
        qi_ref[0, :, h * D_IDX:(h + 1) * D_IDX] = (
            qi[:, h * D_IDX:(h + 1) * D_IDX] * wabs[:, h:h + 1]).astype(BF16)

    z_ref[0] = proj(C_Z, D_SSM)
    es[CARRY:CARRY + tm, :] = proj(C_XBC, D_XBC)
    xc = es[CARRY - 3:CARRY - 3 + tm, :] * csw_ref[0:1, :]
    for j in range(1, SSM_CONV_W):
        xc = xc + es[CARRY - 3 + j:CARRY - 3 + j + tm, :] * csw_ref[j:j + 1, :]
    xc = _silu(xc + csb_ref[...])
    xs_ref[0] = xc[:, :D_SSM]
    bm_ref[0] = xc[:, D_SSM:D_SSM + SSM_GROUPS * SSM_STATE]
    cm_ref[0] = xc[:, D_SSM + SSM_GROUPS * SSM_STATE:]
    tail_s = es[CARRY + tm - 3:CARRY + tm, :]
    bufs_out[0] = tail_s
    es[CARRY - 3:CARRY, :] = tail_s
    dtr = proj(C_DT, LANE) + dtb_ref[...]
    dt_ref[0] = jnp.maximum(dtr, 0.0) + jnp.log1p(jnp.exp(-jnp.abs(dtr)))


def _in_proj(x, nw, w_packed, caw, csw, csb, dtb, bufa, bufs, *, tm, key_side):
    b, t, _ = x.shape
    tq = max(tm, LANE)
    grid = (b, t // tm)
    row = lambda width: pl.BlockSpec((1, tm, width), lambda bi, i: (bi, i, 0))
    full = lambda a: _resident(a.shape, lambda bi, i: (0,) * a.ndim)
    state = lambda r, width: pl.BlockSpec((1, r, width), lambda bi, i: (bi, 0, 0))
    out_shapes = (
        jax.ShapeDtypeStruct((b, t, D_CONV_MIX), BF16),
        jax.ShapeDtypeStruct((b, t, D_ATT), BF16),
        jax.ShapeDtypeStruct((b, t, 128), F32),
        jax.ShapeDtypeStruct((b, t, 128), F32),
        jax.ShapeDtypeStruct((b, t, 256), BF16),
        jax.ShapeDtypeStruct((b, t, D_IDX), F32),
        jax.ShapeDtypeStruct((b, 8, t // tm * tq), F32),
        jax.ShapeDtypeStruct((b, t, D_SSM), F32),
        jax.ShapeDtypeStruct((b, t, D_SSM), F32),
        jax.ShapeDtypeStruct((b, t, 256), F32),
        jax.ShapeDtypeStruct((b, t, 256), F32),
        jax.ShapeDtypeStruct((b, t, LANE), F32),
        jax.ShapeDtypeStruct((b, CONV_A_W - 1, D_CONV_MIX), F32),
        jax.ShapeDtypeStruct((b, SSM_CONV_W - 1, D_XBC), F32),
    )
    out_specs = (row(D_CONV_MIX), row(D_ATT), row(128), row(128), row(256), row(D_IDX),
                 pl.BlockSpec((1, 8, tq), lambda bi, i: (bi, 0, i)),
                 row(D_SSM), row(D_SSM), row(256), row(256), row(LANE),
                 state(CONV_A_W - 1, D_CONV_MIX), state(SSM_CONV_W - 1, D_XBC))
    if key_side:
        out_shapes += (jax.ShapeDtypeStruct((b, t, 128), BF16),
                       jax.ShapeDtypeStruct((b, N_KV_HEADS * VT_ROWS, t), BF16),
                       jax.ShapeDtypeStruct((b, t, D_IDX), BF16))
        out_specs += (row(128), pl.BlockSpec((1, N_KV_HEADS * VT_ROWS, tm), lambda bi, i: (bi, 0, i)), row(D_IDX))
    in_specs = [row(D_MODEL), full(nw), full(w_packed), full(caw), full(csw), full(csb), full(dtb),
                state(CONV_A_W - 1, D_CONV_MIX), state(SSM_CONV_W - 1, D_XBC)]
    return pl.pallas_call(
        functools.partial(_in_proj_kernel, tm=tm, key_side=key_side),
        grid=grid, in_specs=in_specs, out_specs=out_specs, out_shape=out_shapes,
        scratch_shapes=[pltpu.VMEM((CARRY + tm, D_CONV_MIX), F32),
                        pltpu.VMEM((CARRY + tm, D_XBC), F32)],
        compiler_params=pltpu.CompilerParams(
            dimension_semantics=("arbitrary", "arbitrary"), vmem_limit_bytes=VMEM_LIMIT),
        name="in_proj",
    )(x, nw, w_packed, caw, csw, csb, dtb, bufa, bufs)


def _t5_bucket(rel):
    n = jnp.abs(rel)
    large = jnp.full(rel.shape, 8, jnp.int32)
    for brk in (12, 16, 23, 32, 46, 64, 91):
        large = large + jnp.where(n >= brk, 1, 0)
    return jnp.where(rel > 0, N_BUCKETS // 2, 0) + jnp.where(n < 8, n, large)


def _f32_key(x):
    bits = lax.bitcast_convert_type(x, jnp.int32)
    return bits ^ ((bits >> 31) & jnp.int32(0x7FFFFFFF))


def _key_f32(key):
    return lax.bitcast_convert_type(key ^ ((key >> 31) & jnp.int32(0x7FFFFFFF)), F32)


def _dsa_kernel(relb_ref, q_ref, qi_ref, sgn_ref, k_ref, vt_ref, ki_ref, o_ref, sc, bias, sbuf,
                *, q_pos0, l_valid):
    tq = KEY_TILE
    i = pl.program_id(1)
    q0 = q_pos0 + i * tq
    qt = q0 // KEY_TILE
    n0 = jnp.maximum(qt - 1, 0)
    variant = jnp.minimum(qt, 1)
    steps_far = (n0 + BIG - 1) // BIG
    steps_all = (n0 + 2 + BIG - 1) // BIG
    kw = BIG * KEY_TILE
    kf = float(TOPK)

    def step_keys(j):
        return pl.ds(pl.multiple_of(j * kw, kw), kw)

    near_keys = pl.ds(pl.multiple_of(n0 * KEY_TILE, KEY_TILE), 2 * KEY_TILE)

    @pl.when(i == 0)
    def _():
        sc[...] = jnp.full(sc.shape, -jnp.inf, F32)

    @pl.when((i == 0) & (pl.program_id(0) == 0))
    def _():
        kk = lax.broadcasted_iota(jnp.int32, (2 * KEY_TILE, tq), 0)
        r = lax.broadcasted_iota(jnp.int32, (2 * KEY_TILE, tq), 1)
        for var in range(2):
            bucket = _t5_bucket(kk - var * KEY_TILE - r)
            for h in range(N_HEADS):
                tab = jnp.zeros((2 * KEY_TILE, tq), F32)
                for bk in range(N_BUCKETS):
                    tab = jnp.where(bucket == bk, relb_ref[bk, h], tab)
                g, j = divmod(h, GRP)
                bias[var, g, :, j * tq:(j + 1) * tq] = (tab - relb_ref[N_BUCKETS // 2 - 1, h]) * LOG2E

    qit = qi_ref[0].astype(F32).T
    qis = jnp.concatenate([qit[h * D_IDX:(h + 1) * D_IDX, :] for h in range(N_IDX_HEADS)],
                          axis=1).astype(BF16)
    sgn = sgn_ref[0]

    def scores(keys):
        s = jnp.maximum(_dot(ki_ref[0, keys, :], qis), 0.0)
        tot = s[:, 0:tq] * sgn[0:1, :]
        for h in range(1, N_IDX_HEADS):
            tot = tot + s[:, h * tq:(h + 1) * tq] * sgn[h:h + 1, :]
        return tot

    zero = jnp.zeros((KEY_TILE, tq), F32)

    def tally(c, s):
        ge, gt, mx = c
        return (ge + jnp.where(s >= 0.0, 1.0, 0.0), gt + jnp.where(s >= F32_TINY, 1.0, 0.0),
                jnp.maximum(mx, s))

    def score_far(j, c):
        s = scores(step_keys(j))
        for k in range(BIG):
            t = j * BIG + k
            tile = jnp.where(t < n0, s[k * KEY_TILE:(k + 1) * KEY_TILE, :], -jnp.inf)
            sc[t] = tile
            c = tally(c, tile)
        return c

    far_pairs = steps_far // 2
    c = lax.fori_loop(0, far_pairs, lambda j, c: score_far(2 * j + 1, score_far(2 * j, c)),
                      (zero, zero, jnp.full((KEY_TILE, tq), -F32_MAX, F32)))
    c = lax.fori_loop(2 * far_pairs, steps_far, score_far, c)
    s_near = scores(near_keys)
    kpos = n0 * KEY_TILE + lax.broadcasted_iota(jnp.int32, (2 * KEY_TILE, tq), 0)
    qpos = q0 + lax.broadcasted_iota(jnp.int32, (2 * KEY_TILE, tq), 1)
    adm = (kpos < l_valid) & ((kpos >> 6) <= (qpos >> 6))
    s_near = jnp.where(adm, s_near, -jnp.inf)
    sc[n0] = s_near[:KEY_TILE, :]
    sc[n0 + 1] = s_near[KEY_TILE:, :]
    for k in range(BIG - 1):
        @pl.when(n0 + 2 + k < steps_all * BIG)
        def _():
            sc[n0 + 2 + k] = jnp.full((KEY_TILE, tq), -jnp.inf, F32)
    ge, gt, mx = tally(tally(c, s_near[:KEY_TILE, :]), s_near[KEY_TILE:, :])


    def count_ge(t):
        tb = jnp.broadcast_to(t, (KEY_TILE, tq))

        def body(j, acc):
            for k in range(BIG):
                acc = acc + jnp.where(sc[j * BIG + k] >= tb, 1.0, 0.0)
            return acc
        acc = lax.fori_loop(0, steps_all, body, zero)
        parts = [acc[r:r + 8, :] for r in range(0, KEY_TILE, 8)]
        while len(parts) > 1:
            parts = [parts[r] + parts[r + 1] for r in range(0, len(parts), 2)]
        return jnp.sum(parts[0], axis=0, keepdims=True)

    ge0 = jnp.sum(ge, axis=0, keepdims=True)
    gt0 = jnp.sum(gt, axis=0, keepdims=True)
    rmax = jnp.max(mx, axis=0, keepdims=True)
    qrow = q0 + lax.broadcasted_iota(jnp.int32, (1, tq), 1)
    nadm = jnp.minimum(((qrow >> 6) + 1) * CHUNK, l_valid).astype(F32)
    small = nadm <= kf
    tie0 = (~small) & (gt0 < kf) & (ge0 >= kf)
    pos = (~small) & (gt0 >= kf)
    neg = (~small) & (ge0 < kf)
    lo0 = jnp.where(pos, F32_TINY, -F32_MAX)
    flo0 = jnp.where(pos, gt0, nadm)
    hi0 = jnp.where(neg, 0.0, jnp.inf)
    fhi0 = jnp.where(neg, ge0, 0.0)
    done0 = jnp.where(small | tie0 | (pos & (gt0 == kf)), 1.0, 0.0)
    thr0 = jnp.where(small, -F32_MAX, jnp.where(tie0, 0.0, F32_TINY))
    quota0 = jnp.where(tie0, kf - gt0, NO_QUOTA)

    def search_step(it, state):
        lo, flo, hi, fhi, done, thr, quota = state
        klo, khi = _f32_key(lo), _f32_key(hi)
        la = jnp.log(flo)
        frac = (la - jnp.log(kf)) / (la - jnp.log(jnp.maximum(fhi, 0.5)))
        interp = lo + (hi - lo) * jnp.where(it % 4 == 1, 0.5, frac)
        t_up = lo + (rmax - lo) * 0.5
        t_dn = hi - jnp.maximum(jnp.maximum(jnp.abs(rmax - hi), jnp.abs(hi)), F32_TINY)
        open_hi = hi == jnp.inf
        open_lo = lo <= -F32_MAX
        t = jnp.where(open_hi, t_up, jnp.where(open_lo, t_dn, interp))
        tk = jnp.minimum(jnp.maximum(_f32_key(t), klo + 1), khi - 1)
        tk = jnp.where(it % 4 == 3, klo + lax.shift_right_logical(khi - klo, 1), tk)
        t = _key_f32(tk)
        f = count_ge(t)
        act = done == 0.0
        hit = act & (f == kf)
        up = act & (f > kf)
        dn = act & (f < kf)
        lo = jnp.where(up, t, lo)
        flo = jnp.where(up, f, flo)
        hi = jnp.where(dn, t, hi)
        fhi = jnp.where(dn, f, fhi)
        adj = act & (~hit) & (_f32_key(lo) + 1 == _f32_key(hi))
        thr = jnp.where(hit, t, jnp.where(adj, lo, thr))
        quota = jnp.where(adj, kf - fhi, quota)
        done = jnp.where(hit | adj, 1.0, done)
        return lo, flo, hi, fhi, done, thr, quota

    state = lax.fori_loop(0, BLIND_SEARCH_STEPS, search_step, (lo0, flo0, hi0, fhi0, done0, thr0, quota0))

    def search_cond(c):
        return (c[0] < MAX_SEARCH_STEPS) & (c[1] > 0)

    def search_body(c):
        state = search_step(c[0], c[2])
        return c[0] + 1, jnp.sum(1.0 - state[4]), state

    _, _, state = lax.while_loop(search_cond, search_body,
                                 (jnp.int32(BLIND_SEARCH_STEPS), jnp.sum(1.0 - state[4]), state))
    thr, quota = state[5], state[6]
    thr_b = jnp.broadcast_to(thr, (tq, KEY_TILE))
    quota_b = jnp.broadcast_to(quota, (tq, KEY_TILE))
    has_ties = jnp.min(quota) < NO_QUOTA

    @pl.when(jnp.logical_not(has_ties))
    def _():
        def body(j, c):
            for k in range(BIG):
                t = j * BIG + k
                sc[t] = jnp.where(sc[t] >= thr_b, 0.0, NEG)
            return c
        lax.fori_loop(0, steps_all, body, 0)

    @pl.when(has_ties)
    def _():
        ri = lax.broadcasted_iota(jnp.int32, (KEY_TILE, KEY_TILE), 0)
        ci = lax.broadcasted_iota(jnp.int32, (KEY_TILE, KEY_TILE), 1)
        lower = jnp.where(ri >= ci, 1.0, 0.0).astype(BF16)

        def rank_tiles(first, n, seen):
            for k in range(n):
                t = first + k
                s = sc[t]
                tie = jnp.where(s == thr_b, 1.0, 0.0)
                rank = seen + _dot(lower, tie.astype(BF16))
                keep = (s > thr_b) | ((tie > 0.0) & (rank <= quota_b))
                sc[t] = jnp.where(keep, 0.0, NEG)
                seen = seen + jnp.sum(tie, axis=0, keepdims=True)
            return seen

        pairs = steps_all // 2
        seen = lax.fori_loop(0, pairs, lambda j, c: rank_tiles(j * 2 * BIG, 2 * BIG, c),
                             jnp.zeros((1, tq), F32))
        lax.fori_loop(2 * pairs, steps_all, lambda j, c: rank_tiles(j * BIG, BIG, c), seen)

    qt_all = q_ref[0].astype(F32).T
    zeros_half = jnp.zeros((HEAD_DIM, GRP * tq), F32)
    qa = []
    for g in range(N_KV_HEADS):
        piece = jnp.concatenate(
            [qt_all[(g * GRP + j) * HEAD_DIM:(g * GRP + j + 1) * HEAD_DIM, :] for j in range(GRP)],
            axis=1)
        halves = [piece, zeros_half] if g == 0 else [zeros_half, piece]
        qa.append(jnp.concatenate(halves, axis=0).astype(BF16))

    def logits(slot, keys, pens, m_old, extra=None):
        pen = jnp.concatenate(pens, axis=0)
        pen = jnp.concatenate([pen] * GRP, axis=1)
        kblk = k_ref[0, keys, :]
        m_new = []
        for g in range(N_KV_HEADS):
            s = _dot(kblk, qa[g]) + pen
            if extra is not None:
                s = s + extra[g]
            sbuf[slot, g, 0:pen.shape[0], :] = s
            m_new.append(jnp.maximum(m_old[g], jnp.max(s, axis=0, keepdims=True)))
        return tuple(m_new)

    def accumulate(slot, keys, nk, m_old, m_new, accs):
        out = []
        for g in range(N_KV_HEADS):
            p = jnp.exp2(sbuf[slot, g, 0:nk, :] - m_new[g])
            out.append(jnp.exp2(m_old[g] - m_new[g]) * accs[g]
                       + _dot(vt_ref[0, g * VT_ROWS:(g + 1) * VT_ROWS, keys], p.astype(BF16)))
        return tuple(out)

    last_far = jnp.maximum(steps_far - 1, 0)

    def far_keys(j):
        return step_keys(jnp.minimum(j, last_far))

    def far_pens(j):
        return [jnp.where(j * BIG + k < n0, sc[jnp.minimum(j * BIG + k, n0)], NEG) for k in range(BIG)]

    m_init = tuple(jnp.full((1, GRP * tq), NEG, F32) for _ in range(N_KV_HEADS))
    acc_init = tuple(jnp.zeros((VT_ROWS, GRP * tq), F32) for _ in range(N_KV_HEADS))
    m_first = logits(0, far_keys(0), far_pens(0), m_init)

    def stage_pair(i, c):
        m_old, m_cur, accs = c
        a = 2 * i + 1
        m_a = logits(1, far_keys(a), far_pens(a), m_cur)
        accs = accumulate(0, far_keys(a - 1), kw, m_old, m_cur, accs)
        m_b = logits(0, far_keys(a + 1), far_pens(a + 1), m_a)
        accs = accumulate(1, far_keys(a), kw, m_cur, m_a, accs)
        return m_a, m_b, accs

    pairs = steps_far // 2
    m_old, m_cur, accs = lax.fori_loop(0, pairs, stage_pair, (m_init, m_first, acc_init))
    m_last = logits(1, near_keys, [sc[n0], sc[n0 + 1]], m_cur, extra=(bias[variant, 0], bias[variant, 1]))
    accs = accumulate(0, far_keys(2 * pairs), kw, m_old, m_cur, accs)
    accs = accumulate(1, near_keys, 2 * KEY_TILE, m_cur, m_last, accs)
    heads = []
    for g in range(N_KV_HEADS):
        acc = accs[g]
        og = acc[:HEAD_DIM, :] / acc[HEAD_DIM:HEAD_DIM + 1, :]
        heads += [og[:, j * tq:(j + 1) * tq] for j in range(GRP)]
    o_ref[0] = jnp.concatenate(heads, axis=0).T.astype(BF16)


def _dsa(rel_bias, q, qi, sgn, k, vt, ki, *, q_pos0, l_valid):
    b, t, _ = q.shape
    lk = k.shape[1]
    tq = KEY_TILE
    assert lk % (BIG * KEY_TILE) == 0 and q_pos0 % KEY_TILE == 0 and t % tq == 0
    assert q_pos0 + t <= lk
    nt = lk // KEY_TILE
    row = lambda width: pl.BlockSpec((1, tq, width), lambda bi, i: (bi, i, 0))
    return pl.pallas_call(
        functools.partial(_dsa_kernel, q_pos0=q_pos0, l_valid=l_valid),
        grid=(b, t // tq),
        in_specs=[pl.BlockSpec(memory_space=pltpu.SMEM),
                  row(D_ATT), row(N_IDX_HEADS * D_IDX),
                  pl.BlockSpec((1, 8, tq), lambda bi, i: (bi, 0, i)),
                  _resident((1, lk, N_KV_HEADS * HEAD_DIM), lambda bi, i: (bi, 0, 0)),
                  _resident((1, N_KV_HEADS * VT_ROWS, lk), lambda bi, i: (bi, 0, 0)),
                  _resident((1, lk, D_IDX), lambda bi, i: (bi, 0, 0))],
        out_specs=row(D_ATT),
        out_shape=jax.ShapeDtypeStruct((b, t, D_ATT), BF16),
        scratch_shapes=[pltpu.VMEM((nt, KEY_TILE, tq), F32),
                        pltpu.VMEM((2, N_KV_HEADS, 2 * KEY_TILE, GRP * tq), F32),
                        pltpu.VMEM((2, N_KV_HEADS, BIG * KEY_TILE, GRP * tq), F32)],
        compiler_params=pltpu.CompilerParams(
            dimension_semantics=("arbitrary", "arbitrary"), vmem_limit_bytes=VMEM_LIMIT),
        name="dsa",
    )(rel_bias, q, qi, sgn, k, vt, ki)


def _split3(x):
    hi = x.astype(BF16)
    r = x - hi.astype(F32)
    mid = r.astype(BF16)
    lo = (r - mid.astype(F32)).astype(BF16)
    return hi, mid, lo


def _ssd_kernel(xs_ref, bm_ref, cm_ref, dt_ref, z_ref, alog_ref, dsk_ref, nw_ref, h0_ref,
                y_ref, h_ref, ht, *, q_in):
    c = pl.program_id(1)
    nc = pl.num_programs(1)

    @pl.when(c == 0)
    def _():
        for hh in range(SSM_HEADS):
            ht[hh] = h0_ref[0, hh].T

    def rows(ref):
        x = ref[0]
        if q_in == SSD_Q:
            return x
        return jnp.concatenate([x, jnp.zeros((SSD_Q - q_in, x.shape[1]), x.dtype)], axis=0)

    xs, bm, cm, dt, z = rows(xs_ref), rows(bm_ref), rows(cm_ref), rows(dt_ref), rows(z_ref)
    a = -jnp.exp(alog_ref[...])
    da = dt * a
    ri = lax.broadcasted_iota(jnp.int32, (SSD_Q, SSD_Q), 0)
    ci = lax.broadcasted_iota(jnp.int32, (SSD_Q, SSD_Q), 1)
    causal = ri >= ci
    tril = jnp.where(causal, 1.0, 0.0).astype(BF16)
    hi, mid, lo = _split3(da)
    acum = _dot(tril, hi) + _dot(tril, mid) + _dot(tril, lo)
    acum_t = acum.T

    for g in range(SSM_GROUPS):
        bm_g = bm[:, g * SSM_STATE:(g + 1) * SSM_STATE]
        cm_g = cm[:, g * SSM_STATE:(g + 1) * SSM_STATE].astype(BF16)
        bm_t = bm_g.T.astype(BF16)
        cb = _dot(cm_g, bm_t)
        gs = []
        ssq = jnp.zeros((SSD_Q, 1), F32)
        for r in range(SSM_HPG):
            hh = g * SSM_HPG + r
            lanes = slice(hh * SSM_HEAD_DIM, (hh + 1) * SSM_HEAD_DIM)
            xh = xs[:, lanes]
            dl = DT_LANE0 + hh
            col = acum[:, dl:dl + 1]
            rowv = acum_t[dl:dl + 1, :]
            decay = jnp.exp(jnp.where(causal, col - rowv, NEG))
            xdt = xh * dt[:, dl:dl + 1]
            y = _dot((cb * decay).astype(BF16), xdt.astype(BF16))
            h_prev = ht[hh]
            y = y + jnp.exp(col) * _dot(cm_g, h_prev.astype(BF16))
            last = acum[SSD_Q - 1:SSD_Q, dl:dl + 1]
            xw = (xdt * jnp.exp(last - col)).astype(BF16)
            ht[hh] = h_prev * jnp.exp(last) + _dot(bm_t, xw)
            y = y + xh * dsk_ref[:, lanes]
            gate = y * _silu(z[:, lanes])
            ssq = ssq + jnp.sum(gate * gate, axis=1, keepdims=True)
            gs.append(gate)
        scale = lax.rsqrt(ssq / float(SSM_HPG * SSM_HEAD_DIM) + NORM_EPS)
        for r in range(SSM_HPG):
            hh = g * SSM_HPG + r
            lanes = slice(hh * SSM_HEAD_DIM, (hh + 1) * SSM_HEAD_DIM)
            y_ref[0, :, lanes] = (gs[r] * scale * nw_ref[:, lanes])[:q_in].astype(BF16)

    @pl.when(c == nc - 1)
    def _():
        for hh in range(SSM_HEADS):
            h_ref[0, hh] = ht[hh].T


def _ssd(xs, bm, cm, dt, z, alog, dsk, nw, h0, *, q_in):
    b, t, _ = xs.shape
    row = lambda width: pl.BlockSpec((1, q_in, width), lambda bi, c: (bi, c, 0))
    full = lambda a: _resident(a.shape, lambda bi, c: (0,) * a.ndim)
    hspec = pl.BlockSpec((1, SSM_HEADS, SSM_HEAD_DIM, SSM_STATE), lambda bi, c: (bi, 0, 0, 0))
    return pl.pallas_call(
        functools.partial(_ssd_kernel, q_in=q_in),
        grid=(b, t // q_in),
        in_specs=[row(D_SSM), row(256), row(256), row(LANE), row(D_SSM),
                  full(alog), full(dsk), full(nw), hspec],
        out_specs=(row(D_SSM), hspec),
        out_shape=(jax.ShapeDtypeStruct((b, t, D_SSM), BF16),
                   jax.ShapeDtypeStruct((b, SSM_HEADS, SSM_HEAD_DIM, SSM_STATE), F32)),
        scratch_shapes=[pltpu.VMEM((SSM_HEADS, SSM_STATE, SSM_HEAD_DIM), F32)],
        compiler_params=pltpu.CompilerParams(
            dimension_semantics=("arbitrary", "arbitrary"), vmem_limit_bytes=VMEM_LIMIT),
        name="ssd",
    )(xs, bm, cm, dt, z, alog, dsk, nw, h0)


def _mix_ffn_kernel(x_ref, ya_ref, yb_ref, yc_ref, wo_ref, npost_ref, nfpre_ref, wg_ref, wu_ref,
                    wd_ref, cfw_ref, cfb_ref, nfpost_ref, buff_ref, o_ref, buff_out, eg, *, tm):
    i = pl.program_id(1)

    @pl.when(i == 0)
    def _():
        eg[CARRY - 2:CARRY, :] = buff_ref[0]

    mix = (_dot(ya_ref[0], wo_ref[0:D_CONV_MIX, :])
           + _dot(yb_ref[0], wo_ref[D_CONV_MIX:D_CONV_MIX + D_ATT, :])
           + _dot(yc_ref[0], wo_ref[D_CONV_MIX + D_ATT:, :]))
    x1 = x_ref[0] + _rms(mix, npost_ref[...])
    u = _rms(x1, nfpre_ref[...]).astype(BF16)
    eg[CARRY:CARRY + tm, :] = _dot(u, wg_ref[...])
    gc = eg[CARRY - 2:CARRY - 2 + tm, :] * cfw_ref[0:1, :]
    for j in range(1, CONV_F_W):
        gc = gc + eg[CARRY - 2 + j:CARRY - 2 + j + tm, :] * cfw_ref[j:j + 1, :]
    hid = (_silu(gc + cfb_ref[...]) * _dot(u, wu_ref[...])).astype(BF16)
    o_ref[0] = x1 + _rms(_dot(hid, wd_ref[...]), nfpost_ref[...])
    tail = eg[CARRY + tm - 2:CARRY + tm, :]
    buff_out[0] = tail
    eg[CARRY - 2:CARRY, :] = tail


def _mix_ffn(x, ya, yb, yc, wo, npost, nfpre, wg, wu, wd, cfw, cfb, nfpost, buff, *, tm):
    b, t, _ = x.shape
    row = lambda width: pl.BlockSpec((1, tm, width), lambda bi, i: (bi, i, 0))
    full = lambda a: _resident(a.shape, lambda bi, i: (0,) * a.ndim)
    state = pl.BlockSpec((1, CONV_F_W - 1, D_FF), lambda bi, i: (bi, 0, 0))
    return pl.pallas_call(
        functools.partial(_mix_ffn_kernel, tm=tm),
        grid=(b, t // tm),
        in_specs=[row(D_MODEL), row(D_CONV_MIX), row(D_ATT), row(D_SSM), full(wo), full(npost),
                  full(nfpre), full(wg), full(wu), full(wd), full(cfw), full(cfb), full(nfpost),
                  state],
        out_specs=(row(D_MODEL), state),
        out_shape=(jax.ShapeDtypeStruct((b, t, D_MODEL), F32),
                   jax.ShapeDtypeStruct((b, CONV_F_W - 1, D_FF), F32)),
        scratch_shapes=[pltpu.VMEM((CARRY + tm, D_FF), F32)],
        compiler_params=pltpu.CompilerParams(
            dimension_semantics=("arbitrary", "arbitrary"), vmem_limit_bytes=VMEM_LIMIT),
        name="mix_ffn",
    )(x, ya, yb, yc, wo, npost, nfpre, wg, wu, wd, cfw, cfb, nfpost, buff)


def _pack_keys_kernel(ck_ref, cv_ref, cki_ref, k_ref, v_ref, ki_ref, kb_ref, vt_ref, kib_ref, *, past_steps):
    j = pl.program_id(1)
    rows = kb_ref.shape[1]

    def pick(cache_ref, new_ref):
        new = new_ref[0]
        new = jnp.concatenate([new, jnp.zeros((rows - new.shape[0], new.shape[1]), F32)], axis=0)
        return jnp.where(j < past_steps, cache_ref[0], new)

    kb_ref[0] = pick(ck_ref, k_ref).astype(BF16)
    vt_ref[0] = _augment_vt(pick(cv_ref, v_ref).T).astype(BF16)
    kib_ref[0] = pick(cki_ref, ki_ref).astype(BF16)


def _pack_keys(ck, cv, cki, k, v, ki):
    b, past, _ = ck.shape
    t = k.shape[1]
    rows = BIG * KEY_TILE
    assert past % rows == 0 and t <= rows
    past_steps = past // rows
    lk = past + rows
    cache = lambda width: pl.BlockSpec((1, rows, width),
                                       lambda bi, j: (bi, jnp.minimum(j, past_steps - 1), 0))
    new = lambda width: pl.BlockSpec((1, t, width), lambda bi, j: (bi, 0, 0))
    out = lambda width: pl.BlockSpec((1, rows, width), lambda bi, j: (bi, j, 0))
    return pl.pallas_call(
        functools.partial(_pack_keys_kernel, past_steps=past_steps),
        grid=(b, past_steps + 1),
        in_specs=[cache(128), cache(128), cache(D_IDX), new(128), new(128), new(D_IDX)],
        out_specs=(out(128), pl.BlockSpec((1, N_KV_HEADS * VT_ROWS, rows), lambda bi, j: (bi, 0, j)),
                   out(D_IDX)),
        out_shape=(jax.ShapeDtypeStruct((b, lk, 128), BF16),
                   jax.ShapeDtypeStruct((b, N_KV_HEADS * VT_ROWS, lk), BF16),
                   jax.ShapeDtypeStruct((b, lk, D_IDX), BF16)),
        compiler_params=pltpu.CompilerParams(
            dimension_semantics=("arbitrary", "arbitrary"), vmem_limit_bytes=VMEM_LIMIT),
        name="pack_keys",
    )(ck, cv, cki, k, v, ki)


def _layer(x, p, rel_bias, buf_a, buf_s, h0, buf_f, kv_past, *, tm):
    b, t, _ = x.shape
    outs = _in_proj(x, p['norm_mix_pre'], p['w_in'], p['conv_a_w'], p['conv_ssm_w'], p['conv_ssm_b'],
                    p['dt_bias'], buf_a, buf_s, tm=tm, key_side=kv_past is None)
    (ya, q, k, v, qi, ki, sgn, z, xs, bm, cm, dt, buf_a_new, buf_s_new) = outs[:14]
    if kv_past is None:
        kb, vt, kib = outs[14:]
        yb = _dsa(rel_bias, q, qi, sgn, kb, vt, kib, q_pos0=0, l_valid=t)
    else:
        ck, cv, cki = kv_past
        past = ck.shape[1]
        qpad = lambda a: jnp.pad(a, ((0, 0), (0, -t % KEY_TILE), (0, 0)))
        kb, vt, kib = _pack_keys(ck.reshape(b, past, -1), cv.reshape(b, past, -1), cki, k, v, ki)
        yb = _dsa(rel_bias, qpad(q), qpad(qi), sgn, kb, vt, kib, q_pos0=past, l_valid=past + t)[:, :t]
    yc, h_new = _ssd(xs, bm, cm, dt, z, p['a_log'], p['d_skip'], p['ssm_norm_w'],
                     h0.reshape(b, SSM_HEADS, SSM_HEAD_DIM, SSM_STATE), q_in=min(t, SSD_Q))
    x_new, buf_f_new = _mix_ffn(x, ya, yb, yc, p['w_out'], p['norm_mix_post'], p['norm_ffn_pre'],
                                p['w_gate'], p['w_up'], p['w_down'], p['conv_ffn_w'],
                                p['conv_ffn_b'], p['norm_ffn_post'], buf_f, tm=tm)
    st = (k.reshape(b, t, N_KV_HEADS, HEAD_DIM), v.reshape(b, t, N_KV_HEADS, HEAD_DIM), ki,
          buf_a_new, buf_s_new, h_new.reshape(b, SSM_GROUPS, SSM_HPG, SSM_HEAD_DIM, SSM_STATE),
          buf_f_new)
    return x_new, st


def _pack_w_in_kernel(wt_ref, o_ref):
    for c, c0 in enumerate(W_IN_TILE_COLS):
        o_ref[:, c * LANE:(c + 1) * LANE] = wt_ref[c0:c0 + LANE, :].T.astype(BF16)


def _pack_w_in(wt):
    return pl.pallas_call(
        _pack_w_in_kernel,
        out_shape=jax.ShapeDtypeStruct((wt.shape[1], D_IN_PACKED), BF16),
        compiler_params=pltpu.CompilerParams(vmem_limit_bytes=VMEM_LIMIT),
        name="pack_w_in",
    )(wt)


def _dt_lanes(v):
    return jnp.pad(v, (DT_LANE0, 0))[None, :]


def kernel(x_prompt, x_sample, cache_k, cache_v, cache_kidx, state_conv_a, state_conv_ssm, state_ssm, state_conv_ffn, rel_bias, norm_mix_pre, norm_mix_post, norm_ffn_pre, norm_ffn_post, w_in, conv_a_w, conv_ssm_w, conv_ssm_b, dt_bias, a_log, d_skip, ssm_norm_w, w_out, w_gate, w_up, conv_ffn_w, conv_ffn_b, w_down):
    depth = w_in.shape[0]
    bp = x_prompt.shape[0]
    yp, ys = x_prompt, x_sample
    outs_p, outs_s = [], []
    for l in range(depth):
        p = {'norm_mix_pre': norm_mix_pre[l][None], 'norm_mix_post': norm_mix_post[l][None],
             'norm_ffn_pre': norm_ffn_pre[l][None], 'norm_ffn_post': norm_ffn_post[l][None],
             'w_in': _pack_w_in(jnp.swapaxes(w_in[l], 0, 1)), 'conv_a_w': conv_a_w[l], 'conv_ssm_w': conv_ssm_w[l],
             'conv_ssm_b': conv_ssm_b[l][None], 'dt_bias': _dt_lanes(dt_bias[l]),
             'a_log': _dt_lanes(a_log[l]), 'd_skip': jnp.repeat(d_skip[l], SSM_HEAD_DIM)[None],
             'ssm_norm_w': ssm_norm_w[l][None], 'w_out': w_out[l].astype(BF16),
             'w_gate': w_gate[l].astype(BF16), 'w_up': w_up[l].astype(BF16),
             'conv_ffn_w': conv_ffn_w[l], 'conv_ffn_b': conv_ffn_b[l][None],
             'w_down': w_down[l].astype(BF16)}
        yp, st_p = _layer(
            yp, p, rel_bias,
            jnp.zeros((bp, CONV_A_W - 1, D_CONV_MIX), F32),
            jnp.zeros((bp, SSM_CONV_W - 1, D_XBC), F32),
            jnp.zeros((bp, SSM_GROUPS, SSM_HPG, SSM_HEAD_DIM, SSM_STATE), F32),
            jnp.zeros((bp, CONV_F_W - 1, D_FF), F32),
            None, tm=256)
        ys, st_s = _layer(
            ys, p, rel_bias, state_conv_a[l], state_conv_ssm[l], state_ssm[l], state_conv_ffn[l],
            (cache_k[l], cache_v[l], cache_kidx[l]), tm=ys.shape[1])
        outs_p.append(st_p)
        outs_s.append(st_s)

    def stack(outs, i):
        return jnp.stack([o[i] for o in outs], axis=0)

    res = [yp, ys]
    for i in range(7):
        res.append(stack(outs_p, i))
        res.append(stack(outs_s, i))
    return tuple(res)
```

```python
import functools

import jax
import jax.numpy as jnp
from jax import lax
from jax.experimental import pallas as pl
from jax.experimental.pallas import tpu as pltpu

F32 = jnp.float32
BF16 = jnp.bfloat16

D_MODEL = 1024
CHUNK = 64
D_CONV_MIX = 256
CONV_A_W = 3
N_HEADS = 6
N_KV_HEADS = 2
HEAD_DIM = 64
D_ATT = N_HEADS * HEAD_DIM
GRP = N_HEADS // N_KV_HEADS
N_IDX_HEADS = 4
D_IDX = 64
TOPK = 256
N_BUCKETS = 32
SSM_HEADS = 6
SSM_HEAD_DIM = 64
D_SSM = SSM_HEADS * SSM_HEAD_DIM
SSM_GROUPS = 2
SSM_HPG = SSM_HEADS // SSM_GROUPS
SSM_STATE = 128
SSM_CONV_W = 4
D_XBC = D_SSM + 2 * SSM_GROUPS * SSM_STATE
D_FF = 2816
CONV_F_W = 3
NORM_EPS = 1e-6

LANE = 128
CARRY = 8
SSD_Q = 128
KEY_TILE = 128
BIG = 4
VMEM_LIMIT = 56 * 1024 * 1024
NEG = -1e30
LOG2E = 1.4426950408889634
F32_MAX = 3.4028234663852886e38
F32_TINY = 1.1754943508222875e-38
NO_QUOTA = 1e9
MAX_SEARCH_STEPS = 256
BLIND_SEARCH_STEPS = 14

C_AB, C_AC, C_AH = 0, 256, 512
C_Q = 768
C_K = 1152
C_V = 1280
C_QI = 1408
C_KI = 1664
C_Z = 1792
C_XBC = 2176
C_DT = 3072
D_IN_PACKED = 3200
PROJ_GROUPS = (0, C_Q, C_Z, D_IN_PACKED)
D_IN = 3018
D_IN_HEAD = 1732
DT_LANE0 = LANE - SSM_HEADS
W_IN_TILE_COLS = ([c * LANE for c in range(C_Z // LANE)]
                  + [D_IN_HEAD + c * LANE for c in range((C_DT - C_Z) // LANE)] + [D_IN - LANE])


def _rms(x, w):
    return x * lax.rsqrt(jnp.mean(x * x, axis=-1, keepdims=True) + NORM_EPS) * w


def _dot(a, b):
    return jnp.dot(a, b, preferred_element_type=F32)


def _silu(x):
    return x * jax.nn.sigmoid(x)


VT_ROWS = HEAD_DIM + 16


def _augment_vt(vt):
    extra_shape = vt.shape[:-2] + (VT_ROWS - HEAD_DIM, vt.shape[-1])
    row = lax.broadcasted_iota(jnp.int32, extra_shape, len(extra_shape) - 2)
    extra = jnp.where(row == 0, 1.0, 0.0).astype(vt.dtype)
    parts = []
    for g in range(N_KV_HEADS):
        parts += [vt[..., g * HEAD_DIM:(g + 1) * HEAD_DIM, :], extra]
    return jnp.concatenate(parts, axis=-2)


def _resident(shape, index_map):
    return pl.BlockSpec(shape, index_map, pipeline_mode=pl.Buffered(1))


def _in_proj_kernel(x_ref, nw_ref, w_ref, caw_ref, csw_ref, csb_ref, dtb_ref, bufa_ref, bufs_ref,
                    ya_ref, q_ref, k_ref, v_ref, qi_ref, ki_ref, sgn_ref, z_ref, xs_ref, bm_ref,
                    cm_ref, dt_ref, bufa_out, bufs_out, *rest, tm, key_side):
    if key_side:
        kb_ref, vt_ref, kib_ref, ea, es = rest
    else:
        ea, es = rest
    i = pl.program_id(1)
    u = _rms(x_ref[0], nw_ref[...]).astype(BF16)

    slabs = {}

    def proj(c0, width):
        g0, g1 = next((a, b) for a, b in zip(PROJ_GROUPS, PROJ_GROUPS[1:]) if a <= c0 < b)
        if g0 not in slabs:
            slabs[g0] = _dot(u, w_ref[:, g0:g1])
        return slabs[g0][:, c0 - g0:c0 - g0 + width]

    @pl.when(i == 0)
    def _():
        ea[CARRY - 2:CARRY, :] = bufa_ref[0]
        es[CARRY - 3:CARRY, :] = bufs_ref[0]

    ea[CARRY:CARRY + tm, :] = proj(C_AC, D_CONV_MIX) * proj(C_AH, D_CONV_MIX)
    conv = ea[CARRY - 2:CARRY - 2 + tm, :] * caw_ref[0:1, :]
    for j in range(1, CONV_A_W):
        conv = conv + ea[CARRY - 2 + j:CARRY - 2 + j + tm, :] * caw_ref[j:j + 1, :]
    ya_ref[0] = (proj(C_AB, D_CONV_MIX) * conv).astype(BF16)
    tail_a = ea[CARRY + tm - 2:CARRY + tm, :]
    bufa_out[0] = tail_a
    ea[CARRY - 2:CARRY, :] = tail_a

    q_ref[0] = (proj(C_Q, D_ATT) * (HEAD_DIM ** -0.5 * LOG2E)).astype(BF16)
    k = proj(C_K, N_KV_HEADS * HEAD_DIM)
    v = proj(C_V, N_KV_HEADS * HEAD_DIM)
    kiw = proj(C_KI, LANE)
    k_ref[0] = k
    v_ref[0] = v
    ki_ref[0] = kiw[:, :D_IDX]
    wi = kiw[:, D_IDX:D_IDX + N_IDX_HEADS]
    kiw_rows = kiw if tm >= LANE else jnp.concatenate([kiw, jnp.zeros((LANE - tm, LANE), F32)], axis=0)
    sgn_ref[0] = jnp.where(kiw_rows.T[D_IDX:D_IDX + 8, :] >= 0.0, 1.0, -1.0).astype(F32)
    if key_side:
        kb_ref[0] = k.astype(BF16)
        vt_ref[0] = _augment_vt(v.T).astype(BF16)
        kib_ref[0] = kiw[:, :D_IDX].astype(BF16)
    wabs = jnp.abs(wi) * (D_IDX ** -0.5 * N_IDX_HEADS ** -0.5)
    qi = proj(C_QI, N_IDX_HEADS * D_IDX)
    for h in range(N_IDX_HEADS):
        qi_ref[0, :, h * D_IDX:(h + 1) * D_IDX] = (
            qi[:, h * D_IDX:(h + 1) * D_IDX] * wabs[:, h:h + 1]).astype(BF16)

    z_ref[0] = proj(C_Z, D_SSM)
    es[CARRY:CARRY + tm, :] = proj(C_XBC, D_XBC)
    xc = es[CARRY - 3:CARRY - 3 + tm, :] * csw_ref[0:1, :]
    for j in range(1, SSM_CONV_W):
        xc = xc + es[CARRY - 3 + j:CARRY - 3 + j + tm, :] * csw_ref[j:j + 1, :]
    xc = _silu(xc + csb_ref[...])
    xs_ref[0] = xc[:, :D_SSM]
    bm_ref[0] = xc[:, D_SSM:D_SSM + SSM_GROUPS * SSM_STATE]
    cm_ref[0] = xc[:, D_SSM + SSM_GROUPS * SSM_STATE:]
    tail_s = es[CARRY + tm - 3:CARRY + tm, :]
    bufs_out[0] = tail_s
    es[CARRY - 3:CARRY, :] = tail_s
    dtr = proj(C_DT, LANE) + dtb_ref[...]
    dt_ref[0] = jnp.maximum(dtr, 0.0) + jnp.log1p(jnp.exp(-jnp.abs(dtr)))


def _in_proj(x, nw, w_packed, caw, csw, csb, dtb, bufa, bufs, *, tm, key_side):
    b, t, _ = x.shape
    tq = max(tm, LANE)
    grid = (b, t // tm)
    row = lambda width: pl.BlockSpec((1, tm, width), lambda bi, i: (bi, i, 0))
    full = lambda a: _resident(a.shape, lambda bi, i: (0,) * a.ndim)
    state = lambda r, width: pl.BlockSpec((1, r, width), lambda bi, i: (bi, 0, 0))
    out_shapes = (
        jax.ShapeDtypeStruct((b, t, D_CONV_MIX), BF16),
        jax.ShapeDtypeStruct((b, t, D_ATT), BF16),
        jax.ShapeDtypeStruct((b, t, 128), F32),
        jax.ShapeDtypeStruct((b, t, 128), F32),
        jax.ShapeDtypeStruct((b, t, 256), BF16),
        jax.ShapeDtypeStruct((b, t, D_IDX), F32),
        jax.ShapeDtypeStruct((b, 8, t // tm * tq), F32),
        jax.ShapeDtypeStruct((b, t, D_SSM), F32),
        jax.ShapeDtypeStruct((b, t, D_SSM), F32),
        jax.ShapeDtypeStruct((b, t, 256), F32),
        jax.ShapeDtypeStruct((b, t, 256), F32),
        jax.ShapeDtypeStruct((b, t, LANE), F32),
        jax.ShapeDtypeStruct((b, CONV_A_W - 1, D_CONV_MIX), F32),
        jax.ShapeDtypeStruct((b, SSM_CONV_W - 1, D_XBC), F32),
    )
    out_specs = (row(D_CONV_MIX), row(D_ATT), row(128), row(128), row(256), row(D_IDX),
                 pl.BlockSpec((1, 8, tq), lambda bi, i: (bi, 0, i)),
                 row(D_SSM), row(D_SSM), row(256), row(256), row(LANE),
                 state(CONV_A_W - 1, D_CONV_MIX), state(SSM_CONV_W - 1, D_XBC))
    if key_side:
        out_shapes += (jax.ShapeDtypeStruct((b, t, 128), BF16),
                       jax.ShapeDtypeStruct((b, N_KV_HEADS * VT_ROWS, t), BF16),
                       jax.ShapeDtypeStruct((b, t, D_IDX), BF16))
        out_specs += (row(128), pl.BlockSpec((1, N_KV_HEADS * VT_ROWS, tm), lambda bi, i: (bi, 0, i)), row(D_IDX))
    in_specs = [row(D_MODEL), full(nw), full(w_packed), full(caw), full(csw), full(csb), full(dtb),
                state(CONV_A_W - 1, D_CONV_MIX), state(SSM_CONV_W - 1, D_XBC)]
    return pl.pallas_call(
        functools.partial(_in_proj_kernel, tm=tm, key_side=key_side),
        grid=grid, in_specs=in_specs, out_specs=out_specs, out_shape=out_shapes,
        scratch_shapes=[pltpu.VMEM((CARRY + tm, D_CONV_MIX), F32),
                        pltpu.VMEM((CARRY + tm, D_XBC), F32)],
        compiler_params=pltpu.CompilerParams(
            dimension_semantics=("arbitrary", "arbitrary"), vmem_limit_bytes=VMEM_LIMIT),
        name="in_proj",
    )(x, nw, w_packed, caw, csw, csb, dtb, bufa, bufs)


def _t5_bucket(rel):
    n = jnp.abs(rel)
    large = jnp.full(rel.shape, 8, jnp.int32)
    for brk in (12, 16, 23, 32, 46, 64, 91):
        large = large + jnp.where(n >= brk, 1, 0)
    return jnp.where(rel > 0, N_BUCKETS // 2, 0) + jnp.where(n < 8, n, large)


def _f32_key(x):
    bits = lax.bitcast_convert_type(x, jnp.int32)
    return bits ^ ((bits >> 31) & jnp.int32(0x7FFFFFFF))


def _key_f32(key):
    return lax.bitcast_convert_type(key ^ ((key >> 31) & jnp.int32(0x7FFFFFFF)), F32)


def _dsa_kernel(relb_ref, q_ref, qi_ref, sgn_ref, k_ref, vt_ref, ki_ref, o_ref, sc, bias, sbuf,
                *, q_pos0, l_valid):
    tq = KEY_TILE
    i = pl.program_id(1)
    q0 = q_pos0 + i * tq
    qt = q0 // KEY_TILE
    n0 = jnp.maximum(qt - 1, 0)
    variant = jnp.minimum(qt, 1)
    steps_far = (n0 + BIG - 1) // BIG
    steps_all = (n0 + 2 + BIG - 1) // BIG
    kw = BIG * KEY_TILE
    kf = float(TOPK)

    def step_keys(j):
        return pl.ds(pl.multiple_of(j * kw, kw), kw)

    near_keys = pl.ds(pl.multiple_of(n0 * KEY_TILE, KEY_TILE), 2 * KEY_TILE)

    @pl.when(i == 0)
    def _():
        sc[...] = jnp.full(sc.shape, -jnp.inf, F32)

    @pl.when((i == 0) & (pl.program_id(0) == 0))
    def _():
        kk = lax.broadcasted_iota(jnp.int32, (2 * KEY_TILE, tq), 0)
        r = lax.broadcasted_iota(jnp.int32, (2 * KEY_TILE, tq), 1)
        for var in range(2):
            bucket = _t5_bucket(kk - var * KEY_TILE - r)
            for h in range(N_HEADS):
                tab = jnp.zeros((2 * KEY_TILE, tq), F32)
                for bk in range(N_BUCKETS):
                    tab = jnp.where(bucket == bk, relb_ref[bk, h], tab)
                g, j = divmod(h, GRP)
                bias[var, g, :, j * tq:(j + 1) * tq] = (tab - relb_ref[N_BUCKETS // 2 - 1, h]) * LOG2E

    qit = qi_ref[0].astype(F32).T
    qis = jnp.concatenate([qit[h * D_IDX:(h + 1) * D_IDX, :] for h in range(N_IDX_HEADS)],
                          axis=1).astype(BF16)
    sgn = sgn_ref[0]

    def scores(keys):
        s = jnp.maximum(_dot(ki_ref[0, keys, :], qis), 0.0)
        tot = s[:, 0:tq] * sgn[0:1, :]
        for h in range(1, N_IDX_HEADS):
            tot = tot + s[:, h * tq:(h + 1) * tq] * sgn[h:h + 1, :]
        return tot

    zero = jnp.zeros((KEY_TILE, tq), F32)

    def tally(c, s):
        ge, gt, mx = c
        return (ge + jnp.where(s >= 0.0, 1.0, 0.0), gt + jnp.where(s >= F32_TINY, 1.0, 0.0),
                jnp.maximum(mx, s))

    def score_far(j, c):
        s = scores(step_keys(j))
        for k in range(BIG):
            t = j * BIG + k
            tile = jnp.where(t < n0, s[k * KEY_TILE:(k + 1) * KEY_TILE, :], -jnp.inf)
            sc[t] = tile
            c = tally(c, tile)
        return c

    far_pairs = steps_far // 2
    c = lax.fori_loop(0, far_pairs, lambda j, c: score_far(2 * j + 1, score_far(2 * j, c)),
                      (zero, zero, jnp.full((KEY_TILE, tq), -F32_MAX, F32)))
    c = lax.fori_loop(2 * far_pairs, steps_far, score_far, c)
    s_near = scores(near_keys)
    kpos = n0 * KEY_TILE + lax.broadcasted_iota(jnp.int32, (2 * KEY_TILE, tq), 0)
    qpos = q0 + lax.broadcasted_iota(jnp.int32, (2 * KEY_TILE, tq), 1)
    adm = (kpos < l_valid) & ((kpos >> 6) <= (qpos >> 6))
    s_near = jnp.where(adm, s_near, -jnp.inf)
    sc[n0] = s_near[:KEY_TILE, :]
    sc[n0 + 1] = s_near[KEY_TILE:, :]
    for k in range(BIG - 1):
        @pl.when(n0 + 2 + k < steps_all * BIG)
        def _():
            sc[n0 + 2 + k] = jnp.full((KEY_TILE, tq), -jnp.inf, F32)
    ge, gt, mx = tally(tally(c, s_near[:KEY_TILE, :]), s_near[KEY_TILE:, :])


    def col_sum(acc):
        parts = [acc[r:r + 8, :] for r in range(0, KEY_TILE, 8)]
        while len(parts) > 1:
            parts = [parts[r] + parts[r + 1] for r in range(0, len(parts), 2)]
        return jnp.sum(parts[0], axis=0, keepdims=True)

    def counter(value_at):
        def count_ge(t):
            tb = jnp.broadcast_to(t, (KEY_TILE, tq))

            def body(j, acc):
                for k in range(BIG):
                    acc = acc + jnp.where(value_at(j * BIG + k) >= tb, 1.0, 0.0)
                return acc
            return col_sum(lax.fori_loop(0, steps_all, body, zero))
        return count_ge

    def search(count_ge, kq, nfin, ge0, gt0, rmax, active, blind_steps):
        small = (nfin <= kq) | (active == 0.0)
        tie0 = (~small) & (gt0 < kq) & (ge0 >= kq)
        pos = (~small) & (gt0 >= kq)
        neg = (~small) & (ge0 < kq)
        log_kq = jnp.log(kq)
        lo0 = jnp.where(pos, F32_TINY, -F32_MAX)
        flo0 = jnp.where(pos, gt0, nfin)
        hi0 = jnp.where(neg, 0.0, jnp.inf)
        fhi0 = jnp.where(neg, ge0, 0.0)
        done0 = jnp.where(small | tie0 | (pos & (gt0 == kq)), 1.0, 0.0)
        thr0 = jnp.where(small, -F32_MAX, jnp.where(tie0, 0.0, F32_TINY))
        quota0 = jnp.where(tie0, kq - gt0, NO_QUOTA)

        def step(it, state):
            lo, flo, hi, fhi, done, thr, quota, adjacent = state
            klo, khi = _f32_key(lo), _f32_key(hi)
            la = jnp.log(flo)
            frac = (la - log_kq) / (la - jnp.log(jnp.maximum(fhi, 0.5)))
            interp = lo + (hi - lo) * jnp.where(it % 4 == 1, 0.5, frac)
            t_up = lo + (rmax - lo) * 0.5
            t_dn = hi - jnp.maximum(jnp.maximum(jnp.abs(rmax - hi), jnp.abs(hi)), F32_TINY)
            open_hi = hi == jnp.inf
            open_lo = lo <= -F32_MAX
            t = jnp.where(open_hi, t_up, jnp.where(open_lo, t_dn, interp))
            tk = jnp.minimum(jnp.maximum(_f32_key(t), klo + 1), khi - 1)
            tk = jnp.where(it % 4 == 3, klo + lax.shift_right_logical(khi - klo, 1), tk)
            t = _key_f32(tk)
            f = count_ge(t)
            act = done == 0.0
            hit = act & (f == kq)
            up = act & (f > kq)
            dn = act & (f < kq)
            lo = jnp.where(up, t, lo)
            flo = jnp.where(up, f, flo)
            hi = jnp.where(dn, t, hi)
            fhi = jnp.where(dn, f, fhi)
            adj = act & (~hit) & (_f32_key(lo) + 1 == _f32_key(hi))
            thr = jnp.where(hit, t, jnp.where(adj, lo, thr))
            quota = jnp.where(adj, kq - fhi, quota)
            done = jnp.where(hit | adj, 1.0, done)
            return lo, flo, hi, fhi, done, thr, quota, jnp.where(adj, 1.0, adjacent)

        state = (lo0, flo0, hi0, fhi0, done0, thr0, quota0, jnp.zeros((1, tq), F32))
        if blind_steps:
            state = lax.fori_loop(0, blind_steps, step, state)

        def cond(c):
            return (c[0] < MAX_SEARCH_STEPS) & (c[1] > 0)

        def body(c):
            state = step(c[0], c[2])
            return c[0] + 1, jnp.sum(1.0 - state[4]), state

        _, _, state = lax.while_loop(cond, body, (jnp.int32(blind_steps), jnp.sum(1.0 - state[4]), state))
        _, flo, _, fhi, _, thr, quota, adjacent = state
        return thr, quota, adjacent, flo, fhi

    qrow = q0 + lax.broadcasted_iota(jnp.int32, (1, tq), 1)
    nadm = jnp.minimum(((qrow >> 6) + 1) * CHUNK, l_valid).astype(F32)
    thr, quota, adjacent, flo, fhi = search(
        counter(lambda t: sc[t]), jnp.full((1, tq), kf, F32), nadm,
        jnp.sum(ge, axis=0, keepdims=True), jnp.sum(gt, axis=0, keepdims=True),
        jnp.max(mx, axis=0, keepdims=True), jnp.ones((1, tq), F32), BLIND_SEARCH_STEPS)
    thr_b = jnp.broadcast_to(thr, (tq, KEY_TILE))
    quota_b = jnp.broadcast_to(quota, (tq, KEY_TILE))
    has_ties = jnp.min(quota) < NO_QUOTA
    any_adjacent = jnp.max(adjacent) > 0.0

    @pl.when(jnp.logical_not(has_ties))
    def _():
        def body(j, c):
            for k in range(BIG):
                t = j * BIG + k
                sc[t] = jnp.where(sc[t] >= thr_b, 0.0, NEG)
            return c
        lax.fori_loop(0, steps_all, body, 0)

    def rank_pass(is_tie, is_sure, quota_of_ties):
        ri = lax.broadcasted_iota(jnp.int32, (KEY_TILE, KEY_TILE), 0)
        ci = lax.broadcasted_iota(jnp.int32, (KEY_TILE, KEY_TILE), 1)
        lower = jnp.where(ri >= ci, 1.0, 0.0).astype(BF16)

        def rank_tiles(first, n, seen):
            for k in range(n):
                t = first + k
                s = sc[t]
                tie = jnp.where(is_tie(s), 1.0, 0.0)
                rank = seen + _dot(lower, tie.astype(BF16))
                keep = is_sure(s) | ((tie > 0.0) & (rank <= quota_of_ties))
                sc[t] = jnp.where(keep, 0.0, NEG)
                seen = seen + jnp.sum(tie, axis=0, keepdims=True)
            return seen

        pairs = steps_all // 2
        seen = lax.fori_loop(0, pairs, lambda j, c: rank_tiles(j * 2 * BIG, 2 * BIG, c),
                             jnp.zeros((1, tq), F32))
        lax.fori_loop(2 * pairs, steps_all, lambda j, c: rank_tiles(j * BIG, BIG, c), seen)

    @pl.when(has_ties & jnp.logical_not(any_adjacent))
    def _():
        rank_pass(lambda s: s == thr_b, lambda s: s > thr_b, quota_b)

    @pl.when(any_adjacent)
    def _():
        adj_b = jnp.broadcast_to(adjacent, (tq, KEY_TILE)) > 0.0
        next_b = jnp.broadcast_to(_key_f32(_f32_key(thr) + 1), (tq, KEY_TILE))

        def in_cell(s):
            return adj_b & (s >= thr_b) & (s < next_b)

        def offset(s):
            return jnp.where(in_cell(s), s - thr_b, -jnp.inf)

        def tally_offsets(j, c):
            for k in range(BIG):
                c = tally(c, offset(sc[j * BIG + k]))
            return c

        ge2, gt2, mx2 = lax.fori_loop(0, steps_all, tally_offsets,
                                      (zero, zero, jnp.full((KEY_TILE, tq), -F32_MAX, F32)))
        in_cells = col_sum(ge2)
        thr2, quota2, _, _, _ = search(
            counter(lambda t: offset(sc[t])), jnp.where(adjacent > 0.0, quota, 1.0), in_cells, in_cells,
            col_sum(gt2), jnp.max(mx2, axis=0, keepdims=True), adjacent, 0)
        thr2_b = jnp.broadcast_to(thr2, (tq, KEY_TILE))
        quota2_b = jnp.broadcast_to(quota2, (tq, KEY_TILE))
        plain_b = jnp.logical_not(adj_b)
        rank_pass(lambda s: (plain_b & (s == thr_b)) | (offset(s) == thr2_b),
                  lambda s: (plain_b & (s > thr_b)) | (adj_b & (s >= next_b)) | (offset(s) > thr2_b),
                  jnp.where(adj_b, quota2_b, quota_b))

    qt_all = q_ref[0].astype(F32).T
    zeros_half = jnp.zeros((HEAD_DIM, GRP * tq), F32)
    qa = []
    for g in range(N_KV_HEADS):
        piece = jnp.concatenate(
            [qt_all[(g * GRP + j) * HEAD_DIM:(g * GRP + j + 1) * HEAD_DIM, :] for j in range(GRP)],
            axis=1)
        halves = [piece, zeros_half] if g == 0 else [zeros_half, piece]
        qa.append(jnp.concatenate(halves, axis=0).astype(BF16))

    def logits(slot, keys, pens, m_old, extra=None):
        pen = jnp.concatenate(pens, axis=0)
        pen = jnp.concatenate([pen] * GRP, axis=1)
        kblk = k_ref[0, keys, :]
        m_new = []
        for g in range(N_KV_HEADS):
            s = _dot(kblk, qa[g]) + pen
            if extra is not None:
                s = s + extra[g]
            sbuf[slot, g, 0:pen.shape[0], :] = s
            m_new.append(jnp.maximum(m_old[g], jnp.max(s, axis=0, keepdims=True)))
        return tuple(m_new)

    def accumulate(slot, keys, nk, m_old, m_new, accs):
        out = []
        for g in range(N_KV_HEADS):
            p = jnp.exp2(sbuf[slot, g, 0:nk, :] - m_new[g])
            out.append(jnp.exp2(m_old[g] - m_new[g]) * accs[g]
                       + _dot(vt_ref[0, g * VT_ROWS:(g + 1) * VT_ROWS, keys], p.astype(BF16)))
        return tuple(out)

    last_far = jnp.maximum(steps_far - 1, 0)

    def far_keys(j):
        return step_keys(jnp.minimum(j, last_far))

    def far_pens(j):
        return [jnp.where(j * BIG + k < n0, sc[jnp.minimum(j * BIG + k, n0)], NEG) for k in range(BIG)]

    m_init = tuple(jnp.full((1, GRP * tq), NEG, F32) for _ in range(N_KV_HEADS))
    acc_init = tuple(jnp.zeros((VT_ROWS, GRP * tq), F32) for _ in range(N_KV_HEADS))
    m_first = logits(0, far_keys(0), far_pens(0), m_init)

    def stage_pair(i, c):
        m_old, m_cur, accs = c
        a = 2 * i + 1
        m_a = logits(1, far_keys(a), far_pens(a), m_cur)
        accs = accumulate(0, far_keys(a - 1), kw, m_old, m_cur, accs)
        m_b = logits(0, far_keys(a + 1), far_pens(a + 1), m_a)
        accs = accumulate(1, far_keys(a), kw, m_cur, m_a, accs)
        return m_a, m_b, accs

    pairs = steps_far // 2
    m_old, m_cur, accs = lax.fori_loop(0, pairs, stage_pair, (m_init, m_first, acc_init))
    m_last = logits(1, near_keys, [sc[n0], sc[n0 + 1]], m_cur, extra=(bias[variant, 0], bias[variant, 1]))
    accs = accumulate(0, far_keys(2 * pairs), kw, m_old, m_cur, accs)
    accs = accumulate(1, near_keys, 2 * KEY_TILE, m_cur, m_last, accs)
    heads = []
    for g in range(N_KV_HEADS):
        acc = accs[g]
        og = acc[:HEAD_DIM, :] / acc[HEAD_DIM:HEAD_DIM + 1, :]
        heads += [og[:, j * tq:(j + 1) * tq] for j in range(GRP)]
    o_ref[0] = jnp.concatenate(heads, axis=0).T.astype(BF16)


def _dsa(rel_bias, q, qi, sgn, k, vt, ki, *, q_pos0, l_valid):
    b, t, _ = q.shape
    lk = k.shape[1]
    tq = KEY_TILE
    assert lk % (BIG * KEY_TILE) == 0 and q_pos0 % KEY_TILE == 0 and t % tq == 0
    assert q_pos0 + t <= lk
    nt = lk // KEY_TILE
    row = lambda width: pl.BlockSpec((1, tq, width), lambda bi, i: (bi, i, 0))
    return pl.pallas_call(
        functools.partial(_dsa_kernel, q_pos0=q_pos0, l_valid=l_valid),
        grid=(b, t // tq),
        in_specs=[pl.BlockSpec(memory_space=pltpu.SMEM),
                  row(D_ATT), row(N_IDX_HEADS * D_IDX),
                  pl.BlockSpec((1, 8, tq), lambda bi, i: (bi, 0, i)),
                  _resident((1, lk, N_KV_HEADS * HEAD_DIM), lambda bi, i: (bi, 0, 0)),
                  _resident((1, N_KV_HEADS * VT_ROWS, lk), lambda bi, i: (bi, 0, 0)),
                  _resident((1, lk, D_IDX), lambda bi, i: (bi, 0, 0))],
        out_specs=row(D_ATT),
        out_shape=jax.ShapeDtypeStruct((b, t, D_ATT), BF16),
        scratch_shapes=[pltpu.VMEM((nt, KEY_TILE, tq), F32),
                        pltpu.VMEM((2, N_KV_HEADS, 2 * KEY_TILE, GRP * tq), F32),
                        pltpu.VMEM((2, N_KV_HEADS, BIG * KEY_TILE, GRP * tq), F32)],
        compiler_params=pltpu.CompilerParams(
            dimension_semantics=("arbitrary", "arbitrary"), vmem_limit_bytes=VMEM_LIMIT),
        name="dsa",
    )(rel_bias, q, qi, sgn, k, vt, ki)


def _split3(x):
    hi = x.astype(BF16)
    r = x - hi.astype(F32)
    mid = r.astype(BF16)
    lo = (r - mid.astype(F32)).astype(BF16)
    return hi, mid, lo


def _ssd_kernel(xs_ref, bm_ref, cm_ref, dt_ref, z_ref, alog_ref, dsk_ref, nw_ref, h0_ref,
                y_ref, h_ref, ht, *, q_in):
    c = pl.program_id(1)
    nc = pl.num_programs(1)

    @pl.when(c == 0)
    def _():
        for hh in range(SSM_HEADS):
            ht[hh] = h0_ref[0, hh].T

    def rows(ref):
        x = ref[0]
        if q_in == SSD_Q:
            return x
        return jnp.concatenate([x, jnp.zeros((SSD_Q - q_in, x.shape[1]), x.dtype)], axis=0)

    xs, bm, cm, dt, z = rows(xs_ref), rows(bm_ref), rows(cm_ref), rows(dt_ref), rows(z_ref)
    a = -jnp.exp(alog_ref[...])
    da = dt * a
    ri = lax.broadcasted_iota(jnp.int32, (SSD_Q, SSD_Q), 0)
    ci = lax.broadcasted_iota(jnp.int32, (SSD_Q, SSD_Q), 1)
    causal = ri >= ci
    tril = jnp.where(causal, 1.0, 0.0).astype(BF16)
    hi, mid, lo = _split3(da)
    acum = _dot(tril, hi) + _dot(tril, mid) + _dot(tril, lo)
    acum_t = acum.T

    for g in range(SSM_GROUPS):
        bm_g = bm[:, g * SSM_STATE:(g + 1) * SSM_STATE]
        cm_g = cm[:, g * SSM_STATE:(g + 1) * SSM_STATE].astype(BF16)
        bm_t = bm_g.T.astype(BF16)
        cb = _dot(cm_g, bm_t)
        gs = []
        ssq = jnp.zeros((SSD_Q, 1), F32)
        for r in range(SSM_HPG):
            hh = g * SSM_HPG + r
            lanes = slice(hh * SSM_HEAD_DIM, (hh + 1) * SSM_HEAD_DIM)
            xh = xs[:, lanes]
            dl = DT_LANE0 + hh
            col = acum[:, dl:dl + 1]
            rowv = acum_t[dl:dl + 1, :]
            decay = jnp.exp(jnp.where(causal, col - rowv, NEG))
            xdt = xh * dt[:, dl:dl + 1]
            y = _dot((cb * decay).astype(BF16), xdt.astype(BF16))
            h_prev = ht[hh]
            y = y + jnp.exp(col) * _dot(cm_g, h_prev.astype(BF16))
            last = acum[SSD_Q - 1:SSD_Q, dl:dl + 1]
            xw = (xdt * jnp.exp(last - col)).astype(BF16)
            ht[hh] = h_prev * jnp.exp(last) + _dot(bm_t, xw)
            y = y + xh * dsk_ref[:, lanes]
            gate = y * _silu(z[:, lanes])
            ssq = ssq + jnp.sum(gate * gate, axis=1, keepdims=True)
            gs.append(gate)
        scale = lax.rsqrt(ssq / float(SSM_HPG * SSM_HEAD_DIM) + NORM_EPS)
        for r in range(SSM_HPG):
            hh = g * SSM_HPG + r
            lanes = slice(hh * SSM_HEAD_DIM, (hh + 1) * SSM_HEAD_DIM)
            y_ref[0, :, lanes] = (gs[r] * scale * nw_ref[:, lanes])[:q_in].astype(BF16)

    @pl.when(c == nc - 1)
    def _():
        for hh in range(SSM_HEADS):
            h_ref[0, hh] = ht[hh].T


def _ssd(xs, bm, cm, dt, z, alog, dsk, nw, h0, *, q_in):
    b, t, _ = xs.shape
    row = lambda width: pl.BlockSpec((1, q_in, width), lambda bi, c: (bi, c, 0))
    full = lambda a: _resident(a.shape, lambda bi, c: (0,) * a.ndim)
    hspec = pl.BlockSpec((1, SSM_HEADS, SSM_HEAD_DIM, SSM_STATE), lambda bi, c: (bi, 0, 0, 0))
    return pl.pallas_call(
        functools.partial(_ssd_kernel, q_in=q_in),
        grid=(b, t // q_in),
        in_specs=[row(D_SSM), row(256), row(256), row(LANE), row(D_SSM),
                  full(alog), full(dsk), full(nw), hspec],
        out_specs=(row(D_SSM), hspec),
        out_shape=(jax.ShapeDtypeStruct((b, t, D_SSM), BF16),
                   jax.ShapeDtypeStruct((b, SSM_HEADS, SSM_HEAD_DIM, SSM_STATE), F32)),
        scratch_shapes=[pltpu.VMEM((SSM_HEADS, SSM_STATE, SSM_HEAD_DIM), F32)],
        compiler_params=pltpu.CompilerParams(
            dimension_semantics=("arbitrary", "arbitrary"), vmem_limit_bytes=VMEM_LIMIT),
        name="ssd",
    )(xs, bm, cm, dt, z, alog, dsk, nw, h0)


def _mix_ffn_kernel(x_ref, ya_ref, yb_ref, yc_ref, wo_ref, npost_ref, nfpre_ref, wg_ref, wu_ref,
                    wd_ref, cfw_ref, cfb_ref, nfpost_ref, buff_ref, o_ref, buff_out, eg, *, tm):
    i = pl.program_id(1)

    @pl.when(i == 0)
    def _():
        eg[CARRY - 2:CARRY, :] = buff_ref[0]

    mix = (_dot(ya_ref[0], wo_ref[0:D_CONV_MIX, :])
           + _dot(yb_ref[0], wo_ref[D_CONV_MIX:D_CONV_MIX + D_ATT, :])
           + _dot(yc_ref[0], wo_ref[D_CONV_MIX + D_ATT:, :]))
    x1 = x_ref[0] + _rms(mix, npost_ref[...])
    u = _rms(x1, nfpre_ref[...]).astype(BF16)
    eg[CARRY:CARRY + tm, :] = _dot(u, wg_ref[...])
    gc = eg[CARRY - 2:CARRY - 2 + tm, :] * cfw_ref[0:1, :]
    for j in range(1, CONV_F_W):
        gc = gc + eg[CARRY - 2 + j:CARRY - 2 + j + tm, :] * cfw_ref[j:j + 1, :]
    hid = (_silu(gc + cfb_ref[...]) * _dot(u, wu_ref[...])).astype(BF16)
    o_ref[0] = x1 + _rms(_dot(hid, wd_ref[...]), nfpost_ref[...])
    tail = eg[CARRY + tm - 2:CARRY + tm, :]
    buff_out[0] = tail
    eg[CARRY - 2:CARRY, :] = tail


def _mix_ffn(x, ya, yb, yc, wo, npost, nfpre, wg, wu, wd, cfw, cfb, nfpost, buff, *, tm):
    b, t, _ = x.shape
    row = lambda width: pl.BlockSpec((1, tm, width), lambda bi, i: (bi, i, 0))
    full = lambda a: _resident(a.shape, lambda bi, i: (0,) * a.ndim)
    state = pl.BlockSpec((1, CONV_F_W - 1, D_FF), lambda bi, i: (bi, 0, 0))
    return pl.pallas_call(
        functools.partial(_mix_ffn_kernel, tm=tm),
        grid=(b, t // tm),
        in_specs=[row(D_MODEL), row(D_CONV_MIX), row(D_ATT), row(D_SSM), full(wo), full(npost),
                  full(nfpre), full(wg), full(wu), full(wd), full(cfw), full(cfb), full(nfpost),
                  state],
        out_specs=(row(D_MODEL), state),
        out_shape=(jax.ShapeDtypeStruct((b, t, D_MODEL), F32),
                   jax.ShapeDtypeStruct((b, CONV_F_W - 1, D_FF), F32)),
        scratch_shapes=[pltpu.VMEM((CARRY + tm, D_FF), F32)],
        compiler_params=pltpu.CompilerParams(
            dimension_semantics=("arbitrary", "arbitrary"), vmem_limit_bytes=VMEM_LIMIT),
        name="mix_ffn",
    )(x, ya, yb, yc, wo, npost, nfpre, wg, wu, wd, cfw, cfb, nfpost, buff)


def _pack_keys_kernel(ck_ref, cv_ref, cki_ref, k_ref, v_ref, ki_ref, kb_ref, vt_ref, kib_ref, *, past_steps):
    j = pl.program_id(1)
    rows = kb_ref.shape[1]
    cached = j < past_steps

    def with_new(cache, new):
        new = jnp.concatenate([new, jnp.zeros((rows - new.shape[0], new.shape[1]), F32)], axis=0)
        return jnp.where(cached, cache, new)

    heads = lambda ref: [ref[0, 0, g] for g in range(N_KV_HEADS)]
    k_cache = jnp.concatenate([h.T for h in heads(ck_ref)], axis=1)
    kb_ref[0] = with_new(k_cache, k_ref[0]).astype(BF16)
    v_new = jnp.concatenate([v_ref[0], jnp.zeros((rows - v_ref.shape[1], v_ref.shape[2]), F32)], axis=0)
    vt = jnp.where(cached, jnp.concatenate(heads(cv_ref), axis=0), v_new.T)
    vt_ref[0] = _augment_vt(vt).astype(BF16)
    kib_ref[0] = with_new(cki_ref[0, 0].T, ki_ref[0]).astype(BF16)


def _pack_keys(layer, cache_k, cache_v, cache_kidx, k, v, ki):
    b, past = cache_k.shape[1:3]
    t = k.shape[1]
    rows = BIG * KEY_TILE
    assert past % rows == 0 and t <= rows
    past_steps = past // rows
    lk = past + rows
    cache = lambda *major: pl.BlockSpec(
        (1, 1) + major + (rows,),
        lambda bi, j: (layer, bi) + (0,) * len(major) + (jnp.minimum(j, past_steps - 1),))
    cache_k, cache_v = (jnp.transpose(c, (0, 1, 3, 4, 2)) for c in (cache_k, cache_v))
    cache_kidx = jnp.transpose(cache_kidx, (0, 1, 3, 2))
    new = lambda width: pl.BlockSpec((1, t, width), lambda bi, j: (bi, 0, 0))
    out = lambda width: pl.BlockSpec((1, rows, width), lambda bi, j: (bi, j, 0))
    return pl.pallas_call(
        functools.partial(_pack_keys_kernel, past_steps=past_steps),
        grid=(b, past_steps + 1),
        in_specs=[cache(N_KV_HEADS, HEAD_DIM), cache(N_KV_HEADS, HEAD_DIM), cache(D_IDX),
                  new(128), new(128), new(D_IDX)],
        out_specs=(out(128), pl.BlockSpec((1, N_KV_HEADS * VT_ROWS, rows), lambda bi, j: (bi, 0, j)),
                   out(D_IDX)),
        out_shape=(jax.ShapeDtypeStruct((b, lk, 128), BF16),
                   jax.ShapeDtypeStruct((b, N_KV_HEADS * VT_ROWS, lk), BF16),
                   jax.ShapeDtypeStruct((b, lk, D_IDX), BF16)),
        compiler_params=pltpu.CompilerParams(
            dimension_semantics=("arbitrary", "arbitrary"), vmem_limit_bytes=VMEM_LIMIT),
        name="pack_keys",
    )(cache_k, cache_v, cache_kidx, k, v, ki)


def _layer(x, p, rel_bias, buf_a, buf_s, h0, buf_f, kv_past, *, tm):
    b, t, _ = x.shape
    outs = _in_proj(x, p['norm_mix_pre'], p['w_in'], p['conv_a_w'], p['conv_ssm_w'], p['conv_ssm_b'],
                    p['dt_bias'], buf_a, buf_s, tm=tm, key_side=kv_past is None)
    (ya, q, k, v, qi, ki, sgn, z, xs, bm, cm, dt, buf_a_new, buf_s_new) = outs[:14]
    if kv_past is None:
        kb, vt, kib = outs[14:]
        yb = _dsa(rel_bias, q, qi, sgn, kb, vt, kib, q_pos0=0, l_valid=t)
    else:
        layer, cache_k, cache_v, cache_kidx = kv_past
        past = cache_k.shape[2]
        qpad = lambda a: jnp.pad(a, ((0, 0), (0, -t % KEY_TILE), (0, 0)))
        kb, vt, kib = _pack_keys(layer, cache_k, cache_v, cache_kidx, k, v, ki)
        yb = _dsa(rel_bias, qpad(q), qpad(qi), sgn, kb, vt, kib, q_pos0=past, l_valid=past + t)[:, :t]
    yc, h_new = _ssd(xs, bm, cm, dt, z, p['a_log'], p['d_skip'], p['ssm_norm_w'],
                     h0.reshape(b, SSM_HEADS, SSM_HEAD_DIM, SSM_STATE), q_in=min(t, SSD_Q))
    x_new, buf_f_new = _mix_ffn(x, ya, yb, yc, p['w_out'], p['norm_mix_post'], p['norm_ffn_pre'],
                                p['w_gate'], p['w_up'], p['w_down'], p['conv_ffn_w'],
                                p['conv_ffn_b'], p['norm_ffn_post'], buf_f, tm=tm)
    st = (k.reshape(b, t, N_KV_HEADS, HEAD_DIM), v.reshape(b, t, N_KV_HEADS, HEAD_DIM), ki,
          buf_a_new, buf_s_new, h_new.reshape(b, SSM_GROUPS, SSM_HPG, SSM_HEAD_DIM, SSM_STATE),
          buf_f_new)
    return x_new, st


def _pack_w_in_kernel(wt_ref, o_ref):
    for c, c0 in enumerate(W_IN_TILE_COLS):
        o_ref[:, c * LANE:(c + 1) * LANE] = wt_ref[c0:c0 + LANE, :].T.astype(BF16)


def _pack_w_in(wt):
    return pl.pallas_call(
        _pack_w_in_kernel,
        out_shape=jax.ShapeDtypeStruct((wt.shape[1], D_IN_PACKED), BF16),
        compiler_params=pltpu.CompilerParams(vmem_limit_bytes=VMEM_LIMIT),
        name="pack_w_in",
    )(wt)


def _dt_lanes(v):
    return jnp.pad(v, (DT_LANE0, 0))[None, :]


def kernel(x_prompt, x_sample, cache_k, cache_v, cache_kidx, state_conv_a, state_conv_ssm, state_ssm, state_conv_ffn, rel_bias, norm_mix_pre, norm_mix_post, norm_ffn_pre, norm_ffn_post, w_in, conv_a_w, conv_ssm_w, conv_ssm_b, dt_bias, a_log, d_skip, ssm_norm_w, w_out, w_gate, w_up, conv_ffn_w, conv_ffn_b, w_down):
    depth = w_in.shape[0]
    bp = x_prompt.shape[0]
    yp, ys = x_prompt, x_sample
    outs_p, outs_s = [], []
    for l in range(depth):
        p = {'norm_mix_pre': norm_mix_pre[l][None], 'norm_mix_post': norm_mix_post[l][None],
             'norm_ffn_pre': norm_ffn_pre[l][None], 'norm_ffn_post': norm_ffn_post[l][None],
             'w_in': _pack_w_in(jnp.swapaxes(w_in[l], 0, 1)), 'conv_a_w': conv_a_w[l], 'conv_ssm_w': conv_ssm_w[l],
             'conv_ssm_b': conv_ssm_b[l][None], 'dt_bias': _dt_lanes(dt_bias[l]),
             'a_log': _dt_lanes(a_log[l]), 'd_skip': jnp.repeat(d_skip[l], SSM_HEAD_DIM)[None],
             'ssm_norm_w': ssm_norm_w[l][None], 'w_out': w_out[l].astype(BF16),
             'w_gate': w_gate[l].astype(BF16), 'w_up': w_up[l].astype(BF16),
             'conv_ffn_w': conv_ffn_w[l], 'conv_ffn_b': conv_ffn_b[l][None],
             'w_down': w_down[l].astype(BF16)}
        yp, st_p = _layer(
            yp, p, rel_bias,
            jnp.zeros((bp, CONV_A_W - 1, D_CONV_MIX), F32),
            jnp.zeros((bp, SSM_CONV_W - 1, D_XBC), F32),
            jnp.zeros((bp, SSM_GROUPS, SSM_HPG, SSM_HEAD_DIM, SSM_STATE), F32),
            jnp.zeros((bp, CONV_F_W - 1, D_FF), F32),
            None, tm=256)
        ys, st_s = _layer(
            ys, p, rel_bias, state_conv_a[l], state_conv_ssm[l], state_ssm[l], state_conv_ffn[l],
            (l, cache_k, cache_v, cache_kidx), tm=ys.shape[1])
        outs_p.append(st_p)
        outs_s.append(st_s)

    def stack(outs, i):
        return jnp.stack([o[i] for o in outs], axis=0)

    res = [yp, ys]
    for i in range(7):
        res.append(stack(outs_p, i))
        res.append(stack(outs_s, i))
    return tuple(res)
```

```python
import functools

import jax
import jax.numpy as jnp
from jax import lax
from jax.experimental import pallas as pl
from jax.experimental.pallas import tpu as pltpu

F32 = jnp.float32
BF16 = jnp.bfloat16

D_MODEL = 1024
CHUNK = 64
D_CONV_MIX = 256
CONV_A_W = 3
N_HEADS = 6
N_KV_HEADS = 2
HEAD_DIM = 64
D_ATT = N_HEADS * HEAD_DIM
GRP = N_HEADS // N_KV_HEADS
N_IDX_HEADS = 4
D_IDX = 64
TOPK = 256
N_BUCKETS = 32
SSM_HEADS = 6
SSM_HEAD_DIM = 64
D_SSM = SSM_HEADS * SSM_HEAD_DIM
SSM_GROUPS = 2
SSM_HPG = SSM_HEADS // SSM_GROUPS
SSM_STATE = 128
SSM_CONV_W = 4
D_XBC = D_SSM + 2 * SSM_GROUPS * SSM_STATE
D_FF = 2816
CONV_F_W = 3
NORM_EPS = 1e-6

LANE = 128
CARRY = 8
SSD_Q = 128
KEY_TILE = 128
BIG = 4
VMEM_LIMIT = 56 * 1024 * 1024
NEG = -1e30
LOG2E = 1.4426950408889634
F32_MAX = 3.4028234663852886e38
F32_TINY = 1.1754943508222875e-38
NO_QUOTA = 1e9
MAX_SEARCH_STEPS = 256
BLIND_SEARCH_STEPS = 15

C_AB, C_AC, C_AH = 0, 256, 512
C_Q = 768
C_K = 1152
C_V = 1280
C_QI = 1408
C_KI = 1664
C_Z = 1792
C_XBC = 2176
C_DT = 3072
D_IN_PACKED = 3200
PROJ_GROUPS = (0, C_Q, C_Z, D_IN_PACKED)
D_IN = 3018
D_IN_HEAD = 1732
DT_LANE0 = LANE - SSM_HEADS
W_IN_TILE_COLS = ([c * LANE for c in range(C_Z // LANE)]
                  + [D_IN_HEAD + c * LANE for c in range((C_DT - C_Z) // LANE)] + [D_IN - LANE])


def _rms(x, w):
    return x * lax.rsqrt(jnp.mean(x * x, axis=-1, keepdims=True) + NORM_EPS) * w


def _dot(a, b):
    return jnp.dot(a, b, preferred_element_type=F32)


def _silu(x):
    return x * jax.nn.sigmoid(x)


VT_ROWS = HEAD_DIM + 16


def _augment_vt(vt):
    extra_shape = vt.shape[:-2] + (VT_ROWS - HEAD_DIM, vt.shape[-1])
    row = lax.broadcasted_iota(jnp.int32, extra_shape, len(extra_shape) - 2)
    extra = jnp.where(row == 0, 1.0, 0.0).astype(vt.dtype)
    parts = []
    for g in range(N_KV_HEADS):
        parts += [vt[..., g * HEAD_DIM:(g + 1) * HEAD_DIM, :], extra]
    return jnp.concatenate(parts, axis=-2)


def _resident(shape, index_map):
    return pl.BlockSpec(shape, index_map, pipeline_mode=pl.Buffered(1))


def _in_proj_kernel(x_ref, nw_ref, w_ref, caw_ref, csw_ref, csb_ref, dtb_ref, bufa_ref, bufs_ref,
                    ya_ref, q_ref, k_ref, v_ref, qi_ref, ki_ref, sgn_ref, z_ref, xs_ref, bm_ref,
                    cm_ref, dt_ref, bufa_out, bufs_out, *rest, tm, key_side):
    if key_side:
        kb_ref, vt_ref, kib_ref, ea, es = rest
    else:
        ea, es = rest
    i = pl.program_id(1)
    u = _rms(x_ref[0], nw_ref[...]).astype(BF16)

    slabs = {}

    def proj(c0, width):
        g0, g1 = next((a, b) for a, b in zip(PROJ_GROUPS, PROJ_GROUPS[1:]) if a <= c0 < b)
        if g0 not in slabs:
            slabs[g0] = _dot(u, w_ref[:, g0:g1])
        return slabs[g0][:, c0 - g0:c0 - g0 + width]

    @pl.when(i == 0)
    def _():
        ea[CARRY - 2:CARRY, :] = bufa_ref[0]
        es[CARRY - 3:CARRY, :] = bufs_ref[0]

    ea[CARRY:CARRY + tm, :] = proj(C_AC, D_CONV_MIX) * proj(C_AH, D_CONV_MIX)
    conv = ea[CARRY - 2:CARRY - 2 + tm, :] * caw_ref[0:1, :]
    for j in range(1, CONV_A_W):
        conv = conv + ea[CARRY - 2 + j:CARRY - 2 + j + tm, :] * caw_ref[j:j + 1, :]
    ya_ref[0] = (proj(C_AB, D_CONV_MIX) * conv).astype(BF16)
    tail_a = ea[CARRY + tm - 2:CARRY + tm, :]
    bufa_out[0] = tail_a
    ea[CARRY - 2:CARRY, :] = tail_a

    q_ref[0] = (proj(C_Q, D_ATT) * (HEAD_DIM ** -0.5 * LOG2E)).astype(BF16)
    k = proj(C_K, N_KV_HEADS * HEAD_DIM)
    v = proj(C_V, N_KV_HEADS * HEAD_DIM)
    kiw = proj(C_KI, LANE)
    k_ref[0] = k
    v_ref[0] = v
    ki_ref[0] = kiw[:, :D_IDX]
    wi = kiw[:, D_IDX:D_IDX + N_IDX_HEADS]
    kiw_rows = kiw if tm >= LANE else jnp.concatenate([kiw, jnp.zeros((LANE - tm, LANE), F32)], axis=0)
    sgn_ref[0] = jnp.where(kiw_rows.T[D_IDX:D_IDX + 8, :] >= 0.0, 1.0, -1.0).astype(F32)
    if key_side:
        kb_ref[0] = k.astype(BF16)
        vt_ref[0] = _augment_vt(v.T).astype(BF16)
        kib_ref[0] = kiw[:, :D_IDX].astype(BF16)
    wabs = jnp.abs(wi) * (D_IDX ** -0.5 * N_IDX_HEADS ** -0.5)
    qi = proj(C_QI, N_IDX_HEADS * D_IDX)
    for h in range(N_IDX_HEADS):
        qi_ref[0, :, h * D_IDX:(h + 1) * D_IDX] = (
            qi[:, h * D_IDX:(h + 1) * D_IDX] * wabs[:, h:h + 1]).astype(BF16)

    z_ref[0] = proj(C_Z, D_SSM)
    es[CARRY:CARRY + tm, :] = proj(C_XBC, D_XBC)
    xc = es[CARRY - 3:CARRY - 3 + tm, :] * csw_ref[0:1, :]
    for j in range(1, SSM_CONV_W):
        xc = xc + es[CARRY - 3 + j:CARRY - 3 + j + tm, :] * csw_ref[j:j + 1, :]
    xc = _silu(xc + csb_ref[...])
    xs_ref[0] = xc[:, :D_SSM]
    bm_ref[0] = xc[:, D_SSM:D_SSM + SSM_GROUPS * SSM_STATE]
    cm_ref[0] = xc[:, D_SSM + SSM_GROUPS * SSM_STATE:]
    tail_s = es[CARRY + tm - 3:CARRY + tm, :]
    bufs_out[0] = tail_s
    es[CARRY - 3:CARRY, :] = tail_s
    dtr = proj(C_DT, LANE) + dtb_ref[...]
    dt_ref[0] = jnp.maximum(dtr, 0.0) + jnp.log1p(jnp.exp(-jnp.abs(dtr)))


def _in_proj(x, nw, w_packed, caw, csw, csb, dtb, bufa, bufs, *, tm, key_side):
    b, t, _ = x.shape
    tq = max(tm, LANE)
    grid = (b, t // tm)
    row = lambda width: pl.BlockSpec((1, tm, width), lambda bi, i: (bi, i, 0))
    full = lambda a: _resident(a.shape, lambda bi, i: (0,) * a.ndim)
    state = lambda r, width: pl.BlockSpec((1, r, width), lambda bi, i: (bi, 0, 0))
    out_shapes = (
        jax.ShapeDtypeStruct((b, t, D_CONV_MIX), BF16),
        jax.ShapeDtypeStruct((b, t, D_ATT), BF16),
        jax.ShapeDtypeStruct((b, t, 128), F32),
        jax.ShapeDtypeStruct((b, t, 128), F32),
        jax.ShapeDtypeStruct((b, t, 256), BF16),
        jax.ShapeDtypeStruct((b, t, D_IDX), F32),
        jax.ShapeDtypeStruct((b, 8, t // tm * tq), F32),
        jax.ShapeDtypeStruct((b, t, D_SSM), F32),
        jax.ShapeDtypeStruct((b, t, D_SSM), F32),
        jax.ShapeDtypeStruct((b, t, 256), F32),
        jax.ShapeDtypeStruct((b, t, 256), F32),
        jax.ShapeDtypeStruct((b, t, LANE), F32),
        jax.ShapeDtypeStruct((b, CONV_A_W - 1, D_CONV_MIX), F32),
        jax.ShapeDtypeStruct((b, SSM_CONV_W - 1, D_XBC), F32),
    )
    out_specs = (row(D_CONV_MIX), row(D_ATT), row(128), row(128), row(256), row(D_IDX),
                 pl.BlockSpec((1, 8, tq), lambda bi, i: (bi, 0, i)),
                 row(D_SSM), row(D_SSM), row(256), row(256), row(LANE),
                 state(CONV_A_W - 1, D_CONV_MIX), state(SSM_CONV_W - 1, D_XBC))
    if key_side:
        out_shapes += (jax.ShapeDtypeStruct((b, t, 128), BF16),
                       jax.ShapeDtypeStruct((b, N_KV_HEADS * VT_ROWS, t), BF16),
                       jax.ShapeDtypeStruct((b, t, D_IDX), BF16))
        out_specs += (row(128), pl.BlockSpec((1, N_KV_HEADS * VT_ROWS, tm), lambda bi, i: (bi, 0, i)), row(D_IDX))
    in_specs = [row(D_MODEL), full(nw), full(w_packed), full(caw), full(csw), full(csb), full(dtb),
                state(CONV_A_W - 1, D_CONV_MIX), state(SSM_CONV_W - 1, D_XBC)]
    return pl.pallas_call(
        functools.partial(_in_proj_kernel, tm=tm, key_side=key_side),
        grid=grid, in_specs=in_specs, out_specs=out_specs, out_shape=out_shapes,
        scratch_shapes=[pltpu.VMEM((CARRY + tm, D_CONV_MIX), F32),
                        pltpu.VMEM((CARRY + tm, D_XBC), F32)],
        compiler_params=pltpu.CompilerParams(
            dimension_semantics=("arbitrary", "arbitrary"), vmem_limit_bytes=VMEM_LIMIT),
        name="in_proj",
    )(x, nw, w_packed, caw, csw, csb, dtb, bufa, bufs)


def _t5_bucket(rel):
    n = jnp.abs(rel)
    large = jnp.full(rel.shape, 8, jnp.int32)
    for brk in (12, 16, 23, 32, 46, 64, 91):
        large = large + jnp.where(n >= brk, 1, 0)
    return jnp.where(rel > 0, N_BUCKETS // 2, 0) + jnp.where(n < 8, n, large)


def _f32_key(x):
    bits = lax.bitcast_convert_type(x, jnp.int32)
    return bits ^ ((bits >> 31) & jnp.int32(0x7FFFFFFF))


def _key_f32(key):
    return lax.bitcast_convert_type(key ^ ((key >> 31) & jnp.int32(0x7FFFFFFF)), F32)


def _dsa_kernel(relb_ref, q_ref, qi_ref, sgn_ref, k_ref, vt_ref, ki_ref, o_ref, sc, bias, sbuf,
                *, q_pos0, l_valid):
    tq = KEY_TILE
    i = pl.program_id(1)
    q0 = q_pos0 + i * tq
    qt = q0 // KEY_TILE
    n0 = jnp.maximum(qt - 1, 0)
    variant = jnp.minimum(qt, 1)
    steps_far = (n0 + BIG - 1) // BIG
    steps_all = (n0 + 2 + BIG - 1) // BIG
    kw = BIG * KEY_TILE
    kf = float(TOPK)

    def step_keys(j):
        return pl.ds(pl.multiple_of(j * kw, kw), kw)

    near_keys = pl.ds(pl.multiple_of(n0 * KEY_TILE, KEY_TILE), 2 * KEY_TILE)

    @pl.when(i == 0)
    def _():
        sc[...] = jnp.full(sc.shape, -jnp.inf, F32)

    @pl.when((i == 0) & (pl.program_id(0) == 0))
    def _():
        kk = lax.broadcasted_iota(jnp.int32, (2 * KEY_TILE, tq), 0)
        r = lax.broadcasted_iota(jnp.int32, (2 * KEY_TILE, tq), 1)
        for var in range(2):
            bucket = _t5_bucket(kk - var * KEY_TILE - r)
            for h in range(N_HEADS):
                tab = jnp.zeros((2 * KEY_TILE, tq), F32)
                for bk in range(N_BUCKETS):
                    tab = jnp.where(bucket == bk, relb_ref[bk, h], tab)
                g, j = divmod(h, GRP)
                bias[var, g, :, j * tq:(j + 1) * tq] = (tab - relb_ref[N_BUCKETS // 2 - 1, h]) * LOG2E

    qit = qi_ref[0].astype(F32).T
    qis = jnp.concatenate([qit[h * D_IDX:(h + 1) * D_IDX, :] for h in range(N_IDX_HEADS)],
                          axis=1).astype(BF16)
    sgn = sgn_ref[0]

    def scores(keys):
        s = jnp.maximum(_dot(ki_ref[0, keys, :], qis), 0.0)
        tot = s[:, 0:tq] * sgn[0:1, :]
        for h in range(1, N_IDX_HEADS):
            tot = tot + s[:, h * tq:(h + 1) * tq] * sgn[h:h + 1, :]
        return tot

    zero = jnp.zeros((KEY_TILE, tq), F32)

    tally_init = (jnp.zeros((KEY_TILE, tq), BF16), jnp.zeros((KEY_TILE, tq), BF16),
                  jnp.full((KEY_TILE, tq), float(jnp.finfo(BF16).min), BF16))
    one16, nil16 = jnp.ones((KEY_TILE, tq), BF16), jnp.zeros((KEY_TILE, tq), BF16)

    def tally(c, s):
        ge, gt, mx = c
        s16 = s.astype(BF16)
        return (ge + jnp.where(s16 >= 0.0, one16, nil16),
                gt + jnp.where(s16 >= jnp.asarray(F32_TINY, BF16), one16, nil16), jnp.maximum(mx, s16))

    def tally_totals(c):
        ge, gt, mx = c
        return (col_sum(ge.astype(F32)), col_sum(gt.astype(F32)),
                jnp.max(mx.astype(F32), axis=0, keepdims=True))

    def score_far(j, c):
        s = scores(step_keys(j))
        for k in range(BIG):
            t = j * BIG + k
            tile = jnp.where(t < n0, s[k * KEY_TILE:(k + 1) * KEY_TILE, :], -jnp.inf)
            sc[t] = tile
            c = tally(c, tile)
        return c

    far_pairs = steps_far // 2
    c = lax.fori_loop(0, far_pairs, lambda j, c: score_far(2 * j + 1, score_far(2 * j, c)),
                      tally_init)
    c = lax.fori_loop(2 * far_pairs, steps_far, score_far, c)
    s_near = scores(near_keys)
    kpos = n0 * KEY_TILE + lax.broadcasted_iota(jnp.int32, (2 * KEY_TILE, tq), 0)
    qpos = q0 + lax.broadcasted_iota(jnp.int32, (2 * KEY_TILE, tq), 1)
    adm = (kpos < l_valid) & ((kpos >> 6) <= (qpos >> 6))
    s_near = jnp.where(adm, s_near, -jnp.inf)
    sc[n0] = s_near[:KEY_TILE, :]
    sc[n0 + 1] = s_near[KEY_TILE:, :]
    for k in range(BIG - 1):
        @pl.when(n0 + 2 + k < steps_all * BIG)
        def _():
            sc[n0 + 2 + k] = jnp.full((KEY_TILE, tq), -jnp.inf, F32)
    score_tally = tally(tally(c, s_near[:KEY_TILE, :]), s_near[KEY_TILE:, :])


    def col_sum(acc):
        parts = [acc[r:r + 8, :] for r in range(0, KEY_TILE, 8)]
        while len(parts) > 1:
            parts = [parts[r] + parts[r + 1] for r in range(0, len(parts), 2)]
        return jnp.sum(parts[0], axis=0, keepdims=True)

    def counter(value_at):
        def count_ge(t):
            tb = jnp.broadcast_to(t, (KEY_TILE, tq))

            def body(j, acc):
                for k in range(BIG):
                    acc = acc + jnp.where(value_at(j * BIG + k) >= tb, 1.0, 0.0)
                return acc
            return col_sum(lax.fori_loop(0, steps_all, body, zero))
        return count_ge

    def search(count_ge, kq, nfin, ge0, gt0, rmax, active, blind_steps):
        small = (nfin <= kq) | (active == 0.0)
        tie0 = (~small) & (gt0 < kq) & (ge0 >= kq)
        pos = (~small) & (gt0 >= kq)
        neg = (~small) & (ge0 < kq)
        log_kq = jnp.log(kq)
        lo0 = jnp.where(pos, F32_TINY, -F32_MAX)
        flo0 = jnp.where(pos, gt0, nfin)
        hi0 = jnp.where(neg, 0.0, jnp.inf)
        fhi0 = jnp.where(neg, ge0, 0.0)
        done0 = jnp.where(small | tie0 | (pos & (gt0 == kq)), 1.0, 0.0)
        thr0 = jnp.where(small, -F32_MAX, jnp.where(tie0, 0.0, F32_TINY))
        quota0 = jnp.where(tie0, kq - gt0, NO_QUOTA)

        def step(it, state):
            lo, flo, hi, fhi, done, thr, quota, adjacent = state
            klo, khi = _f32_key(lo), _f32_key(hi)
            la = jnp.log(flo)
            frac = (la - log_kq) / (la - jnp.log(jnp.maximum(fhi, 0.5)))
            interp = lo + (hi - lo) * jnp.where(it % 4 == 1, 0.5, frac)
            t_up = lo + (rmax - lo) * 0.5
            t_dn = hi - jnp.maximum(jnp.maximum(jnp.abs(rmax - hi), jnp.abs(hi)), F32_TINY)
            open_hi = hi == jnp.inf
            open_lo = lo <= -F32_MAX
            t = jnp.where(open_hi, t_up, jnp.where(open_lo, t_dn, interp))
            tk = jnp.minimum(jnp.maximum(_f32_key(t), klo + 1), khi - 1)
            tk = jnp.where(it % 4 == 3, klo + lax.shift_right_logical(khi - klo, 1), tk)
            t = _key_f32(tk)
            f = count_ge(t)
            act = done == 0.0
            hit = act & (f == kq)
            up = act & (f > kq)
            dn = act & (f < kq)
            lo = jnp.where(up, t, lo)
            flo = jnp.where(up, f, flo)
            hi = jnp.where(dn, t, hi)
            fhi = jnp.where(dn, f, fhi)
            adj = act & (~hit) & (_f32_key(lo) + 1 == _f32_key(hi))
            thr = jnp.where(hit, t, jnp.where(adj, lo, thr))
            quota = jnp.where(adj, kq - fhi, quota)
            done = jnp.where(hit | adj, 1.0, done)
            return lo, flo, hi, fhi, done, thr, quota, jnp.where(adj, 1.0, adjacent)

        state = (lo0, flo0, hi0, fhi0, done0, thr0, quota0, jnp.zeros((1, tq), F32))
        if blind_steps:
            state = lax.fori_loop(0, blind_steps, step, state)

        def cond(c):
            return (c[0] < MAX_SEARCH_STEPS) & (c[1] > 0)

        def body(c):
            state = step(c[0], c[2])
            return c[0] + 1, jnp.sum(1.0 - state[4]), state

        _, _, state = lax.while_loop(cond, body, (jnp.int32(blind_steps), jnp.sum(1.0 - state[4]), state))
        _, flo, _, fhi, _, thr, quota, adjacent = state
        return thr, quota, adjacent, flo, fhi

    qrow = q0 + lax.broadcasted_iota(jnp.int32, (1, tq), 1)
    nadm = jnp.minimum(((qrow >> 6) + 1) * CHUNK, l_valid).astype(F32)
    thr, quota, adjacent, flo, fhi = search(
        counter(lambda t: sc[t]), jnp.full((1, tq), kf, F32), nadm,
        *tally_totals(score_tally), jnp.ones((1, tq), F32), BLIND_SEARCH_STEPS)
    thr_b = jnp.broadcast_to(thr, (tq, KEY_TILE))
    quota_b = jnp.broadcast_to(quota, (tq, KEY_TILE))
    has_ties = jnp.min(quota) < NO_QUOTA
    any_adjacent = jnp.max(adjacent) > 0.0

    @pl.when(jnp.logical_not(has_ties))
    def _():
        def body(j, c):
            for k in range(BIG):
                t = j * BIG + k
                sc[t] = jnp.where(sc[t] >= thr_b, 0.0, NEG)
            return c
        lax.fori_loop(0, steps_all, body, 0)

    def rank_pass(is_tie, is_sure, quota_of_ties):
        ri = lax.broadcasted_iota(jnp.int32, (KEY_TILE, KEY_TILE), 0)
        ci = lax.broadcasted_iota(jnp.int32, (KEY_TILE, KEY_TILE), 1)
        lower = jnp.where(ri >= ci, 1.0, 0.0).astype(BF16)

        def rank_tiles(first, n, seen):
            for k in range(n):
                t = first + k
                s = sc[t]
                tie = jnp.where(is_tie(s), 1.0, 0.0)
                rank = seen + _dot(lower, tie.astype(BF16))
                keep = is_sure(s) | ((tie > 0.0) & (rank <= quota_of_ties))
                sc[t] = jnp.where(keep, 0.0, NEG)
                seen = seen + jnp.sum(tie, axis=0, keepdims=True)
            return seen

        pairs = steps_all // 2
        seen = lax.fori_loop(0, pairs, lambda j, c: rank_tiles(j * 2 * BIG, 2 * BIG, c),
                             jnp.zeros((1, tq), F32))
        lax.fori_loop(2 * pairs, steps_all, lambda j, c: rank_tiles(j * BIG, BIG, c), seen)

    @pl.when(has_ties & jnp.logical_not(any_adjacent))
    def _():
        rank_pass(lambda s: s == thr_b, lambda s: s > thr_b, quota_b)

    @pl.when(any_adjacent)
    def _():
        adj_b = jnp.broadcast_to(adjacent, (tq, KEY_TILE)) > 0.0
        next_b = jnp.broadcast_to(_key_f32(_f32_key(thr) + 1), (tq, KEY_TILE))

        def in_cell(s):
            return adj_b & (s >= thr_b) & (s < next_b)

        def offset(s):
            return jnp.where(in_cell(s), s - thr_b, -jnp.inf)

        def tally_offsets(j, c):
            for k in range(BIG):
                c = tally(c, offset(sc[j * BIG + k]))
            return c

        in_cells, above_thr, max_offset = tally_totals(lax.fori_loop(0, steps_all, tally_offsets, tally_init))
        thr2, quota2, _, _, _ = search(
            counter(lambda t: offset(sc[t])), jnp.where(adjacent > 0.0, quota, 1.0), in_cells, in_cells,
            above_thr, max_offset, adjacent, 0)
        thr2_b = jnp.broadcast_to(thr2, (tq, KEY_TILE))
        quota2_b = jnp.broadcast_to(quota2, (tq, KEY_TILE))
        plain_b = jnp.logical_not(adj_b)
        rank_pass(lambda s: (plain_b & (s == thr_b)) | (offset(s) == thr2_b),
                  lambda s: (plain_b & (s > thr_b)) | (adj_b & (s >= next_b)) | (offset(s) > thr2_b),
                  jnp.where(adj_b, quota2_b, quota_b))

    qt_all = q_ref[0].astype(F32).T
    zeros_half = jnp.zeros((HEAD_DIM, GRP * tq), F32)
    qa = []
    for g in range(N_KV_HEADS):
        piece = jnp.concatenate(
            [qt_all[(g * GRP + j) * HEAD_DIM:(g * GRP + j + 1) * HEAD_DIM, :] for j in range(GRP)],
            axis=1)
        halves = [piece, zeros_half] if g == 0 else [zeros_half, piece]
        qa.append(jnp.concatenate(halves, axis=0).astype(BF16))

    def logits(slot, keys, pens, m_old, extra=None):
        pen = jnp.concatenate(pens, axis=0)
        pen = jnp.concatenate([pen] * GRP, axis=1)
        kblk = k_ref[0, keys, :]
        m_new = []
        for g in range(N_KV_HEADS):
            s = _dot(kblk, qa[g]) + pen
            if extra is not None:
                s = s + extra[g]
            sbuf[slot, g, 0:pen.shape[0], :] = s
            m_new.append(jnp.maximum(m_old[g], jnp.max(s, axis=0, keepdims=True)))
        return tuple(m_new)

    def accumulate(slot, keys, nk, m_old, m_new, accs):
        out = []
        for g in range(N_KV_HEADS):
            p = jnp.exp2(sbuf[slot, g, 0:nk, :] - m_new[g])
            out.append(jnp.exp2(m_old[g] - m_new[g]) * accs[g]
                       + _dot(vt_ref[0, g * VT_ROWS:(g + 1) * VT_ROWS, keys], p.astype(BF16)))
        return tuple(out)

    last_far = jnp.maximum(steps_far - 1, 0)

    def far_keys(j):
        return step_keys(jnp.minimum(j, last_far))

    def far_pens(j):
        return [jnp.where(j * BIG + k < n0, sc[jnp.minimum(j * BIG + k, n0)], NEG) for k in range(BIG)]

    m_init = tuple(jnp.full((1, GRP * tq), NEG, F32) for _ in range(N_KV_HEADS))
    acc_init = tuple(jnp.zeros((VT_ROWS, GRP * tq), F32) for _ in range(N_KV_HEADS))
    m_first = logits(0, far_keys(0), far_pens(0), m_init)

    def stage_pair(i, c):
        m_old, m_cur, accs = c
        a = 2 * i + 1
        m_a = logits(1, far_keys(a), far_pens(a), m_cur)
        accs = accumulate(0, far_keys(a - 1), kw, m_old, m_cur, accs)
        m_b = logits(0, far_keys(a + 1), far_pens(a + 1), m_a)
        accs = accumulate(1, far_keys(a), kw, m_cur, m_a, accs)
        return m_a, m_b, accs

    pairs = steps_far // 2
    m_old, m_cur, accs = lax.fori_loop(0, pairs, stage_pair, (m_init, m_first, acc_init))
    m_last = logits(1, near_keys, [sc[n0], sc[n0 + 1]], m_cur, extra=(bias[variant, 0], bias[variant, 1]))
    accs = accumulate(0, far_keys(2 * pairs), kw, m_old, m_cur, accs)
    accs = accumulate(1, near_keys, 2 * KEY_TILE, m_cur, m_last, accs)
    heads = []
    for g in range(N_KV_HEADS):
        acc = accs[g]
        og = acc[:HEAD_DIM, :] / acc[HEAD_DIM:HEAD_DIM + 1, :]
        heads += [og[:, j * tq:(j + 1) * tq] for j in range(GRP)]
    o_ref[0] = jnp.concatenate(heads, axis=0).T.astype(BF16)


def _dsa(rel_bias, q, qi, sgn, k, vt, ki, *, q_pos0, l_valid):
    b, t, _ = q.shape
    lk = k.shape[1]
    tq = KEY_TILE
    assert lk % (BIG * KEY_TILE) == 0 and q_pos0 % KEY_TILE == 0 and t % tq == 0
    assert q_pos0 + t <= lk
    nt = lk // KEY_TILE
    row = lambda width: pl.BlockSpec((1, tq, width), lambda bi, i: (bi, i, 0))
    return pl.pallas_call(
        functools.partial(_dsa_kernel, q_pos0=q_pos0, l_valid=l_valid),
        grid=(b, t // tq),
        in_specs=[pl.BlockSpec(memory_space=pltpu.SMEM),
                  row(D_ATT), row(N_IDX_HEADS * D_IDX),
                  pl.BlockSpec((1, 8, tq), lambda bi, i: (bi, 0, i)),
                  _resident((1, lk, N_KV_HEADS * HEAD_DIM), lambda bi, i: (bi, 0, 0)),
                  _resident((1, N_KV_HEADS * VT_ROWS, lk), lambda bi, i: (bi, 0, 0)),
                  _resident((1, lk, D_IDX), lambda bi, i: (bi, 0, 0))],
        out_specs=row(D_ATT),
        out_shape=jax.ShapeDtypeStruct((b, t, D_ATT), BF16),
        scratch_shapes=[pltpu.VMEM((nt, KEY_TILE, tq), F32),
                        pltpu.VMEM((2, N_KV_HEADS, 2 * KEY_TILE, GRP * tq), F32),
                        pltpu.VMEM((2, N_KV_HEADS, BIG * KEY_TILE, GRP * tq), F32)],
        compiler_params=pltpu.CompilerParams(
            dimension_semantics=("arbitrary", "arbitrary"), vmem_limit_bytes=VMEM_LIMIT),
        name="dsa",
    )(rel_bias, q, qi, sgn, k, vt, ki)


def _split3(x):
    hi = x.astype(BF16)
    r = x - hi.astype(F32)
    mid = r.astype(BF16)
    lo = (r - mid.astype(F32)).astype(BF16)
    return hi, mid, lo


def _ssd_kernel(xs_ref, bm_ref, cm_ref, dt_ref, z_ref, alog_ref, dsk_ref, nw_ref, h0_ref,
                y_ref, h_ref, ht, *, q_in):
    c = pl.program_id(1)
    nc = pl.num_programs(1)

    @pl.when(c == 0)
    def _():
        for hh in range(SSM_HEADS):
            ht[hh] = h0_ref[0, hh].T

    def rows(ref):
        x = ref[0]
        if q_in == SSD_Q:
            return x
        return jnp.concatenate([x, jnp.zeros((SSD_Q - q_in, x.shape[1]), x.dtype)], axis=0)

    xs, bm, cm, dt, z = rows(xs_ref), rows(bm_ref), rows(cm_ref), rows(dt_ref), rows(z_ref)
    a = -jnp.exp(alog_ref[...])
    da = dt * a
    ri = lax.broadcasted_iota(jnp.int32, (SSD_Q, SSD_Q), 0)
    ci = lax.broadcasted_iota(jnp.int32, (SSD_Q, SSD_Q), 1)
    causal = ri >= ci
    tril = jnp.where(causal, 1.0, 0.0).astype(BF16)
    hi, mid, lo = _split3(da)
    acum = _dot(tril, hi) + _dot(tril, mid) + _dot(tril, lo)
    acum_t = acum.T

    for g in range(SSM_GROUPS):
        bm_g = bm[:, g * SSM_STATE:(g + 1) * SSM_STATE]
        cm_g = cm[:, g * SSM_STATE:(g + 1) * SSM_STATE].astype(BF16)
        bm_t = bm_g.T.astype(BF16)
        cb = _dot(cm_g, bm_t)
        gs = []
        ssq = jnp.zeros((SSD_Q, 1), F32)
        for r in range(SSM_HPG):
            hh = g * SSM_HPG + r
            lanes = slice(hh * SSM_HEAD_DIM, (hh + 1) * SSM_HEAD_DIM)
            xh = xs[:, lanes]
            dl = DT_LANE0 + hh
            col = acum[:, dl:dl + 1]
            rowv = acum_t[dl:dl + 1, :]
            decay = jnp.exp(jnp.where(causal, col - rowv, NEG))
            xdt = xh * dt[:, dl:dl + 1]
            y = _dot((cb * decay).astype(BF16), xdt.astype(BF16))
            h_prev = ht[hh]
            y = y + jnp.exp(col) * _dot(cm_g, h_prev.astype(BF16))
            last = acum[SSD_Q - 1:SSD_Q, dl:dl + 1]
            xw = (xdt * jnp.exp(last - col)).astype(BF16)
            ht[hh] = h_prev * jnp.exp(last) + _dot(bm_t, xw)
            y = y + xh * dsk_ref[:, lanes]
            gate = y * _silu(z[:, lanes])
            ssq = ssq + jnp.sum(gate * gate, axis=1, keepdims=True)
            gs.append(gate)
        scale = lax.rsqrt(ssq / float(SSM_HPG * SSM_HEAD_DIM) + NORM_EPS)
        for r in range(SSM_HPG):
            hh = g * SSM_HPG + r
            lanes = slice(hh * SSM_HEAD_DIM, (hh + 1) * SSM_HEAD_DIM)
            y_ref[0, :, lanes] = (gs[r] * scale * nw_ref[:, lanes])[:q_in].astype(BF16)

    @pl.when(c == nc - 1)
    def _():
        for hh in range(SSM_HEADS):
            h_ref[0, hh] = ht[hh].T


def _ssd(xs, bm, cm, dt, z, alog, dsk, nw, h0, *, q_in):
    b, t, _ = xs.shape
    row = lambda width: pl.BlockSpec((1, q_in, width), lambda bi, c: (bi, c, 0))
    full = lambda a: _resident(a.shape, lambda bi, c: (0,) * a.ndim)
    hspec = pl.BlockSpec((1, SSM_HEADS, SSM_HEAD_DIM, SSM_STATE), lambda bi, c: (bi, 0, 0, 0))
    return pl.pallas_call(
        functools.partial(_ssd_kernel, q_in=q_in),
        grid=(b, t // q_in),
        in_specs=[row(D_SSM), row(256), row(256), row(LANE), row(D_SSM),
                  full(alog), full(dsk), full(nw), hspec],
        out_specs=(row(D_SSM), hspec),
        out_shape=(jax.ShapeDtypeStruct((b, t, D_SSM), BF16),
                   jax.ShapeDtypeStruct((b, SSM_HEADS, SSM_HEAD_DIM, SSM_STATE), F32)),
        scratch_shapes=[pltpu.VMEM((SSM_HEADS, SSM_STATE, SSM_HEAD_DIM), F32)],
        compiler_params=pltpu.CompilerParams(
            dimension_semantics=("arbitrary", "arbitrary"), vmem_limit_bytes=VMEM_LIMIT),
        name="ssd",
    )(xs, bm, cm, dt, z, alog, dsk, nw, h0)


def _mix_ffn_kernel(x_ref, ya_ref, yb_ref, yc_ref, wo_ref, npost_ref, nfpre_ref, wg_ref, wu_ref,
                    wd_ref, cfw_ref, cfb_ref, nfpost_ref, buff_ref, o_ref, buff_out, eg, *, tm):
    i = pl.program_id(1)

    @pl.when(i == 0)
    def _():
        eg[CARRY - 2:CARRY, :] = buff_ref[0]

    mix = _dot(jnp.concatenate([ya_ref[0], yb_ref[0], yc_ref[0]], axis=1), wo_ref[...])
    x1 = x_ref[0] + _rms(mix, npost_ref[...])
    u = _rms(x1, nfpre_ref[...]).astype(BF16)
    eg[CARRY:CARRY + tm, :] = _dot(u, wg_ref[...])
    gc = eg[CARRY - 2:CARRY - 2 + tm, :] * cfw_ref[0:1, :]
    for j in range(1, CONV_F_W):
        gc = gc + eg[CARRY - 2 + j:CARRY - 2 + j + tm, :] * cfw_ref[j:j + 1, :]
    hid = (_silu(gc + cfb_ref[...]) * _dot(u, wu_ref[...])).astype(BF16)
    o_ref[0] = x1 + _rms(_dot(hid, wd_ref[...]), nfpost_ref[...])
    tail = eg[CARRY + tm - 2:CARRY + tm, :]
    buff_out[0] = tail
    eg[CARRY - 2:CARRY, :] = tail


def _mix_ffn(x, ya, yb, yc, wo, npost, nfpre, wg, wu, wd, cfw, cfb, nfpost, buff, *, tm):
    b, t, _ = x.shape
    row = lambda width: pl.BlockSpec((1, tm, width), lambda bi, i: (bi, i, 0))
    full = lambda a: _resident(a.shape, lambda bi, i: (0,) * a.ndim)
    state = pl.BlockSpec((1, CONV_F_W - 1, D_FF), lambda bi, i: (bi, 0, 0))
    return pl.pallas_call(
        functools.partial(_mix_ffn_kernel, tm=tm),
        grid=(b, t // tm),
        in_specs=[row(D_MODEL), row(D_CONV_MIX), row(D_ATT), row(D_SSM), full(wo), full(npost),
                  full(nfpre), full(wg), full(wu), full(wd), full(cfw), full(cfb), full(nfpost),
                  state],
        out_specs=(row(D_MODEL), state),
        out_shape=(jax.ShapeDtypeStruct((b, t, D_MODEL), F32),
                   jax.ShapeDtypeStruct((b, CONV_F_W - 1, D_FF), F32)),
        scratch_shapes=[pltpu.VMEM((CARRY + tm, D_FF), F32)],
        compiler_params=pltpu.CompilerParams(
            dimension_semantics=("arbitrary", "arbitrary"), vmem_limit_bytes=VMEM_LIMIT),
        name="mix_ffn",
    )(x, ya, yb, yc, wo, npost, nfpre, wg, wu, wd, cfw, cfb, nfpost, buff)


def _pack_keys_kernel(ck_ref, cv_ref, cki_ref, k_ref, v_ref, ki_ref, kb_ref, vt_ref, kib_ref, *, past_steps):
    j = pl.program_id(1)
    rows = kb_ref.shape[1]
    cached = j < past_steps

    def with_new(cache, new):
        new = jnp.concatenate([new, jnp.zeros((rows - new.shape[0], new.shape[1]), F32)], axis=0)
        return jnp.where(cached, cache, new)

    heads = lambda ref: [ref[0, 0, g] for g in range(N_KV_HEADS)]
    k_cache = jnp.concatenate([h.T for h in heads(ck_ref)], axis=1)
    kb_ref[0] = with_new(k_cache, k_ref[0]).astype(BF16)
    v_new = jnp.concatenate([v_ref[0], jnp.zeros((rows - v_ref.shape[1], v_ref.shape[2]), F32)], axis=0)
    vt = jnp.where(cached, jnp.concatenate(heads(cv_ref), axis=0), v_new.T)
    vt_ref[0] = _augment_vt(vt).astype(BF16)
    kib_ref[0] = with_new(cki_ref[0, 0].T, ki_ref[0]).astype(BF16)


def _pack_keys(layer, cache_k, cache_v, cache_kidx, k, v, ki):
    b, past = cache_k.shape[1:3]
    t = k.shape[1]
    rows = BIG * KEY_TILE
    assert past % rows == 0 and t <= rows
    past_steps = past // rows
    lk = past + rows
    cache = lambda *major: pl.BlockSpec(
        (1, 1) + major + (rows,),
        lambda bi, j: (layer, bi) + (0,) * len(major) + (jnp.minimum(j, past_steps - 1),))
    cache_k, cache_v = (jnp.transpose(c, (0, 1, 3, 4, 2)) for c in (cache_k, cache_v))
    cache_kidx = jnp.transpose(cache_kidx, (0, 1, 3, 2))
    new = lambda width: pl.BlockSpec((1, t, width), lambda bi, j: (bi, 0, 0))
    out = lambda width: pl.BlockSpec((1, rows, width), lambda bi, j: (bi, j, 0))
    return pl.pallas_call(
        functools.partial(_pack_keys_kernel, past_steps=past_steps),
        grid=(b, past_steps + 1),
        in_specs=[cache(N_KV_HEADS, HEAD_DIM), cache(N_KV_HEADS, HEAD_DIM), cache(D_IDX),
                  new(128), new(128), new(D_IDX)],
        out_specs=(out(128), pl.BlockSpec((1, N_KV_HEADS * VT_ROWS, rows), lambda bi, j: (bi, 0, j)),
                   out(D_IDX)),
        out_shape=(jax.ShapeDtypeStruct((b, lk, 128), BF16),
                   jax.ShapeDtypeStruct((b, N_KV_HEADS * VT_ROWS, lk), BF16),
                   jax.ShapeDtypeStruct((b, lk, D_IDX), BF16)),
        compiler_params=pltpu.CompilerParams(
            dimension_semantics=("arbitrary", "arbitrary"), vmem_limit_bytes=VMEM_LIMIT),
        name="pack_keys",
    )(cache_k, cache_v, cache_kidx, k, v, ki)


def _layer(x, p, rel_bias, buf_a, buf_s, h0, buf_f, kv_past, *, tm):
    b, t, _ = x.shape
    outs = _in_proj(x, p['norm_mix_pre'], p['w_in'], p['conv_a_w'], p['conv_ssm_w'], p['conv_ssm_b'],
                    p['dt_bias'], buf_a, buf_s, tm=tm, key_side=kv_past is None)
    (ya, q, k, v, qi, ki, sgn, z, xs, bm, cm, dt, buf_a_new, buf_s_new) = outs[:14]
    if kv_past is None:
        kb, vt, kib = outs[14:]
        yb = _dsa(rel_bias, q, qi, sgn, kb, vt, kib, q_pos0=0, l_valid=t)
    else:
        layer, cache_k, cache_v, cache_kidx = kv_past
        past = cache_k.shape[2]
        qpad = lambda a: jnp.pad(a, ((0, 0), (0, -t % KEY_TILE), (0, 0)))
        kb, vt, kib = _pack_keys(layer, cache_k, cache_v, cache_kidx, k, v, ki)
        yb = _dsa(rel_bias, qpad(q), qpad(qi), sgn, kb, vt, kib, q_pos0=past, l_valid=past + t)[:, :t]
    yc, h_new = _ssd(xs, bm, cm, dt, z, p['a_log'], p['d_skip'], p['ssm_norm_w'],
                     h0.reshape(b, SSM_HEADS, SSM_HEAD_DIM, SSM_STATE), q_in=min(t, SSD_Q))
    x_new, buf_f_new = _mix_ffn(x, ya, yb, yc, p['w_out'], p['norm_mix_post'], p['norm_ffn_pre'],
                                p['w_gate'], p['w_up'], p['w_down'], p['conv_ffn_w'],
                                p['conv_ffn_b'], p['norm_ffn_post'], buf_f, tm=tm)
    st = (k.reshape(b, t, N_KV_HEADS, HEAD_DIM), v.reshape(b, t, N_KV_HEADS, HEAD_DIM), ki,
          buf_a_new, buf_s_new, h_new.reshape(b, SSM_GROUPS, SSM_HPG, SSM_HEAD_DIM, SSM_STATE),
          buf_f_new)
    return x_new, st


def _pack_w_in_kernel(wt_ref, o_ref):
    for c, c0 in enumerate(W_IN_TILE_COLS):
        o_ref[:, c * LANE:(c + 1) * LANE] = wt_ref[c0:c0 + LANE, :].T.astype(BF16)


def _pack_w_in(wt):
    return pl.pallas_call(
        _pack_w_in_kernel,
        out_shape=jax.ShapeDtypeStruct((wt.shape[1], D_IN_PACKED), BF16),
        compiler_params=pltpu.CompilerParams(vmem_limit_bytes=VMEM_LIMIT),
        name="pack_w_in",
    )(wt)


def _dt_lanes(v):
    return jnp.pad(v, (DT_LANE0, 0))[None, :]


def kernel(x_prompt, x_sample, cache_k, cache_v, cache_kidx, state_conv_a, state_conv_ssm, state_ssm, state_conv_ffn, rel_bias, norm_mix_pre, norm_mix_post, norm_ffn_pre, norm_ffn_post, w_in, conv_a_w, conv_ssm_w, conv_ssm_b, dt_bias, a_log, d_skip, ssm_norm_w, w_out, w_gate, w_up, conv_ffn_w, conv_ffn_b, w_down):
    depth = w_in.shape[0]
    bp = x_prompt.shape[0]
    yp, ys = x_prompt, x_sample
    outs_p, outs_s = [], []
    for l in range(depth):
        p = {'norm_mix_pre': norm_mix_pre[l][None], 'norm_mix_post': norm_mix_post[l][None],
             'norm_ffn_pre': norm_ffn_pre[l][None], 'norm_ffn_post': norm_ffn_post[l][None],
             'w_in': _pack_w_in(jnp.swapaxes(w_in[l], 0, 1)), 'conv_a_w': conv_a_w[l], 'conv_ssm_w': conv_ssm_w[l],
             'conv_ssm_b': conv_ssm_b[l][None], 'dt_bias': _dt_lanes(dt_bias[l]),
             'a_log': _dt_lanes(a_log[l]), 'd_skip': jnp.repeat(d_skip[l], SSM_HEAD_DIM)[None],
             'ssm_norm_w': ssm_norm_w[l][None], 'w_out': w_out[l].astype(BF16),
             'w_gate': w_gate[l].astype(BF16), 'w_up': w_up[l].astype(BF16),
             'conv_ffn_w': conv_ffn_w[l], 'conv_ffn_b': conv_ffn_b[l][None],
             'w_down': w_down[l].astype(BF16)}
        yp, st_p = _layer(
            yp, p, rel_bias,
            jnp.zeros((bp, CONV_A_W - 1, D_CONV_MIX), F32),
            jnp.zeros((bp, SSM_CONV_W - 1, D_XBC), F32),
            jnp.zeros((bp, SSM_GROUPS, SSM_HPG, SSM_HEAD_DIM, SSM_STATE), F32),
            jnp.zeros((bp, CONV_F_W - 1, D_FF), F32),
            None, tm=256)
        ys, st_s = _layer(
            ys, p, rel_bias, state_conv_a[l], state_conv_ssm[l], state_ssm[l], state_conv_ffn[l],
            (l, cache_k, cache_v, cache_kidx), tm=ys.shape[1])
        outs_p.append(st_p)
        outs_s.append(st_s)

    def stack(outs, i):
        return jnp.stack([o[i] for o in outs], axis=0)

    res = [yp, ys]
    for i in range(7):
        res.append(stack(outs_p, i))
        res.append(stack(outs_s, i))
    return tuple(res)
```

```python
import functools

import jax
import jax.numpy as jnp
from jax import lax
from jax.experimental import pallas as pl
from jax.experimental.pallas import tpu as pltpu

F32 = jnp.float32
BF16 = jnp.bfloat16

D_MODEL = 1024
CHUNK = 64
CHUNK_SHIFT = CHUNK.bit_length() - 1
assert 1 << CHUNK_SHIFT == CHUNK
D_CONV_MIX = 256
CONV_A_W = 3
N_HEADS = 6
N_KV_HEADS = 2
HEAD_DIM = 64
D_ATT = N_HEADS * HEAD_DIM
GRP = N_HEADS // N_KV_HEADS
N_IDX_HEADS = 4
D_IDX = 64
TOPK = 256
N_BUCKETS = 32
SSM_HEADS = 6
SSM_HEAD_DIM = 64
D_SSM = SSM_HEADS * SSM_HEAD_DIM
SSM_GROUPS = 2
SSM_HPG = SSM_HEADS // SSM_GROUPS
SSM_STATE = 128
SSM_CONV_W = 4
D_XBC = D_SSM + 2 * SSM_GROUPS * SSM_STATE
D_FF = 2816
CONV_F_W = 3
NORM_EPS = 1e-6

LANE = 128
CARRY = 8
SSD_Q = 128
KEY_TILE = 128
BIG = 4
VMEM_LIMIT = 56 * 1024 * 1024
NEG = -1e30
LOG2E = 1.4426950408889634
F32_MAX = 3.4028234663852886e38
F32_TINY = 1.1754943508222875e-38
NO_QUOTA = 1e9
MAX_SEARCH_STEPS = 256
BLIND_SEARCH_STEPS = 15

C_AB, C_AC, C_AH = 0, 256, 512
C_Q = 768
C_K = 1152
C_V = 1280
C_QI = 1408
C_KI = 1664
C_Z = 1792
C_XBC = 2176
C_DT = 3072
D_IN_PACKED = 3200
PROJ_GROUPS = (0, C_Q, C_Z, D_IN_PACKED)
D_IN = 3018
D_IN_HEAD = 1732
DT_LANE0 = LANE - SSM_HEADS
W_IN_TILE_COLS = ([c * LANE for c in range(C_Z // LANE)]
                  + [D_IN_HEAD + c * LANE for c in range((C_DT - C_Z) // LANE)] + [D_IN - LANE])


def _rms(x, w):
    return x * lax.rsqrt(jnp.mean(x * x, axis=-1, keepdims=True) + NORM_EPS) * w


def _dot(a, b):
    return jnp.dot(a, b, preferred_element_type=F32)


def _silu(x):
    return x * jax.nn.sigmoid(x)


VT_ROWS = HEAD_DIM + 16


def _augment_vt(vt):
    extra_shape = vt.shape[:-2] + (VT_ROWS - HEAD_DIM, vt.shape[-1])
    row = lax.broadcasted_iota(jnp.int32, extra_shape, len(extra_shape) - 2)
    extra = jnp.where(row == 0, 1.0, 0.0).astype(vt.dtype)
    parts = []
    for g in range(N_KV_HEADS):
        parts += [vt[..., g * HEAD_DIM:(g + 1) * HEAD_DIM, :], extra]
    return jnp.concatenate(parts, axis=-2)


def _resident(shape, index_map):
    return pl.BlockSpec(shape, index_map, pipeline_mode=pl.Buffered(1))


def _in_proj_kernel(x_ref, nw_ref, w_ref, caw_ref, csw_ref, csb_ref, dtb_ref, bufa_ref, bufs_ref,
                    ya_ref, q_ref, k_ref, v_ref, qi_ref, ki_ref, sgn_ref, z_ref, xs_ref, bm_ref,
                    cm_ref, dt_ref, bufa_out, bufs_out, *rest, tm, key_side):
    if key_side:
        kb_ref, vt_ref, kib_ref, ea, es = rest
    else:
        ea, es = rest
    i = pl.program_id(1)
    u = _rms(x_ref[0], nw_ref[...]).astype(BF16)

    slabs = {}

    def proj(c0, width):
        g0, g1 = next((a, b) for a, b in zip(PROJ_GROUPS, PROJ_GROUPS[1:]) if a <= c0 < b)
        if g0 not in slabs:
            slabs[g0] = _dot(u, w_ref[:, g0:g1])
        return slabs[g0][:, c0 - g0:c0 - g0 + width]

    @pl.when(i == 0)
    def _():
        ea[CARRY - 2:CARRY, :] = bufa_ref[0]
        es[CARRY - 3:CARRY, :] = bufs_ref[0]

    ea[CARRY:CARRY + tm, :] = proj(C_AC, D_CONV_MIX) * proj(C_AH, D_CONV_MIX)
    conv = ea[CARRY - 2:CARRY - 2 + tm, :] * caw_ref[0:1, :]
    for j in range(1, CONV_A_W):
        conv = conv + ea[CARRY - 2 + j:CARRY - 2 + j + tm, :] * caw_ref[j:j + 1, :]
    ya_ref[0] = (proj(C_AB, D_CONV_MIX) * conv).astype(BF16)
    tail_a = ea[CARRY + tm - 2:CARRY + tm, :]
    bufa_out[0] = tail_a
    ea[CARRY - 2:CARRY, :] = tail_a

    q_ref[0] = (proj(C_Q, D_ATT) * (HEAD_DIM ** -0.5 * LOG2E)).astype(BF16)
    k = proj(C_K, N_KV_HEADS * HEAD_DIM)
    v = proj(C_V, N_KV_HEADS * HEAD_DIM)
    kiw = proj(C_KI, LANE)
    k_ref[0] = k
    v_ref[0] = v
    ki_ref[0] = kiw[:, :D_IDX]
    wi = kiw[:, D_IDX:D_IDX + N_IDX_HEADS]
    kiw_rows = kiw if tm >= LANE else jnp.concatenate([kiw, jnp.zeros((LANE - tm, LANE), F32)], axis=0)
    sgn_ref[0] = jnp.where(kiw_rows.T[D_IDX:D_IDX + 8, :] >= 0.0, 1.0, -1.0).astype(F32)
    if key_side:
        kb_ref[0] = k.astype(BF16)
        vt_ref[0] = _augment_vt(v.T).astype(BF16)
        kib_ref[0] = kiw[:, :D_IDX].astype(BF16)
    wabs = jnp.abs(wi) * (D_IDX ** -0.5 * N_IDX_HEADS ** -0.5)
    qi = proj(C_QI, N_IDX_HEADS * D_IDX)
    for h in range(N_IDX_HEADS):
        qi_ref[0, :, h * D_IDX:(h + 1) * D_IDX] = (
            qi[:, h * D_IDX:(h + 1) * D_IDX] * wabs[:, h:h + 1]).astype(BF16)

    z_ref[0] = proj(C_Z, D_SSM)
    es[CARRY:CARRY + tm, :] = proj(C_XBC, D_XBC)
    xc = es[CARRY - 3:CARRY - 3 + tm, :] * csw_ref[0:1, :]
    for j in range(1, SSM_CONV_W):
        xc = xc + es[CARRY - 3 + j:CARRY - 3 + j + tm, :] * csw_ref[j:j + 1, :]
    xc = _silu(xc + csb_ref[...])
    xs_ref[0] = xc[:, :D_SSM]
    bm_ref[0] = xc[:, D_SSM:D_SSM + SSM_GROUPS * SSM_STATE]
    cm_ref[0] = xc[:, D_SSM + SSM_GROUPS * SSM_STATE:]
    tail_s = es[CARRY + tm - 3:CARRY + tm, :]
    bufs_out[0] = tail_s
    es[CARRY - 3:CARRY, :] = tail_s
    dtr = proj(C_DT, LANE) + dtb_ref[...]
    dt_ref[0] = jnp.maximum(dtr, 0.0) + jnp.log1p(jnp.exp(-jnp.abs(dtr)))


def _in_proj(x, nw, w_packed, caw, csw, csb, dtb, bufa, bufs, *, tm, key_side):
    b, t, _ = x.shape
    tq = max(tm, LANE)
    grid = (b, t // tm)
    row = lambda width: pl.BlockSpec((1, tm, width), lambda bi, i: (bi, i, 0))
    full = lambda a: _resident(a.shape, lambda bi, i: (0,) * a.ndim)
    state = lambda r, width: pl.BlockSpec((1, r, width), lambda bi, i: (bi, 0, 0))
    out_shapes = (
        jax.ShapeDtypeStruct((b, t, D_CONV_MIX), BF16),
        jax.ShapeDtypeStruct((b, t, D_ATT), BF16),
        jax.ShapeDtypeStruct((b, t, 128), F32),
        jax.ShapeDtypeStruct((b, t, 128), F32),
        jax.ShapeDtypeStruct((b, t, 256), BF16),
        jax.ShapeDtypeStruct((b, t, D_IDX), F32),
        jax.ShapeDtypeStruct((b, 8, t // tm * tq), F32),
        jax.ShapeDtypeStruct((b, t, D_SSM), F32),
        jax.ShapeDtypeStruct((b, t, D_SSM), F32),
        jax.ShapeDtypeStruct((b, t, 256), F32),
        jax.ShapeDtypeStruct((b, t, 256), F32),
        jax.ShapeDtypeStruct((b, t, LANE), F32),
        jax.ShapeDtypeStruct((b, CONV_A_W - 1, D_CONV_MIX), F32),
        jax.ShapeDtypeStruct((b, SSM_CONV_W - 1, D_XBC), F32),
    )
    out_specs = (row(D_CONV_MIX), row(D_ATT), row(128), row(128), row(256), row(D_IDX),
                 pl.BlockSpec((1, 8, tq), lambda bi, i: (bi, 0, i)),
                 row(D_SSM), row(D_SSM), row(256), row(256), row(LANE),
                 state(CONV_A_W - 1, D_CONV_MIX), state(SSM_CONV_W - 1, D_XBC))
    if key_side:
        out_shapes += (jax.ShapeDtypeStruct((b, t, 128), BF16),
                       jax.ShapeDtypeStruct((b, N_KV_HEADS * VT_ROWS, t), BF16),
                       jax.ShapeDtypeStruct((b, t, D_IDX), BF16))
        out_specs += (row(128), pl.BlockSpec((1, N_KV_HEADS * VT_ROWS, tm), lambda bi, i: (bi, 0, i)), row(D_IDX))
    in_specs = [row(D_MODEL), full(nw), full(w_packed), full(caw), full(csw), full(csb), full(dtb),
                state(CONV_A_W - 1, D_CONV_MIX), state(SSM_CONV_W - 1, D_XBC)]
    return pl.pallas_call(
        functools.partial(_in_proj_kernel, tm=tm, key_side=key_side),
        grid=grid, in_specs=in_specs, out_specs=out_specs, out_shape=out_shapes,
        scratch_shapes=[pltpu.VMEM((CARRY + tm, D_CONV_MIX), F32),
                        pltpu.VMEM((CARRY + tm, D_XBC), F32)],
        compiler_params=pltpu.CompilerParams(
            dimension_semantics=("arbitrary", "arbitrary"), vmem_limit_bytes=VMEM_LIMIT),
        name="in_proj",
    )(x, nw, w_packed, caw, csw, csb, dtb, bufa, bufs)


def _t5_bucket(rel):
    n = jnp.abs(rel)
    large = jnp.full(rel.shape, 8, jnp.int32)
    for brk in (12, 16, 23, 32, 46, 64, 91):
        large = large + jnp.where(n >= brk, 1, 0)
    return jnp.where(rel > 0, N_BUCKETS // 2, 0) + jnp.where(n < 8, n, large)


def _f32_key(x):
    bits = lax.bitcast_convert_type(x, jnp.int32)
    return bits ^ ((bits >> 31) & jnp.int32(0x7FFFFFFF))


def _key_f32(key):
    return lax.bitcast_convert_type(key ^ ((key >> 31) & jnp.int32(0x7FFFFFFF)), F32)


def _dsa_kernel(relb_ref, q_ref, qi_ref, sgn_ref, k_ref, vt_ref, ki_ref, o_ref, sc, bias, sbuf,
                *, q_pos0, l_valid):
    tq = KEY_TILE
    i = pl.program_id(1)
    q0 = q_pos0 + i * tq
    qt = q0 // KEY_TILE
    n0 = jnp.maximum(qt - 1, 0)
    variant = jnp.minimum(qt, 1)
    steps_far = (n0 + BIG - 1) // BIG
    steps_all = (n0 + 2 + BIG - 1) // BIG
    kw = BIG * KEY_TILE
    kf = float(TOPK)

    def step_keys(j):
        return pl.ds(pl.multiple_of(j * kw, kw), kw)

    near_keys = pl.ds(pl.multiple_of(n0 * KEY_TILE, KEY_TILE), 2 * KEY_TILE)

    @pl.when(i == 0)
    def _():
        sc[...] = jnp.full(sc.shape, -jnp.inf, F32)

    @pl.when((i == 0) & (pl.program_id(0) == 0))
    def _():
        kk = lax.broadcasted_iota(jnp.int32, (2 * KEY_TILE, tq), 0)
        r = lax.broadcasted_iota(jnp.int32, (2 * KEY_TILE, tq), 1)
        for var in range(2):
            bucket = _t5_bucket(kk - var * KEY_TILE - r)
            for h in range(N_HEADS):
                tab = jnp.zeros((2 * KEY_TILE, tq), F32)
                for bk in range(N_BUCKETS):
                    tab = jnp.where(bucket == bk, relb_ref[bk, h], tab)
                g, j = divmod(h, GRP)
                bias[var, g, :, j * tq:(j + 1) * tq] = (tab - relb_ref[N_BUCKETS // 2 - 1, h]) * LOG2E

    qit = qi_ref[0].astype(F32).T
    qis = jnp.concatenate([qit[h * D_IDX:(h + 1) * D_IDX, :] for h in range(N_IDX_HEADS)],
                          axis=1).astype(BF16)
    sgn = sgn_ref[0]

    def scores(keys):
        s = jnp.maximum(_dot(ki_ref[0, keys, :], qis), 0.0)
        tot = s[:, 0:tq] * sgn[0:1, :]
        for h in range(1, N_IDX_HEADS):
            tot = tot + s[:, h * tq:(h + 1) * tq] * sgn[h:h + 1, :]
        return tot

    zero = jnp.zeros((KEY_TILE, tq), F32)

    tally_init = (jnp.zeros((KEY_TILE, tq), BF16), jnp.zeros((KEY_TILE, tq), BF16),
                  jnp.full((KEY_TILE, tq), float(jnp.finfo(BF16).min), BF16))
    one16, nil16 = jnp.ones((KEY_TILE, tq), BF16), jnp.zeros((KEY_TILE, tq), BF16)

    def tally(c, s):
        ge, gt, mx = c
        s16 = s.astype(BF16)
        return (ge + jnp.where(s16 >= 0.0, one16, nil16),
                gt + jnp.where(s16 >= jnp.asarray(F32_TINY, BF16), one16, nil16), jnp.maximum(mx, s16))

    def tally_totals(c):
        ge, gt, mx = c
        return (col_sum(ge.astype(F32)), col_sum(gt.astype(F32)),
                jnp.max(mx.astype(F32), axis=0, keepdims=True))

    def score_far(j, c):
        s = scores(step_keys(j))
        for k in range(BIG):
            t = j * BIG + k
            tile = jnp.where(t < n0, s[k * KEY_TILE:(k + 1) * KEY_TILE, :], -jnp.inf)
            sc[t] = tile
            c = tally(c, tile)
        return c

    far_pairs = steps_far // 2
    c = lax.fori_loop(0, far_pairs, lambda j, c: score_far(2 * j + 1, score_far(2 * j, c)),
                      tally_init)
    c = lax.fori_loop(2 * far_pairs, steps_far, score_far, c)
    s_near = scores(near_keys)
    kpos = n0 * KEY_TILE + lax.broadcasted_iota(jnp.int32, (2 * KEY_TILE, tq), 0)
    qpos = q0 + lax.broadcasted_iota(jnp.int32, (2 * KEY_TILE, tq), 1)
    adm = (kpos < l_valid) & ((kpos >> CHUNK_SHIFT) <= (qpos >> CHUNK_SHIFT))
    s_near = jnp.where(adm, s_near, -jnp.inf)
    sc[n0] = s_near[:KEY_TILE, :]
    sc[n0 + 1] = s_near[KEY_TILE:, :]
    for k in range(BIG - 1):
        @pl.when(n0 + 2 + k < steps_all * BIG)
        def _():
            sc[n0 + 2 + k] = jnp.full((KEY_TILE, tq), -jnp.inf, F32)
    score_tally = tally(tally(c, s_near[:KEY_TILE, :]), s_near[KEY_TILE:, :])


    def col_sum(acc):
        parts = [acc[r:r + 8, :] for r in range(0, KEY_TILE, 8)]
        while len(parts) > 1:
            parts = [parts[r] + parts[r + 1] for r in range(0, len(parts), 2)]
        return jnp.sum(parts[0], axis=0, keepdims=True)

    def counter(value_at):
        def count_ge(t):
            tb = jnp.broadcast_to(t, (KEY_TILE, tq))

            def body(j, acc):
                for k in range(BIG):
                    acc = acc + jnp.where(value_at(j * BIG + k) >= tb, 1.0, 0.0)
                return acc
            return col_sum(lax.fori_loop(0, steps_all, body, zero))
        return count_ge

    def search(count_ge, kq, nfin, ge0, gt0, rmax, active, blind_steps):
        small = (nfin <= kq) | (active == 0.0)
        tie0 = (~small) & (gt0 < kq) & (ge0 >= kq)
        pos = (~small) & (gt0 >= kq)
        neg = (~small) & (ge0 < kq)
        log_kq = jnp.log(kq)
        lo0 = jnp.where(pos, F32_TINY, -F32_MAX)
        flo0 = jnp.where(pos, gt0, nfin)
        hi0 = jnp.where(neg, 0.0, jnp.inf)
        fhi0 = jnp.where(neg, ge0, 0.0)
        done0 = jnp.where(small | tie0 | (pos & (gt0 == kq)), 1.0, 0.0)
        thr0 = jnp.where(small, -F32_MAX, jnp.where(tie0, 0.0, F32_TINY))
        quota0 = jnp.where(tie0, kq - gt0, NO_QUOTA)

        def step(it, state):
            lo, flo, hi, fhi, done, thr, quota, adjacent = state
            klo, khi = _f32_key(lo), _f32_key(hi)
            la = jnp.log(flo)
            frac = (la - log_kq) / (la - jnp.log(jnp.maximum(fhi, 0.5)))
            interp = lo + (hi - lo) * jnp.where(it % 4 == 1, 0.5, frac)
            t_up = lo + (rmax - lo) * 0.5
            t_dn = hi - jnp.maximum(jnp.maximum(jnp.abs(rmax - hi), jnp.abs(hi)), F32_TINY)
            open_hi = hi == jnp.inf
            open_lo = lo <= -F32_MAX
            t = jnp.where(open_hi, t_up, jnp.where(open_lo, t_dn, interp))
            tk = jnp.minimum(jnp.maximum(_f32_key(t), klo + 1), khi - 1)
            tk = jnp.where(it % 4 == 3, klo + lax.shift_right_logical(khi - klo, 1), tk)
            t = _key_f32(tk)
            f = count_ge(t)
            act = done == 0.0
            hit = act & (f == kq)
            up = act & (f > kq)
            dn = act & (f < kq)
            lo = jnp.where(up, t, lo)
            flo = jnp.where(up, f, flo)
            hi = jnp.where(dn, t, hi)
            fhi = jnp.where(dn, f, fhi)
            adj = act & (~hit) & (_f32_key(lo) + 1 == _f32_key(hi))
            thr = jnp.where(hit, t, jnp.where(adj, lo, thr))
            quota = jnp.where(adj, kq - fhi, quota)
            done = jnp.where(hit | adj, 1.0, done)
            return lo, flo, hi, fhi, done, thr, quota, jnp.where(adj, 1.0, adjacent)

        state = (lo0, flo0, hi0, fhi0, done0, thr0, quota0, jnp.zeros((1, tq), F32))
        if blind_steps:
            state = lax.fori_loop(0, blind_steps, step, state)

        def cond(c):
            return (c[0] < MAX_SEARCH_STEPS) & (c[1] > 0)

        def body(c):
            state = step(c[0], c[2])
            return c[0] + 1, jnp.sum(1.0 - state[4]), state

        _, _, state = lax.while_loop(cond, body, (jnp.int32(blind_steps), jnp.sum(1.0 - state[4]), state))
        _, flo, _, fhi, _, thr, quota, adjacent = state
        return thr, quota, adjacent, flo, fhi

    qrow = q0 + lax.broadcasted_iota(jnp.int32, (1, tq), 1)
    nadm = jnp.minimum(((qrow >> CHUNK_SHIFT) + 1) * CHUNK, l_valid).astype(F32)
    thr, quota, adjacent, flo, fhi = search(
        counter(lambda t: sc[t]), jnp.full((1, tq), kf, F32), nadm,
        *tally_totals(score_tally), jnp.ones((1, tq), F32), BLIND_SEARCH_STEPS)
    thr_b = jnp.broadcast_to(thr, (tq, KEY_TILE))
    quota_b = jnp.broadcast_to(quota, (tq, KEY_TILE))
    has_ties = jnp.min(quota) < NO_QUOTA
    any_adjacent = jnp.max(adjacent) > 0.0

    @pl.when(jnp.logical_not(has_ties))
    def _():
        def body(j, c):
            for k in range(BIG):
                t = j * BIG + k
                sc[t] = jnp.where(sc[t] >= thr_b, 0.0, NEG)
            return c
        lax.fori_loop(0, steps_all, body, 0)

    def rank_pass(is_tie, is_sure, quota_of_ties):
        ri = lax.broadcasted_iota(jnp.int32, (KEY_TILE, KEY_TILE), 0)
        ci = lax.broadcasted_iota(jnp.int32, (KEY_TILE, KEY_TILE), 1)
        lower = jnp.where(ri >= ci, 1.0, 0.0).astype(BF16)

        def rank_tiles(first, n, seen):
            for k in range(n):
                t = first + k
                s = sc[t]
                tie = jnp.where(is_tie(s), 1.0, 0.0)
                rank = seen + _dot(lower, tie.astype(BF16))
                keep = is_sure(s) | ((tie > 0.0) & (rank <= quota_of_ties))
                sc[t] = jnp.where(keep, 0.0, NEG)
                seen = seen + jnp.sum(tie, axis=0, keepdims=True)
            return seen

        pairs = steps_all // 2
        seen = lax.fori_loop(0, pairs, lambda j, c: rank_tiles(j * 2 * BIG, 2 * BIG, c),
                             jnp.zeros((1, tq), F32))
        lax.fori_loop(2 * pairs, steps_all, lambda j, c: rank_tiles(j * BIG, BIG, c), seen)

    @pl.when(has_ties & jnp.logical_not(any_adjacent))
    def _():
        rank_pass(lambda s: s == thr_b, lambda s: s > thr_b, quota_b)

    @pl.when(any_adjacent)
    def _():
        adj_b = jnp.broadcast_to(adjacent, (tq, KEY_TILE)) > 0.0
        next_b = jnp.broadcast_to(_key_f32(_f32_key(thr) + 1), (tq, KEY_TILE))

        def in_cell(s):
            return adj_b & (s >= thr_b) & (s < next_b)

        def offset(s):
            return jnp.where(in_cell(s), s - thr_b, -jnp.inf)

        def tally_offsets(j, c):
            for k in range(BIG):
                c = tally(c, offset(sc[j * BIG + k]))
            return c

        in_cells, above_thr, max_offset = tally_totals(lax.fori_loop(0, steps_all, tally_offsets, tally_init))
        thr2, quota2, _, _, _ = search(
            counter(lambda t: offset(sc[t])), jnp.where(adjacent > 0.0, quota, 1.0), in_cells, in_cells,
            above_thr, max_offset, adjacent, 0)
        thr2_b = jnp.broadcast_to(thr2, (tq, KEY_TILE))
        quota2_b = jnp.broadcast_to(quota2, (tq, KEY_TILE))
        plain_b = jnp.logical_not(adj_b)
        rank_pass(lambda s: (plain_b & (s == thr_b)) | (offset(s) == thr2_b),
                  lambda s: (plain_b & (s > thr_b)) | (adj_b & (s >= next_b)) | (offset(s) > thr2_b),
                  jnp.where(adj_b, quota2_b, quota_b))

    qt_all = q_ref[0].astype(F32).T
    zeros_half = jnp.zeros((HEAD_DIM, GRP * tq), F32)
    qa = []
    for g in range(N_KV_HEADS):
        piece = jnp.concatenate(
            [qt_all[(g * GRP + j) * HEAD_DIM:(g * GRP + j + 1) * HEAD_DIM, :] for j in range(GRP)],
            axis=1)
        halves = [piece, zeros_half] if g == 0 else [zeros_half, piece]
        qa.append(jnp.concatenate(halves, axis=0).astype(BF16))

    def logits(slot, keys, pens, m_old, extra=None):
        pen = jnp.concatenate(pens, axis=0)
        pen = jnp.concatenate([pen] * GRP, axis=1)
        kblk = k_ref[0, keys, :]
        m_new = []
        for g in range(N_KV_HEADS):
            s = _dot(kblk, qa[g]) + pen
            if extra is not None:
                s = s + extra[g]
            sbuf[slot, g, 0:pen.shape[0], :] = s
            m_new.append(jnp.maximum(m_old[g], jnp.max(s, axis=0, keepdims=True)))
        return tuple(m_new)

    def accumulate(slot, keys, nk, m_old, m_new, accs):
        out = []
        for g in range(N_KV_HEADS):
            p = jnp.exp2(sbuf[slot, g, 0:nk, :] - m_new[g])
            out.append(jnp.exp2(m_old[g] - m_new[g]) * accs[g]
                       + _dot(vt_ref[0, g * VT_ROWS:(g + 1) * VT_ROWS, keys], p.astype(BF16)))
        return tuple(out)

    last_far = jnp.maximum(steps_far - 1, 0)

    def far_keys(j):
        return step_keys(jnp.minimum(j, last_far))

    def far_pens(j):
        return [jnp.where(j * BIG + k < n0, sc[jnp.minimum(j * BIG + k, n0)], NEG) for k in range(BIG)]

    m_init = tuple(jnp.full((1, GRP * tq), NEG, F32) for _ in range(N_KV_HEADS))
    acc_init = tuple(jnp.zeros((VT_ROWS, GRP * tq), F32) for _ in range(N_KV_HEADS))
    m_first = logits(0, far_keys(0), far_pens(0), m_init)

    def stage_pair(i, c):
        m_old, m_cur, accs = c
        a = 2 * i + 1
        m_a = logits(1, far_keys(a), far_pens(a), m_cur)
        accs = accumulate(0, far_keys(a - 1), kw, m_old, m_cur, accs)
        m_b = logits(0, far_keys(a + 1), far_pens(a + 1), m_a)
        accs = accumulate(1, far_keys(a), kw, m_cur, m_a, accs)
        return m_a, m_b, accs

    pairs = steps_far // 2
    m_old, m_cur, accs = lax.fori_loop(0, pairs, stage_pair, (m_init, m_first, acc_init))
    m_last = logits(1, near_keys, [sc[n0], sc[n0 + 1]], m_cur, extra=(bias[variant, 0], bias[variant, 1]))
    accs = accumulate(0, far_keys(2 * pairs), kw, m_old, m_cur, accs)
    accs = accumulate(1, near_keys, 2 * KEY_TILE, m_cur, m_last, accs)
    heads = []
    for g in range(N_KV_HEADS):
        acc = accs[g]
        og = acc[:HEAD_DIM, :] / acc[HEAD_DIM:HEAD_DIM + 1, :]
        heads += [og[:, j * tq:(j + 1) * tq] for j in range(GRP)]
    o_ref[0] = jnp.concatenate(heads, axis=0).T.astype(BF16)


def _dsa(rel_bias, q, qi, sgn, k, vt, ki, *, q_pos0, l_valid):
    b, t, _ = q.shape
    lk = k.shape[1]
    tq = KEY_TILE
    assert lk % (BIG * KEY_TILE) == 0 and q_pos0 % KEY_TILE == 0 and t % tq == 0
    assert q_pos0 + t <= lk
    nt = lk // KEY_TILE
    row = lambda width: pl.BlockSpec((1, tq, width), lambda bi, i: (bi, i, 0))
    return pl.pallas_call(
        functools.partial(_dsa_kernel, q_pos0=q_pos0, l_valid=l_valid),
        grid=(b, t // tq),
        in_specs=[pl.BlockSpec(memory_space=pltpu.SMEM),
                  row(D_ATT), row(N_IDX_HEADS * D_IDX),
                  pl.BlockSpec((1, 8, tq), lambda bi, i: (bi, 0, i)),
                  _resident((1, lk, N_KV_HEADS * HEAD_DIM), lambda bi, i: (bi, 0, 0)),
                  _resident((1, N_KV_HEADS * VT_ROWS, lk), lambda bi, i: (bi, 0, 0)),
                  _resident((1, lk, D_IDX), lambda bi, i: (bi, 0, 0))],
        out_specs=row(D_ATT),
        out_shape=jax.ShapeDtypeStruct((b, t, D_ATT), BF16),
        scratch_shapes=[pltpu.VMEM((nt, KEY_TILE, tq), F32),
                        pltpu.VMEM((2, N_KV_HEADS, 2 * KEY_TILE, GRP * tq), F32),
                        pltpu.VMEM((2, N_KV_HEADS, BIG * KEY_TILE, GRP * tq), F32)],
        compiler_params=pltpu.CompilerParams(
            dimension_semantics=("arbitrary", "arbitrary"), vmem_limit_bytes=VMEM_LIMIT),
        name="dsa",
    )(rel_bias, q, qi, sgn, k, vt, ki)


def _split3(x):
    hi = x.astype(BF16)
    r = x - hi.astype(F32)
    mid = r.astype(BF16)
    lo = (r - mid.astype(F32)).astype(BF16)
    return hi, mid, lo


def _ssd_kernel(xs_ref, bm_ref, cm_ref, dt_ref, z_ref, alog_ref, dsk_ref, nw_ref, h0_ref,
                y_ref, h_ref, ht, *, q_in):
    c = pl.program_id(1)
    nc = pl.num_programs(1)

    @pl.when(c == 0)
    def _():
        for hh in range(SSM_HEADS):
            ht[hh] = h0_ref[0, hh].T

    def rows(ref):
        x = ref[0]
        if q_in == SSD_Q:
            return x
        return jnp.concatenate([x, jnp.zeros((SSD_Q - q_in, x.shape[1]), x.dtype)], axis=0)

    xs, bm, cm, dt, z = rows(xs_ref), rows(bm_ref), rows(cm_ref), rows(dt_ref), rows(z_ref)
    a = -jnp.exp(alog_ref[...])
    da = dt * a
    ri = lax.broadcasted_iota(jnp.int32, (SSD_Q, SSD_Q), 0)
    ci = lax.broadcasted_iota(jnp.int32, (SSD_Q, SSD_Q), 1)
    causal = ri >= ci
    tril = jnp.where(causal, 1.0, 0.0).astype(BF16)
    hi, mid, lo = _split3(da)
    acum = _dot(tril, hi) + _dot(tril, mid) + _dot(tril, lo)
    acum_t = acum.T

    for g in range(SSM_GROUPS):
        bm_g = bm[:, g * SSM_STATE:(g + 1) * SSM_STATE]
        cm_g = cm[:, g * SSM_STATE:(g + 1) * SSM_STATE].astype(BF16)
        bm_t = bm_g.T.astype(BF16)
        cb = _dot(cm_g, bm_t)
        gs = []
        ssq = jnp.zeros((SSD_Q, 1), F32)
        for r in range(SSM_HPG):
            hh = g * SSM_HPG + r
            lanes = slice(hh * SSM_HEAD_DIM, (hh + 1) * SSM_HEAD_DIM)
            xh = xs[:, lanes]
            dl = DT_LANE0 + hh
            col = acum[:, dl:dl + 1]
            rowv = acum_t[dl:dl + 1, :]
            decay = jnp.exp(jnp.where(causal, col - rowv, NEG))
            xdt = xh * dt[:, dl:dl + 1]
            y = _dot((cb * decay).astype(BF16), xdt.astype(BF16))
            h_prev = ht[hh]
            y = y + jnp.exp(col) * _dot(cm_g, h_prev.astype(BF16))
            last = acum[SSD_Q - 1:SSD_Q, dl:dl + 1]
            xw = (xdt * jnp.exp(last - col)).astype(BF16)
            ht[hh] = h_prev * jnp.exp(last) + _dot(bm_t, xw)
            y = y + xh * dsk_ref[:, lanes]
            gate = y * _silu(z[:, lanes])
            ssq = ssq + jnp.sum(gate * gate, axis=1, keepdims=True)
            gs.append(gate)
        scale = lax.rsqrt(ssq / float(SSM_HPG * SSM_HEAD_DIM) + NORM_EPS)
        for r in range(SSM_HPG):
            hh = g * SSM_HPG + r
            lanes = slice(hh * SSM_HEAD_DIM, (hh + 1) * SSM_HEAD_DIM)
            y_ref[0, :, lanes] = (gs[r] * scale * nw_ref[:, lanes])[:q_in].astype(BF16)

    @pl.when(c == nc - 1)
    def _():
        for hh in range(SSM_HEADS):
            h_ref[0, hh] = ht[hh].T


def _ssd(xs, bm, cm, dt, z, alog, dsk, nw, h0, *, q_in):
    b, t, _ = xs.shape
    row = lambda width: pl.BlockSpec((1, q_in, width), lambda bi, c: (bi, c, 0))
    full = lambda a: _resident(a.shape, lambda bi, c: (0,) * a.ndim)
    hspec = pl.BlockSpec((1, SSM_HEADS, SSM_HEAD_DIM, SSM_STATE), lambda bi, c: (bi, 0, 0, 0))
    return pl.pallas_call(
        functools.partial(_ssd_kernel, q_in=q_in),
        grid=(b, t // q_in),
        in_specs=[row(D_SSM), row(256), row(256), row(LANE), row(D_SSM),
                  full(alog), full(dsk), full(nw), hspec],
        out_specs=(row(D_SSM), hspec),
        out_shape=(jax.ShapeDtypeStruct((b, t, D_SSM), BF16),
                   jax.ShapeDtypeStruct((b, SSM_HEADS, SSM_HEAD_DIM, SSM_STATE), F32)),
        scratch_shapes=[pltpu.VMEM((SSM_HEADS, SSM_STATE, SSM_HEAD_DIM), F32)],
        compiler_params=pltpu.CompilerParams(
            dimension_semantics=("arbitrary", "arbitrary"), vmem_limit_bytes=VMEM_LIMIT),
        name="ssd",
    )(xs, bm, cm, dt, z, alog, dsk, nw, h0)


def _mix_ffn_kernel(x_ref, ya_ref, yb_ref, yc_ref, wo_ref, npost_ref, nfpre_ref, wg_ref, wu_ref,
                    wd_ref, cfw_ref, cfb_ref, nfpost_ref, buff_ref, o_ref, buff_out, eg, *, tm):
    i = pl.program_id(1)

    @pl.when(i == 0)
    def _():
        eg[CARRY - 2:CARRY, :] = buff_ref[0]

    mix = _dot(jnp.concatenate([ya_ref[0], yb_ref[0], yc_ref[0]], axis=1), wo_ref[...])
    x1 = x_ref[0] + _rms(mix, npost_ref[...])
    u = _rms(x1, nfpre_ref[...]).astype(BF16)
    eg[CARRY:CARRY + tm, :] = _dot(u, wg_ref[...])
    gc = eg[CARRY - 2:CARRY - 2 + tm, :] * cfw_ref[0:1, :]
    for j in range(1, CONV_F_W):
        gc = gc + eg[CARRY - 2 + j:CARRY - 2 + j + tm, :] * cfw_ref[j:j + 1, :]
    hid = (_silu(gc + cfb_ref[...]) * _dot(u, wu_ref[...])).astype(BF16)
    o_ref[0] = x1 + _rms(_dot(hid, wd_ref[...]), nfpost_ref[...])
    tail = eg[CARRY + tm - 2:CARRY + tm, :]
    buff_out[0] = tail
    eg[CARRY - 2:CARRY, :] = tail


def _mix_ffn(x, ya, yb, yc, wo, npost, nfpre, wg, wu, wd, cfw, cfb, nfpost, buff, *, tm):
    b, t, _ = x.shape
    row = lambda width: pl.BlockSpec((1, tm, width), lambda bi, i: (bi, i, 0))
    full = lambda a: _resident(a.shape, lambda bi, i: (0,) * a.ndim)
    state = pl.BlockSpec((1, CONV_F_W - 1, D_FF), lambda bi, i: (bi, 0, 0))
    return pl.pallas_call(
        functools.partial(_mix_ffn_kernel, tm=tm),
        grid=(b, t // tm),
        in_specs=[row(D_MODEL), row(D_CONV_MIX), row(D_ATT), row(D_SSM), full(wo), full(npost),
                  full(nfpre), full(wg), full(wu), full(wd), full(cfw), full(cfb), full(nfpost),
                  state],
        out_specs=(row(D_MODEL), state),
        out_shape=(jax.ShapeDtypeStruct((b, t, D_MODEL), F32),
                   jax.ShapeDtypeStruct((b, CONV_F_W - 1, D_FF), F32)),
        scratch_shapes=[pltpu.VMEM((CARRY + tm, D_FF), F32)],
        compiler_params=pltpu.CompilerParams(
            dimension_semantics=("arbitrary", "arbitrary"), vmem_limit_bytes=VMEM_LIMIT),
        name="mix_ffn",
    )(x, ya, yb, yc, wo, npost, nfpre, wg, wu, wd, cfw, cfb, nfpost, buff)


def _pack_keys_kernel(ck_ref, cv_ref, cki_ref, k_ref, v_ref, ki_ref, kb_ref, vt_ref, kib_ref, *, past_steps):
    j = pl.program_id(1)
    rows = kb_ref.shape[1]
    cached = j < past_steps

    def with_new(cache, new):
        new = jnp.concatenate([new, jnp.zeros((rows - new.shape[0], new.shape[1]), F32)], axis=0)
        return jnp.where(cached, cache, new)

    heads = lambda ref: [ref[0, 0, g] for g in range(N_KV_HEADS)]
    k_cache = jnp.concatenate([h.T for h in heads(ck_ref)], axis=1)
    kb_ref[0] = with_new(k_cache, k_ref[0]).astype(BF16)
    v_new = jnp.concatenate([v_ref[0], jnp.zeros((rows - v_ref.shape[1], v_ref.shape[2]), F32)], axis=0)
    vt = jnp.where(cached, jnp.concatenate(heads(cv_ref), axis=0), v_new.T)
    vt_ref[0] = _augment_vt(vt).astype(BF16)
    kib_ref[0] = with_new(cki_ref[0, 0].T, ki_ref[0]).astype(BF16)


def _pack_keys(layer, cache_k, cache_v, cache_kidx, k, v, ki):
    b, past = cache_k.shape[1:3]
    t = k.shape[1]
    rows = 2 * BIG * KEY_TILE
    assert past % rows == 0 and t <= rows
    past_steps = past // rows
    lk = past + rows
    cache = lambda *major: pl.BlockSpec(
        (1, 1) + major + (rows,),
        lambda bi, j: (layer, bi) + (0,) * len(major) + (jnp.minimum(j, past_steps - 1),))
    cache_k, cache_v = (jnp.transpose(c, (0, 1, 3, 4, 2)) for c in (cache_k, cache_v))
    cache_kidx = jnp.transpose(cache_kidx, (0, 1, 3, 2))
    new = lambda width: pl.BlockSpec((1, t, width), lambda bi, j: (bi, 0, 0))
    out = lambda width: pl.BlockSpec((1, rows, width), lambda bi, j: (bi, j, 0))
    return pl.pallas_call(
        functools.partial(_pack_keys_kernel, past_steps=past_steps),
        grid=(b, past_steps + 1),
        in_specs=[cache(N_KV_HEADS, HEAD_DIM), cache(N_KV_HEADS, HEAD_DIM), cache(D_IDX),
                  new(128), new(128), new(D_IDX)],
        out_specs=(out(128), pl.BlockSpec((1, N_KV_HEADS * VT_ROWS, rows), lambda bi, j: (bi, 0, j)),
                   out(D_IDX)),
        out_shape=(jax.ShapeDtypeStruct((b, lk, 128), BF16),
                   jax.ShapeDtypeStruct((b, N_KV_HEADS * VT_ROWS, lk), BF16),
                   jax.ShapeDtypeStruct((b, lk, D_IDX), BF16)),
        compiler_params=pltpu.CompilerParams(
            dimension_semantics=("arbitrary", "arbitrary"), vmem_limit_bytes=VMEM_LIMIT),
        name="pack_keys",
    )(cache_k, cache_v, cache_kidx, k, v, ki)


def _layer(x, p, rel_bias, buf_a, buf_s, h0, buf_f, kv_past, *, tm, tm_ffn):
    b, t, _ = x.shape
    outs = _in_proj(x, p['norm_mix_pre'], p['w_in'], p['conv_a_w'], p['conv_ssm_w'], p['conv_ssm_b'],
                    p['dt_bias'], buf_a, buf_s, tm=tm, key_side=kv_past is None)
    (ya, q, k, v, qi, ki, sgn, z, xs, bm, cm, dt, buf_a_new, buf_s_new) = outs[:14]
    if kv_past is None:
        kb, vt, kib = outs[14:]
        yb = _dsa(rel_bias, q, qi, sgn, kb, vt, kib, q_pos0=0, l_valid=t)
    else:
        layer, cache_k, cache_v, cache_kidx = kv_past
        past = cache_k.shape[2]
        qpad = lambda a: jnp.pad(a, ((0, 0), (0, -t % KEY_TILE), (0, 0)))
        kb, vt, kib = _pack_keys(layer, cache_k, cache_v, cache_kidx, k, v, ki)
        yb = _dsa(rel_bias, qpad(q), qpad(qi), sgn, kb, vt, kib, q_pos0=past, l_valid=past + t)[:, :t]
    yc, h_new = _ssd(xs, bm, cm, dt, z, p['a_log'], p['d_skip'], p['ssm_norm_w'],
                     h0.reshape(b, SSM_HEADS, SSM_HEAD_DIM, SSM_STATE), q_in=min(t, SSD_Q))
    x_new, buf_f_new = _mix_ffn(x, ya, yb, yc, p['w_out'], p['norm_mix_post'], p['norm_ffn_pre'],
                                p['w_gate'], p['w_up'], p['w_down'], p['conv_ffn_w'],
                                p['conv_ffn_b'], p['norm_ffn_post'], buf_f, tm=tm_ffn)
    st = (k.reshape(b, t, N_KV_HEADS, HEAD_DIM), v.reshape(b, t, N_KV_HEADS, HEAD_DIM), ki,
          buf_a_new, buf_s_new, h_new.reshape(b, SSM_GROUPS, SSM_HPG, SSM_HEAD_DIM, SSM_STATE),
          buf_f_new)
    return x_new, st


def _pack_w_in_kernel(wt_ref, o_ref):
    for c, c0 in enumerate(W_IN_TILE_COLS):
        o_ref[:, c * LANE:(c + 1) * LANE] = wt_ref[c0:c0 + LANE, :].T.astype(BF16)


def _pack_w_in(wt):
    return pl.pallas_call(
        _pack_w_in_kernel,
        out_shape=jax.ShapeDtypeStruct((wt.shape[1], D_IN_PACKED), BF16),
        compiler_params=pltpu.CompilerParams(vmem_limit_bytes=VMEM_LIMIT),
        name="pack_w_in",
    )(wt)


def _dt_lanes(v):
    return jnp.pad(v, (DT_LANE0, 0))[None, :]


def kernel(x_prompt, x_sample, cache_k, cache_v, cache_kidx, state_conv_a, state_conv_ssm, state_ssm, state_conv_ffn, rel_bias, norm_mix_pre, norm_mix_post, norm_ffn_pre, norm_ffn_post, w_in, conv_a_w, conv_ssm_w, conv_ssm_b, dt_bias, a_log, d_skip, ssm_norm_w, w_out, w_gate, w_up, conv_ffn_w, conv_ffn_b, w_down):
    depth = w_in.shape[0]
    bp = x_prompt.shape[0]
    yp, ys = x_prompt, x_sample
    outs_p, outs_s = [], []
    for l in range(depth):
        p = {'norm_mix_pre': norm_mix_pre[l][None], 'norm_mix_post': norm_mix_post[l][None],
             'norm_ffn_pre': norm_ffn_pre[l][None], 'norm_ffn_post': norm_ffn_post[l][None],
             'w_in': _pack_w_in(jnp.swapaxes(w_in[l], 0, 1)), 'conv_a_w': conv_a_w[l], 'conv_ssm_w': conv_ssm_w[l],
             'conv_ssm_b': conv_ssm_b[l][None], 'dt_bias': _dt_lanes(dt_bias[l]),
             'a_log': _dt_lanes(a_log[l]), 'd_skip': jnp.repeat(d_skip[l], SSM_HEAD_DIM)[None],
             'ssm_norm_w': ssm_norm_w[l][None], 'w_out': w_out[l].astype(BF16),
             'w_gate': w_gate[l].astype(BF16), 'w_up': w_up[l].astype(BF16),
             'conv_ffn_w': conv_ffn_w[l], 'conv_ffn_b': conv_ffn_b[l][None],
             'w_down': w_down[l].astype(BF16)}
        yp, st_p = _layer(
            yp, p, rel_bias,
            jnp.zeros((bp, CONV_A_W - 1, D_CONV_MIX), F32),
            jnp.zeros((bp, SSM_CONV_W - 1, D_XBC), F32),
            jnp.zeros((bp, SSM_GROUPS, SSM_HPG, SSM_HEAD_DIM, SSM_STATE), F32),
            jnp.zeros((bp, CONV_F_W - 1, D_FF), F32),
            None, tm=256, tm_ffn=512)
        ys, st_s = _layer(
            ys, p, rel_bias, state_conv_a[l], state_conv_ssm[l], state_ssm[l], state_conv_ffn[l],
            (l, cache_k, cache_v, cache_kidx), tm=ys.shape[1], tm_ffn=ys.shape[1])
        outs_p.append(st_p)
        outs_s.append(st_s)

    def stack(outs, i):
        return jnp.stack([o[i] for o in outs], axis=0)

    res = [yp, ys]
    for i in range(7):
        res.append(stack(outs_p, i))
        res.append(stack(outs_s, i))
    return tuple(res)
```

```python
import functools

import jax
import jax.numpy as jnp
from jax import lax
from jax.experimental import pallas as pl
from jax.experimental.pallas import tpu as pltpu

F32 = jnp.float32
BF16 = jnp.bfloat16

D_MODEL = 1024
CHUNK = 64
CHUNK_SHIFT = CHUNK.bit_length() - 1
assert 1 << CHUNK_SHIFT == CHUNK
D_CONV_MIX = 256
CONV_A_W = 3
N_HEADS = 6
N_KV_HEADS = 2
HEAD_DIM = 64
D_ATT = N_HEADS * HEAD_DIM
GRP = N_HEADS // N_KV_HEADS
N_IDX_HEADS = 4
D_IDX = 64
TOPK = 256
N_BUCKETS = 32
SSM_HEADS = 6
SSM_HEAD_DIM = 64
D_SSM = SSM_HEADS * SSM_HEAD_DIM
SSM_GROUPS = 2
SSM_HPG = SSM_HEADS // SSM_GROUPS
SSM_STATE = 128
SSM_CONV_W = 4
D_XBC = D_SSM + 2 * SSM_GROUPS * SSM_STATE
D_FF = 2816
CONV_F_W = 3
NORM_EPS = 1e-6

LANE = 128
CARRY = 8
SSD_Q = 128
KEY_TILE = 128
BIG = 4
VMEM_LIMIT = 56 * 1024 * 1024
NEG = -1e30
LOG2E = 1.4426950408889634
F32_MAX = 3.4028234663852886e38
F32_TINY = 1.1754943508222875e-38
NO_QUOTA = 1e9
MAX_SEARCH_STEPS = 256
BLIND_SEARCH_STEPS = 15

C_AB, C_AC, C_AH = 0, 256, 512
C_Q = 768
C_K = 1152
C_V = 1280
C_QI = 1408
C_KI = 1664
C_Z = 1792
C_XBC = 2176
C_DT = 3072
D_IN_PACKED = 3200
PROJ_GROUPS = (0, C_Q, C_Z, D_IN_PACKED)
D_IN = 3018
D_IN_HEAD = 1732
DT_LANE0 = LANE - SSM_HEADS
W_IN_TILE_COLS = ([c * LANE for c in range(C_Z // LANE)]
                  + [D_IN_HEAD + c * LANE for c in range((C_DT - C_Z) // LANE)] + [D_IN - LANE])


def _rms(x, w):
    return x * lax.rsqrt(jnp.mean(x * x, axis=-1, keepdims=True) + NORM_EPS) * w


def _dot(a, b):
    return jnp.dot(a, b, preferred_element_type=F32)


def _silu(x):
    return x * jax.nn.sigmoid(x)


VT_ROWS = HEAD_DIM + 16


def _augment_vt(vt):
    extra_shape = vt.shape[:-2] + (VT_ROWS - HEAD_DIM, vt.shape[-1])
    row = lax.broadcasted_iota(jnp.int32, extra_shape, len(extra_shape) - 2)
    extra = jnp.where(row == 0, 1.0, 0.0).astype(vt.dtype)
    parts = []
    for g in range(N_KV_HEADS):
        parts += [vt[..., g * HEAD_DIM:(g + 1) * HEAD_DIM, :], extra]
    return jnp.concatenate(parts, axis=-2)


def _resident(shape, index_map):
    return pl.BlockSpec(shape, index_map, pipeline_mode=pl.Buffered(1))


def _in_proj_kernel(x_ref, nw_ref, w_ref, caw_ref, csw_ref, csb_ref, dtb_ref, bufa_ref, bufs_ref,
                    ya_ref, q_ref, k_ref, v_ref, qi_ref, ki_ref, sgn_ref, z_ref, xs_ref, bm_ref,
                    cm_ref, dt_ref, bufa_out, bufs_out, *rest, tm, key_side):
    if key_side:
        kb_ref, vt_ref, kib_ref, ea, es = rest
    else:
        ea, es = rest
    i = pl.program_id(1)
    u = _rms(x_ref[0], nw_ref[...]).astype(BF16)

    slabs = {}

    def proj(c0, width):
        g0, g1 = next((a, b) for a, b in zip(PROJ_GROUPS, PROJ_GROUPS[1:]) if a <= c0 < b)
        if g0 not in slabs:
            slabs[g0] = _dot(u, w_ref[:, g0:g1])
        return slabs[g0][:, c0 - g0:c0 - g0 + width]

    @pl.when(i == 0)
    def _():
        ea[CARRY - 2:CARRY, :] = bufa_ref[0]
        es[CARRY - 3:CARRY, :] = bufs_ref[0]

    ea[CARRY:CARRY + tm, :] = proj(C_AC, D_CONV_MIX) * proj(C_AH, D_CONV_MIX)
    conv = ea[CARRY - 2:CARRY - 2 + tm, :] * caw_ref[0:1, :]
    for j in range(1, CONV_A_W):
        conv = conv + ea[CARRY - 2 + j:CARRY - 2 + j + tm, :] * caw_ref[j:j + 1, :]
    ya_ref[0] = (proj(C_AB, D_CONV_MIX) * conv).astype(BF16)
    tail_a = ea[CARRY + tm - 2:CARRY + tm, :]
    bufa_out[0] = tail_a
    ea[CARRY - 2:CARRY, :] = tail_a

    q_ref[0] = (proj(C_Q, D_ATT) * (HEAD_DIM ** -0.5 * LOG2E)).astype(BF16)
    k = proj(C_K, N_KV_HEADS * HEAD_DIM)
    v = proj(C_V, N_KV_HEADS * HEAD_DIM)
    kiw = proj(C_KI, LANE)
    k_ref[0] = k
    v_ref[0] = v
    ki_ref[0] = kiw[:, :D_IDX]
    wi = kiw[:, D_IDX:D_IDX + N_IDX_HEADS]
    kiw_rows = kiw if tm >= LANE else jnp.concatenate([kiw, jnp.zeros((LANE - tm, LANE), F32)], axis=0)
    sgn_ref[0] = jnp.where(kiw_rows.T[D_IDX:D_IDX + 8, :] >= 0.0, 1.0, -1.0).astype(F32)
    if key_side:
        kb_ref[0] = k.astype(BF16)
        vt_ref[0] = _augment_vt(v.T).astype(BF16)
        kib_ref[0] = kiw[:, :D_IDX].astype(BF16)
    wabs = jnp.abs(wi) * (D_IDX ** -0.5 * N_IDX_HEADS ** -0.5)
    qi = proj(C_QI, N_IDX_HEADS * D_IDX)
    for h in range(N_IDX_HEADS):
        qi_ref[0, :, h * D_IDX:(h + 1) * D_IDX] = (
            qi[:, h * D_IDX:(h + 1) * D_IDX] * wabs[:, h:h + 1]).astype(BF16)

    z_ref[0] = proj(C_Z, D_SSM)
    es[CARRY:CARRY + tm, :] = proj(C_XBC, D_XBC)
    xc = es[CARRY - 3:CARRY - 3 + tm, :] * csw_ref[0:1, :]
    for j in range(1, SSM_CONV_W):
        xc = xc + es[CARRY - 3 + j:CARRY - 3 + j + tm, :] * csw_ref[j:j + 1, :]
    xc = _silu(xc + csb_ref[...])
    xs_ref[0] = xc[:, :D_SSM]
    bm_ref[0] = xc[:, D_SSM:D_SSM + SSM_GROUPS * SSM_STATE]
    cm_ref[0] = xc[:, D_SSM + SSM_GROUPS * SSM_STATE:]
    tail_s = es[CARRY + tm - 3:CARRY + tm, :]
    bufs_out[0] = tail_s
    es[CARRY - 3:CARRY, :] = tail_s
    dtr = proj(C_DT, LANE) + dtb_ref[...]
    dt_ref[0] = jnp.maximum(dtr, 0.0) + jnp.log1p(jnp.exp(-jnp.abs(dtr)))


def _in_proj(x, nw, w_packed, caw, csw, csb, dtb, bufa, bufs, *, tm, key_side):
    b, t, _ = x.shape
    tq = max(tm, LANE)
    grid = (b, t // tm)
    row = lambda width: pl.BlockSpec((1, tm, width), lambda bi, i: (bi, i, 0))
    full = lambda a: _resident(a.shape, lambda bi, i: (0,) * a.ndim)
    state = lambda r, width: pl.BlockSpec((1, r, width), lambda bi, i: (bi, 0, 0))
    out_shapes = (
        jax.ShapeDtypeStruct((b, t, D_CONV_MIX), BF16),
        jax.ShapeDtypeStruct((b, t, D_ATT), BF16),
        jax.ShapeDtypeStruct((b, t, 128), F32),
        jax.ShapeDtypeStruct((b, t, 128), F32),
        jax.ShapeDtypeStruct((b, t, 256), BF16),
        jax.ShapeDtypeStruct((b, t, D_IDX), F32),
        jax.ShapeDtypeStruct((b, 8, t // tm * tq), F32),
        jax.ShapeDtypeStruct((b, t, D_SSM), F32),
        jax.ShapeDtypeStruct((b, t, D_SSM), F32),
        jax.ShapeDtypeStruct((b, t, 256), F32),
        jax.ShapeDtypeStruct((b, t, 256), F32),
        jax.ShapeDtypeStruct((b, t, LANE), F32),
        jax.ShapeDtypeStruct((b, CONV_A_W - 1, D_CONV_MIX), F32),
        jax.ShapeDtypeStruct((b, SSM_CONV_W - 1, D_XBC), F32),
    )
    out_specs = (row(D_CONV_MIX), row(D_ATT), row(128), row(128), row(256), row(D_IDX),
                 pl.BlockSpec((1, 8, tq), lambda bi, i: (bi, 0, i)),
                 row(D_SSM), row(D_SSM), row(256), row(256), row(LANE),
                 state(CONV_A_W - 1, D_CONV_MIX), state(SSM_CONV_W - 1, D_XBC))
    if key_side:
        out_shapes += (jax.ShapeDtypeStruct((b, t, 128), BF16),
                       jax.ShapeDtypeStruct((b, N_KV_HEADS * VT_ROWS, t), BF16),
                       jax.ShapeDtypeStruct((b, t, D_IDX), BF16))
        out_specs += (row(128), pl.BlockSpec((1, N_KV_HEADS * VT_ROWS, tm), lambda bi, i: (bi, 0, i)), row(D_IDX))
    in_specs = [row(D_MODEL), full(nw), full(w_packed), full(caw), full(csw), full(csb), full(dtb),
                state(CONV_A_W - 1, D_CONV_MIX), state(SSM_CONV_W - 1, D_XBC)]
    return pl.pallas_call(
        functools.partial(_in_proj_kernel, tm=tm, key_side=key_side),
        grid=grid, in_specs=in_specs, out_specs=out_specs, out_shape=out_shapes,
        scratch_shapes=[pltpu.VMEM((CARRY + tm, D_CONV_MIX), F32),
                        pltpu.VMEM((CARRY + tm, D_XBC), F32)],
        compiler_params=pltpu.CompilerParams(
            dimension_semantics=("arbitrary", "arbitrary"), vmem_limit_bytes=VMEM_LIMIT),
        name="in_proj",
    )(x, nw, w_packed, caw, csw, csb, dtb, bufa, bufs)


def _t5_bucket(rel):
    n = jnp.abs(rel)
    large = jnp.full(rel.shape, 8, jnp.int32)
    for brk in (12, 16, 23, 32, 46, 64, 91):
        large = large + jnp.where(n >= brk, 1, 0)
    return jnp.where(rel > 0, N_BUCKETS // 2, 0) + jnp.where(n < 8, n, large)


def _f32_key(x):
    bits = lax.bitcast_convert_type(x, jnp.int32)
    return bits ^ ((bits >> 31) & jnp.int32(0x7FFFFFFF))


def _key_f32(key):
    return lax.bitcast_convert_type(key ^ ((key >> 31) & jnp.int32(0x7FFFFFFF)), F32)


def _dsa_kernel(relb_ref, q_ref, qi_ref, sgn_ref, k_ref, vt_ref, ki_ref, o_ref, sc, bias, sbuf,
                *, q_pos0, l_valid):
    tq = KEY_TILE
    i = pl.program_id(1)
    q0 = q_pos0 + i * tq
    qt = q0 // KEY_TILE
    n0 = jnp.maximum(qt - 1, 0)
    variant = jnp.minimum(qt, 1)
    steps_far = (n0 + BIG - 1) // BIG
    steps_all = (n0 + 2 + BIG - 1) // BIG
    kw = BIG * KEY_TILE
    kf = float(TOPK)

    def step_keys(j):
        return pl.ds(pl.multiple_of(j * kw, kw), kw)

    near_keys = pl.ds(pl.multiple_of(n0 * KEY_TILE, KEY_TILE), 2 * KEY_TILE)

    @pl.when(i == 0)
    def _():
        sc[...] = jnp.full(sc.shape, -jnp.inf, F32)

    @pl.when((i == 0) & (pl.program_id(0) == 0))
    def _():
        kk = lax.broadcasted_iota(jnp.int32, (2 * KEY_TILE, tq), 0)
        r = lax.broadcasted_iota(jnp.int32, (2 * KEY_TILE, tq), 1)
        for var in range(2):
            bucket = _t5_bucket(kk - var * KEY_TILE - r)
            for h in range(N_HEADS):
                tab = jnp.zeros((2 * KEY_TILE, tq), F32)
                for bk in range(N_BUCKETS):
                    tab = jnp.where(bucket == bk, relb_ref[bk, h], tab)
                g, j = divmod(h, GRP)
                bias[var, g, :, j * tq:(j + 1) * tq] = (tab - relb_ref[N_BUCKETS // 2 - 1, h]) * LOG2E

    qit = qi_ref[0].astype(F32).T
    qis = jnp.concatenate([qit[h * D_IDX:(h + 1) * D_IDX, :] for h in range(N_IDX_HEADS)],
                          axis=1).astype(BF16)
    sgn = sgn_ref[0]

    def scores(keys):
        s = jnp.maximum(_dot(ki_ref[0, keys, :], qis), 0.0)
        tot = s[:, 0:tq] * sgn[0:1, :]
        for h in range(1, N_IDX_HEADS):
            tot = tot + s[:, h * tq:(h + 1) * tq] * sgn[h:h + 1, :]
        return tot

    zero = jnp.zeros((KEY_TILE, tq), F32)

    tally_init = (jnp.zeros((KEY_TILE, tq), BF16), jnp.zeros((KEY_TILE, tq), BF16),
                  jnp.full((KEY_TILE, tq), float(jnp.finfo(BF16).min), BF16))
    one16, nil16 = jnp.ones((KEY_TILE, tq), BF16), jnp.zeros((KEY_TILE, tq), BF16)

    def tally(c, s):
        ge, gt, mx = c
        s16 = s.astype(BF16)
        return (ge + jnp.where(s16 >= 0.0, one16, nil16),
                gt + jnp.where(s16 >= jnp.asarray(F32_TINY, BF16), one16, nil16), jnp.maximum(mx, s16))

    def tally_totals(c):
        ge, gt, mx = c
        return (col_sum(ge.astype(F32)), col_sum(gt.astype(F32)),
                jnp.max(mx.astype(F32), axis=0, keepdims=True))

    def score_far(j, c):
        s = scores(step_keys(j))
        for k in range(BIG):
            t = j * BIG + k
            tile = jnp.where(t < n0, s[k * KEY_TILE:(k + 1) * KEY_TILE, :], -jnp.inf)
            sc[t] = tile
            c = tally(c, tile)
        return c

    far_pairs = steps_far // 2
    c = lax.fori_loop(0, far_pairs, lambda j, c: score_far(2 * j + 1, score_far(2 * j, c)),
                      tally_init)
    c = lax.fori_loop(2 * far_pairs, steps_far, score_far, c)
    s_near = scores(near_keys)
    kpos = n0 * KEY_TILE + lax.broadcasted_iota(jnp.int32, (2 * KEY_TILE, tq), 0)
    qpos = q0 + lax.broadcasted_iota(jnp.int32, (2 * KEY_TILE, tq), 1)
    adm = (kpos < l_valid) & ((kpos >> CHUNK_SHIFT) <= (qpos >> CHUNK_SHIFT))
    s_near = jnp.where(adm, s_near, -jnp.inf)
    sc[n0] = s_near[:KEY_TILE, :]
    sc[n0 + 1] = s_near[KEY_TILE:, :]
    for k in range(BIG - 1):
        @pl.when(n0 + 2 + k < steps_all * BIG)
        def _():
            sc[n0 + 2 + k] = jnp.full((KEY_TILE, tq), -jnp.inf, F32)
    score_tally = tally(tally(c, s_near[:KEY_TILE, :]), s_near[KEY_TILE:, :])


    def col_sum(acc):
        parts = [acc[r:r + 8, :] for r in range(0, KEY_TILE, 8)]
        while len(parts) > 1:
            parts = [parts[r] + parts[r + 1] for r in range(0, len(parts), 2)]
        return jnp.sum(parts[0], axis=0, keepdims=True)

    def counter(value_at):
        def count_ge(t):
            tb = jnp.broadcast_to(t, (KEY_TILE, tq))

            def body(j, acc):
                for k in range(BIG):
                    acc = acc + jnp.where(value_at(j * BIG + k) >= tb, 1.0, 0.0)
                return acc
            return col_sum(lax.fori_loop(0, steps_all, body, zero))
        return count_ge

    def search(count_ge, kq, nfin, ge0, gt0, rmax, active, blind_steps):
        small = (nfin <= kq) | (active == 0.0)
        tie0 = (~small) & (gt0 < kq) & (ge0 >= kq)
        pos = (~small) & (gt0 >= kq)
        neg = (~small) & (ge0 < kq)
        log_kq = jnp.log(kq)
        lo0 = jnp.where(pos, F32_TINY, -F32_MAX)
        flo0 = jnp.where(pos, gt0, nfin)
        hi0 = jnp.where(neg, 0.0, jnp.inf)
        fhi0 = jnp.where(neg, ge0, 0.0)
        done0 = jnp.where(small | tie0 | (pos & (gt0 == kq)), 1.0, 0.0)
        thr0 = jnp.where(small, -F32_MAX, jnp.where(tie0, 0.0, F32_TINY))
        quota0 = jnp.where(tie0, kq - gt0, NO_QUOTA)

        def step(it, state):
            lo, flo, hi, fhi, done, thr, quota, adjacent = state
            klo, khi = _f32_key(lo), _f32_key(hi)
            la = jnp.log(flo)
            frac = (la - log_kq) / (la - jnp.log(jnp.maximum(fhi, 0.5)))
            interp = lo + (hi - lo) * jnp.where(it % 4 == 1, 0.5, frac)
            t_up = lo + (rmax - lo) * 0.5
            t_dn = hi - jnp.maximum(jnp.maximum(jnp.abs(rmax - hi), jnp.abs(hi)), F32_TINY)
            open_hi = hi == jnp.inf
            open_lo = lo <= -F32_MAX
            t = jnp.where(open_hi, t_up, jnp.where(open_lo, t_dn, interp))
            tk = jnp.minimum(jnp.maximum(_f32_key(t), klo + 1), khi - 1)
            tk = jnp.where(it % 4 == 3, klo + lax.shift_right_logical(khi - klo, 1), tk)
            t = _key_f32(tk)
            f = count_ge(t)
            act = done == 0.0
            hit = act & (f == kq)
            up = act & (f > kq)
            dn = act & (f < kq)
            lo = jnp.where(up, t, lo)
            flo = jnp.where(up, f, flo)
            hi = jnp.where(dn, t, hi)
            fhi = jnp.where(dn, f, fhi)
            adj = act & (~hit) & (_f32_key(lo) + 1 == _f32_key(hi))
            thr = jnp.where(hit, t, jnp.where(adj, lo, thr))
            quota = jnp.where(adj, kq - fhi, quota)
            done = jnp.where(hit | adj, 1.0, done)
            return lo, flo, hi, fhi, done, thr, quota, jnp.where(adj, 1.0, adjacent)

        state = (lo0, flo0, hi0, fhi0, done0, thr0, quota0, jnp.zeros((1, tq), F32))
        if blind_steps:
            state = lax.fori_loop(0, blind_steps, step, state)

        def cond(c):
            return (c[0] < MAX_SEARCH_STEPS) & (c[1] > 0)

        def body(c):
            state = step(c[0], c[2])
            return c[0] + 1, jnp.sum(1.0 - state[4]), state

        _, _, state = lax.while_loop(cond, body, (jnp.int32(blind_steps), jnp.sum(1.0 - state[4]), state))
        _, flo, _, fhi, _, thr, quota, adjacent = state
        return thr, quota, adjacent, flo, fhi

    qrow = q0 + lax.broadcasted_iota(jnp.int32, (1, tq), 1)
    nadm = jnp.minimum(((qrow >> CHUNK_SHIFT) + 1) * CHUNK, l_valid).astype(F32)
    thr, quota, adjacent, flo, fhi = search(
        counter(lambda t: sc[t]), jnp.full((1, tq), kf, F32), nadm,
        *tally_totals(score_tally), jnp.ones((1, tq), F32), BLIND_SEARCH_STEPS)
    thr_b = jnp.broadcast_to(thr, (tq, KEY_TILE))
    quota_b = jnp.broadcast_to(quota, (tq, KEY_TILE))
    has_ties = jnp.min(quota) < NO_QUOTA
    any_adjacent = jnp.max(adjacent) > 0.0

    @pl.when(jnp.logical_not(has_ties))
    def _():
        def body(j, c):
            for k in range(BIG):
                t = j * BIG + k
                sc[t] = jnp.where(sc[t] >= thr_b, 0.0, NEG)
            return c
        lax.fori_loop(0, steps_all, body, 0)

    def rank_pass(is_tie, is_sure, quota_of_ties):
        ri = lax.broadcasted_iota(jnp.int32, (KEY_TILE, KEY_TILE), 0)
        ci = lax.broadcasted_iota(jnp.int32, (KEY_TILE, KEY_TILE), 1)
        lower = jnp.where(ri >= ci, 1.0, 0.0).astype(BF16)

        def rank_tiles(first, n, seen):
            for k in range(n):
                t = first + k
                s = sc[t]
                tie = jnp.where(is_tie(s), 1.0, 0.0)
                rank = seen + _dot(lower, tie.astype(BF16))
                keep = is_sure(s) | ((tie > 0.0) & (rank <= quota_of_ties))
                sc[t] = jnp.where(keep, 0.0, NEG)
                seen = seen + jnp.sum(tie, axis=0, keepdims=True)
            return seen

        pairs = steps_all // 2
        seen = lax.fori_loop(0, pairs, lambda j, c: rank_tiles(j * 2 * BIG, 2 * BIG, c),
                             jnp.zeros((1, tq), F32))
        lax.fori_loop(2 * pairs, steps_all, lambda j, c: rank_tiles(j * BIG, BIG, c), seen)

    @pl.when(has_ties & jnp.logical_not(any_adjacent))
    def _():
        rank_pass(lambda s: s == thr_b, lambda s: s > thr_b, quota_b)

    @pl.when(any_adjacent)
    def _():
        adj_b = jnp.broadcast_to(adjacent, (tq, KEY_TILE)) > 0.0
        next_b = jnp.broadcast_to(_key_f32(_f32_key(thr) + 1), (tq, KEY_TILE))

        def in_cell(s):
            return adj_b & (s >= thr_b) & (s < next_b)

        def offset(s):
            return jnp.where(in_cell(s), s - thr_b, -jnp.inf)

        def tally_offsets(j, c):
            for k in range(BIG):
                c = tally(c, offset(sc[j * BIG + k]))
            return c

        in_cells, above_thr, max_offset = tally_totals(lax.fori_loop(0, steps_all, tally_offsets, tally_init))
        thr2, quota2, _, _, _ = search(
            counter(lambda t: offset(sc[t])), jnp.where(adjacent > 0.0, quota, 1.0), in_cells, in_cells,
            above_thr, max_offset, adjacent, 0)
        thr2_b = jnp.broadcast_to(thr2, (tq, KEY_TILE))
        quota2_b = jnp.broadcast_to(quota2, (tq, KEY_TILE))
        plain_b = jnp.logical_not(adj_b)
        rank_pass(lambda s: (plain_b & (s == thr_b)) | (offset(s) == thr2_b),
                  lambda s: (plain_b & (s > thr_b)) | (adj_b & (s >= next_b)) | (offset(s) > thr2_b),
                  jnp.where(adj_b, quota2_b, quota_b))

    qt_all = q_ref[0].astype(F32).T
    zeros_half = jnp.zeros((HEAD_DIM, GRP * tq), F32)
    qa = []
    for g in range(N_KV_HEADS):
        piece = jnp.concatenate(
            [qt_all[(g * GRP + j) * HEAD_DIM:(g * GRP + j + 1) * HEAD_DIM, :] for j in range(GRP)],
            axis=1)
        halves = [piece, zeros_half] if g == 0 else [zeros_half, piece]
        qa.append(jnp.concatenate(halves, axis=0).astype(BF16))

    def logits(slot, keys, pens, m_old, extra=None):
        pen = jnp.concatenate(pens, axis=0)
        pen = jnp.concatenate([pen] * GRP, axis=1)
        kblk = k_ref[0, keys, :]
        m_new = []
        for g in range(N_KV_HEADS):
            s = _dot(kblk, qa[g]) + pen
            if extra is not None:
                s = s + extra[g]
            sbuf[slot, g, 0:pen.shape[0], :] = s
            m_new.append(jnp.maximum(m_old[g], jnp.max(s, axis=0, keepdims=True)))
        return tuple(m_new)

    def accumulate(slot, keys, nk, m_old, m_new, accs):
        out = []
        for g in range(N_KV_HEADS):
            p = jnp.exp2(sbuf[slot, g, 0:nk, :] - m_new[g])
            out.append(jnp.exp2(m_old[g] - m_new[g]) * accs[g]
                       + _dot(vt_ref[0, g * VT_ROWS:(g + 1) * VT_ROWS, keys], p.astype(BF16)))
        return tuple(out)

    last_far = jnp.maximum(steps_far - 1, 0)

    def far_keys(j):
        return step_keys(jnp.minimum(j, last_far))

    def far_pens(j):
        return [jnp.where(j * BIG + k < n0, sc[jnp.minimum(j * BIG + k, n0)], NEG) for k in range(BIG)]

    m_init = tuple(jnp.full((1, GRP * tq), NEG, F32) for _ in range(N_KV_HEADS))
    acc_init = tuple(jnp.zeros((VT_ROWS, GRP * tq), F32) for _ in range(N_KV_HEADS))
    m_first = logits(0, far_keys(0), far_pens(0), m_init)

    def stage_pair(i, c):
        m_old, m_cur, accs = c
        a = 2 * i + 1
        m_a = logits(1, far_keys(a), far_pens(a), m_cur)
        accs = accumulate(0, far_keys(a - 1), kw, m_old, m_cur, accs)
        m_b = logits(0, far_keys(a + 1), far_pens(a + 1), m_a)
        accs = accumulate(1, far_keys(a), kw, m_cur, m_a, accs)
        return m_a, m_b, accs

    pairs = steps_far // 2
    m_old, m_cur, accs = lax.fori_loop(0, pairs, stage_pair, (m_init, m_first, acc_init))
    m_last = logits(1, near_keys, [sc[n0], sc[n0 + 1]], m_cur, extra=(bias[variant, 0], bias[variant, 1]))
    accs = accumulate(0, far_keys(2 * pairs), kw, m_old, m_cur, accs)
    accs = accumulate(1, near_keys, 2 * KEY_TILE, m_cur, m_last, accs)
    heads = []
    for g in range(N_KV_HEADS):
        acc = accs[g]
        og = acc[:HEAD_DIM, :] / acc[HEAD_DIM:HEAD_DIM + 1, :]
        heads += [og[:, j * tq:(j + 1) * tq] for j in range(GRP)]
    o_ref[0] = jnp.concatenate(heads, axis=0).T.astype(BF16)


def _dsa(rel_bias, q, qi, sgn, k, vt, ki, *, q_pos0, l_valid):
    b, t, _ = q.shape
    lk = k.shape[1]
    tq = KEY_TILE
    assert lk % (BIG * KEY_TILE) == 0 and q_pos0 % KEY_TILE == 0 and t % tq == 0
    assert q_pos0 + t <= lk
    nt = lk // KEY_TILE
    row = lambda width: pl.BlockSpec((1, tq, width), lambda bi, i: (bi, i, 0))
    keys = lambda shape: (_resident if b == 1 else pl.BlockSpec)(shape, lambda bi, i: (bi, 0, 0))
    return pl.pallas_call(
        functools.partial(_dsa_kernel, q_pos0=q_pos0, l_valid=l_valid),
        grid=(b, t // tq),
        in_specs=[pl.BlockSpec(memory_space=pltpu.SMEM),
                  row(D_ATT), row(N_IDX_HEADS * D_IDX),
                  pl.BlockSpec((1, 8, tq), lambda bi, i: (bi, 0, i)),
                  keys((1, lk, N_KV_HEADS * HEAD_DIM)), keys((1, N_KV_HEADS * VT_ROWS, lk)),
                  keys((1, lk, D_IDX))],
        out_specs=row(D_ATT),
        out_shape=jax.ShapeDtypeStruct((b, t, D_ATT), BF16),
        scratch_shapes=[pltpu.VMEM((nt, KEY_TILE, tq), F32),
                        pltpu.VMEM((2, N_KV_HEADS, 2 * KEY_TILE, GRP * tq), F32),
                        pltpu.VMEM((2, N_KV_HEADS, BIG * KEY_TILE, GRP * tq), F32)],
        compiler_params=pltpu.CompilerParams(
            dimension_semantics=("arbitrary", "arbitrary"), vmem_limit_bytes=VMEM_LIMIT),
        name="dsa",
    )(rel_bias, q, qi, sgn, k, vt, ki)


def _split3(x):
    hi = x.astype(BF16)
    r = x - hi.astype(F32)
    mid = r.astype(BF16)
    lo = (r - mid.astype(F32)).astype(BF16)
    return hi, mid, lo


def _ssd_kernel(xs_ref, bm_ref, cm_ref, dt_ref, z_ref, alog_ref, dsk_ref, nw_ref, h0_ref,
                y_ref, h_ref, ht, *, q_in):
    c = pl.program_id(1)
    nc = pl.num_programs(1)

    @pl.when(c == 0)
    def _():
        for hh in range(SSM_HEADS):
            ht[hh] = h0_ref[0, hh].T

    def rows(ref):
        x = ref[0]
        if q_in == SSD_Q:
            return x
        return jnp.concatenate([x, jnp.zeros((SSD_Q - q_in, x.shape[1]), x.dtype)], axis=0)

    xs, bm, cm, dt, z = rows(xs_ref), rows(bm_ref), rows(cm_ref), rows(dt_ref), rows(z_ref)
    a = -jnp.exp(alog_ref[...])
    da = dt * a
    ri = lax.broadcasted_iota(jnp.int32, (SSD_Q, SSD_Q), 0)
    ci = lax.broadcasted_iota(jnp.int32, (SSD_Q, SSD_Q), 1)
    causal = ri >= ci
    tril = jnp.where(causal, 1.0, 0.0).astype(BF16)
    hi, mid, lo = _split3(da)
    acum = _dot(tril, hi) + _dot(tril, mid) + _dot(tril, lo)
    acum_t = acum.T

    for g in range(SSM_GROUPS):
        bm_g = bm[:, g * SSM_STATE:(g + 1) * SSM_STATE]
        cm_g = cm[:, g * SSM_STATE:(g + 1) * SSM_STATE].astype(BF16)
        bm_t = bm_g.T.astype(BF16)
        cb = _dot(cm_g, bm_t)
        gs = []
        ssq = jnp.zeros((SSD_Q, 1), F32)
        for r in range(SSM_HPG):
            hh = g * SSM_HPG + r
            lanes = slice(hh * SSM_HEAD_DIM, (hh + 1) * SSM_HEAD_DIM)
            xh = xs[:, lanes]
            dl = DT_LANE0 + hh
            col = acum[:, dl:dl + 1]
            rowv = acum_t[dl:dl + 1, :]
            decay = jnp.exp(jnp.where(causal, col - rowv, NEG))
            xdt = xh * dt[:, dl:dl + 1]
            y = _dot((cb * decay).astype(BF16), xdt.astype(BF16))
            h_prev = ht[hh]
            y = y + jnp.exp(col) * _dot(cm_g, h_prev.astype(BF16))
            last = acum[SSD_Q - 1:SSD_Q, dl:dl + 1]
            xw = (xdt * jnp.exp(last - col)).astype(BF16)
            ht[hh] = h_prev * jnp.exp(last) + _dot(bm_t, xw)
            y = y + xh * dsk_ref[:, lanes]
            gate = y * _silu(z[:, lanes])
            ssq = ssq + jnp.sum(gate * gate, axis=1, keepdims=True)
            gs.append(gate)
        scale = lax.rsqrt(ssq / float(SSM_HPG * SSM_HEAD_DIM) + NORM_EPS)
        for r in range(SSM_HPG):
            hh = g * SSM_HPG + r
            lanes = slice(hh * SSM_HEAD_DIM, (hh + 1) * SSM_HEAD_DIM)
            y_ref[0, :, lanes] = (gs[r] * scale * nw_ref[:, lanes])[:q_in].astype(BF16)

    @pl.when(c == nc - 1)
    def _():
        for hh in range(SSM_HEADS):
            h_ref[0, hh] = ht[hh].T


def _ssd(xs, bm, cm, dt, z, alog, dsk, nw, h0, *, q_in):
    b, t, _ = xs.shape
    row = lambda width: pl.BlockSpec((1, q_in, width), lambda bi, c: (bi, c, 0))
    full = lambda a: _resident(a.shape, lambda bi, c: (0,) * a.ndim)
    hspec = pl.BlockSpec((1, SSM_HEADS, SSM_HEAD_DIM, SSM_STATE), lambda bi, c: (bi, 0, 0, 0))
    return pl.pallas_call(
        functools.partial(_ssd_kernel, q_in=q_in),
        grid=(b, t // q_in),
        in_specs=[row(D_SSM), row(256), row(256), row(LANE), row(D_SSM),
                  full(alog), full(dsk), full(nw), hspec],
        out_specs=(row(D_SSM), hspec),
        out_shape=(jax.ShapeDtypeStruct((b, t, D_SSM), BF16),
                   jax.ShapeDtypeStruct((b, SSM_HEADS, SSM_HEAD_DIM, SSM_STATE), F32)),
        scratch_shapes=[pltpu.VMEM((SSM_HEADS, SSM_STATE, SSM_HEAD_DIM), F32)],
        compiler_params=pltpu.CompilerParams(
            dimension_semantics=("arbitrary", "arbitrary"), vmem_limit_bytes=VMEM_LIMIT),
        name="ssd",
    )(xs, bm, cm, dt, z, alog, dsk, nw, h0)


def _mix_ffn_kernel(x_ref, ya_ref, yb_ref, yc_ref, wo_ref, npost_ref, nfpre_ref, wg_ref, wu_ref,
                    wd_ref, cfw_ref, cfb_ref, nfpost_ref, buff_ref, o_ref, buff_out, eg, *, tm):
    i = pl.program_id(1)

    @pl.when(i == 0)
    def _():
        eg[CARRY - 2:CARRY, :] = buff_ref[0]

    mix = _dot(jnp.concatenate([ya_ref[0], yb_ref[0], yc_ref[0]], axis=1), wo_ref[...])
    x1 = x_ref[0] + _rms(mix, npost_ref[...])
    u = _rms(x1, nfpre_ref[...]).astype(BF16)
    eg[CARRY:CARRY + tm, :] = _dot(u, wg_ref[...])
    gc = eg[CARRY - 2:CARRY - 2 + tm, :] * cfw_ref[0:1, :]
    for j in range(1, CONV_F_W):
        gc = gc + eg[CARRY - 2 + j:CARRY - 2 + j + tm, :] * cfw_ref[j:j + 1, :]
    hid = (_silu(gc + cfb_ref[...]) * _dot(u, wu_ref[...])).astype(BF16)
    o_ref[0] = x1 + _rms(_dot(hid, wd_ref[...]), nfpost_ref[...])
    tail = eg[CARRY + tm - 2:CARRY + tm, :]
    buff_out[0] = tail
    eg[CARRY - 2:CARRY, :] = tail


def _mix_ffn(x, ya, yb, yc, wo, npost, nfpre, wg, wu, wd, cfw, cfb, nfpost, buff, *, tm):
    b, t, _ = x.shape
    row = lambda width: pl.BlockSpec((1, tm, width), lambda bi, i: (bi, i, 0))
    full = lambda a: _resident(a.shape, lambda bi, i: (0,) * a.ndim)
    state = pl.BlockSpec((1, CONV_F_W - 1, D_FF), lambda bi, i: (bi, 0, 0))
    return pl.pallas_call(
        functools.partial(_mix_ffn_kernel, tm=tm),
        grid=(b, t // tm),
        in_specs=[row(D_MODEL), row(D_CONV_MIX), row(D_ATT), row(D_SSM), full(wo), full(npost),
                  full(nfpre), full(wg), full(wu), full(wd), full(cfw), full(cfb), full(nfpost),
                  state],
        out_specs=(row(D_MODEL), state),
        out_shape=(jax.ShapeDtypeStruct((b, t, D_MODEL), F32),
                   jax.ShapeDtypeStruct((b, CONV_F_W - 1, D_FF), F32)),
        scratch_shapes=[pltpu.VMEM((CARRY + tm, D_FF), F32)],
        compiler_params=pltpu.CompilerParams(
            dimension_semantics=("arbitrary", "arbitrary"), vmem_limit_bytes=VMEM_LIMIT),
        name="mix_ffn",
    )(x, ya, yb, yc, wo, npost, nfpre, wg, wu, wd, cfw, cfb, nfpost, buff)


def _pack_keys_kernel(ck_ref, cv_ref, cki_ref, k_ref, v_ref, ki_ref, kb_ref, vt_ref, kib_ref, *, past_steps):
    j = pl.program_id(1)
    rows = kb_ref.shape[1]
    cached = j < past_steps

    def with_new(cache, new):
        new = jnp.concatenate([new, jnp.zeros((rows - new.shape[0], new.shape[1]), F32)], axis=0)
        return jnp.where(cached, cache, new)

    heads = lambda ref: [ref[0, 0, g] for g in range(N_KV_HEADS)]
    k_cache = jnp.concatenate([h.T for h in heads(ck_ref)], axis=1)
    kb_ref[0] = with_new(k_cache, k_ref[0]).astype(BF16)
    v_new = jnp.concatenate([v_ref[0], jnp.zeros((rows - v_ref.shape[1], v_ref.shape[2]), F32)], axis=0)
    vt = jnp.where(cached, jnp.concatenate(heads(cv_ref), axis=0), v_new.T)
    vt_ref[0] = _augment_vt(vt).astype(BF16)
    kib_ref[0] = with_new(cki_ref[0, 0].T, ki_ref[0]).astype(BF16)


def _pack_keys(layer, cache_k, cache_v, cache_kidx, k, v, ki):
    b, past = cache_k.shape[1:3]
    t = k.shape[1]
    rows = 2 * BIG * KEY_TILE
    assert past % rows == 0 and t <= rows
    past_steps = past // rows
    lk = past + rows
    cache = lambda *major: pl.BlockSpec(
        (1, 1) + major + (rows,),
        lambda bi, j: (layer, bi) + (0,) * len(major) + (jnp.minimum(j, past_steps - 1),))
    cache_k, cache_v = (jnp.transpose(c, (0, 1, 3, 4, 2)) for c in (cache_k, cache_v))
    cache_kidx = jnp.transpose(cache_kidx, (0, 1, 3, 2))
    new = lambda width: pl.BlockSpec((1, t, width), lambda bi, j: (bi, 0, 0))
    out = lambda width: pl.BlockSpec((1, rows, width), lambda bi, j: (bi, j, 0))
    return pl.pallas_call(
        functools.partial(_pack_keys_kernel, past_steps=past_steps),
        grid=(b, past_steps + 1),
        in_specs=[cache(N_KV_HEADS, HEAD_DIM), cache(N_KV_HEADS, HEAD_DIM), cache(D_IDX),
                  new(128), new(128), new(D_IDX)],
        out_specs=(out(128), pl.BlockSpec((1, N_KV_HEADS * VT_ROWS, rows), lambda bi, j: (bi, 0, j)),
                   out(D_IDX)),
        out_shape=(jax.ShapeDtypeStruct((b, lk, 128), BF16),
                   jax.ShapeDtypeStruct((b, N_KV_HEADS * VT_ROWS, lk), BF16),
                   jax.ShapeDtypeStruct((b, lk, D_IDX), BF16)),
        compiler_params=pltpu.CompilerParams(
            dimension_semantics=("arbitrary", "arbitrary"), vmem_limit_bytes=VMEM_LIMIT),
        name="pack_keys",
    )(cache_k, cache_v, cache_kidx, k, v, ki)


def _layer(x, p, rel_bias, buf_a, buf_s, h0, buf_f, kv_past, *, tm, tm_ffn):
    b, t, _ = x.shape
    outs = _in_proj(x, p['norm_mix_pre'], p['w_in'], p['conv_a_w'], p['conv_ssm_w'], p['conv_ssm_b'],
                    p['dt_bias'], buf_a, buf_s, tm=tm, key_side=kv_past is None)
    (ya, q, k, v, qi, ki, sgn, z, xs, bm, cm, dt, buf_a_new, buf_s_new) = outs[:14]
    if kv_past is None:
        kb, vt, kib = outs[14:]
        yb = _dsa(rel_bias, q, qi, sgn, kb, vt, kib, q_pos0=0, l_valid=t)
    else:
        layer, cache_k, cache_v, cache_kidx = kv_past
        past = cache_k.shape[2]
        qpad = lambda a: jnp.pad(a, ((0, 0), (0, -t % KEY_TILE), (0, 0)))
        kb, vt, kib = _pack_keys(layer, cache_k, cache_v, cache_kidx, k, v, ki)
        yb = _dsa(rel_bias, qpad(q), qpad(qi), sgn, kb, vt, kib, q_pos0=past, l_valid=past + t)[:, :t]
    yc, h_new = _ssd(xs, bm, cm, dt, z, p['a_log'], p['d_skip'], p['ssm_norm_w'],
                     h0.reshape(b, SSM_HEADS, SSM_HEAD_DIM, SSM_STATE), q_in=min(t, SSD_Q))
    x_new, buf_f_new = _mix_ffn(x, ya, yb, yc, p['w_out'], p['norm_mix_post'], p['norm_ffn_pre'],
                                p['w_gate'], p['w_up'], p['w_down'], p['conv_ffn_w'],
                                p['conv_ffn_b'], p['norm_ffn_post'], buf_f, tm=tm_ffn)
    st = (k.reshape(b, t, N_KV_HEADS, HEAD_DIM), v.reshape(b, t, N_KV_HEADS, HEAD_DIM), ki,
          buf_a_new, buf_s_new, h_new.reshape(b, SSM_GROUPS, SSM_HPG, SSM_HEAD_DIM, SSM_STATE),
          buf_f_new)
    return x_new, st


def _pack_w_in_kernel(wt_ref, o_ref):
    for c, c0 in enumerate(W_IN_TILE_COLS):
        o_ref[:, c * LANE:(c + 1) * LANE] = wt_ref[c0:c0 + LANE, :].T.astype(BF16)


def _pack_w_in(wt):
    return pl.pallas_call(
        _pack_w_in_kernel,
        out_shape=jax.ShapeDtypeStruct((wt.shape[1], D_IN_PACKED), BF16),
        compiler_params=pltpu.CompilerParams(vmem_limit_bytes=VMEM_LIMIT),
        name="pack_w_in",
    )(wt)


def _dt_lanes(v):
    return jnp.pad(v, (DT_LANE0, 0))[None, :]


def kernel(x_prompt, x_sample, cache_k, cache_v, cache_kidx, state_conv_a, state_conv_ssm, state_ssm, state_conv_ffn, rel_bias, norm_mix_pre, norm_mix_post, norm_ffn_pre, norm_ffn_post, w_in, conv_a_w, conv_ssm_w, conv_ssm_b, dt_bias, a_log, d_skip, ssm_norm_w, w_out, w_gate, w_up, conv_ffn_w, conv_ffn_b, w_down):
    depth = w_in.shape[0]
    bp = x_prompt.shape[0]
    yp, ys = x_prompt, x_sample
    outs_p, outs_s = [], []
    for l in range(depth):
        p = {'norm_mix_pre': norm_mix_pre[l][None], 'norm_mix_post': norm_mix_post[l][None],
             'norm_ffn_pre': norm_ffn_pre[l][None], 'norm_ffn_post': norm_ffn_post[l][None],
             'w_in': _pack_w_in(jnp.swapaxes(w_in[l], 0, 1)), 'conv_a_w': conv_a_w[l], 'conv_ssm_w': conv_ssm_w[l],
             'conv_ssm_b': conv_ssm_b[l][None], 'dt_bias': _dt_lanes(dt_bias[l]),
             'a_log': _dt_lanes(a_log[l]), 'd_skip': jnp.repeat(d_skip[l], SSM_HEAD_DIM)[None],
             'ssm_norm_w': ssm_norm_w[l][None], 'w_out': w_out[l].astype(BF16),
             'w_gate': w_gate[l].astype(BF16), 'w_up': w_up[l].astype(BF16),
             'conv_ffn_w': conv_ffn_w[l], 'conv_ffn_b': conv_ffn_b[l][None],
             'w_down': w_down[l].astype(BF16)}
        yp, st_p = _layer(
            yp, p, rel_bias,
            jnp.zeros((bp, CONV_A_W - 1, D_CONV_MIX), F32),
            jnp.zeros((bp, SSM_CONV_W - 1, D_XBC), F32),
            jnp.zeros((bp, SSM_GROUPS, SSM_HPG, SSM_HEAD_DIM, SSM_STATE), F32),
            jnp.zeros((bp, CONV_F_W - 1, D_FF), F32),
            None, tm=256, tm_ffn=512)
        ys, st_s = _layer(
            ys, p, rel_bias, state_conv_a[l], state_conv_ssm[l], state_ssm[l], state_conv_ffn[l],
            (l, cache_k, cache_v, cache_kidx), tm=ys.shape[1], tm_ffn=ys.shape[1])
        outs_p.append(st_p)
        outs_s.append(st_s)

    def stack(outs, i):
        return jnp.stack([o[i] for o in outs], axis=0)

    res = [yp, ys]
    for i in range(7):
        res.append(stack(outs_p, i))
        res.append(stack(outs_s, i))
    return tuple(res)
```

```python
import functools

import jax
import jax.numpy as jnp
from jax import lax
from jax.experimental import pallas as pl
from jax.experimental.pallas import tpu as pltpu

F32 = jnp.float32
BF16 = jnp.bfloat16

D_MODEL = 1024
CHUNK = 64
CHUNK_SHIFT = CHUNK.bit_length() - 1
assert 1 << CHUNK_SHIFT == CHUNK
D_CONV_MIX = 256
CONV_A_W = 3
N_HEADS = 6
N_KV_HEADS = 2
HEAD_DIM = 64
D_ATT = N_HEADS * HEAD_DIM
GRP = N_HEADS // N_KV_HEADS
N_IDX_HEADS = 4
D_IDX = 64
TOPK = 256
N_BUCKETS = 32
SSM_HEADS = 6
SSM_HEAD_DIM = 64
D_SSM = SSM_HEADS * SSM_HEAD_DIM
SSM_GROUPS = 2
SSM_HPG = SSM_HEADS // SSM_GROUPS
SSM_STATE = 128
SSM_CONV_W = 4
D_XBC = D_SSM + 2 * SSM_GROUPS * SSM_STATE
D_FF = 2816
CONV_F_W = 3
NORM_EPS = 1e-6

LANE = 128
CARRY = 8
SSD_Q = 128
KEY_TILE = 128
BIG = 4
VMEM_LIMIT = 56 * 1024 * 1024
NEG = -1e30
LOG2E = 1.4426950408889634
F32_MAX = 3.4028234663852886e38
F32_TINY = 1.1754943508222875e-38
NO_QUOTA = 1e9
MAX_SEARCH_STEPS = 256
BLIND_SEARCH_STEPS = 15

C_AB, C_AC, C_AH = 0, 256, 512
C_Q = 768
C_K = 1152
C_V = 1280
C_QI = 1408
C_KI = 1664
C_Z = 1792
C_XBC = 2176
C_DT = 3072
D_IN_PACKED = 3200
PROJ_GROUPS = (0, C_Q, C_Z, D_IN_PACKED)
D_IN = 3018
D_IN_HEAD = 1732
DT_LANE0 = LANE - SSM_HEADS
W_IN_TILE_COLS = ([c * LANE for c in range(C_Z // LANE)]
                  + [D_IN_HEAD + c * LANE for c in range((C_DT - C_Z) // LANE)] + [D_IN - LANE])


def _rms(x, w):
    return x * lax.rsqrt(jnp.mean(x * x, axis=-1, keepdims=True) + NORM_EPS) * w


def _dot(a, b):
    return jnp.dot(a, b, preferred_element_type=F32)


def _silu(x):
    return x * jax.nn.sigmoid(x)


VT_ROWS = HEAD_DIM + 16


def _augment_vt(vt):
    extra_shape = vt.shape[:-2] + (VT_ROWS - HEAD_DIM, vt.shape[-1])
    row = lax.broadcasted_iota(jnp.int32, extra_shape, len(extra_shape) - 2)
    extra = jnp.where(row == 0, 1.0, 0.0).astype(vt.dtype)
    parts = []
    for g in range(N_KV_HEADS):
        parts += [vt[..., g * HEAD_DIM:(g + 1) * HEAD_DIM, :], extra]
    return jnp.concatenate(parts, axis=-2)


def _resident(shape, index_map):
    return pl.BlockSpec(shape, index_map, pipeline_mode=pl.Buffered(1))


def _in_proj_kernel(x_ref, nw_ref, w_ref, caw_ref, csw_ref, csb_ref, dtb_ref, bufa_ref, bufs_ref,
                    ya_ref, q_ref, k_ref, v_ref, qi_ref, ki_ref, sgn_ref, z_ref, xs_ref, bm_ref,
                    cm_ref, dt_ref, bufa_out, bufs_out, *rest, tm, key_side):
    if key_side:
        kb_ref, vt_ref, kib_ref, ea, es = rest
    else:
        ea, es = rest
    i = pl.program_id(1)
    u = _rms(x_ref[0], nw_ref[...]).astype(BF16)

    slabs = {}

    def proj(c0, width):
        g0, g1 = next((a, b) for a, b in zip(PROJ_GROUPS, PROJ_GROUPS[1:]) if a <= c0 < b)
        if g0 not in slabs:
            slabs[g0] = _dot(u, w_ref[:, g0:g1])
        return slabs[g0][:, c0 - g0:c0 - g0 + width]

    @pl.when(i == 0)
    def _():
        ea[CARRY - 2:CARRY, :] = bufa_ref[0]
        es[CARRY - 3:CARRY, :] = bufs_ref[0]

    ea[CARRY:CARRY + tm, :] = proj(C_AC, D_CONV_MIX) * proj(C_AH, D_CONV_MIX)
    conv = ea[CARRY - 2:CARRY - 2 + tm, :] * caw_ref[0:1, :]
    for j in range(1, CONV_A_W):
        conv = conv + ea[CARRY - 2 + j:CARRY - 2 + j + tm, :] * caw_ref[j:j + 1, :]
    ya_ref[0] = (proj(C_AB, D_CONV_MIX) * conv).astype(BF16)
    tail_a = ea[CARRY + tm - 2:CARRY + tm, :]
    bufa_out[0] = tail_a
    ea[CARRY - 2:CARRY, :] = tail_a

    q_ref[0] = (proj(C_Q, D_ATT) * (HEAD_DIM ** -0.5 * LOG2E)).astype(BF16)
    k = proj(C_K, N_KV_HEADS * HEAD_DIM)
    v = proj(C_V, N_KV_HEADS * HEAD_DIM)
    kiw = proj(C_KI, LANE)
    k_ref[0] = k
    v_ref[0] = v
    ki_ref[0] = kiw[:, :D_IDX]
    wi = kiw[:, D_IDX:D_IDX + N_IDX_HEADS]
    kiw_rows = kiw if tm >= LANE else jnp.concatenate([kiw, jnp.zeros((LANE - tm, LANE), F32)], axis=0)
    sgn_ref[0] = jnp.where(kiw_rows.T[D_IDX:D_IDX + 8, :] >= 0.0, 1.0, -1.0).astype(F32)
    if key_side:
        kb_ref[0] = k.astype(BF16)
        vt_ref[0] = _augment_vt(v.T).astype(BF16)
        kib_ref[0] = kiw[:, :D_IDX].astype(BF16)
    wabs = jnp.abs(wi) * (D_IDX ** -0.5 * N_IDX_HEADS ** -0.5)
    qi = proj(C_QI, N_IDX_HEADS * D_IDX)
    for h in range(N_IDX_HEADS):
        qi_ref[0, :, h * D_IDX:(h + 1) * D_IDX] = (
            qi[:, h * D_IDX:(h + 1) * D_IDX] * wabs[:, h:h + 1]).astype(BF16)

    z_ref[0] = proj(C_Z, D_SSM)
    es[CARRY:CARRY + tm, :] = proj(C_XBC, D_XBC)
    xc = es[CARRY - 3:CARRY - 3 + tm, :] * csw_ref[0:1, :]
    for j in range(1, SSM_CONV_W):
        xc = xc + es[CARRY - 3 + j:CARRY - 3 + j + tm, :] * csw_ref[j:j + 1, :]
    xc = _silu(xc + csb_ref[...])
    xs_ref[0] = xc[:, :D_SSM]
    bm_ref[0] = xc[:, D_SSM:D_SSM + SSM_GROUPS * SSM_STATE]
    cm_ref[0] = xc[:, D_SSM + SSM_GROUPS * SSM_STATE:]
    tail_s = es[CARRY + tm - 3:CARRY + tm, :]
    bufs_out[0] = tail_s
    es[CARRY - 3:CARRY, :] = tail_s
    dtr = proj(C_DT, LANE) + dtb_ref[...]
    dt_ref[0] = jnp.maximum(dtr, 0.0) + jnp.log1p(jnp.exp(-jnp.abs(dtr)))


def _in_proj(x, nw, w_packed, caw, csw, csb, dtb, bufa, bufs, *, tm, key_side):
    b, t, _ = x.shape
    tq = max(tm, LANE)
    grid = (b, t // tm)
    row = lambda width: pl.BlockSpec((1, tm, width), lambda bi, i: (bi, i, 0))
    full = lambda a: _resident(a.shape, lambda bi, i: (0,) * a.ndim)
    state = lambda r, width: pl.BlockSpec((1, r, width), lambda bi, i: (bi, 0, 0))
    out_shapes = (
        jax.ShapeDtypeStruct((b, t, D_CONV_MIX), BF16),
        jax.ShapeDtypeStruct((b, t, D_ATT), BF16),
        jax.ShapeDtypeStruct((b, t, 128), F32),
        jax.ShapeDtypeStruct((b, t, 128), F32),
        jax.ShapeDtypeStruct((b, t, 256), BF16),
        jax.ShapeDtypeStruct((b, t, D_IDX), F32),
        jax.ShapeDtypeStruct((b, 8, t // tm * tq), F32),
        jax.ShapeDtypeStruct((b, t, D_SSM), F32),
        jax.ShapeDtypeStruct((b, t, D_SSM), F32),
        jax.ShapeDtypeStruct((b, t, 256), F32),
        jax.ShapeDtypeStruct((b, t, 256), F32),
        jax.ShapeDtypeStruct((b, t, LANE), F32),
        jax.ShapeDtypeStruct((b, CONV_A_W - 1, D_CONV_MIX), F32),
        jax.ShapeDtypeStruct((b, SSM_CONV_W - 1, D_XBC), F32),
    )
    out_specs = (row(D_CONV_MIX), row(D_ATT), row(128), row(128), row(256), row(D_IDX),
                 pl.BlockSpec((1, 8, tq), lambda bi, i: (bi, 0, i)),
                 row(D_SSM), row(D_SSM), row(256), row(256), row(LANE),
                 state(CONV_A_W - 1, D_CONV_MIX), state(SSM_CONV_W - 1, D_XBC))
    if key_side:
        out_shapes += (jax.ShapeDtypeStruct((b, t, 128), BF16),
                       jax.ShapeDtypeStruct((b, N_KV_HEADS * VT_ROWS, t), BF16),
                       jax.ShapeDtypeStruct((b, t, D_IDX), BF16))
        out_specs += (row(128), pl.BlockSpec((1, N_KV_HEADS * VT_ROWS, tm), lambda bi, i: (bi, 0, i)), row(D_IDX))
    in_specs = [row(D_MODEL), full(nw), full(w_packed), full(caw), full(csw), full(csb), full(dtb),
                state(CONV_A_W - 1, D_CONV_MIX), state(SSM_CONV_W - 1, D_XBC)]
    return pl.pallas_call(
        functools.partial(_in_proj_kernel, tm=tm, key_side=key_side),
        grid=grid, in_specs=in_specs, out_specs=out_specs, out_shape=out_shapes,
        scratch_shapes=[pltpu.VMEM((CARRY + tm, D_CONV_MIX), F32),
                        pltpu.VMEM((CARRY + tm, D_XBC), F32)],
        compiler_params=pltpu.CompilerParams(
            dimension_semantics=("arbitrary", "arbitrary"), vmem_limit_bytes=VMEM_LIMIT),
        name="in_proj",
    )(x, nw, w_packed, caw, csw, csb, dtb, bufa, bufs)


def _t5_bucket(rel):
    n = jnp.abs(rel)
    large = jnp.full(rel.shape, 8, jnp.int32)
    for brk in (12, 16, 23, 32, 46, 64, 91):
        large = large + jnp.where(n >= brk, 1, 0)
    return jnp.where(rel > 0, N_BUCKETS // 2, 0) + jnp.where(n < 8, n, large)


def _f32_key(x):
    bits = lax.bitcast_convert_type(x, jnp.int32)
    return bits ^ ((bits >> 31) & jnp.int32(0x7FFFFFFF))


def _key_f32(key):
    return lax.bitcast_convert_type(key ^ ((key >> 31) & jnp.int32(0x7FFFFFFF)), F32)


def _dsa_kernel(relb_ref, q_ref, qi_ref, sgn_ref, k_ref, vt_ref, ki_ref, o_ref, sc, bias, sbuf,
                *, q_pos0, l_valid):
    tq = KEY_TILE
    i = pl.program_id(1)
    q0 = q_pos0 + i * tq
    qt = q0 // KEY_TILE
    n0 = jnp.maximum(qt - 1, 0)
    variant = jnp.minimum(qt, 1)
    steps_far = (n0 + BIG - 1) // BIG
    steps_all = (n0 + 2 + BIG - 1) // BIG
    kw = BIG * KEY_TILE
    kf = float(TOPK)

    def step_keys(j):
        return pl.ds(pl.multiple_of(j * kw, kw), kw)

    near_keys = pl.ds(pl.multiple_of(n0 * KEY_TILE, KEY_TILE), 2 * KEY_TILE)

    @pl.when(i == 0)
    def _():
        sc[...] = jnp.full(sc.shape, -jnp.inf, F32)

    @pl.when((i == 0) & (pl.program_id(0) == 0))
    def _():
        kk = lax.broadcasted_iota(jnp.int32, (2 * KEY_TILE, tq), 0)
        r = lax.broadcasted_iota(jnp.int32, (2 * KEY_TILE, tq), 1)
        for var in range(2):
            bucket = _t5_bucket(kk - var * KEY_TILE - r)
            for h in range(N_HEADS):
                tab = jnp.zeros((2 * KEY_TILE, tq), F32)
                for bk in range(N_BUCKETS):
                    tab = jnp.where(bucket == bk, relb_ref[bk, h], tab)
                g, j = divmod(h, GRP)
                bias[var, g, :, j * tq:(j + 1) * tq] = (tab - relb_ref[N_BUCKETS // 2 - 1, h]) * LOG2E

    qit = qi_ref[0].astype(F32).T
    qis = jnp.concatenate([qit[h * D_IDX:(h + 1) * D_IDX, :] for h in range(N_IDX_HEADS)],
                          axis=1).astype(BF16)
    sgn = sgn_ref[0]

    def scores(keys):
        s = jnp.maximum(_dot(ki_ref[0, keys, :], qis), 0.0)
        tot = s[:, 0:tq] * sgn[0:1, :]
        for h in range(1, N_IDX_HEADS):
            tot = tot + s[:, h * tq:(h + 1) * tq] * sgn[h:h + 1, :]
        return tot

    zero = jnp.zeros((KEY_TILE, tq), F32)

    tally_init = (jnp.zeros((KEY_TILE, tq), BF16), jnp.zeros((KEY_TILE, tq), BF16),
                  jnp.full((KEY_TILE, tq), float(jnp.finfo(BF16).min), BF16))
    one16, nil16 = jnp.ones((KEY_TILE, tq), BF16), jnp.zeros((KEY_TILE, tq), BF16)

    def tally(c, s):
        ge, gt, mx = c
        s16 = s.astype(BF16)
        return (ge + jnp.where(s16 >= 0.0, one16, nil16),
                gt + jnp.where(s16 >= jnp.asarray(F32_TINY, BF16), one16, nil16), jnp.maximum(mx, s16))

    def tally_totals(c):
        ge, gt, mx = c
        return (col_sum(ge.astype(F32)), col_sum(gt.astype(F32)),
                jnp.max(mx.astype(F32), axis=0, keepdims=True))

    def score_far(j, c):
        s = scores(step_keys(j))
        for k in range(BIG):
            t = j * BIG + k
            tile = jnp.where(t < n0, s[k * KEY_TILE:(k + 1) * KEY_TILE, :], -jnp.inf)
            sc[t] = tile
            c = tally(c, tile)
        return c

    far_pairs = steps_far // 2
    c = lax.fori_loop(0, far_pairs, lambda j, c: score_far(2 * j + 1, score_far(2 * j, c)),
                      tally_init)
    c = lax.fori_loop(2 * far_pairs, steps_far, score_far, c)
    s_near = scores(near_keys)
    kpos = n0 * KEY_TILE + lax.broadcasted_iota(jnp.int32, (2 * KEY_TILE, tq), 0)
    qpos = q0 + lax.broadcasted_iota(jnp.int32, (2 * KEY_TILE, tq), 1)
    adm = (kpos < l_valid) & ((kpos >> CHUNK_SHIFT) <= (qpos >> CHUNK_SHIFT))
    s_near = jnp.where(adm, s_near, -jnp.inf)
    sc[n0] = s_near[:KEY_TILE, :]
    sc[n0 + 1] = s_near[KEY_TILE:, :]
    for k in range(BIG - 1):
        @pl.when(n0 + 2 + k < steps_all * BIG)
        def _():
            sc[n0 + 2 + k] = jnp.full((KEY_TILE, tq), -jnp.inf, F32)
    score_tally = tally(tally(c, s_near[:KEY_TILE, :]), s_near[KEY_TILE:, :])


    def col_sum(acc):
        parts = [acc[r:r + 8, :] for r in range(0, KEY_TILE, 8)]
        while len(parts) > 1:
            parts = [parts[r] + parts[r + 1] for r in range(0, len(parts), 2)]
        return jnp.sum(parts[0], axis=0, keepdims=True)

    def counter(value_at):
        def count_ge(t):
            tb = jnp.broadcast_to(t, (KEY_TILE, tq))

            def body(j, acc):
                for k in range(BIG):
                    acc = acc + jnp.where(value_at(j * BIG + k) >= tb, 1.0, 0.0)
                return acc
            return col_sum(lax.fori_loop(0, steps_all, body, zero))
        return count_ge

    def search(count_ge, kq, nfin, ge0, gt0, rmax, active, blind_steps):
        small = (nfin <= kq) | (active == 0.0)
        tie0 = (~small) & (gt0 < kq) & (ge0 >= kq)
        pos = (~small) & (gt0 >= kq)
        neg = (~small) & (ge0 < kq)
        log_kq = jnp.log(kq)
        lo0 = jnp.where(pos, F32_TINY, -F32_MAX)
        flo0 = jnp.where(pos, gt0, nfin)
        hi0 = jnp.where(neg, 0.0, jnp.inf)
        fhi0 = jnp.where(neg, ge0, 0.0)
        done0 = jnp.where(small | tie0 | (pos & (gt0 == kq)), 1.0, 0.0)
        thr0 = jnp.where(small, -F32_MAX, jnp.where(tie0, 0.0, F32_TINY))
        quota0 = jnp.where(tie0, kq - gt0, NO_QUOTA)

        def step(it, state):
            lo, flo, hi, fhi, done, thr, quota, adjacent = state
            klo, khi = _f32_key(lo), _f32_key(hi)
            la = jnp.log(flo)
            frac = (la - log_kq) / (la - jnp.log(jnp.maximum(fhi, 0.5)))
            interp = lo + (hi - lo) * jnp.where(it % 4 == 1, 0.5, frac)
            t_up = lo + (rmax - lo) * 0.5
            t_dn = hi - jnp.maximum(jnp.maximum(jnp.abs(rmax - hi), jnp.abs(hi)), F32_TINY)
            open_hi = hi == jnp.inf
            open_lo = lo <= -F32_MAX
            t = jnp.where(open_hi, t_up, jnp.where(open_lo, t_dn, interp))
            tk = jnp.minimum(jnp.maximum(_f32_key(t), klo + 1), khi - 1)
            tk = jnp.where(it % 4 == 3, klo + lax.shift_right_logical(khi - klo, 1), tk)
            t = _key_f32(tk)
            f = count_ge(t)
            act = done == 0.0
            hit = act & (f == kq)
            up = act & (f > kq)
            dn = act & (f < kq)
            lo = jnp.where(up, t, lo)
            flo = jnp.where(up, f, flo)
            hi = jnp.where(dn, t, hi)
            fhi = jnp.where(dn, f, fhi)
            adj = act & (~hit) & (_f32_key(lo) + 1 == _f32_key(hi))
            thr = jnp.where(hit, t, jnp.where(adj, lo, thr))
            quota = jnp.where(adj, kq - fhi, quota)
            done = jnp.where(hit | adj, 1.0, done)
            return lo, flo, hi, fhi, done, thr, quota, jnp.where(adj, 1.0, adjacent)

        state = (lo0, flo0, hi0, fhi0, done0, thr0, quota0, jnp.zeros((1, tq), F32))
        if blind_steps:
            state = lax.fori_loop(0, blind_steps, step, state)

        def cond(c):
            return (c[0] < MAX_SEARCH_STEPS) & (c[1] > 0)

        def body(c):
            state = step(c[0], c[2])
            return c[0] + 1, jnp.sum(1.0 - state[4]), state

        _, _, state = lax.while_loop(cond, body, (jnp.int32(blind_steps), jnp.sum(1.0 - state[4]), state))
        _, flo, _, fhi, _, thr, quota, adjacent = state
        return thr, quota, adjacent, flo, fhi

    qrow = q0 + lax.broadcasted_iota(jnp.int32, (1, tq), 1)
    nadm = jnp.minimum(((qrow >> CHUNK_SHIFT) + 1) * CHUNK, l_valid).astype(F32)
    thr, quota, adjacent, flo, fhi = search(
        counter(lambda t: sc[t]), jnp.full((1, tq), kf, F32), nadm,
        *tally_totals(score_tally), jnp.ones((1, tq), F32), BLIND_SEARCH_STEPS)
    thr_b = jnp.broadcast_to(thr, (tq, KEY_TILE))
    quota_b = jnp.broadcast_to(quota, (tq, KEY_TILE))
    has_ties = jnp.min(quota) < NO_QUOTA
    any_adjacent = jnp.max(adjacent) > 0.0

    @pl.when(jnp.logical_not(has_ties))
    def _():
        def body(j, c):
            for k in range(BIG):
                t = j * BIG + k
                sc[t] = jnp.where(sc[t] >= thr_b, 0.0, NEG)
            return c
        lax.fori_loop(0, steps_all, body, 0)

    def rank_pass(is_tie, is_sure, quota_of_ties):
        ri = lax.broadcasted_iota(jnp.int32, (KEY_TILE, KEY_TILE), 0)
        ci = lax.broadcasted_iota(jnp.int32, (KEY_TILE, KEY_TILE), 1)
        lower = jnp.where(ri >= ci, 1.0, 0.0).astype(BF16)

        def rank_tiles(first, n, seen):
            for k in range(n):
                t = first + k
                s = sc[t]
                tie = jnp.where(is_tie(s), 1.0, 0.0)
                rank = seen + _dot(lower, tie.astype(BF16))
                keep = is_sure(s) | ((tie > 0.0) & (rank <= quota_of_ties))
                sc[t] = jnp.where(keep, 0.0, NEG)
                seen = seen + jnp.sum(tie, axis=0, keepdims=True)
            return seen

        pairs = steps_all // 2
        seen = lax.fori_loop(0, pairs, lambda j, c: rank_tiles(j * 2 * BIG, 2 * BIG, c),
                             jnp.zeros((1, tq), F32))
        lax.fori_loop(2 * pairs, steps_all, lambda j, c: rank_tiles(j * BIG, BIG, c), seen)

    @pl.when(has_ties & jnp.logical_not(any_adjacent))
    def _():
        rank_pass(lambda s: s == thr_b, lambda s: s > thr_b, quota_b)

    @pl.when(any_adjacent)
    def _():
        adj_b = jnp.broadcast_to(adjacent, (tq, KEY_TILE)) > 0.0
        next_b = jnp.broadcast_to(_key_f32(_f32_key(thr) + 1), (tq, KEY_TILE))

        def in_cell(s):
            return adj_b & (s >= thr_b) & (s < next_b)

        def offset(s):
            return jnp.where(in_cell(s), s - thr_b, -jnp.inf)

        def tally_offsets(j, c):
            for k in range(BIG):
                c = tally(c, offset(sc[j * BIG + k]))
            return c

        in_cells, above_thr, max_offset = tally_totals(lax.fori_loop(0, steps_all, tally_offsets, tally_init))
        thr2, quota2, _, _, _ = search(
            counter(lambda t: offset(sc[t])), jnp.where(adjacent > 0.0, quota, 1.0), in_cells, in_cells,
            above_thr, max_offset, adjacent, 0)
        thr2_b = jnp.broadcast_to(thr2, (tq, KEY_TILE))
        quota2_b = jnp.broadcast_to(quota2, (tq, KEY_TILE))
        plain_b = jnp.logical_not(adj_b)
        rank_pass(lambda s: (plain_b & (s == thr_b)) | (offset(s) == thr2_b),
                  lambda s: (plain_b & (s > thr_b)) | (adj_b & (s >= next_b)) | (offset(s) > thr2_b),
                  jnp.where(adj_b, quota2_b, quota_b))

    qt_all = q_ref[0].astype(F32).T
    zeros_half = jnp.zeros((HEAD_DIM, GRP * tq), F32)
    qa = []
    for g in range(N_KV_HEADS):
        piece = jnp.concatenate(
            [qt_all[(g * GRP + j) * HEAD_DIM:(g * GRP + j + 1) * HEAD_DIM, :] for j in range(GRP)],
            axis=1)
        halves = [piece, zeros_half] if g == 0 else [zeros_half, piece]
        qa.append(jnp.concatenate(halves, axis=0).astype(BF16))

    def logits(slot, keys, pens, m_old, extra=None):
        pen = jnp.concatenate(pens, axis=0)
        pen = jnp.concatenate([pen] * GRP, axis=1)
        kblk = k_ref[0, keys, :]
        m_new = []
        for g in range(N_KV_HEADS):
            s = _dot(kblk, qa[g]) + pen
            if extra is not None:
                s = s + extra[g]
            sbuf[slot, g, 0:pen.shape[0], :] = s
            m_new.append(jnp.maximum(m_old[g], jnp.max(s, axis=0, keepdims=True)))
        return tuple(m_new)

    def accumulate(slot, keys, nk, m_old, m_new, accs):
        out = []
        for g in range(N_KV_HEADS):
            p = jnp.exp2(sbuf[slot, g, 0:nk, :] - m_new[g])
            out.append(jnp.exp2(m_old[g] - m_new[g]) * accs[g]
                       + _dot(vt_ref[0, g * VT_ROWS:(g + 1) * VT_ROWS, keys], p.astype(BF16)))
        return tuple(out)

    last_far = jnp.maximum(steps_far - 1, 0)

    def far_keys(j):
        return step_keys(jnp.minimum(j, last_far))

    def far_pens(j):
        return [jnp.where(j * BIG + k < n0, sc[jnp.minimum(j * BIG + k, n0)], NEG) for k in range(BIG)]

    m_init = tuple(jnp.full((1, GRP * tq), NEG, F32) for _ in range(N_KV_HEADS))
    acc_init = tuple(jnp.zeros((VT_ROWS, GRP * tq), F32) for _ in range(N_KV_HEADS))
    m_first = logits(0, far_keys(0), far_pens(0), m_init)

    def stage_pair(i, c):
        m_old, m_cur, accs = c
        a = 2 * i + 1
        m_a = logits(1, far_keys(a), far_pens(a), m_cur)
        accs = accumulate(0, far_keys(a - 1), kw, m_old, m_cur, accs)
        m_b = logits(0, far_keys(a + 1), far_pens(a + 1), m_a)
        accs = accumulate(1, far_keys(a), kw, m_cur, m_a, accs)
        return m_a, m_b, accs

    pairs = steps_far // 2
    m_old, m_cur, accs = lax.fori_loop(0, pairs, stage_pair, (m_init, m_first, acc_init))
    m_last = logits(1, near_keys, [sc[n0], sc[n0 + 1]], m_cur, extra=(bias[variant, 0], bias[variant, 1]))
    accs = accumulate(0, far_keys(2 * pairs), kw, m_old, m_cur, accs)
    accs = accumulate(1, near_keys, 2 * KEY_TILE, m_cur, m_last, accs)
    heads = []
    for g in range(N_KV_HEADS):
        acc = accs[g]
        og = acc[:HEAD_DIM, :] / acc[HEAD_DIM:HEAD_DIM + 1, :]
        heads += [og[:, j * tq:(j + 1) * tq] for j in range(GRP)]
    o_ref[0] = jnp.concatenate(heads, axis=0).T.astype(BF16)


def _dsa(rel_bias, q, qi, sgn, k, vt, ki, *, q_pos0, l_valid):
    b, t, _ = q.shape
    lk = k.shape[1]
    tq = KEY_TILE
    assert lk % (BIG * KEY_TILE) == 0 and q_pos0 % KEY_TILE == 0 and t % tq == 0
    assert q_pos0 + t <= lk
    nt = lk // KEY_TILE
    row = lambda width: pl.BlockSpec((1, tq, width), lambda bi, i: (bi, i, 0))
    keys = lambda shape: (_resident if b == 1 else pl.BlockSpec)(shape, lambda bi, i: (bi, 0, 0))
    return pl.pallas_call(
        functools.partial(_dsa_kernel, q_pos0=q_pos0, l_valid=l_valid),
        grid=(b, t // tq),
        in_specs=[pl.BlockSpec(memory_space=pltpu.SMEM),
                  row(D_ATT), row(N_IDX_HEADS * D_IDX),
                  pl.BlockSpec((1, 8, tq), lambda bi, i: (bi, 0, i)),
                  keys((1, lk, N_KV_HEADS * HEAD_DIM)), keys((1, N_KV_HEADS * VT_ROWS, lk)),
                  keys((1, lk, D_IDX))],
        out_specs=row(D_ATT),
        out_shape=jax.ShapeDtypeStruct((b, t, D_ATT), BF16),
        scratch_shapes=[pltpu.VMEM((nt, KEY_TILE, tq), F32),
                        pltpu.VMEM((2, N_KV_HEADS, 2 * KEY_TILE, GRP * tq), F32),
                        pltpu.VMEM((2, N_KV_HEADS, BIG * KEY_TILE, GRP * tq), F32)],
        compiler_params=pltpu.CompilerParams(
            dimension_semantics=("arbitrary", "arbitrary"), vmem_limit_bytes=VMEM_LIMIT),
        name="dsa",
    )(rel_bias, q, qi, sgn, k, vt, ki)


def _split3(x):
    hi = x.astype(BF16)
    r = x - hi.astype(F32)
    mid = r.astype(BF16)
    lo = (r - mid.astype(F32)).astype(BF16)
    return hi, mid, lo


def _ssd_kernel(xs_ref, bm_ref, cm_ref, dt_ref, z_ref, alog_ref, dsk_ref, nw_ref, h0_ref,
                y_ref, h_ref, ht, *, q_in):
    c = pl.program_id(1)
    nc = pl.num_programs(1)

    @pl.when(c == 0)
    def _():
        for hh in range(SSM_HEADS):
            ht[hh] = h0_ref[0, hh].T

    def rows(ref):
        x = ref[0]
        if q_in == SSD_Q:
            return x
        return jnp.concatenate([x, jnp.zeros((SSD_Q - q_in, x.shape[1]), x.dtype)], axis=0)

    xs, bm, cm, dt, z = rows(xs_ref), rows(bm_ref), rows(cm_ref), rows(dt_ref), rows(z_ref)
    a = -jnp.exp(alog_ref[...])
    da = dt * a
    ri = lax.broadcasted_iota(jnp.int32, (SSD_Q, SSD_Q), 0)
    ci = lax.broadcasted_iota(jnp.int32, (SSD_Q, SSD_Q), 1)
    causal = ri >= ci
    tril = jnp.where(causal, 1.0, 0.0).astype(BF16)
    hi, mid, lo = _split3(da)
    acum = _dot(tril, hi) + _dot(tril, mid) + _dot(tril, lo)
    acum_t = acum.T

    for g in range(SSM_GROUPS):
        bm_g = bm[:, g * SSM_STATE:(g + 1) * SSM_STATE]
        cm_g = cm[:, g * SSM_STATE:(g + 1) * SSM_STATE].astype(BF16)
        bm_t = bm_g.T.astype(BF16)
        cb = _dot(cm_g, bm_t)
        gs = []
        ssq = jnp.zeros((SSD_Q, 1), F32)
        for r in range(SSM_HPG):
            hh = g * SSM_HPG + r
            lanes = slice(hh * SSM_HEAD_DIM, (hh + 1) * SSM_HEAD_DIM)
            xh = xs[:, lanes]
            dl = DT_LANE0 + hh
            col = acum[:, dl:dl + 1]
            rowv = acum_t[dl:dl + 1, :]
            decay = jnp.exp(jnp.where(causal, col - rowv, NEG))
            xdt = xh * dt[:, dl:dl + 1]
            y = _dot((cb * decay).astype(BF16), xdt.astype(BF16))
            h_prev = ht[hh]
            y = y + jnp.exp(col) * _dot(cm_g, h_prev.astype(BF16))
            last = acum[SSD_Q - 1:SSD_Q, dl:dl + 1]
            xw = (xdt * jnp.exp(last - col)).astype(BF16)
            ht[hh] = h_prev * jnp.exp(last) + _dot(bm_t, xw)
            y = y + xh * dsk_ref[:, lanes]
            gate = y * _silu(z[:, lanes])
            ssq = ssq + jnp.sum(gate * gate, axis=1, keepdims=True)
            gs.append(gate)
        scale = lax.rsqrt(ssq / float(SSM_HPG * SSM_HEAD_DIM) + NORM_EPS)
        for r in range(SSM_HPG):
            hh = g * SSM_HPG + r
            lanes = slice(hh * SSM_HEAD_DIM, (hh + 1) * SSM_HEAD_DIM)
            y_ref[0, :, lanes] = (gs[r] * scale * nw_ref[:, lanes])[:q_in].astype(BF16)

    @pl.when(c == nc - 1)
    def _():
        for hh in range(SSM_HEADS):
            h_ref[0, hh] = ht[hh].T


def _ssd(xs, bm, cm, dt, z, alog, dsk, nw, h0, *, q_in):
    b, t, _ = xs.shape
    row = lambda width: pl.BlockSpec((1, q_in, width), lambda bi, c: (bi, c, 0))
    full = lambda a: _resident(a.shape, lambda bi, c: (0,) * a.ndim)
    hspec = pl.BlockSpec((1, SSM_HEADS, SSM_HEAD_DIM, SSM_STATE), lambda bi, c: (bi, 0, 0, 0))
    return pl.pallas_call(
        functools.partial(_ssd_kernel, q_in=q_in),
        grid=(b, t // q_in),
        in_specs=[row(D_SSM), row(256), row(256), row(LANE), row(D_SSM),
                  full(alog), full(dsk), full(nw), hspec],
        out_specs=(row(D_SSM), hspec),
        out_shape=(jax.ShapeDtypeStruct((b, t, D_SSM), BF16),
                   jax.ShapeDtypeStruct((b, SSM_HEADS, SSM_HEAD_DIM, SSM_STATE), F32)),
        scratch_shapes=[pltpu.VMEM((SSM_HEADS, SSM_STATE, SSM_HEAD_DIM), F32)],
        compiler_params=pltpu.CompilerParams(
            dimension_semantics=("arbitrary", "arbitrary"), vmem_limit_bytes=VMEM_LIMIT),
        name="ssd",
    )(xs, bm, cm, dt, z, alog, dsk, nw, h0)


def _mix_ffn_kernel(x_ref, ya_ref, yb_ref, yc_ref, wo_ref, npost_ref, nfpre_ref, wg_ref, wu_ref,
                    wd_ref, cfw_ref, cfb_ref, nfpost_ref, buff_ref, o_ref, buff_out, eg, *, seg):
    i = pl.program_id(1)
    nseq = eg.shape[0]

    @pl.when(i == 0)
    def _():
        for s in range(nseq):
            eg[s, CARRY - 2:CARRY, :] = buff_ref[0, s]

    mix = _dot(jnp.concatenate([ya_ref[0], yb_ref[0], yc_ref[0]], axis=1), wo_ref[...])
    x1 = x_ref[0] + _rms(mix, npost_ref[...])
    u = _rms(x1, nfpre_ref[...]).astype(BF16)
    g = _dot(u, wg_ref[...])
    convs = []
    for s in range(nseq):
        eg[s, CARRY:CARRY + seg, :] = g[s * seg:(s + 1) * seg, :]
        gc = eg[s, CARRY - 2:CARRY - 2 + seg, :] * cfw_ref[0:1, :]
        for j in range(1, CONV_F_W):
            gc = gc + eg[s, CARRY - 2 + j:CARRY - 2 + j + seg, :] * cfw_ref[j:j + 1, :]
        convs.append(gc)
        tail = eg[s, CARRY + seg - 2:CARRY + seg, :]
        buff_out[0, s] = tail
        eg[s, CARRY - 2:CARRY, :] = tail
    gc = convs[0] if nseq == 1 else jnp.concatenate(convs, axis=0)
    hid = (_silu(gc + cfb_ref[...]) * _dot(u, wu_ref[...])).astype(BF16)
    o_ref[0] = x1 + _rms(_dot(hid, wd_ref[...]), nfpost_ref[...])


def _mix_ffn(x, ya, yb, yc, wo, npost, nfpre, wg, wu, wd, cfw, cfb, nfpost, buff, *, tm):
    b, t, _ = x.shape
    seg = min(t, tm)
    nseq = tm // seg
    assert tm % seg == 0 and b % nseq == 0 and (nseq == 1 or seg == t)
    tiles = lambda a: a.reshape(b // nseq, nseq * t, a.shape[-1])
    row = lambda width: pl.BlockSpec((1, tm, width), lambda bi, i: (bi, i, 0))
    full = lambda a: _resident(a.shape, lambda bi, i: (0,) * a.ndim)
    state = pl.BlockSpec((1, nseq, CONV_F_W - 1, D_FF), lambda bi, i: (bi, 0, 0, 0))
    x_new, buff_new = pl.pallas_call(
        functools.partial(_mix_ffn_kernel, seg=seg),
        grid=(b // nseq, nseq * t // tm),
        in_specs=[row(D_MODEL), row(D_CONV_MIX), row(D_ATT), row(D_SSM), full(wo), full(npost),
                  full(nfpre), full(wg), full(wu), full(wd), full(cfw), full(cfb), full(nfpost),
                  state],
        out_specs=(row(D_MODEL), state),
        out_shape=(jax.ShapeDtypeStruct((b // nseq, nseq * t, D_MODEL), F32),
                   jax.ShapeDtypeStruct((b // nseq, nseq, CONV_F_W - 1, D_FF), F32)),
        scratch_shapes=[pltpu.VMEM((nseq, CARRY + seg, D_FF), F32)],
        compiler_params=pltpu.CompilerParams(
            dimension_semantics=("arbitrary", "arbitrary"), vmem_limit_bytes=VMEM_LIMIT),
        name="mix_ffn",
    )(tiles(x), tiles(ya), tiles(yb), tiles(yc), wo, npost, nfpre, wg, wu, wd, cfw, cfb, nfpost,
      buff.reshape(b // nseq, nseq, CONV_F_W - 1, D_FF))
    return x_new.reshape(b, t, D_MODEL), buff_new.reshape(b, CONV_F_W - 1, D_FF)


def _pack_keys_kernel(ck_ref, cv_ref, cki_ref, k_ref, v_ref, ki_ref, kb_ref, vt_ref, kib_ref, *, past_steps):
    j = pl.program_id(1)
    rows = kb_ref.shape[1]
    cached = j < past_steps

    def with_new(cache, new):
        new = jnp.concatenate([new, jnp.zeros((rows - new.shape[0], new.shape[1]), F32)], axis=0)
        return jnp.where(cached, cache, new)

    heads = lambda ref: [ref[0, 0, g] for g in range(N_KV_HEADS)]
    k_cache = jnp.concatenate([h.T for h in heads(ck_ref)], axis=1)
    kb_ref[0] = with_new(k_cache, k_ref[0]).astype(BF16)
    v_new = jnp.concatenate([v_ref[0], jnp.zeros((rows - v_ref.shape[1], v_ref.shape[2]), F32)], axis=0)
    vt = jnp.where(cached, jnp.concatenate(heads(cv_ref), axis=0), v_new.T)
    vt_ref[0] = _augment_vt(vt).astype(BF16)
    kib_ref[0] = with_new(cki_ref[0, 0].T, ki_ref[0]).astype(BF16)


def _pack_keys(layer, cache_k, cache_v, cache_kidx, k, v, ki):
    b, past = cache_k.shape[1:3]
    t = k.shape[1]
    rows = 2 * BIG * KEY_TILE
    assert past % rows == 0 and t <= rows
    past_steps = past // rows
    lk = past + rows
    cache = lambda *major: pl.BlockSpec(
        (1, 1) + major + (rows,),
        lambda bi, j: (layer, bi) + (0,) * len(major) + (jnp.minimum(j, past_steps - 1),))
    cache_k, cache_v = (jnp.transpose(c, (0, 1, 3, 4, 2)) for c in (cache_k, cache_v))
    cache_kidx = jnp.transpose(cache_kidx, (0, 1, 3, 2))
    new = lambda width: pl.BlockSpec((1, t, width), lambda bi, j: (bi, 0, 0))
    out = lambda width: pl.BlockSpec((1, rows, width), lambda bi, j: (bi, j, 0))
    return pl.pallas_call(
        functools.partial(_pack_keys_kernel, past_steps=past_steps),
        grid=(b, past_steps + 1),
        in_specs=[cache(N_KV_HEADS, HEAD_DIM), cache(N_KV_HEADS, HEAD_DIM), cache(D_IDX),
                  new(128), new(128), new(D_IDX)],
        out_specs=(out(128), pl.BlockSpec((1, N_KV_HEADS * VT_ROWS, rows), lambda bi, j: (bi, 0, j)),
                   out(D_IDX)),
        out_shape=(jax.ShapeDtypeStruct((b, lk, 128), BF16),
                   jax.ShapeDtypeStruct((b, N_KV_HEADS * VT_ROWS, lk), BF16),
                   jax.ShapeDtypeStruct((b, lk, D_IDX), BF16)),
        compiler_params=pltpu.CompilerParams(
            dimension_semantics=("arbitrary", "arbitrary"), vmem_limit_bytes=VMEM_LIMIT),
        name="pack_keys",
    )(cache_k, cache_v, cache_kidx, k, v, ki)


def _layer(x, p, rel_bias, buf_a, buf_s, h0, buf_f, kv_past, *, tm, tm_ffn):
    b, t, _ = x.shape
    outs = _in_proj(x, p['norm_mix_pre'], p['w_in'], p['conv_a_w'], p['conv_ssm_w'], p['conv_ssm_b'],
                    p['dt_bias'], buf_a, buf_s, tm=tm, key_side=kv_past is None)
    (ya, q, k, v, qi, ki, sgn, z, xs, bm, cm, dt, buf_a_new, buf_s_new) = outs[:14]
    if kv_past is None:
        kb, vt, kib = outs[14:]
        yb = _dsa(rel_bias, q, qi, sgn, kb, vt, kib, q_pos0=0, l_valid=t)
    else:
        layer, cache_k, cache_v, cache_kidx = kv_past
        past = cache_k.shape[2]
        qpad = lambda a: jnp.pad(a, ((0, 0), (0, -t % KEY_TILE), (0, 0)))
        kb, vt, kib = _pack_keys(layer, cache_k, cache_v, cache_kidx, k, v, ki)
        yb = _dsa(rel_bias, qpad(q), qpad(qi), sgn, kb, vt, kib, q_pos0=past, l_valid=past + t)[:, :t]
    yc, h_new = _ssd(xs, bm, cm, dt, z, p['a_log'], p['d_skip'], p['ssm_norm_w'],
                     h0.reshape(b, SSM_HEADS, SSM_HEAD_DIM, SSM_STATE), q_in=min(t, SSD_Q))
    x_new, buf_f_new = _mix_ffn(x, ya, yb, yc, p['w_out'], p['norm_mix_post'], p['norm_ffn_pre'],
                                p['w_gate'], p['w_up'], p['w_down'], p['conv_ffn_w'],
                                p['conv_ffn_b'], p['norm_ffn_post'], buf_f, tm=tm_ffn)
    st = (k.reshape(b, t, N_KV_HEADS, HEAD_DIM), v.reshape(b, t, N_KV_HEADS, HEAD_DIM), ki,
          buf_a_new, buf_s_new, h_new.reshape(b, SSM_GROUPS, SSM_HPG, SSM_HEAD_DIM, SSM_STATE),
          buf_f_new)
    return x_new, st


def _pack_w_in_kernel(wt_ref, o_ref):
    for c, c0 in enumerate(W_IN_TILE_COLS):
        o_ref[:, c * LANE:(c + 1) * LANE] = wt_ref[c0:c0 + LANE, :].T.astype(BF16)


def _pack_w_in(wt):
    return pl.pallas_call(
        _pack_w_in_kernel,
        out_shape=jax.ShapeDtypeStruct((wt.shape[1], D_IN_PACKED), BF16),
        compiler_params=pltpu.CompilerParams(vmem_limit_bytes=VMEM_LIMIT),
        name="pack_w_in",
    )(wt)


def _dt_lanes(v):
    return jnp.pad(v, (DT_LANE0, 0))[None, :]


def kernel(x_prompt, x_sample, cache_k, cache_v, cache_kidx, state_conv_a, state_conv_ssm, state_ssm, state_conv_ffn, rel_bias, norm_mix_pre, norm_mix_post, norm_ffn_pre, norm_ffn_post, w_in, conv_a_w, conv_ssm_w, conv_ssm_b, dt_bias, a_log, d_skip, ssm_norm_w, w_out, w_gate, w_up, conv_ffn_w, conv_ffn_b, w_down):
    depth = w_in.shape[0]
    bp = x_prompt.shape[0]
    yp, ys = x_prompt, x_sample
    outs_p, outs_s = [], []
    for l in range(depth):
        p = {'norm_mix_pre': norm_mix_pre[l][None], 'norm_mix_post': norm_mix_post[l][None],
             'norm_ffn_pre': norm_ffn_pre[l][None], 'norm_ffn_post': norm_ffn_post[l][None],
             'w_in': _pack_w_in(jnp.swapaxes(w_in[l], 0, 1)), 'conv_a_w': conv_a_w[l], 'conv_ssm_w': conv_ssm_w[l],
             'conv_ssm_b': conv_ssm_b[l][None], 'dt_bias': _dt_lanes(dt_bias[l]),
             'a_log': _dt_lanes(a_log[l]), 'd_skip': jnp.repeat(d_skip[l], SSM_HEAD_DIM)[None],
             'ssm_norm_w': ssm_norm_w[l][None], 'w_out': w_out[l].astype(BF16),
             'w_gate': w_gate[l].astype(BF16), 'w_up': w_up[l].astype(BF16),
             'conv_ffn_w': conv_ffn_w[l], 'conv_ffn_b': conv_ffn_b[l][None],
             'w_down': w_down[l].astype(BF16)}
        yp, st_p = _layer(
            yp, p, rel_bias,
            jnp.zeros((bp, CONV_A_W - 1, D_CONV_MIX), F32),
            jnp.zeros((bp, SSM_CONV_W - 1, D_XBC), F32),
            jnp.zeros((bp, SSM_GROUPS, SSM_HPG, SSM_HEAD_DIM, SSM_STATE), F32),
            jnp.zeros((bp, CONV_F_W - 1, D_FF), F32),
            None, tm=256, tm_ffn=512)
        ys, st_s = _layer(
            ys, p, rel_bias, state_conv_a[l], state_conv_ssm[l], state_ssm[l], state_conv_ffn[l],
            (l, cache_k, cache_v, cache_kidx), tm=ys.shape[1], tm_ffn=ys.shape[0] * ys.shape[1])
        outs_p.append(st_p)
        outs_s.append(st_s)

    def stack(outs, i):
        return jnp.stack([o[i] for o in outs], axis=0)

    res = [yp, ys]
    for i in range(7):
        res.append(stack(outs_p, i))
        res.append(stack(outs_s, i))
    return tuple(res)
```

```python
import functools

import jax
import jax.numpy as jnp
from jax import lax
from jax.experimental import pallas as pl
from jax.experimental.pallas import tpu as pltpu

F32 = jnp.float32
BF16 = jnp.bfloat16

D_MODEL = 1024
CHUNK = 64
CHUNK_SHIFT = CHUNK.bit_length() - 1
assert 1 << CHUNK_SHIFT == CHUNK
D_CONV_MIX = 256
CONV_A_W = 3
N_HEADS = 6
N_KV_HEADS = 2
HEAD_DIM = 64
D_ATT = N_HEADS * HEAD_DIM
GRP = N_HEADS // N_KV_HEADS
N_IDX_HEADS = 4
D_IDX = 64
TOPK = 256
N_BUCKETS = 32
SSM_HEADS = 6
SSM_HEAD_DIM = 64
D_SSM = SSM_HEADS * SSM_HEAD_DIM
SSM_GROUPS = 2
SSM_HPG = SSM_HEADS // SSM_GROUPS
SSM_STATE = 128
SSM_CONV_W = 4
D_XBC = D_SSM + 2 * SSM_GROUPS * SSM_STATE
D_FF = 2816
CONV_F_W = 3
NORM_EPS = 1e-6

LANE = 128
CARRY = 8
SSD_Q = 128
KEY_TILE = 128
BIG = 4
VMEM_LIMIT = 56 * 1024 * 1024
NEG = -1e30
LOG2E = 1.4426950408889634
F32_MAX = 3.4028234663852886e38
F32_TINY = 1.1754943508222875e-38
NO_QUOTA = 1e9
MAX_SEARCH_STEPS = 256
BLIND_SEARCH_STEPS = 15

C_AB, C_AC, C_AH = 0, 256, 512
C_Q = 768
C_K = 1152
C_V = 1280
C_QI = 1408
C_KI = 1664
C_Z = 1792
C_XBC = 2176
C_DT = 3072
D_IN_PACKED = 3200
PROJ_GROUPS = (0, C_Q, C_Z, D_IN_PACKED)
D_IN = 3018
D_IN_HEAD = 1732
DT_LANE0 = LANE - SSM_HEADS
W_IN_TILE_COLS = ([c * LANE for c in range(C_Z // LANE)]
                  + [D_IN_HEAD + c * LANE for c in range((C_DT - C_Z) // LANE)] + [D_IN - LANE])


def _rms(x, w):
    return x * lax.rsqrt(jnp.mean(x * x, axis=-1, keepdims=True) + NORM_EPS) * w


def _dot(a, b):
    return jnp.dot(a, b, preferred_element_type=F32)


def _silu(x):
    return x * jax.nn.sigmoid(x)


VT_ROWS = HEAD_DIM + 16


def _augment_vt(vt):
    extra_shape = vt.shape[:-2] + (VT_ROWS - HEAD_DIM, vt.shape[-1])
    row = lax.broadcasted_iota(jnp.int32, extra_shape, len(extra_shape) - 2)
    extra = jnp.where(row == 0, 1.0, 0.0).astype(vt.dtype)
    parts = []
    for g in range(N_KV_HEADS):
        parts += [vt[..., g * HEAD_DIM:(g + 1) * HEAD_DIM, :], extra]
    return jnp.concatenate(parts, axis=-2)


def _resident(shape, index_map):
    return pl.BlockSpec(shape, index_map, pipeline_mode=pl.Buffered(1))


def _in_proj_kernel(x_ref, nw_ref, w_ref, caw_ref, csw_ref, csb_ref, dtb_ref, bufa_ref, bufs_ref,
                    ya_ref, q_ref, k_ref, v_ref, qi_ref, ki_ref, sgn_ref, z_ref, xs_ref, bm_ref,
                    cm_ref, dt_ref, bufa_out, bufs_out, *rest, tm, key_side):
    if key_side:
        kb_ref, vt_ref, kib_ref, ea, es = rest
    else:
        ea, es = rest
    i = pl.program_id(1)
    u = _rms(x_ref[0], nw_ref[...]).astype(BF16)

    slabs = {}

    def proj(c0, width):
        g0, g1 = next((a, b) for a, b in zip(PROJ_GROUPS, PROJ_GROUPS[1:]) if a <= c0 < b)
        if g0 not in slabs:
            slabs[g0] = _dot(u, w_ref[:, g0:g1])
        return slabs[g0][:, c0 - g0:c0 - g0 + width]

    @pl.when(i == 0)
    def _():
        ea[CARRY - 2:CARRY, :] = bufa_ref[0]
        es[CARRY - 3:CARRY, :] = bufs_ref[0]

    ea[CARRY:CARRY + tm, :] = proj(C_AC, D_CONV_MIX) * proj(C_AH, D_CONV_MIX)
    conv = ea[CARRY - 2:CARRY - 2 + tm, :] * caw_ref[0:1, :]
    for j in range(1, CONV_A_W):
        conv = conv + ea[CARRY - 2 + j:CARRY - 2 + j + tm, :] * caw_ref[j:j + 1, :]
    ya_ref[0] = (proj(C_AB, D_CONV_MIX) * conv).astype(BF16)
    tail_a = ea[CARRY + tm - 2:CARRY + tm, :]
    bufa_out[0] = tail_a
    ea[CARRY - 2:CARRY, :] = tail_a

    q_ref[0] = (proj(C_Q, D_ATT) * (HEAD_DIM ** -0.5 * LOG2E)).astype(BF16)
    k = proj(C_K, N_KV_HEADS * HEAD_DIM)
    v = proj(C_V, N_KV_HEADS * HEAD_DIM)
    kiw = proj(C_KI, LANE)
    k_ref[0] = k
    v_ref[0] = v
    ki_ref[0] = kiw[:, :D_IDX]
    wi = kiw[:, D_IDX:D_IDX + N_IDX_HEADS]
    kiw_rows = kiw if tm >= LANE else jnp.concatenate([kiw, jnp.zeros((LANE - tm, LANE), F32)], axis=0)
    sgn_ref[0] = jnp.where(kiw_rows.T[D_IDX:D_IDX + 8, :] >= 0.0, 1.0, -1.0).astype(F32)
    if key_side:
        kb_ref[0] = k.astype(BF16)
        vt_ref[0] = _augment_vt(v.T).astype(BF16)
        kib_ref[0] = kiw[:, :D_IDX].astype(BF16)
    wabs = jnp.abs(wi) * (D_IDX ** -0.5 * N_IDX_HEADS ** -0.5)
    qi = proj(C_QI, N_IDX_HEADS * D_IDX)
    for h in range(N_IDX_HEADS):
        qi_ref[0, :, h * D_IDX:(h + 1) * D_IDX] = (
            qi[:, h * D_IDX:(h + 1) * D_IDX] * wabs[:, h:h + 1]).astype(BF16)

    z_ref[0] = proj(C_Z, D_SSM)
    es[CARRY:CARRY + tm, :] = proj(C_XBC, D_XBC)
    xc = es[CARRY - 3:CARRY - 3 + tm, :] * csw_ref[0:1, :]
    for j in range(1, SSM_CONV_W):
        xc = xc + es[CARRY - 3 + j:CARRY - 3 + j + tm, :] * csw_ref[j:j + 1, :]
    xc = _silu(xc + csb_ref[...])
    xs_ref[0] = xc[:, :D_SSM]
    bm_ref[0] = xc[:, D_SSM:D_SSM + SSM_GROUPS * SSM_STATE]
    cm_ref[0] = xc[:, D_SSM + SSM_GROUPS * SSM_STATE:]
    tail_s = es[CARRY + tm - 3:CARRY + tm, :]
    bufs_out[0] = tail_s
    es[CARRY - 3:CARRY, :] = tail_s
    dtr = proj(C_DT, LANE) + dtb_ref[...]
    dt_ref[0] = jnp.maximum(dtr, 0.0) + jnp.log1p(jnp.exp(-jnp.abs(dtr)))


def _in_proj(x, nw, w_packed, caw, csw, csb, dtb, bufa, bufs, *, tm, key_side):
    b, t, _ = x.shape
    tq = max(tm, LANE)
    grid = (b, t // tm)
    row = lambda width: pl.BlockSpec((1, tm, width), lambda bi, i: (bi, i, 0))
    full = lambda a: _resident(a.shape, lambda bi, i: (0,) * a.ndim)
    state = lambda r, width: pl.BlockSpec((1, r, width), lambda bi, i: (bi, 0, 0))
    out_shapes = (
        jax.ShapeDtypeStruct((b, t, D_CONV_MIX), BF16),
        jax.ShapeDtypeStruct((b, t, D_ATT), BF16),
        jax.ShapeDtypeStruct((b, t, 128), F32),
        jax.ShapeDtypeStruct((b, t, 128), F32),
        jax.ShapeDtypeStruct((b, t, 256), BF16),
        jax.ShapeDtypeStruct((b, t, D_IDX), F32),
        jax.ShapeDtypeStruct((b, 8, t // tm * tq), F32),
        jax.ShapeDtypeStruct((b, t, D_SSM), F32),
        jax.ShapeDtypeStruct((b, t, D_SSM), F32),
        jax.ShapeDtypeStruct((b, t, 256), F32),
        jax.ShapeDtypeStruct((b, t, 256), F32),
        jax.ShapeDtypeStruct((b, t, LANE), F32),
        jax.ShapeDtypeStruct((b, CONV_A_W - 1, D_CONV_MIX), F32),
        jax.ShapeDtypeStruct((b, SSM_CONV_W - 1, D_XBC), F32),
    )
    out_specs = (row(D_CONV_MIX), row(D_ATT), row(128), row(128), row(256), row(D_IDX),
                 pl.BlockSpec((1, 8, tq), lambda bi, i: (bi, 0, i)),
                 row(D_SSM), row(D_SSM), row(256), row(256), row(LANE),
                 state(CONV_A_W - 1, D_CONV_MIX), state(SSM_CONV_W - 1, D_XBC))
    if key_side:
        out_shapes += (jax.ShapeDtypeStruct((b, t, 128), BF16),
                       jax.ShapeDtypeStruct((b, N_KV_HEADS * VT_ROWS, t), BF16),
                       jax.ShapeDtypeStruct((b, t, D_IDX), BF16))
        out_specs += (row(128), pl.BlockSpec((1, N_KV_HEADS * VT_ROWS, tm), lambda bi, i: (bi, 0, i)), row(D_IDX))
    in_specs = [row(D_MODEL), full(nw), full(w_packed), full(caw), full(csw), full(csb), full(dtb),
                state(CONV_A_W - 1, D_CONV_MIX), state(SSM_CONV_W - 1, D_XBC)]
    return pl.pallas_call(
        functools.partial(_in_proj_kernel, tm=tm, key_side=key_side),
        grid=grid, in_specs=in_specs, out_specs=out_specs, out_shape=out_shapes,
        scratch_shapes=[pltpu.VMEM((CARRY + tm, D_CONV_MIX), F32),
                        pltpu.VMEM((CARRY + tm, D_XBC), F32)],
        compiler_params=pltpu.CompilerParams(
            dimension_semantics=("arbitrary", "arbitrary"), vmem_limit_bytes=VMEM_LIMIT),
        name="in_proj",
    )(x, nw, w_packed, caw, csw, csb, dtb, bufa, bufs)


def _t5_bucket(rel):
    n = jnp.abs(rel)
    large = jnp.full(rel.shape, 8, jnp.int32)
    for brk in (12, 16, 23, 32, 46, 64, 91):
        large = large + jnp.where(n >= brk, 1, 0)
    return jnp.where(rel > 0, N_BUCKETS // 2, 0) + jnp.where(n < 8, n, large)


def _f32_key(x):
    bits = lax.bitcast_convert_type(x, jnp.int32)
    return bits ^ ((bits >> 31) & jnp.int32(0x7FFFFFFF))


def _key_f32(key):
    return lax.bitcast_convert_type(key ^ ((key >> 31) & jnp.int32(0x7FFFFFFF)), F32)


def _dsa_kernel(relb_ref, q_ref, qi_ref, sgn_ref, k_ref, vt_ref, ki_ref, o_ref, sc, bias, sbuf,
                *, q_pos0, l_valid):
    tq = KEY_TILE
    i = pl.program_id(1)
    q0 = q_pos0 + i * tq
    qt = q0 // KEY_TILE
    n0 = jnp.maximum(qt - 1, 0)
    variant = jnp.minimum(qt, 1)
    steps_far = (n0 + BIG - 1) // BIG
    steps_all = (n0 + 2 + BIG - 1) // BIG
    kw = BIG * KEY_TILE
    kf = float(TOPK)

    def step_keys(j):
        return pl.ds(pl.multiple_of(j * kw, kw), kw)

    near_keys = pl.ds(pl.multiple_of(n0 * KEY_TILE, KEY_TILE), 2 * KEY_TILE)

    @pl.when(i == 0)
    def _():
        sc[...] = jnp.full(sc.shape, -jnp.inf, F32)

    @pl.when((i == 0) & (pl.program_id(0) == 0))
    def _():
        kk = lax.broadcasted_iota(jnp.int32, (2 * KEY_TILE, tq), 0)
        r = lax.broadcasted_iota(jnp.int32, (2 * KEY_TILE, tq), 1)
        for var in range(2):
            bucket = _t5_bucket(kk - var * KEY_TILE - r)
            for h in range(N_HEADS):
                tab = jnp.zeros((2 * KEY_TILE, tq), F32)
                for bk in range(N_BUCKETS):
                    tab = jnp.where(bucket == bk, relb_ref[bk, h], tab)
                g, j = divmod(h, GRP)
                bias[var, g, :, j * tq:(j + 1) * tq] = (tab - relb_ref[N_BUCKETS // 2 - 1, h]) * LOG2E

    qit = qi_ref[0].astype(F32).T
    qis = jnp.concatenate([qit[h * D_IDX:(h + 1) * D_IDX, :] for h in range(N_IDX_HEADS)],
                          axis=1).astype(BF16)
    sgn = sgn_ref[0]

    def scores(keys):
        s = jnp.maximum(_dot(ki_ref[0, keys, :], qis), 0.0)
        tot = s[:, 0:tq] * sgn[0:1, :]
        for h in range(1, N_IDX_HEADS):
            tot = tot + s[:, h * tq:(h + 1) * tq] * sgn[h:h + 1, :]
        return tot

    zero = jnp.zeros((KEY_TILE, tq), F32)

    tally_init = (jnp.zeros((KEY_TILE, tq), BF16), jnp.zeros((KEY_TILE, tq), BF16),
                  jnp.full((KEY_TILE, tq), float(jnp.finfo(BF16).min), BF16))
    one16, nil16 = jnp.ones((KEY_TILE, tq), BF16), jnp.zeros((KEY_TILE, tq), BF16)

    def tally(c, s):
        ge, gt, mx = c
        s16 = s.astype(BF16)
        return (ge + jnp.where(s16 >= 0.0, one16, nil16),
                gt + jnp.where(s16 >= jnp.asarray(F32_TINY, BF16), one16, nil16), jnp.maximum(mx, s16))

    def tally_totals(c):
        ge, gt, mx = c
        return (col_sum(ge.astype(F32)), col_sum(gt.astype(F32)),
                jnp.max(mx.astype(F32), axis=0, keepdims=True))

    def score_far(j, c):
        s = scores(step_keys(j))
        for k in range(BIG):
            t = j * BIG + k
            tile = jnp.where(t < n0, s[k * KEY_TILE:(k + 1) * KEY_TILE, :], -jnp.inf)
            sc[t] = tile
            c = tally(c, tile)
        return c

    far_pairs = steps_far // 2
    c = lax.fori_loop(0, far_pairs, lambda j, c: score_far(2 * j + 1, score_far(2 * j, c)),
                      tally_init)
    c = lax.fori_loop(2 * far_pairs, steps_far, score_far, c)
    s_near = scores(near_keys)
    kpos = n0 * KEY_TILE + lax.broadcasted_iota(jnp.int32, (2 * KEY_TILE, tq), 0)
    qpos = q0 + lax.broadcasted_iota(jnp.int32, (2 * KEY_TILE, tq), 1)
    adm = (kpos < l_valid) & ((kpos >> CHUNK_SHIFT) <= (qpos >> CHUNK_SHIFT))
    s_near = jnp.where(adm, s_near, -jnp.inf)
    sc[n0] = s_near[:KEY_TILE, :]
    sc[n0 + 1] = s_near[KEY_TILE:, :]
    for k in range(BIG - 1):
        @pl.when(n0 + 2 + k < steps_all * BIG)
        def _():
            sc[n0 + 2 + k] = jnp.full((KEY_TILE, tq), -jnp.inf, F32)
    score_tally = tally(tally(c, s_near[:KEY_TILE, :]), s_near[KEY_TILE:, :])


    def col_sum(acc):
        parts = [acc[r:r + 8, :] for r in range(0, KEY_TILE, 8)]
        while len(parts) > 1:
            parts = [parts[r] + parts[r + 1] for r in range(0, len(parts), 2)]
        return jnp.sum(parts[0], axis=0, keepdims=True)

    def counter(value_at):
        def count_ge(t):
            tb = jnp.broadcast_to(t, (KEY_TILE, tq))

            def body(j, acc):
                for k in range(BIG):
                    acc = acc + jnp.where(value_at(j * BIG + k) >= tb, 1.0, 0.0)
                return acc
            return col_sum(lax.fori_loop(0, steps_all, body, zero))
        return count_ge

    def search(count_ge, kq, nfin, ge0, gt0, rmax, active, blind_steps):
        small = (nfin <= kq) | (active == 0.0)
        tie0 = (~small) & (gt0 < kq) & (ge0 >= kq)
        pos = (~small) & (gt0 >= kq)
        neg = (~small) & (ge0 < kq)
        log_kq = jnp.log(kq)
        lo0 = jnp.where(pos, F32_TINY, -F32_MAX)
        flo0 = jnp.where(pos, gt0, nfin)
        hi0 = jnp.where(neg, 0.0, jnp.inf)
        fhi0 = jnp.where(neg, ge0, 0.0)
        done0 = jnp.where(small | tie0 | (pos & (gt0 == kq)), 1.0, 0.0)
        thr0 = jnp.where(small, -F32_MAX, jnp.where(tie0, 0.0, F32_TINY))
        quota0 = jnp.where(tie0, kq - gt0, NO_QUOTA)

        def step(it, state):
            lo, flo, hi, fhi, done, thr, quota, adjacent = state
            klo, khi = _f32_key(lo), _f32_key(hi)
            la = jnp.log(flo)
            frac = (la - log_kq) / (la - jnp.log(jnp.maximum(fhi, 0.5)))
            interp = lo + (hi - lo) * jnp.where(it % 4 == 1, 0.5, frac)
            t_up = lo + (rmax - lo) * 0.5
            t_dn = hi - jnp.maximum(jnp.maximum(jnp.abs(rmax - hi), jnp.abs(hi)), F32_TINY)
            open_hi = hi == jnp.inf
            open_lo = lo <= -F32_MAX
            t = jnp.where(open_hi, t_up, jnp.where(open_lo, t_dn, interp))
            tk = jnp.minimum(jnp.maximum(_f32_key(t), klo + 1), khi - 1)
            tk = jnp.where(it % 4 == 3, klo + lax.shift_right_logical(khi - klo, 1), tk)
            t = _key_f32(tk)
            f = count_ge(t)
            act = done == 0.0
            hit = act & (f == kq)
            up = act & (f > kq)
            dn = act & (f < kq)
            lo = jnp.where(up, t, lo)
            flo = jnp.where(up, f, flo)
            hi = jnp.where(dn, t, hi)
            fhi = jnp.where(dn, f, fhi)
            adj = act & (~hit) & (_f32_key(lo) + 1 == _f32_key(hi))
            thr = jnp.where(hit, t, jnp.where(adj, lo, thr))
            quota = jnp.where(adj, kq - fhi, quota)
            done = jnp.where(hit | adj, 1.0, done)
            return lo, flo, hi, fhi, done, thr, quota, jnp.where(adj, 1.0, adjacent)

        state = (lo0, flo0, hi0, fhi0, done0, thr0, quota0, jnp.zeros((1, tq), F32))
        if blind_steps:
            state = lax.fori_loop(0, blind_steps, step, state)

        def cond(c):
            return (c[0] < MAX_SEARCH_STEPS) & (c[1] > 0)

        def body(c):
            state = step(c[0], c[2])
            return c[0] + 1, jnp.sum(1.0 - state[4]), state

        _, _, state = lax.while_loop(cond, body, (jnp.int32(blind_steps), jnp.sum(1.0 - state[4]), state))
        _, flo, _, fhi, _, thr, quota, adjacent = state
        return thr, quota, adjacent, flo, fhi

    qrow = q0 + lax.broadcasted_iota(jnp.int32, (1, tq), 1)
    nadm = jnp.minimum(((qrow >> CHUNK_SHIFT) + 1) * CHUNK, l_valid).astype(F32)
    thr, quota, adjacent, flo, fhi = search(
        counter(lambda t: sc[t]), jnp.full((1, tq), kf, F32), nadm,
        *tally_totals(score_tally), jnp.ones((1, tq), F32), BLIND_SEARCH_STEPS)
    thr_b = jnp.broadcast_to(thr, (tq, KEY_TILE))
    quota_b = jnp.broadcast_to(quota, (tq, KEY_TILE))
    has_ties = jnp.min(quota) < NO_QUOTA
    any_adjacent = jnp.max(adjacent) > 0.0

    @pl.when(jnp.logical_not(has_ties))
    def _():
        def body(j, c):
            for k in range(BIG):
                t = j * BIG + k
                sc[t] = jnp.where(sc[t] >= thr_b, 0.0, NEG)
            return c
        lax.fori_loop(0, steps_all, body, 0)

    def rank_pass(is_tie, is_sure, quota_of_ties):
        ri = lax.broadcasted_iota(jnp.int32, (KEY_TILE, KEY_TILE), 0)
        ci = lax.broadcasted_iota(jnp.int32, (KEY_TILE, KEY_TILE), 1)
        lower = jnp.where(ri >= ci, 1.0, 0.0).astype(BF16)

        def rank_tiles(first, n, seen):
            for k in range(n):
                t = first + k
                s = sc[t]
                tie = jnp.where(is_tie(s), 1.0, 0.0)
                rank = seen + _dot(lower, tie.astype(BF16))
                keep = is_sure(s) | ((tie > 0.0) & (rank <= quota_of_ties))
                sc[t] = jnp.where(keep, 0.0, NEG)
                seen = seen + jnp.sum(tie, axis=0, keepdims=True)
            return seen

        pairs = steps_all // 2
        seen = lax.fori_loop(0, pairs, lambda j, c: rank_tiles(j * 2 * BIG, 2 * BIG, c),
                             jnp.zeros((1, tq), F32))
        lax.fori_loop(2 * pairs, steps_all, lambda j, c: rank_tiles(j * BIG, BIG, c), seen)

    @pl.when(has_ties & jnp.logical_not(any_adjacent))
    def _():
        rank_pass(lambda s: s == thr_b, lambda s: s > thr_b, quota_b)

    @pl.when(any_adjacent)
    def _():
        adj_b = jnp.broadcast_to(adjacent, (tq, KEY_TILE)) > 0.0
        next_b = jnp.broadcast_to(_key_f32(_f32_key(thr) + 1), (tq, KEY_TILE))

        def in_cell(s):
            return adj_b & (s >= thr_b) & (s < next_b)

        def offset(s):
            return jnp.where(in_cell(s), s - thr_b, -jnp.inf)

        def tally_offsets(j, c):
            for k in range(BIG):
                c = tally(c, offset(sc[j * BIG + k]))
            return c

        in_cells, above_thr, max_offset = tally_totals(lax.fori_loop(0, steps_all, tally_offsets, tally_init))
        thr2, quota2, _, _, _ = search(
            counter(lambda t: offset(sc[t])), jnp.where(adjacent > 0.0, quota, 1.0), in_cells, in_cells,
            above_thr, max_offset, adjacent, 0)
        thr2_b = jnp.broadcast_to(thr2, (tq, KEY_TILE))
        quota2_b = jnp.broadcast_to(quota2, (tq, KEY_TILE))
        plain_b = jnp.logical_not(adj_b)
        rank_pass(lambda s: (plain_b & (s == thr_b)) | (offset(s) == thr2_b),
                  lambda s: (plain_b & (s > thr_b)) | (adj_b & (s >= next_b)) | (offset(s) > thr2_b),
                  jnp.where(adj_b, quota2_b, quota_b))

    qt_all = q_ref[0].astype(F32).T
    zeros_half = jnp.zeros((HEAD_DIM, GRP * tq), F32)
    qa = []
    for g in range(N_KV_HEADS):
        piece = jnp.concatenate(
            [qt_all[(g * GRP + j) * HEAD_DIM:(g * GRP + j + 1) * HEAD_DIM, :] for j in range(GRP)],
            axis=1)
        halves = [piece, zeros_half] if g == 0 else [zeros_half, piece]
        qa.append(jnp.concatenate(halves, axis=0).astype(BF16))

    def logits(slot, keys, pens, m_old, extra=None):
        pen = jnp.concatenate(pens, axis=0)
        pen = jnp.concatenate([pen] * GRP, axis=1)
        kblk = k_ref[0, keys, :]
        m_new = []
        for g in range(N_KV_HEADS):
            s = _dot(kblk, qa[g]) + pen
            if extra is not None:
                s = s + extra[g]
            sbuf[slot, g, 0:pen.shape[0], :] = s
            m_new.append(jnp.maximum(m_old[g], jnp.max(s, axis=0, keepdims=True)))
        return tuple(m_new)

    def accumulate(slot, keys, nk, m_old, m_new, accs):
        out = []
        for g in range(N_KV_HEADS):
            p = jnp.exp2(sbuf[slot, g, 0:nk, :] - m_new[g])
            out.append(jnp.exp2(m_old[g] - m_new[g]) * accs[g]
                       + _dot(vt_ref[0, g * VT_ROWS:(g + 1) * VT_ROWS, keys], p.astype(BF16)))
        return tuple(out)

    last_far = jnp.maximum(steps_far - 1, 0)

    def far_keys(j):
        return step_keys(jnp.minimum(j, last_far))

    def far_pens(j):
        return [jnp.where(j * BIG + k < n0, sc[jnp.minimum(j * BIG + k, n0)], NEG) for k in range(BIG)]

    m_init = tuple(jnp.full((1, GRP * tq), NEG, F32) for _ in range(N_KV_HEADS))
    acc_init = tuple(jnp.zeros((VT_ROWS, GRP * tq), F32) for _ in range(N_KV_HEADS))
    m_first = logits(0, far_keys(0), far_pens(0), m_init)

    def stage_pair(i, c):
        m_old, m_cur, accs = c
        a = 2 * i + 1
        m_a = logits(1, far_keys(a), far_pens(a), m_cur)
        accs = accumulate(0, far_keys(a - 1), kw, m_old, m_cur, accs)
        m_b = logits(0, far_keys(a + 1), far_pens(a + 1), m_a)
        accs = accumulate(1, far_keys(a), kw, m_cur, m_a, accs)
        return m_a, m_b, accs

    pairs = steps_far // 2
    m_old, m_cur, accs = lax.fori_loop(0, pairs, stage_pair, (m_init, m_first, acc_init))
    m_last = logits(1, near_keys, [sc[n0], sc[n0 + 1]], m_cur, extra=(bias[variant, 0], bias[variant, 1]))
    accs = accumulate(0, far_keys(2 * pairs), kw, m_old, m_cur, accs)
    accs = accumulate(1, near_keys, 2 * KEY_TILE, m_cur, m_last, accs)
    heads = []
    for g in range(N_KV_HEADS):
        acc = accs[g]
        og = acc[:HEAD_DIM, :] / acc[HEAD_DIM:HEAD_DIM + 1, :]
        heads += [og[:, j * tq:(j + 1) * tq] for j in range(GRP)]
    o_ref[0] = jnp.concatenate(heads, axis=0).T.astype(BF16)


def _dsa(rel_bias, q, qi, sgn, k, vt, ki, *, q_pos0, l_valid):
    b, t, _ = q.shape
    lk = k.shape[1]
    tq = KEY_TILE
    assert lk % (BIG * KEY_TILE) == 0 and q_pos0 % KEY_TILE == 0 and t % tq == 0
    assert q_pos0 + t <= lk
    nt = lk // KEY_TILE
    row = lambda width: pl.BlockSpec((1, tq, width), lambda bi, i: (bi, i, 0))
    keys = lambda shape: (_resident if b == 1 else pl.BlockSpec)(shape, lambda bi, i: (bi, 0, 0))
    return pl.pallas_call(
        functools.partial(_dsa_kernel, q_pos0=q_pos0, l_valid=l_valid),
        grid=(b, t // tq),
        in_specs=[pl.BlockSpec(memory_space=pltpu.SMEM),
                  row(D_ATT), row(N_IDX_HEADS * D_IDX),
                  pl.BlockSpec((1, 8, tq), lambda bi, i: (bi, 0, i)),
                  keys((1, lk, N_KV_HEADS * HEAD_DIM)), keys((1, N_KV_HEADS * VT_ROWS, lk)),
                  keys((1, lk, D_IDX))],
        out_specs=row(D_ATT),
        out_shape=jax.ShapeDtypeStruct((b, t, D_ATT), BF16),
        scratch_shapes=[pltpu.VMEM((nt, KEY_TILE, tq), F32),
                        pltpu.VMEM((2, N_KV_HEADS, 2 * KEY_TILE, GRP * tq), F32),
                        pltpu.VMEM((2, N_KV_HEADS, BIG * KEY_TILE, GRP * tq), F32)],
        compiler_params=pltpu.CompilerParams(
            dimension_semantics=("arbitrary", "arbitrary"), vmem_limit_bytes=VMEM_LIMIT),
        name="dsa",
    )(rel_bias, q, qi, sgn, k, vt, ki)


def _split3(x):
    hi = x.astype(BF16)
    r = x - hi.astype(F32)
    mid = r.astype(BF16)
    lo = (r - mid.astype(F32)).astype(BF16)
    return hi, mid, lo


def _ssd_kernel(xs_ref, bm_ref, cm_ref, dt_ref, z_ref, alog_ref, dsk_ref, nw_ref, h0_ref,
                y_ref, h_ref, ht, *, q_in):
    c = pl.program_id(1)
    nc = pl.num_programs(1)

    @pl.when(c == 0)
    def _():
        for hh in range(SSM_HEADS):
            ht[hh] = h0_ref[0, hh].T

    def rows(ref):
        x = ref[0]
        if q_in == SSD_Q:
            return x
        return jnp.concatenate([x, jnp.zeros((SSD_Q - q_in, x.shape[1]), x.dtype)], axis=0)

    xs, bm, cm, dt, z = rows(xs_ref), rows(bm_ref), rows(cm_ref), rows(dt_ref), rows(z_ref)
    a = -jnp.exp(alog_ref[...])
    da = dt * a
    ri = lax.broadcasted_iota(jnp.int32, (SSD_Q, SSD_Q), 0)
    ci = lax.broadcasted_iota(jnp.int32, (SSD_Q, SSD_Q), 1)
    causal = ri >= ci
    tril = jnp.where(causal, 1.0, 0.0).astype(BF16)
    hi, mid, lo = _split3(da)
    acum = _dot(tril, hi) + _dot(tril, mid) + _dot(tril, lo)
    acum_t = acum.T

    for g in range(SSM_GROUPS):
        bm_g = bm[:, g * SSM_STATE:(g + 1) * SSM_STATE]
        cm_g = cm[:, g * SSM_STATE:(g + 1) * SSM_STATE].astype(BF16)
        bm_t = bm_g.T.astype(BF16)
        cb = _dot(cm_g, bm_t)
        gs = []
        ssq = jnp.zeros((SSD_Q, 1), F32)
        for r in range(SSM_HPG):
            hh = g * SSM_HPG + r
            lanes = slice(hh * SSM_HEAD_DIM, (hh + 1) * SSM_HEAD_DIM)
            xh = xs[:, lanes]
            dl = DT_LANE0 + hh
            col = acum[:, dl:dl + 1]
            rowv = acum_t[dl:dl + 1, :]
            decay = jnp.exp(jnp.where(causal, col - rowv, NEG))
            xdt = xh * dt[:, dl:dl + 1]
            y = _dot((cb * decay).astype(BF16), xdt.astype(BF16))
            h_prev = ht[hh]
            y = y + jnp.exp(col) * _dot(cm_g, h_prev.astype(BF16))
            last = acum[SSD_Q - 1:SSD_Q, dl:dl + 1]
            xw = (xdt * jnp.exp(last - col)).astype(BF16)
            ht[hh] = h_prev * jnp.exp(last) + _dot(bm_t, xw)
            y = y + xh * dsk_ref[:, lanes]
            gate = y * _silu(z[:, lanes])
            ssq = ssq + jnp.sum(gate * gate, axis=1, keepdims=True)
            gs.append(gate)
        scale = lax.rsqrt(ssq / float(SSM_HPG * SSM_HEAD_DIM) + NORM_EPS)
        for r in range(SSM_HPG):
            hh = g * SSM_HPG + r
            lanes = slice(hh * SSM_HEAD_DIM, (hh + 1) * SSM_HEAD_DIM)
            y_ref[0, :, lanes] = (gs[r] * scale * nw_ref[:, lanes])[:q_in].astype(BF16)

    @pl.when(c == nc - 1)
    def _():
        for hh in range(SSM_HEADS):
            h_ref[0, hh] = ht[hh].T


def _ssd(xs, bm, cm, dt, z, alog, dsk, nw, h0, *, q_in):
    b, t, _ = xs.shape
    row = lambda width: pl.BlockSpec((1, q_in, width), lambda bi, c: (bi, c, 0))
    full = lambda a: _resident(a.shape, lambda bi, c: (0,) * a.ndim)
    hspec = pl.BlockSpec((1, SSM_HEADS, SSM_HEAD_DIM, SSM_STATE), lambda bi, c: (bi, 0, 0, 0))
    return pl.pallas_call(
        functools.partial(_ssd_kernel, q_in=q_in),
        grid=(b, t // q_in),
        in_specs=[row(D_SSM), row(256), row(256), row(LANE), row(D_SSM),
                  full(alog), full(dsk), full(nw), hspec],
        out_specs=(row(D_SSM), hspec),
        out_shape=(jax.ShapeDtypeStruct((b, t, D_SSM), BF16),
                   jax.ShapeDtypeStruct((b, SSM_HEADS, SSM_HEAD_DIM, SSM_STATE), F32)),
        scratch_shapes=[pltpu.VMEM((SSM_HEADS, SSM_STATE, SSM_HEAD_DIM), F32)],
        compiler_params=pltpu.CompilerParams(
            dimension_semantics=("arbitrary", "arbitrary"), vmem_limit_bytes=VMEM_LIMIT),
        name="ssd",
    )(xs, bm, cm, dt, z, alog, dsk, nw, h0)


def _mix_ffn_kernel(x_ref, ya_ref, yb_ref, yc_ref, wo_ref, npost_ref, nfpre_ref, wg_ref, wu_ref,
                    wd_ref, cfw_ref, cfb_ref, nfpost_ref, buff_ref, o_ref, buff_out, eg, *, seg):
    i = pl.program_id(1)
    nseq = eg.shape[0]

    @pl.when(i == 0)
    def _():
        for s in range(nseq):
            eg[s, CARRY - 2:CARRY, :] = buff_ref[0, s]

    mix = _dot(jnp.concatenate([ya_ref[0], yb_ref[0], yc_ref[0]], axis=1), wo_ref[...])
    x1 = x_ref[0] + _rms(mix, npost_ref[...])
    u = _rms(x1, nfpre_ref[...]).astype(BF16)
    g = _dot(u, wg_ref[...])
    convs = []
    for s in range(nseq):
        eg[s, CARRY:CARRY + seg, :] = g[s * seg:(s + 1) * seg, :]
        gc = eg[s, CARRY - 2:CARRY - 2 + seg, :] * cfw_ref[0:1, :]
        for j in range(1, CONV_F_W):
            gc = gc + eg[s, CARRY - 2 + j:CARRY - 2 + j + seg, :] * cfw_ref[j:j + 1, :]
        convs.append(gc)
        tail = eg[s, CARRY + seg - 2:CARRY + seg, :]
        buff_out[0, s] = tail
        eg[s, CARRY - 2:CARRY, :] = tail
    gc = convs[0] if nseq == 1 else jnp.concatenate(convs, axis=0)
    hid = (_silu(gc + cfb_ref[...]) * _dot(u, wu_ref[...])).astype(BF16)
    o_ref[0] = x1 + _rms(_dot(hid, wd_ref[...]), nfpost_ref[...])


def _mix_ffn(x, ya, yb, yc, wo, npost, nfpre, wg, wu, wd, cfw, cfb, nfpost, buff, *, layer, tm):
    b, t, _ = x.shape
    stacked = lambda a: pl.BlockSpec((None,) + a.shape[1:], lambda bi, i: (layer, 0, 0),
                                     pipeline_mode=pl.Buffered(1))
    seg = min(t, tm)
    nseq = tm // seg
    assert tm % seg == 0 and b % nseq == 0 and (nseq == 1 or seg == t)
    tiles = lambda a: a.reshape(b // nseq, nseq * t, a.shape[-1])
    row = lambda width: pl.BlockSpec((1, tm, width), lambda bi, i: (bi, i, 0))
    full = lambda a: _resident(a.shape, lambda bi, i: (0,) * a.ndim)
    state = pl.BlockSpec((1, nseq, CONV_F_W - 1, D_FF), lambda bi, i: (bi, 0, 0, 0))
    x_new, buff_new = pl.pallas_call(
        functools.partial(_mix_ffn_kernel, seg=seg),
        grid=(b // nseq, nseq * t // tm),
        in_specs=[row(D_MODEL), row(D_CONV_MIX), row(D_ATT), row(D_SSM), stacked(wo), full(npost),
                  full(nfpre), stacked(wg), stacked(wu), stacked(wd), full(cfw), full(cfb),
                  full(nfpost), state],
        out_specs=(row(D_MODEL), state),
        out_shape=(jax.ShapeDtypeStruct((b // nseq, nseq * t, D_MODEL), F32),
                   jax.ShapeDtypeStruct((b // nseq, nseq, CONV_F_W - 1, D_FF), F32)),
        scratch_shapes=[pltpu.VMEM((nseq, CARRY + seg, D_FF), F32)],
        compiler_params=pltpu.CompilerParams(
            dimension_semantics=("arbitrary", "arbitrary"), vmem_limit_bytes=VMEM_LIMIT),
        name="mix_ffn",
    )(tiles(x), tiles(ya), tiles(yb), tiles(yc), wo, npost, nfpre, wg, wu, wd, cfw, cfb, nfpost,
      buff.reshape(b // nseq, nseq, CONV_F_W - 1, D_FF))
    return x_new.reshape(b, t, D_MODEL), buff_new.reshape(b, CONV_F_W - 1, D_FF)


def _pack_keys_kernel(ck_ref, cv_ref, cki_ref, k_ref, v_ref, ki_ref, kb_ref, vt_ref, kib_ref, *, past_steps):
    j = pl.program_id(1)
    rows = kb_ref.shape[1]
    cached = j < past_steps

    def with_new(cache, new):
        new = jnp.concatenate([new, jnp.zeros((rows - new.shape[0], new.shape[1]), F32)], axis=0)
        return jnp.where(cached, cache, new)

    heads = lambda ref: [ref[0, 0, g] for g in range(N_KV_HEADS)]
    k_cache = jnp.concatenate([h.T for h in heads(ck_ref)], axis=1)
    kb_ref[0] = with_new(k_cache, k_ref[0]).astype(BF16)
    v_new = jnp.concatenate([v_ref[0], jnp.zeros((rows - v_ref.shape[1], v_ref.shape[2]), F32)], axis=0)
    vt = jnp.where(cached, jnp.concatenate(heads(cv_ref), axis=0), v_new.T)
    vt_ref[0] = _augment_vt(vt).astype(BF16)
    kib_ref[0] = with_new(cki_ref[0, 0].T, ki_ref[0]).astype(BF16)


def _pack_keys(layer, cache_k, cache_v, cache_kidx, k, v, ki):
    b, past = cache_k.shape[1:3]
    t = k.shape[1]
    rows = 2 * BIG * KEY_TILE
    assert past % rows == 0 and t <= rows
    past_steps = past // rows
    lk = past + rows
    cache = lambda *major: pl.BlockSpec(
        (1, 1) + major + (rows,),
        lambda bi, j: (layer, bi) + (0,) * len(major) + (jnp.minimum(j, past_steps - 1),))
    cache_k, cache_v = (jnp.transpose(c, (0, 1, 3, 4, 2)) for c in (cache_k, cache_v))
    cache_kidx = jnp.transpose(cache_kidx, (0, 1, 3, 2))
    new = lambda width: pl.BlockSpec((1, t, width), lambda bi, j: (bi, 0, 0))
    out = lambda width: pl.BlockSpec((1, rows, width), lambda bi, j: (bi, j, 0))
    return pl.pallas_call(
        functools.partial(_pack_keys_kernel, past_steps=past_steps),
        grid=(b, past_steps + 1),
        in_specs=[cache(N_KV_HEADS, HEAD_DIM), cache(N_KV_HEADS, HEAD_DIM), cache(D_IDX),
                  new(128), new(128), new(D_IDX)],
        out_specs=(out(128), pl.BlockSpec((1, N_KV_HEADS * VT_ROWS, rows), lambda bi, j: (bi, 0, j)),
                   out(D_IDX)),
        out_shape=(jax.ShapeDtypeStruct((b, lk, 128), BF16),
                   jax.ShapeDtypeStruct((b, N_KV_HEADS * VT_ROWS, lk), BF16),
                   jax.ShapeDtypeStruct((b, lk, D_IDX), BF16)),
        compiler_params=pltpu.CompilerParams(
            dimension_semantics=("arbitrary", "arbitrary"), vmem_limit_bytes=VMEM_LIMIT),
        name="pack_keys",
    )(cache_k, cache_v, cache_kidx, k, v, ki)


def _layer(x, p, rel_bias, buf_a, buf_s, h0, buf_f, kv_past, *, tm, tm_ffn):
    b, t, _ = x.shape
    outs = _in_proj(x, p['norm_mix_pre'], p['w_in'], p['conv_a_w'], p['conv_ssm_w'], p['conv_ssm_b'],
                    p['dt_bias'], buf_a, buf_s, tm=tm, key_side=kv_past is None)
    (ya, q, k, v, qi, ki, sgn, z, xs, bm, cm, dt, buf_a_new, buf_s_new) = outs[:14]
    if kv_past is None:
        kb, vt, kib = outs[14:]
        yb = _dsa(rel_bias, q, qi, sgn, kb, vt, kib, q_pos0=0, l_valid=t)
    else:
        layer, cache_k, cache_v, cache_kidx = kv_past
        past = cache_k.shape[2]
        qpad = lambda a: jnp.pad(a, ((0, 0), (0, -t % KEY_TILE), (0, 0)))
        kb, vt, kib = _pack_keys(layer, cache_k, cache_v, cache_kidx, k, v, ki)
        yb = _dsa(rel_bias, qpad(q), qpad(qi), sgn, kb, vt, kib, q_pos0=past, l_valid=past + t)[:, :t]
    yc, h_new = _ssd(xs, bm, cm, dt, z, p['a_log'], p['d_skip'], p['ssm_norm_w'],
                     h0.reshape(b, SSM_HEADS, SSM_HEAD_DIM, SSM_STATE), q_in=min(t, SSD_Q))
    x_new, buf_f_new = _mix_ffn(x, ya, yb, yc, p['w_out'], p['norm_mix_post'], p['norm_ffn_pre'],
                                p['w_gate'], p['w_up'], p['w_down'], p['conv_ffn_w'],
                                p['conv_ffn_b'], p['norm_ffn_post'], buf_f, layer=p['layer'], tm=tm_ffn)
    st = (k.reshape(b, t, N_KV_HEADS, HEAD_DIM), v.reshape(b, t, N_KV_HEADS, HEAD_DIM), ki,
          buf_a_new, buf_s_new, h_new.reshape(b, SSM_GROUPS, SSM_HPG, SSM_HEAD_DIM, SSM_STATE),
          buf_f_new)
    return x_new, st


def _pack_w_in_kernel(wt_ref, o_ref):
    for c, c0 in enumerate(W_IN_TILE_COLS):
        o_ref[:, c * LANE:(c + 1) * LANE] = wt_ref[c0:c0 + LANE, :].T.astype(BF16)


def _pack_w_in(wt):
    return pl.pallas_call(
        _pack_w_in_kernel,
        out_shape=jax.ShapeDtypeStruct((wt.shape[1], D_IN_PACKED), BF16),
        compiler_params=pltpu.CompilerParams(vmem_limit_bytes=VMEM_LIMIT),
        name="pack_w_in",
    )(wt)


def _dt_lanes(v):
    return jnp.pad(v, (DT_LANE0, 0))[None, :]


def kernel(x_prompt, x_sample, cache_k, cache_v, cache_kidx, state_conv_a, state_conv_ssm, state_ssm, state_conv_ffn, rel_bias, norm_mix_pre, norm_mix_post, norm_ffn_pre, norm_ffn_post, w_in, conv_a_w, conv_ssm_w, conv_ssm_b, dt_bias, a_log, d_skip, ssm_norm_w, w_out, w_gate, w_up, conv_ffn_w, conv_ffn_b, w_down):
    depth = w_in.shape[0]
    bp = x_prompt.shape[0]
    yp, ys = x_prompt, x_sample
    outs_p, outs_s = [], []
    w_out16, w_gate16, w_up16, w_down16 = (w.astype(BF16) for w in (w_out, w_gate, w_up, w_down))
    for l in range(depth):
        p = {'norm_mix_pre': norm_mix_pre[l][None], 'norm_mix_post': norm_mix_post[l][None],
             'norm_ffn_pre': norm_ffn_pre[l][None], 'norm_ffn_post': norm_ffn_post[l][None],
             'w_in': _pack_w_in(jnp.swapaxes(w_in[l], 0, 1)), 'conv_a_w': conv_a_w[l], 'conv_ssm_w': conv_ssm_w[l],
             'conv_ssm_b': conv_ssm_b[l][None], 'dt_bias': _dt_lanes(dt_bias[l]),
             'a_log': _dt_lanes(a_log[l]), 'd_skip': jnp.repeat(d_skip[l], SSM_HEAD_DIM)[None],
             'ssm_norm_w': ssm_norm_w[l][None], 'layer': l, 'w_out': w_out16,
             'w_gate': w_gate16, 'w_up': w_up16,
             'conv_ffn_w': conv_ffn_w[l], 'conv_ffn_b': conv_ffn_b[l][None],
             'w_down': w_down16}
        yp, st_p = _layer(
            yp, p, rel_bias,
            jnp.zeros((bp, CONV_A_W - 1, D_CONV_MIX), F32),
            jnp.zeros((bp, SSM_CONV_W - 1, D_XBC), F32),
            jnp.zeros((bp, SSM_GROUPS, SSM_HPG, SSM_HEAD_DIM, SSM_STATE), F32),
            jnp.zeros((bp, CONV_F_W - 1, D_FF), F32),
            None, tm=256, tm_ffn=512)
        ys, st_s = _layer(
            ys, p, rel_bias, state_conv_a[l], state_conv_ssm[l], state_ssm[l], state_conv_ffn[l],
            (l, cache_k, cache_v, cache_kidx), tm=ys.shape[1], tm_ffn=ys.shape[0] * ys.shape[1])
        outs_p.append(st_p)
        outs_s.append(st_s)

    def stack(outs, i):
        return jnp.stack([o[i] for o in outs], axis=0)

    res = [yp, ys]
    for i in range(7):
        res.append(stack(outs_p, i))
        res.append(stack(outs_s, i))
    return tuple(res)
```

```python
import functools

import jax
import jax.numpy as jnp
from jax import lax
from jax.experimental import pallas as pl
from jax.experimental.pallas import tpu as pltpu

F32 = jnp.float32
BF16 = jnp.bfloat16

D_MODEL = 1024
CHUNK = 64
CHUNK_SHIFT = CHUNK.bit_length() - 1
assert 1 << CHUNK_SHIFT == CHUNK
D_CONV_MIX = 256
CONV_A_W = 3
N_HEADS = 6
N_KV_HEADS = 2
HEAD_DIM = 64
D_ATT = N_HEADS * HEAD_DIM
GRP = N_HEADS // N_KV_HEADS
N_IDX_HEADS = 4
D_IDX = 64
TOPK = 256
N_BUCKETS = 32
SSM_HEADS = 6
SSM_HEAD_DIM = 64
D_SSM = SSM_HEADS * SSM_HEAD_DIM
SSM_GROUPS = 2
SSM_HPG = SSM_HEADS // SSM_GROUPS
SSM_STATE = 128
SSM_CONV_W = 4
D_XBC = D_SSM + 2 * SSM_GROUPS * SSM_STATE
D_FF = 2816
CONV_F_W = 3
NORM_EPS = 1e-6

LANE = 128
CARRY = 8
SSD_Q = 128
KEY_TILE = 128
BIG = 4
ATT_TILES = 2
VMEM_LIMIT = 56 * 1024 * 1024
NEG = -1e30
LOG2E = 1.4426950408889634
F32_MAX = 3.4028234663852886e38
F32_TINY = 1.1754943508222875e-38
NO_QUOTA = 1e9
MAX_SEARCH_STEPS = 256
BLIND_SEARCH_STEPS = 15

C_AB, C_AC, C_AH = 0, 256, 512
C_Q = 768
C_K = 1152
C_V = 1280
C_QI = 1408
C_KI = 1664
C_Z = 1792
C_XBC = 2176
C_DT = 3072
D_IN_PACKED = 3200
PROJ_GROUPS = (0, C_Q, C_Z, D_IN_PACKED)
D_IN = 3018
D_IN_HEAD = 1732
DT_LANE0 = LANE - SSM_HEADS
W_IN_TILE_COLS = ([c * LANE for c in range(C_Z // LANE)]
                  + [D_IN_HEAD + c * LANE for c in range((C_DT - C_Z) // LANE)] + [D_IN - LANE])


def _rms(x, w):
    return x * lax.rsqrt(jnp.mean(x * x, axis=-1, keepdims=True) + NORM_EPS) * w


def _dot(a, b):
    return jnp.dot(a, b, preferred_element_type=F32)


def _silu(x):
    return x * jax.nn.sigmoid(x)


VT_ROWS = HEAD_DIM + 16


def _augment_vt(vt):
    extra_shape = vt.shape[:-2] + (VT_ROWS - HEAD_DIM, vt.shape[-1])
    row = lax.broadcasted_iota(jnp.int32, extra_shape, len(extra_shape) - 2)
    extra = jnp.where(row == 0, 1.0, 0.0).astype(vt.dtype)
    parts = []
    for g in range(N_KV_HEADS):
        parts += [vt[..., g * HEAD_DIM:(g + 1) * HEAD_DIM, :], extra]
    return jnp.concatenate(parts, axis=-2)


def _resident(shape, index_map):
    return pl.BlockSpec(shape, index_map, pipeline_mode=pl.Buffered(1))


def _in_proj_kernel(x_ref, nw_ref, w_ref, caw_ref, csw_ref, csb_ref, dtb_ref, bufa_ref, bufs_ref,
                    ya_ref, q_ref, k_ref, v_ref, qi_ref, ki_ref, sgn_ref, z_ref, xs_ref, bm_ref,
                    cm_ref, dt_ref, bufa_out, bufs_out, *rest, tm, key_side):
    if key_side:
        kb_ref, vt_ref, kib_ref, ea, es = rest
    else:
        ea, es = rest
    i = pl.program_id(1)
    u = _rms(x_ref[0], nw_ref[...]).astype(BF16)

    slabs = {}

    def proj(c0, width):
        g0, g1 = next((a, b) for a, b in zip(PROJ_GROUPS, PROJ_GROUPS[1:]) if a <= c0 < b)
        if g0 not in slabs:
            slabs[g0] = _dot(u, w_ref[:, g0:g1])
        return slabs[g0][:, c0 - g0:c0 - g0 + width]

    @pl.when(i == 0)
    def _():
        ea[CARRY - 2:CARRY, :] = bufa_ref[0]
        es[CARRY - 3:CARRY, :] = bufs_ref[0]

    ea[CARRY:CARRY + tm, :] = proj(C_AC, D_CONV_MIX) * proj(C_AH, D_CONV_MIX)
    conv = ea[CARRY - 2:CARRY - 2 + tm, :] * caw_ref[0:1, :]
    for j in range(1, CONV_A_W):
        conv = conv + ea[CARRY - 2 + j:CARRY - 2 + j + tm, :] * caw_ref[j:j + 1, :]
    ya_ref[0] = (proj(C_AB, D_CONV_MIX) * conv).astype(BF16)
    tail_a = ea[CARRY + tm - 2:CARRY + tm, :]
    bufa_out[0] = tail_a
    ea[CARRY - 2:CARRY, :] = tail_a

    q_ref[0] = (proj(C_Q, D_ATT) * (HEAD_DIM ** -0.5 * LOG2E)).astype(BF16)
    k = proj(C_K, N_KV_HEADS * HEAD_DIM)
    v = proj(C_V, N_KV_HEADS * HEAD_DIM)
    kiw = proj(C_KI, LANE)
    k_ref[0] = k
    v_ref[0] = v
    ki_ref[0] = kiw[:, :D_IDX]
    wi = kiw[:, D_IDX:D_IDX + N_IDX_HEADS]
    kiw_rows = kiw if tm >= LANE else jnp.concatenate([kiw, jnp.zeros((LANE - tm, LANE), F32)], axis=0)
    sgn_ref[0] = jnp.where(kiw_rows.T[D_IDX:D_IDX + 8, :] >= 0.0, 1.0, -1.0).astype(F32)
    if key_side:
        kb_ref[0] = k.astype(BF16)
        vt_ref[0] = _augment_vt(v.T).astype(BF16)
        kib_ref[0] = kiw[:, :D_IDX].astype(BF16)
    wabs = jnp.abs(wi) * (D_IDX ** -0.5 * N_IDX_HEADS ** -0.5)
    qi = proj(C_QI, N_IDX_HEADS * D_IDX)
    for h in range(N_IDX_HEADS):
        qi_ref[0, :, h * D_IDX:(h + 1) * D_IDX] = (
            qi[:, h * D_IDX:(h + 1) * D_IDX] * wabs[:, h:h + 1]).astype(BF16)

    z_ref[0] = proj(C_Z, D_SSM)
    es[CARRY:CARRY + tm, :] = proj(C_XBC, D_XBC)
    xc = es[CARRY - 3:CARRY - 3 + tm, :] * csw_ref[0:1, :]
    for j in range(1, SSM_CONV_W):
        xc = xc + es[CARRY - 3 + j:CARRY - 3 + j + tm, :] * csw_ref[j:j + 1, :]
    xc = _silu(xc + csb_ref[...])
    xs_ref[0] = xc[:, :D_SSM]
    bm_ref[0] = xc[:, D_SSM:D_SSM + SSM_GROUPS * SSM_STATE]
    cm_ref[0] = xc[:, D_SSM + SSM_GROUPS * SSM_STATE:]
    tail_s = es[CARRY + tm - 3:CARRY + tm, :]
    bufs_out[0] = tail_s
    es[CARRY - 3:CARRY, :] = tail_s
    dtr = proj(C_DT, LANE) + dtb_ref[...]
    dt_ref[0] = jnp.maximum(dtr, 0.0) + jnp.log1p(jnp.exp(-jnp.abs(dtr)))


def _in_proj(x, nw, w_packed, caw, csw, csb, dtb, bufa, bufs, *, tm, key_side):
    b, t, _ = x.shape
    tq = max(tm, LANE)
    grid = (b, t // tm)
    row = lambda width: pl.BlockSpec((1, tm, width), lambda bi, i: (bi, i, 0))
    full = lambda a: _resident(a.shape, lambda bi, i: (0,) * a.ndim)
    state = lambda r, width: pl.BlockSpec((1, r, width), lambda bi, i: (bi, 0, 0))
    out_shapes = (
        jax.ShapeDtypeStruct((b, t, D_CONV_MIX), BF16),
        jax.ShapeDtypeStruct((b, t, D_ATT), BF16),
        jax.ShapeDtypeStruct((b, t, 128), F32),
        jax.ShapeDtypeStruct((b, t, 128), F32),
        jax.ShapeDtypeStruct((b, t, 256), BF16),
        jax.ShapeDtypeStruct((b, t, D_IDX), F32),
        jax.ShapeDtypeStruct((b, 8, t // tm * tq), F32),
        jax.ShapeDtypeStruct((b, t, D_SSM), F32),
        jax.ShapeDtypeStruct((b, t, D_SSM), F32),
        jax.ShapeDtypeStruct((b, t, 256), F32),
        jax.ShapeDtypeStruct((b, t, 256), F32),
        jax.ShapeDtypeStruct((b, t, LANE), F32),
        jax.ShapeDtypeStruct((b, CONV_A_W - 1, D_CONV_MIX), F32),
        jax.ShapeDtypeStruct((b, SSM_CONV_W - 1, D_XBC), F32),
    )
    out_specs = (row(D_CONV_MIX), row(D_ATT), row(128), row(128), row(256), row(D_IDX),
                 pl.BlockSpec((1, 8, tq), lambda bi, i: (bi, 0, i)),
                 row(D_SSM), row(D_SSM), row(256), row(256), row(LANE),
                 state(CONV_A_W - 1, D_CONV_MIX), state(SSM_CONV_W - 1, D_XBC))
    if key_side:
        out_shapes += (jax.ShapeDtypeStruct((b, t, 128), BF16),
                       jax.ShapeDtypeStruct((b, N_KV_HEADS * VT_ROWS, t), BF16),
                       jax.ShapeDtypeStruct((b, t, D_IDX), BF16))
        out_specs += (row(128), pl.BlockSpec((1, N_KV_HEADS * VT_ROWS, tm), lambda bi, i: (bi, 0, i)), row(D_IDX))
    in_specs = [row(D_MODEL), full(nw), full(w_packed), full(caw), full(csw), full(csb), full(dtb),
                state(CONV_A_W - 1, D_CONV_MIX), state(SSM_CONV_W - 1, D_XBC)]
    return pl.pallas_call(
        functools.partial(_in_proj_kernel, tm=tm, key_side=key_side),
        grid=grid, in_specs=in_specs, out_specs=out_specs, out_shape=out_shapes,
        scratch_shapes=[pltpu.VMEM((CARRY + tm, D_CONV_MIX), F32),
                        pltpu.VMEM((CARRY + tm, D_XBC), F32)],
        compiler_params=pltpu.CompilerParams(
            dimension_semantics=("arbitrary", "arbitrary"), vmem_limit_bytes=VMEM_LIMIT),
        name="in_proj",
    )(x, nw, w_packed, caw, csw, csb, dtb, bufa, bufs)


def _t5_bucket(rel):
    n = jnp.abs(rel)
    large = jnp.full(rel.shape, 8, jnp.int32)
    for brk in (12, 16, 23, 32, 46, 64, 91):
        large = large + jnp.where(n >= brk, 1, 0)
    return jnp.where(rel > 0, N_BUCKETS // 2, 0) + jnp.where(n < 8, n, large)


def _f32_key(x):
    bits = lax.bitcast_convert_type(x, jnp.int32)
    return bits ^ ((bits >> 31) & jnp.int32(0x7FFFFFFF))


def _key_f32(key):
    return lax.bitcast_convert_type(key ^ ((key >> 31) & jnp.int32(0x7FFFFFFF)), F32)


def _dsa_kernel(relb_ref, q_ref, qi_ref, sgn_ref, k_ref, vt_ref, ki_ref, o_ref, sc, bias, sbuf,
                *, q_pos0, l_valid):
    tq = KEY_TILE
    i = pl.program_id(1)
    q0 = q_pos0 + i * tq
    qt = q0 // KEY_TILE
    n0 = jnp.maximum(qt - 1, 0)
    variant = jnp.minimum(qt, 1)
    steps_far = (n0 + BIG - 1) // BIG
    steps_all = (n0 + 2 + BIG - 1) // BIG
    kw = BIG * KEY_TILE
    kf = float(TOPK)

    def step_keys(j):
        return pl.ds(pl.multiple_of(j * kw, kw), kw)

    near_keys = pl.ds(pl.multiple_of(n0 * KEY_TILE, KEY_TILE), 2 * KEY_TILE)

    @pl.when(i == 0)
    def _():
        sc[...] = jnp.full(sc.shape, -jnp.inf, F32)

    @pl.when((i == 0) & (pl.program_id(0) == 0))
    def _():
        kk = lax.broadcasted_iota(jnp.int32, (2 * KEY_TILE, tq), 0)
        r = lax.broadcasted_iota(jnp.int32, (2 * KEY_TILE, tq), 1)
        for var in range(2):
            bucket = _t5_bucket(kk - var * KEY_TILE - r)
            for h in range(N_HEADS):
                tab = jnp.zeros((2 * KEY_TILE, tq), F32)
                for bk in range(N_BUCKETS):
                    tab = jnp.where(bucket == bk, relb_ref[bk, h], tab)
                g, j = divmod(h, GRP)
                bias[var, g, :, j * tq:(j + 1) * tq] = (tab - relb_ref[N_BUCKETS // 2 - 1, h]) * LOG2E

    qit = qi_ref[0].astype(F32).T
    qis = jnp.concatenate([qit[h * D_IDX:(h + 1) * D_IDX, :] for h in range(N_IDX_HEADS)],
                          axis=1).astype(BF16)
    sgn = sgn_ref[0]

    def scores(keys):
        s = jnp.maximum(_dot(ki_ref[0, keys, :], qis), 0.0)
        tot = s[:, 0:tq] * sgn[0:1, :]
        for h in range(1, N_IDX_HEADS):
            tot = tot + s[:, h * tq:(h + 1) * tq] * sgn[h:h + 1, :]
        return tot

    zero = jnp.zeros((KEY_TILE, tq), F32)

    tally_init = (jnp.zeros((KEY_TILE, tq), BF16), jnp.zeros((KEY_TILE, tq), BF16),
                  jnp.full((KEY_TILE, tq), float(jnp.finfo(BF16).min), BF16))
    one16, nil16 = jnp.ones((KEY_TILE, tq), BF16), jnp.zeros((KEY_TILE, tq), BF16)

    def tally(c, s):
        ge, gt, mx = c
        s16 = s.astype(BF16)
        return (ge + jnp.where(s16 >= 0.0, one16, nil16),
                gt + jnp.where(s16 >= jnp.asarray(F32_TINY, BF16), one16, nil16), jnp.maximum(mx, s16))

    def tally_totals(c):
        ge, gt, mx = c
        return (col_sum(ge.astype(F32)), col_sum(gt.astype(F32)),
                jnp.max(mx.astype(F32), axis=0, keepdims=True))

    def score_far(j, c):
        s = scores(step_keys(j))
        for k in range(BIG):
            t = j * BIG + k
            tile = jnp.where(t < n0, s[k * KEY_TILE:(k + 1) * KEY_TILE, :], -jnp.inf)
            sc[t] = tile
            c = tally(c, tile)
        return c

    far_pairs = steps_far // 2
    c = lax.fori_loop(0, far_pairs, lambda j, c: score_far(2 * j + 1, score_far(2 * j, c)),
                      tally_init)
    c = lax.fori_loop(2 * far_pairs, steps_far, score_far, c)
    s_near = scores(near_keys)
    kpos = n0 * KEY_TILE + lax.broadcasted_iota(jnp.int32, (2 * KEY_TILE, tq), 0)
    qpos = q0 + lax.broadcasted_iota(jnp.int32, (2 * KEY_TILE, tq), 1)
    adm = (kpos < l_valid) & ((kpos >> CHUNK_SHIFT) <= (qpos >> CHUNK_SHIFT))
    s_near = jnp.where(adm, s_near, -jnp.inf)
    sc[n0] = s_near[:KEY_TILE, :]
    sc[n0 + 1] = s_near[KEY_TILE:, :]
    for k in range(BIG - 1):
        @pl.when(n0 + 2 + k < steps_all * BIG)
        def _():
            sc[n0 + 2 + k] = jnp.full((KEY_TILE, tq), -jnp.inf, F32)
    score_tally = tally(tally(c, s_near[:KEY_TILE, :]), s_near[KEY_TILE:, :])


    def col_sum(acc):
        parts = [acc[r:r + 8, :] for r in range(0, KEY_TILE, 8)]
        while len(parts) > 1:
            parts = [parts[r] + parts[r + 1] for r in range(0, len(parts), 2)]
        return jnp.sum(parts[0], axis=0, keepdims=True)

    def counter(value_at):
        def count_ge(t):
            tb = jnp.broadcast_to(t, (KEY_TILE, tq))

            def body(j, acc):
                for k in range(BIG):
                    acc = acc + jnp.where(value_at(j * BIG + k) >= tb, 1.0, 0.0)
                return acc
            return col_sum(lax.fori_loop(0, steps_all, body, zero))
        return count_ge

    def search(count_ge, kq, nfin, ge0, gt0, rmax, active, blind_steps):
        small = (nfin <= kq) | (active == 0.0)
        tie0 = (~small) & (gt0 < kq) & (ge0 >= kq)
        pos = (~small) & (gt0 >= kq)
        neg = (~small) & (ge0 < kq)
        log_kq = jnp.log(kq)
        lo0 = jnp.where(pos, F32_TINY, -F32_MAX)
        flo0 = jnp.where(pos, gt0, nfin)
        hi0 = jnp.where(neg, 0.0, jnp.inf)
        fhi0 = jnp.where(neg, ge0, 0.0)
        done0 = jnp.where(small | tie0 | (pos & (gt0 == kq)), 1.0, 0.0)
        thr0 = jnp.where(small, -F32_MAX, jnp.where(tie0, 0.0, F32_TINY))
        quota0 = jnp.where(tie0, kq - gt0, NO_QUOTA)

        def step(it, state):
            lo, flo, hi, fhi, done, thr, quota, adjacent = state
            klo, khi = _f32_key(lo), _f32_key(hi)
            la = jnp.log(flo)
            frac = (la - log_kq) / (la - jnp.log(jnp.maximum(fhi, 0.5)))
            interp = lo + (hi - lo) * jnp.where(it % 4 == 1, 0.5, frac)
            t_up = lo + (rmax - lo) * 0.5
            t_dn = hi - jnp.maximum(jnp.maximum(jnp.abs(rmax - hi), jnp.abs(hi)), F32_TINY)
            open_hi = hi == jnp.inf
            open_lo = lo <= -F32_MAX
            t = jnp.where(open_hi, t_up, jnp.where(open_lo, t_dn, interp))
            tk = jnp.minimum(jnp.maximum(_f32_key(t), klo + 1), khi - 1)
            tk = jnp.where(it % 4 == 3, klo + lax.shift_right_logical(khi - klo, 1), tk)
            t = _key_f32(tk)
            f = count_ge(t)
            act = done == 0.0
            hit = act & (f == kq)
            up = act & (f > kq)
            dn = act & (f < kq)
            lo = jnp.where(up, t, lo)
            flo = jnp.where(up, f, flo)
            hi = jnp.where(dn, t, hi)
            fhi = jnp.where(dn, f, fhi)
            adj = act & (~hit) & (_f32_key(lo) + 1 == _f32_key(hi))
            thr = jnp.where(hit, t, jnp.where(adj, lo, thr))
            quota = jnp.where(adj, kq - fhi, quota)
            done = jnp.where(hit | adj, 1.0, done)
            return lo, flo, hi, fhi, done, thr, quota, jnp.where(adj, 1.0, adjacent)

        state = (lo0, flo0, hi0, fhi0, done0, thr0, quota0, jnp.zeros((1, tq), F32))
        if blind_steps:
            state = lax.fori_loop(0, blind_steps, step, state)

        def cond(c):
            return (c[0] < MAX_SEARCH_STEPS) & (c[1] > 0)

        def body(c):
            state = step(c[0], c[2])
            return c[0] + 1, jnp.sum(1.0 - state[4]), state

        _, _, state = lax.while_loop(cond, body, (jnp.int32(blind_steps), jnp.sum(1.0 - state[4]), state))
        _, flo, _, fhi, _, thr, quota, adjacent = state
        return thr, quota, adjacent, flo, fhi

    qrow = q0 + lax.broadcasted_iota(jnp.int32, (1, tq), 1)
    nadm = jnp.minimum(((qrow >> CHUNK_SHIFT) + 1) * CHUNK, l_valid).astype(F32)
    thr, quota, adjacent, flo, fhi = search(
        counter(lambda t: sc[t]), jnp.full((1, tq), kf, F32), nadm,
        *tally_totals(score_tally), jnp.ones((1, tq), F32), BLIND_SEARCH_STEPS)
    thr_b = jnp.broadcast_to(thr, (tq, KEY_TILE))
    quota_b = jnp.broadcast_to(quota, (tq, KEY_TILE))
    has_ties = jnp.min(quota) < NO_QUOTA
    any_adjacent = jnp.max(adjacent) > 0.0

    @pl.when(jnp.logical_not(has_ties))
    def _():
        def body(j, c):
            for k in range(BIG):
                t = j * BIG + k
                sc[t] = jnp.where(sc[t] >= thr_b, 0.0, NEG)
            return c
        lax.fori_loop(0, steps_all, body, 0)

    def rank_pass(is_tie, is_sure, quota_of_ties):
        ri = lax.broadcasted_iota(jnp.int32, (KEY_TILE, KEY_TILE), 0)
        ci = lax.broadcasted_iota(jnp.int32, (KEY_TILE, KEY_TILE), 1)
        lower = jnp.where(ri >= ci, 1.0, 0.0).astype(BF16)

        def rank_tiles(first, n, seen):
            for k in range(n):
                t = first + k
                s = sc[t]
                tie = jnp.where(is_tie(s), 1.0, 0.0)
                rank = seen + _dot(lower, tie.astype(BF16))
                keep = is_sure(s) | ((tie > 0.0) & (rank <= quota_of_ties))
                sc[t] = jnp.where(keep, 0.0, NEG)
                seen = seen + jnp.sum(tie, axis=0, keepdims=True)
            return seen

        pairs = steps_all // 2
        seen = lax.fori_loop(0, pairs, lambda j, c: rank_tiles(j * 2 * BIG, 2 * BIG, c),
                             jnp.zeros((1, tq), F32))
        lax.fori_loop(2 * pairs, steps_all, lambda j, c: rank_tiles(j * BIG, BIG, c), seen)

    @pl.when(has_ties & jnp.logical_not(any_adjacent))
    def _():
        rank_pass(lambda s: s == thr_b, lambda s: s > thr_b, quota_b)

    @pl.when(any_adjacent)
    def _():
        adj_b = jnp.broadcast_to(adjacent, (tq, KEY_TILE)) > 0.0
        next_b = jnp.broadcast_to(_key_f32(_f32_key(thr) + 1), (tq, KEY_TILE))

        def in_cell(s):
            return adj_b & (s >= thr_b) & (s < next_b)

        def offset(s):
            return jnp.where(in_cell(s), s - thr_b, -jnp.inf)

        def tally_offsets(j, c):
            for k in range(BIG):
                c = tally(c, offset(sc[j * BIG + k]))
            return c

        in_cells, above_thr, max_offset = tally_totals(lax.fori_loop(0, steps_all, tally_offsets, tally_init))
        thr2, quota2, _, _, _ = search(
            counter(lambda t: offset(sc[t])), jnp.where(adjacent > 0.0, quota, 1.0), in_cells, in_cells,
            above_thr, max_offset, adjacent, 0)
        thr2_b = jnp.broadcast_to(thr2, (tq, KEY_TILE))
        quota2_b = jnp.broadcast_to(quota2, (tq, KEY_TILE))
        plain_b = jnp.logical_not(adj_b)
        rank_pass(lambda s: (plain_b & (s == thr_b)) | (offset(s) == thr2_b),
                  lambda s: (plain_b & (s > thr_b)) | (adj_b & (s >= next_b)) | (offset(s) > thr2_b),
                  jnp.where(adj_b, quota2_b, quota_b))

    qt_all = q_ref[0].astype(F32).T
    zeros_half = jnp.zeros((HEAD_DIM, GRP * tq), F32)
    qa = []
    for g in range(N_KV_HEADS):
        piece = jnp.concatenate(
            [qt_all[(g * GRP + j) * HEAD_DIM:(g * GRP + j + 1) * HEAD_DIM, :] for j in range(GRP)],
            axis=1)
        halves = [piece, zeros_half] if g == 0 else [zeros_half, piece]
        qa.append(jnp.concatenate(halves, axis=0).astype(BF16))

    def logits(slot, keys, pens, m_old, extra=None):
        pen = jnp.concatenate(pens, axis=0)
        pen = jnp.concatenate([pen] * GRP, axis=1)
        kblk = k_ref[0, keys, :]
        m_new = []
        for g in range(N_KV_HEADS):
            s = _dot(kblk, qa[g]) + pen
            if extra is not None:
                s = s + extra[g]
            sbuf[slot, g, 0:pen.shape[0], :] = s
            m_new.append(jnp.maximum(m_old[g], jnp.max(s, axis=0, keepdims=True)))
        return tuple(m_new)

    def accumulate(slot, keys, nk, m_old, m_new, accs):
        out = []
        for g in range(N_KV_HEADS):
            p = jnp.exp2(sbuf[slot, g, 0:nk, :] - m_new[g])
            out.append(jnp.exp2(m_old[g] - m_new[g]) * accs[g]
                       + _dot(vt_ref[0, g * VT_ROWS:(g + 1) * VT_ROWS, keys], p.astype(BF16)))
        return tuple(out)

    stages_far = (n0 + ATT_TILES - 1) // ATT_TILES
    last_far = jnp.maximum(stages_far - 1, 0)
    ka = ATT_TILES * KEY_TILE

    def far_keys(j):
        return pl.ds(pl.multiple_of(jnp.minimum(j, last_far) * ka, ka), ka)

    def far_pens(j):
        return [jnp.where(j * ATT_TILES + k < n0, sc[jnp.minimum(j * ATT_TILES + k, n0)], NEG)
                for k in range(ATT_TILES)]

    m_init = tuple(jnp.full((1, GRP * tq), NEG, F32) for _ in range(N_KV_HEADS))
    acc_init = tuple(jnp.zeros((VT_ROWS, GRP * tq), F32) for _ in range(N_KV_HEADS))
    m_first = logits(0, far_keys(0), far_pens(0), m_init)

    def stage_pair(i, c):
        m_old, m_cur, accs = c
        a = 2 * i + 1
        m_a = logits(1, far_keys(a), far_pens(a), m_cur)
        accs = accumulate(0, far_keys(a - 1), ka, m_old, m_cur, accs)
        m_b = logits(0, far_keys(a + 1), far_pens(a + 1), m_a)
        accs = accumulate(1, far_keys(a), ka, m_cur, m_a, accs)
        return m_a, m_b, accs

    pairs = stages_far // 2
    m_old, m_cur, accs = lax.fori_loop(0, pairs, stage_pair, (m_init, m_first, acc_init))
    m_last = logits(1, near_keys, [sc[n0], sc[n0 + 1]], m_cur, extra=(bias[variant, 0], bias[variant, 1]))
    accs = accumulate(0, far_keys(2 * pairs), ka, m_old, m_cur, accs)
    accs = accumulate(1, near_keys, 2 * KEY_TILE, m_cur, m_last, accs)
    heads = []
    for g in range(N_KV_HEADS):
        acc = accs[g]
        og = acc[:HEAD_DIM, :] / acc[HEAD_DIM:HEAD_DIM + 1, :]
        heads += [og[:, j * tq:(j + 1) * tq] for j in range(GRP)]
    o_ref[0] = jnp.concatenate(heads, axis=0).T.astype(BF16)


def _dsa(rel_bias, q, qi, sgn, k, vt, ki, *, q_pos0, l_valid):
    b, t, _ = q.shape
    lk = k.shape[1]
    tq = KEY_TILE
    assert lk % (BIG * KEY_TILE) == 0 and q_pos0 % KEY_TILE == 0 and t % tq == 0
    assert q_pos0 + t <= lk
    nt = lk // KEY_TILE
    row = lambda width: pl.BlockSpec((1, tq, width), lambda bi, i: (bi, i, 0))
    keys = lambda shape: (_resident if b == 1 else pl.BlockSpec)(shape, lambda bi, i: (bi, 0, 0))
    return pl.pallas_call(
        functools.partial(_dsa_kernel, q_pos0=q_pos0, l_valid=l_valid),
        grid=(b, t // tq),
        in_specs=[pl.BlockSpec(memory_space=pltpu.SMEM),
                  row(D_ATT), row(N_IDX_HEADS * D_IDX),
                  pl.BlockSpec((1, 8, tq), lambda bi, i: (bi, 0, i)),
                  keys((1, lk, N_KV_HEADS * HEAD_DIM)), keys((1, N_KV_HEADS * VT_ROWS, lk)),
                  keys((1, lk, D_IDX))],
        out_specs=row(D_ATT),
        out_shape=jax.ShapeDtypeStruct((b, t, D_ATT), BF16),
        scratch_shapes=[pltpu.VMEM((nt, KEY_TILE, tq), F32),
                        pltpu.VMEM((2, N_KV_HEADS, 2 * KEY_TILE, GRP * tq), F32),
                        pltpu.VMEM((2, N_KV_HEADS, ATT_TILES * KEY_TILE, GRP * tq), F32)],
        compiler_params=pltpu.CompilerParams(
            dimension_semantics=("arbitrary", "arbitrary"), vmem_limit_bytes=VMEM_LIMIT),
        name="dsa",
    )(rel_bias, q, qi, sgn, k, vt, ki)


def _split3(x):
    hi = x.astype(BF16)
    r = x - hi.astype(F32)
    mid = r.astype(BF16)
    lo = (r - mid.astype(F32)).astype(BF16)
    return hi, mid, lo


def _ssd_kernel(xs_ref, bm_ref, cm_ref, dt_ref, z_ref, alog_ref, dsk_ref, nw_ref, h0_ref,
                y_ref, h_ref, ht, *, q_in):
    c = pl.program_id(1)
    nc = pl.num_programs(1)

    @pl.when(c == 0)
    def _():
        for hh in range(SSM_HEADS):
            ht[hh] = h0_ref[0, hh].T

    def rows(ref):
        x = ref[0]
        if q_in == SSD_Q:
            return x
        return jnp.concatenate([x, jnp.zeros((SSD_Q - q_in, x.shape[1]), x.dtype)], axis=0)

    xs, bm, cm, dt, z = rows(xs_ref), rows(bm_ref), rows(cm_ref), rows(dt_ref), rows(z_ref)
    a = -jnp.exp(alog_ref[...])
    da = dt * a
    ri = lax.broadcasted_iota(jnp.int32, (SSD_Q, SSD_Q), 0)
    ci = lax.broadcasted_iota(jnp.int32, (SSD_Q, SSD_Q), 1)
    causal = ri >= ci
    tril = jnp.where(causal, 1.0, 0.0).astype(BF16)
    hi, mid, lo = _split3(da)
    acum = _dot(tril, hi) + _dot(tril, mid) + _dot(tril, lo)
    acum_t = acum.T

    for g in range(SSM_GROUPS):
        bm_g = bm[:, g * SSM_STATE:(g + 1) * SSM_STATE]
        cm_g = cm[:, g * SSM_STATE:(g + 1) * SSM_STATE].astype(BF16)
        bm_t = bm_g.T.astype(BF16)
        cb = _dot(cm_g, bm_t)
        gs = []
        ssq = jnp.zeros((SSD_Q, 1), F32)
        for r in range(SSM_HPG):
            hh = g * SSM_HPG + r
            lanes = slice(hh * SSM_HEAD_DIM, (hh + 1) * SSM_HEAD_DIM)
            xh = xs[:, lanes]
            dl = DT_LANE0 + hh
            col = acum[:, dl:dl + 1]
            rowv = acum_t[dl:dl + 1, :]
            decay = jnp.exp(jnp.where(causal, col - rowv, NEG))
            xdt = xh * dt[:, dl:dl + 1]
            y = _dot((cb * decay).astype(BF16), xdt.astype(BF16))
            h_prev = ht[hh]
            y = y + jnp.exp(col) * _dot(cm_g, h_prev.astype(BF16))
            last = acum[SSD_Q - 1:SSD_Q, dl:dl + 1]
            xw = (xdt * jnp.exp(last - col)).astype(BF16)
            ht[hh] = h_prev * jnp.exp(last) + _dot(bm_t, xw)
            y = y + xh * dsk_ref[:, lanes]
            gate = y * _silu(z[:, lanes])
            ssq = ssq + jnp.sum(gate * gate, axis=1, keepdims=True)
            gs.append(gate)
        scale = lax.rsqrt(ssq / float(SSM_HPG * SSM_HEAD_DIM) + NORM_EPS)
        for r in range(SSM_HPG):
            hh = g * SSM_HPG + r
            lanes = slice(hh * SSM_HEAD_DIM, (hh + 1) * SSM_HEAD_DIM)
            y_ref[0, :, lanes] = (gs[r] * scale * nw_ref[:, lanes])[:q_in].astype(BF16)

    @pl.when(c == nc - 1)
    def _():
        for hh in range(SSM_HEADS):
            h_ref[0, hh] = ht[hh].T


def _ssd(xs, bm, cm, dt, z, alog, dsk, nw, h0, *, q_in):
    b, t, _ = xs.shape
    row = lambda width: pl.BlockSpec((1, q_in, width), lambda bi, c: (bi, c, 0))
    full = lambda a: _resident(a.shape, lambda bi, c: (0,) * a.ndim)
    hspec = pl.BlockSpec((1, SSM_HEADS, SSM_HEAD_DIM, SSM_STATE), lambda bi, c: (bi, 0, 0, 0))
    return pl.pallas_call(
        functools.partial(_ssd_kernel, q_in=q_in),
        grid=(b, t // q_in),
        in_specs=[row(D_SSM), row(256), row(256), row(LANE), row(D_SSM),
                  full(alog), full(dsk), full(nw), hspec],
        out_specs=(row(D_SSM), hspec),
        out_shape=(jax.ShapeDtypeStruct((b, t, D_SSM), BF16),
                   jax.ShapeDtypeStruct((b, SSM_HEADS, SSM_HEAD_DIM, SSM_STATE), F32)),
        scratch_shapes=[pltpu.VMEM((SSM_HEADS, SSM_STATE, SSM_HEAD_DIM), F32)],
        compiler_params=pltpu.CompilerParams(
            dimension_semantics=("arbitrary", "arbitrary"), vmem_limit_bytes=VMEM_LIMIT),
        name="ssd",
    )(xs, bm, cm, dt, z, alog, dsk, nw, h0)


def _mix_ffn_kernel(x_ref, ya_ref, yb_ref, yc_ref, wo_ref, npost_ref, nfpre_ref, wg_ref, wu_ref,
                    wd_ref, cfw_ref, cfb_ref, nfpost_ref, buff_ref, o_ref, buff_out, eg, *, seg):
    i = pl.program_id(1)
    nseq = eg.shape[0]

    @pl.when(i == 0)
    def _():
        for s in range(nseq):
            eg[s, CARRY - 2:CARRY, :] = buff_ref[0, s]

    mix = _dot(jnp.concatenate([ya_ref[0], yb_ref[0], yc_ref[0]], axis=1), wo_ref[...])
    x1 = x_ref[0] + _rms(mix, npost_ref[...])
    u = _rms(x1, nfpre_ref[...]).astype(BF16)
    g = _dot(u, wg_ref[...])
    convs = []
    for s in range(nseq):
        eg[s, CARRY:CARRY + seg, :] = g[s * seg:(s + 1) * seg, :]
        gc = eg[s, CARRY - 2:CARRY - 2 + seg, :] * cfw_ref[0:1, :]
        for j in range(1, CONV_F_W):
            gc = gc + eg[s, CARRY - 2 + j:CARRY - 2 + j + seg, :] * cfw_ref[j:j + 1, :]
        convs.append(gc)
        tail = eg[s, CARRY + seg - 2:CARRY + seg, :]
        buff_out[0, s] = tail
        eg[s, CARRY - 2:CARRY, :] = tail
    gc = convs[0] if nseq == 1 else jnp.concatenate(convs, axis=0)
    hid = (_silu(gc + cfb_ref[...]) * _dot(u, wu_ref[...])).astype(BF16)
    o_ref[0] = x1 + _rms(_dot(hid, wd_ref[...]), nfpost_ref[...])


def _mix_ffn(x, ya, yb, yc, wo, npost, nfpre, wg, wu, wd, cfw, cfb, nfpost, buff, *, layer, tm):
    b, t, _ = x.shape
    stacked = lambda a: pl.BlockSpec((None,) + a.shape[1:], lambda bi, i: (layer, 0, 0),
                                     pipeline_mode=pl.Buffered(1))
    seg = min(t, tm)
    nseq = tm // seg
    assert tm % seg == 0 and b % nseq == 0 and (nseq == 1 or seg == t)
    tiles = lambda a: a.reshape(b // nseq, nseq * t, a.shape[-1])
    row = lambda width: pl.BlockSpec((1, tm, width), lambda bi, i: (bi, i, 0))
    full = lambda a: _resident(a.shape, lambda bi, i: (0,) * a.ndim)
    state = pl.BlockSpec((1, nseq, CONV_F_W - 1, D_FF), lambda bi, i: (bi, 0, 0, 0))
    x_new, buff_new = pl.pallas_call(
        functools.partial(_mix_ffn_kernel, seg=seg),
        grid=(b // nseq, nseq * t // tm),
        in_specs=[row(D_MODEL), row(D_CONV_MIX), row(D_ATT), row(D_SSM), stacked(wo), full(npost),
                  full(nfpre), stacked(wg), stacked(wu), stacked(wd), full(cfw), full(cfb),
                  full(nfpost), state],
        out_specs=(row(D_MODEL), state),
        out_shape=(jax.ShapeDtypeStruct((b // nseq, nseq * t, D_MODEL), F32),
                   jax.ShapeDtypeStruct((b // nseq, nseq, CONV_F_W - 1, D_FF), F32)),
        scratch_shapes=[pltpu.VMEM((nseq, CARRY + seg, D_FF), F32)],
        compiler_params=pltpu.CompilerParams(
            dimension_semantics=("arbitrary", "arbitrary"), vmem_limit_bytes=VMEM_LIMIT),
        name="mix_ffn",
    )(tiles(x), tiles(ya), tiles(yb), tiles(yc), wo, npost, nfpre, wg, wu, wd, cfw, cfb, nfpost,
      buff.reshape(b // nseq, nseq, CONV_F_W - 1, D_FF))
    return x_new.reshape(b, t, D_MODEL), buff_new.reshape(b, CONV_F_W - 1, D_FF)


def _pack_keys_kernel(ck_ref, cv_ref, cki_ref, k_ref, v_ref, ki_ref, kb_ref, vt_ref, kib_ref, *, past_steps):
    j = pl.program_id(1)
    rows = kb_ref.shape[1]
    cached = j < past_steps

    def with_new(cache, new):
        new = jnp.concatenate([new, jnp.zeros((rows - new.shape[0], new.shape[1]), F32)], axis=0)
        return jnp.where(cached, cache, new)

    heads = lambda ref: [ref[0, 0, g] for g in range(N_KV_HEADS)]
    k_cache = jnp.concatenate([h.T for h in heads(ck_ref)], axis=1)
    kb_ref[0] = with_new(k_cache, k_ref[0]).astype(BF16)
    v_new = jnp.concatenate([v_ref[0], jnp.zeros((rows - v_ref.shape[1], v_ref.shape[2]), F32)], axis=0)
    vt = jnp.where(cached, jnp.concatenate(heads(cv_ref), axis=0), v_new.T)
    vt_ref[0] = _augment_vt(vt).astype(BF16)
    kib_ref[0] = with_new(cki_ref[0, 0].T, ki_ref[0]).astype(BF16)


def _pack_keys(layer, cache_k, cache_v, cache_kidx, k, v, ki):
    b, past = cache_k.shape[1:3]
    t = k.shape[1]
    rows = 2 * BIG * KEY_TILE
    assert past % rows == 0 and t <= rows
    past_steps = past // rows
    lk = past + rows
    cache = lambda *major: pl.BlockSpec(
        (1, 1) + major + (rows,),
        lambda bi, j: (layer, bi) + (0,) * len(major) + (jnp.minimum(j, past_steps - 1),))
    cache_k, cache_v = (jnp.transpose(c, (0, 1, 3, 4, 2)) for c in (cache_k, cache_v))
    cache_kidx = jnp.transpose(cache_kidx, (0, 1, 3, 2))
    new = lambda width: pl.BlockSpec((1, t, width), lambda bi, j: (bi, 0, 0))
    out = lambda width: pl.BlockSpec((1, rows, width), lambda bi, j: (bi, j, 0))
    return pl.pallas_call(
        functools.partial(_pack_keys_kernel, past_steps=past_steps),
        grid=(b, past_steps + 1),
        in_specs=[cache(N_KV_HEADS, HEAD_DIM), cache(N_KV_HEADS, HEAD_DIM), cache(D_IDX),
                  new(128), new(128), new(D_IDX)],
        out_specs=(out(128), pl.BlockSpec((1, N_KV_HEADS * VT_ROWS, rows), lambda bi, j: (bi, 0, j)),
                   out(D_IDX)),
        out_shape=(jax.ShapeDtypeStruct((b, lk, 128), BF16),
                   jax.ShapeDtypeStruct((b, N_KV_HEADS * VT_ROWS, lk), BF16),
                   jax.ShapeDtypeStruct((b, lk, D_IDX), BF16)),
        compiler_params=pltpu.CompilerParams(
            dimension_semantics=("arbitrary", "arbitrary"), vmem_limit_bytes=VMEM_LIMIT),
        name="pack_keys",
    )(cache_k, cache_v, cache_kidx, k, v, ki)


def _layer(x, p, rel_bias, buf_a, buf_s, h0, buf_f, kv_past, *, tm, tm_ffn):
    b, t, _ = x.shape
    outs = _in_proj(x, p['norm_mix_pre'], p['w_in'], p['conv_a_w'], p['conv_ssm_w'], p['conv_ssm_b'],
                    p['dt_bias'], buf_a, buf_s, tm=tm, key_side=kv_past is None)
    (ya, q, k, v, qi, ki, sgn, z, xs, bm, cm, dt, buf_a_new, buf_s_new) = outs[:14]
    if kv_past is None:
        kb, vt, kib = outs[14:]
        yb = _dsa(rel_bias, q, qi, sgn, kb, vt, kib, q_pos0=0, l_valid=t)
    else:
        layer, cache_k, cache_v, cache_kidx = kv_past
        past = cache_k.shape[2]
        qpad = lambda a: jnp.pad(a, ((0, 0), (0, -t % KEY_TILE), (0, 0)))
        kb, vt, kib = _pack_keys(layer, cache_k, cache_v, cache_kidx, k, v, ki)
        yb = _dsa(rel_bias, qpad(q), qpad(qi), sgn, kb, vt, kib, q_pos0=past, l_valid=past + t)[:, :t]
    yc, h_new = _ssd(xs, bm, cm, dt, z, p['a_log'], p['d_skip'], p['ssm_norm_w'],
                     h0.reshape(b, SSM_HEADS, SSM_HEAD_DIM, SSM_STATE), q_in=min(t, SSD_Q))
    x_new, buf_f_new = _mix_ffn(x, ya, yb, yc, p['w_out'], p['norm_mix_post'], p['norm_ffn_pre'],
                                p['w_gate'], p['w_up'], p['w_down'], p['conv_ffn_w'],
                                p['conv_ffn_b'], p['norm_ffn_post'], buf_f, layer=p['layer'], tm=tm_ffn)
    st = (k.reshape(b, t, N_KV_HEADS, HEAD_DIM), v.reshape(b, t, N_KV_HEADS, HEAD_DIM), ki,
          buf_a_new, buf_s_new, h_new.reshape(b, SSM_GROUPS, SSM_HPG, SSM_HEAD_DIM, SSM_STATE),
          buf_f_new)
    return x_new, st


def _pack_w_in_kernel(wt_ref, o_ref):
    for c, c0 in enumerate(W_IN_TILE_COLS):
        o_ref[:, c * LANE:(c + 1) * LANE] = wt_ref[c0:c0 + LANE, :].T.astype(BF16)


def _pack_w_in(wt):
    return pl.pallas_call(
        _pack_w_in_kernel,
        out_shape=jax.ShapeDtypeStruct((wt.shape[1], D_IN_PACKED), BF16),
        compiler_params=pltpu.CompilerParams(vmem_limit_bytes=VMEM_LIMIT),
        name="pack_w_in",
    )(wt)


def _dt_lanes(v):
    return jnp.pad(v, (DT_LANE0, 0))[None, :]


def kernel(x_prompt, x_sample, cache_k, cache_v, cache_kidx, state_conv_a, state_conv_ssm, state_ssm, state_conv_ffn, rel_bias, norm_mix_pre, norm_mix_post, norm_ffn_pre, norm_ffn_post, w_in, conv_a_w, conv_ssm_w, conv_ssm_b, dt_bias, a_log, d_skip, ssm_norm_w, w_out, w_gate, w_up, conv_ffn_w, conv_ffn_b, w_down):
    depth = w_in.shape[0]
    bp = x_prompt.shape[0]
    yp, ys = x_prompt, x_sample
    outs_p, outs_s = [], []
    w_out16, w_gate16, w_up16, w_down16 = (w.astype(BF16) for w in (w_out, w_gate, w_up, w_down))
    for l in range(depth):
        p = {'norm_mix_pre': norm_mix_pre[l][None], 'norm_mix_post': norm_mix_post[l][None],
             'norm_ffn_pre': norm_ffn_pre[l][None], 'norm_ffn_post': norm_ffn_post[l][None],
             'w_in': _pack_w_in(jnp.swapaxes(w_in[l], 0, 1)), 'conv_a_w': conv_a_w[l], 'conv_ssm_w': conv_ssm_w[l],
             'conv_ssm_b': conv_ssm_b[l][None], 'dt_bias': _dt_lanes(dt_bias[l]),
             'a_log': _dt_lanes(a_log[l]), 'd_skip': jnp.repeat(d_skip[l], SSM_HEAD_DIM)[None],
             'ssm_norm_w': ssm_norm_w[l][None], 'layer': l, 'w_out': w_out16,
             'w_gate': w_gate16, 'w_up': w_up16,
             'conv_ffn_w': conv_ffn_w[l], 'conv_ffn_b': conv_ffn_b[l][None],
             'w_down': w_down16}
        yp, st_p = _layer(
            yp, p, rel_bias,
            jnp.zeros((bp, CONV_A_W - 1, D_CONV_MIX), F32),
            jnp.zeros((bp, SSM_CONV_W - 1, D_XBC), F32),
            jnp.zeros((bp, SSM_GROUPS, SSM_HPG, SSM_HEAD_DIM, SSM_STATE), F32),
            jnp.zeros((bp, CONV_F_W - 1, D_FF), F32),
            None, tm=256, tm_ffn=512)
        ys, st_s = _layer(
            ys, p, rel_bias, state_conv_a[l], state_conv_ssm[l], state_ssm[l], state_conv_ffn[l],
            (l, cache_k, cache_v, cache_kidx), tm=ys.shape[1], tm_ffn=ys.shape[0] * ys.shape[1])
        outs_p.append(st_p)
        outs_s.append(st_s)

    def stack(outs, i):
        return jnp.stack([o[i] for o in outs], axis=0)

    res = [yp, ys]
    for i in range(7):
        res.append(stack(outs_p, i))
        res.append(stack(outs_s, i))
    return tuple(res)
```
